```python
import jax
import jax.numpy as jnp
from jax import lax
import numpy as np

D_MODEL = 2048
BATCH = 2
SEQ = 4096
DEPTH = 1

MEM_LEN = 256
EPS = 1e-6
MLSTM_HEADS = 4
MLSTM_QK = 128
MLSTM_V = 256
MLSTM_CHUNK = 64
CONV_WIDTH = 4
HGRN_HEADS = 8
HGRN_DK = 128
HGRN_DV = 128
HGRN_CHUNK = 16
XATTN_HEADS = 4
XATTN_HEAD_DIM = D_MODEL // XATTN_HEADS
D_FF = 5632

MLSTM_QK_W = MLSTM_HEADS * MLSTM_QK
MLSTM_V_W = MLSTM_HEADS * MLSTM_V
HGRN_K_W = HGRN_HEADS * HGRN_DK
HGRN_V_W = HGRN_HEADS * HGRN_DV
SPLIT_SIZES = (MLSTM_QK_W, MLSTM_QK_W, MLSTM_V_W, MLSTM_V_W, MLSTM_HEADS, MLSTM_HEADS,
               HGRN_K_W, HGRN_K_W, HGRN_V_W, HGRN_V_W, D_MODEL, D_MODEL)
D_IN = (2 * MLSTM_QK_W + 2 * MLSTM_V_W + 2 * MLSTM_HEADS + 2 * HGRN_K_W + 2 * HGRN_V_W + 2 * D_MODEL)

kernel_name = 'hybrid_mlstm_hgrn2_macaron_block'


def rmsnorm(x, g):
    xf = x.astype(jnp.float32)
    y = xf * lax.rsqrt(jnp.mean(xf * xf, axis=-1, keepdims=True) + EPS)
    return (y * g.astype(jnp.float32)).astype(x.dtype)


def head_rmsnorm(h, g):
    n_h, d = h.shape[1], h.shape[3]
    y = h * lax.rsqrt(jnp.mean(h * h, axis=-1, keepdims=True) + EPS)
    return y * g.astype(jnp.float32).reshape(1, n_h, 1, d)


def split_heads(t, n_heads):
    b, s, _ = t.shape
    return t.reshape(b, s, n_heads, -1).transpose(0, 2, 1, 3)


def merge_heads(t):
    b, h, s, d = t.shape
    return t.transpose(0, 2, 1, 3).reshape(b, s, h * d)


def to_chunks(t, chunk):
    b, h, s = t.shape[:3]
    return jnp.moveaxis(t.reshape((b, h, s // chunk, chunk) + t.shape[3:]), 2, 0)


def from_chunks(t):
    nc, b, h, l, d = t.shape
    return jnp.moveaxis(t, 0, 2).reshape(b, h, nc * l, d)


def causal_conv(x, w, b):
    c = x.shape[-1]
    y = lax.conv_general_dilated(x, w[:, None, :].astype(x.dtype), window_strides=(1,),
                                 padding=[(CONV_WIDTH - 1, 0)],
                                 dimension_numbers=('NWC', 'WIO', 'NWC'), feature_group_count=c)
    return y + b.astype(x.dtype)


def swiglu(h, w1, w3, w2):
    return (jax.nn.silu(h @ w1) * (h @ w3)) @ w2


def mlstm_chunkwise(q, k, v, i_pre, f_log):
    b_, h_, _, dk = q.shape
    dv = v.shape[-1]
    causal = jnp.tril(jnp.ones((MLSTM_CHUNK, MLSTM_CHUNK), dtype=bool))

    def step(carry, inp):
        c_state, n_state, m_state = carry
        qc, kc, vc, ic, fc = inp
        b = jnp.cumsum(fc, axis=-1)
        d_log = jnp.where(causal, b[..., :, None] - b[..., None, :] + ic[..., None, :], -jnp.inf)
        inter_log = b + m_state[..., None]
        m_t = jnp.maximum(jnp.max(d_log, axis=-1), inter_log)
        s = jnp.einsum('bhtd,bhsd->bhts', qc, kc) * jnp.exp(d_log - m_t[..., None])
        w_inter = jnp.exp(inter_log - m_t)
        num = jnp.einsum('bhts,bhse->bhte', s, vc) + w_inter[..., None] * jnp.einsum('bhtd,bhde->bhte', qc, c_state)
        den = jnp.sum(s, axis=-1) + w_inter * jnp.einsum('bhtd,bhd->bht', qc, n_state)
        h = num / jnp.maximum(jnp.abs(den), jnp.exp(-m_t))[..., None]
        b_last = b[..., -1]
        a_log = b_last[..., None] - b + ic
        m_new = jnp.maximum(b_last + m_state, jnp.max(a_log, axis=-1))
        w_a = jnp.exp(a_log - m_new[..., None])
        decay = jnp.exp(b_last + m_state - m_new)
        c_state = decay[..., None, None] * c_state + jnp.einsum('bhs,bhsd,bhse->bhde', w_a, kc, vc)
        n_state = decay[..., None] * n_state + jnp.einsum('bhs,bhsd->bhd', w_a, kc)
        return (c_state, n_state, m_new), h

    init = (jnp.zeros((b_, h_, dk, dv), jnp.float32), jnp.zeros((b_, h_, dk), jnp.float32),
            jnp.zeros((b_, h_), jnp.float32))
    xs = (to_chunks(q, MLSTM_CHUNK), to_chunks(k, MLSTM_CHUNK), to_chunks(v, MLSTM_CHUNK),
          to_chunks(i_pre, MLSTM_CHUNK), to_chunks(f_log, MLSTM_CHUNK))
    _, h = lax.scan(step, init, xs)
    return from_chunks(h)


def hgrn2_chunkwise(q, k, v, g_log):
    b_, h_, _, dk = q.shape
    dv = v.shape[-1]
    causal = jnp.tril(jnp.ones((HGRN_CHUNK, HGRN_CHUNK), dtype=bool))

    def step(state, inp):
        qc, kc, vc, gc = inp
        g = jnp.cumsum(gc, axis=-2)
        diff = jnp.where(causal[:, :, None], g[..., :, None, :] - g[..., None, :, :], -jnp.inf)
        a = jnp.einsum('bhtd,bhsd,bhtsd->bhts', qc, kc, jnp.exp(diff))
        o = jnp.einsum('bhts,bhse->bhte', a, vc) + jnp.einsum('bhtd,bhde->bhte', qc * jnp.exp(g), state)
        g_last = g[..., -1, :]
        state = jnp.exp(g_last)[..., None] * state + jnp.einsum(
            'bhsd,bhse->bhde', kc * jnp.exp(g_last[..., None, :] - g), vc)
        return state, o

    init = jnp.zeros((b_, h_, dk, dv), jnp.float32)
    xs = (to_chunks(q, HGRN_CHUNK), to_chunks(k, HGRN_CHUNK), to_chunks(v, HGRN_CHUNK),
          to_chunks(g_log, HGRN_CHUNK))
    _, o = lax.scan(step, init, xs)
    return from_chunks(o)


def mlstm_mixer(q_raw, k_raw, v_raw, o_raw, i_raw, f_raw, conv_w, conv_b, ig_bias, fg_bias, head_norm):
    qk = jax.nn.silu(causal_conv(jnp.concatenate([q_raw, k_raw], axis=-1), conv_w, conv_b)).astype(jnp.float32)
    q, k = jnp.split(qk, 2, axis=-1)
    q = split_heads(q, MLSTM_HEADS)
    k = split_heads(k, MLSTM_HEADS) * (MLSTM_QK ** -0.5)
    v = split_heads(v_raw.astype(jnp.float32), MLSTM_HEADS)
    i_pre = (i_raw.astype(jnp.float32) + ig_bias.astype(jnp.float32)).transpose(0, 2, 1)
    f_log = jax.nn.log_sigmoid(f_raw.astype(jnp.float32) + fg_bias.astype(jnp.float32)).transpose(0, 2, 1)
    h = mlstm_chunkwise(q, k, v, i_pre, f_log)
    h = merge_heads(head_rmsnorm(h, head_norm))
    return h * jax.nn.sigmoid(o_raw.astype(jnp.float32))


def hgrn2_mixer(q_raw, f_raw, i_raw, g_raw, lb, head_norm):
    q = split_heads(jax.nn.silu(q_raw.astype(jnp.float32)), HGRN_HEADS) * (HGRN_DK ** -0.5)
    f = lb + (1.0 - lb) * jax.nn.sigmoid(f_raw.astype(jnp.float32))
    k = split_heads(1.0 - f, HGRN_HEADS)
    g_log = split_heads(jnp.log(f), HGRN_HEADS)
    v = split_heads(i_raw.astype(jnp.float32), HGRN_HEADS)
    o = hgrn2_chunkwise(q, k, v, g_log)
    o = merge_heads(head_rmsnorm(o, head_norm))
    return o * jax.nn.silu(g_raw.astype(jnp.float32))


def cross_attention(xn, memn, w_q, w_kv, w_o):
    b, s, _ = xn.shape
    m = memn.shape[1]
    q = (xn @ w_q).reshape(b, s, XATTN_HEADS, XATTN_HEAD_DIM)
    k, v = jnp.split(memn @ w_kv, 2, axis=-1)
    k = k.reshape(b, m, XATTN_HEADS, XATTN_HEAD_DIM)
    v = v.reshape(b, m, XATTN_HEADS, XATTN_HEAD_DIM)
    scores = jnp.einsum('bthd,bmhd->bhtm', q, k).astype(jnp.float32) * (XATTN_HEAD_DIM ** -0.5)
    p = jax.nn.softmax(scores, axis=-1).astype(v.dtype)
    o = jnp.einsum('bhtm,bmhd->bthd', p, v).reshape(b, s, D_MODEL)
    return o @ w_o


def setup_inputs(seed: int = 0) -> dict:
    key = jax.random.key(seed)
    ks = jax.random.split(key, 32)

    def normal(k, shape, scale):
        return scale * jax.random.normal(k, shape, jnp.float32)

    def gain(k, shape):
        return 1.0 + normal(k, shape, 0.02)

    fg_base = jnp.linspace(3.0, 6.0, MLSTM_HEADS, dtype=jnp.float32)[None, :]
    return {
        'x': normal(ks[0], (BATCH, SEQ, D_MODEL), 1.0),
        'mem': normal(ks[1], (BATCH, MEM_LEN, D_MODEL), 1.0),
        'norm_ffn1': gain(ks[2], (DEPTH, D_MODEL)),
        'ffn1_w1': normal(ks[3], (DEPTH, D_MODEL, D_FF), D_MODEL ** -0.5),
        'ffn1_w3': normal(ks[4], (DEPTH, D_MODEL, D_FF), D_MODEL ** -0.5),
        'ffn1_w2': normal(ks[5], (DEPTH, D_FF, D_MODEL), D_FF ** -0.5),
        'norm_mix': gain(ks[6], (DEPTH, D_MODEL)),
        'w_in': normal(ks[7], (DEPTH, D_MODEL, D_IN), D_MODEL ** -0.5),
        'mlstm_conv_w': normal(ks[8], (DEPTH, CONV_WIDTH, 2 * MLSTM_QK_W), CONV_WIDTH ** -0.5),
        'mlstm_conv_b': normal(ks[9], (DEPTH, 2 * MLSTM_QK_W), 0.02),
        'mlstm_ig_bias': normal(ks[10], (DEPTH, MLSTM_HEADS), 0.1),
        'mlstm_fg_bias': fg_base + normal(ks[11], (DEPTH, MLSTM_HEADS), 0.01),
        'mlstm_head_norm': gain(ks[12], (DEPTH, MLSTM_V_W)),
        'hgrn_lb_logits': normal(ks[13], (DEPTH + 1, HGRN_K_W), 0.1),
        'hgrn_head_norm': gain(ks[14], (DEPTH, HGRN_V_W)),
        'w_proj_m': normal(ks[15], (DEPTH, MLSTM_V_W, D_MODEL), MLSTM_V_W ** -0.5),
        'w_proj_h': normal(ks[16], (DEPTH, HGRN_V_W, D_MODEL), HGRN_V_W ** -0.5),
        'w_out': normal(ks[17], (DEPTH, D_MODEL, D_MODEL), D_MODEL ** -0.5),
        'norm_xattn': gain(ks[18], (DEPTH, D_MODEL)),
        'norm_mem': gain(ks[19], (DEPTH, D_MODEL)),
        'xattn_wq': normal(ks[20], (DEPTH, D_MODEL, D_MODEL), D_MODEL ** -0.5),
        'xattn_wkv': normal(ks[21], (DEPTH, D_MODEL, 2 * D_MODEL), D_MODEL ** -0.5),
        'xattn_wo': normal(ks[22], (DEPTH, D_MODEL, D_MODEL), D_MODEL ** -0.5),
        'norm_ffn2': gain(ks[23], (DEPTH, D_MODEL)),
        'ffn2_w1': normal(ks[24], (DEPTH, D_MODEL, D_FF), D_MODEL ** -0.5),
        'ffn2_w3': normal(ks[25], (DEPTH, D_MODEL, D_FF), D_MODEL ** -0.5),
        'ffn2_w2': normal(ks[26], (DEPTH, D_FF, D_MODEL), D_FF ** -0.5),
        'norm_final': gain(ks[27], (D_MODEL,)),
    }


def reference(x, mem, norm_ffn1, ffn1_w1, ffn1_w3, ffn1_w2, norm_mix, w_in, mlstm_conv_w, mlstm_conv_b,
              mlstm_ig_bias, mlstm_fg_bias, mlstm_head_norm, hgrn_lb_logits, hgrn_head_norm, w_proj_m,
              w_proj_h, w_out, norm_xattn, norm_mem, xattn_wq, xattn_wkv, xattn_wo, norm_ffn2, ffn2_w1,
              ffn2_w3, ffn2_w2, norm_final):
    split_points = []
    acc = 0
    for size in SPLIT_SIZES[:-1]:
        acc += size
        split_points.append(acc)
    lb_all = jnp.cumsum(jax.nn.softmax(hgrn_lb_logits.astype(jnp.float32), axis=0), axis=0)
    for l in range(DEPTH):
        h = rmsnorm(x, norm_ffn1[l])
        x = x + 0.5 * swiglu(h, ffn1_w1[l], ffn1_w3[l], ffn1_w2[l])
        h = rmsnorm(x, norm_mix[l])
        (mq, mk, mv, mo, mi, mf, hq, hf, hi, hg, gate_m, gate_h) = jnp.split(h @ w_in[l], split_points, axis=-1)
        y_m = mlstm_mixer(mq, mk, mv, mo, mi, mf, mlstm_conv_w[l], mlstm_conv_b[l], mlstm_ig_bias[l],
                          mlstm_fg_bias[l], mlstm_head_norm[l]).astype(x.dtype)
        lb = lb_all[l + 1] - lb_all[0]
        y_h = hgrn2_mixer(hq, hf, hi, hg, lb, hgrn_head_norm[l]).astype(x.dtype)
        merged = jax.nn.sigmoid(gate_m) * (y_m @ w_proj_m[l]) + jax.nn.sigmoid(gate_h) * (y_h @ w_proj_h[l])
        x = x + merged @ w_out[l]
        h = rmsnorm(x, norm_xattn[l])
        m = rmsnorm(mem, norm_mem[l])
        x = x + cross_attention(h, m, xattn_wq[l], xattn_wkv[l], xattn_wo[l])
        h = rmsnorm(x, norm_ffn2[l])
        x = x + 0.5 * swiglu(h, ffn2_w1[l], ffn2_w3[l], ffn2_w2[l])
    return rmsnorm(x, norm_final)
```

```python
import functools

import jax
import jax.numpy as jnp
from jax import lax
from jax.experimental import pallas as pl
from jax.experimental.pallas import tpu as pltpu

F32 = jnp.float32
BF16 = jnp.bfloat16
EPS = 1e-6

D_MODEL = 2048
D_FF = 5632
MLSTM_HEADS = 4
MLSTM_QK = 128
MLSTM_V = 256
CONV_WIDTH = 4
HGRN_HEADS = 8
HGRN_DK = 128
HGRN_DV = 128
XATTN_HEADS = 4
XATTN_HEAD_DIM = D_MODEL // XATTN_HEADS

MLSTM_QK_W = MLSTM_HEADS * MLSTM_QK
MLSTM_V_W = MLSTM_HEADS * MLSTM_V
HGRN_W = HGRN_HEADS * HGRN_DK

VMEM_LIMIT_BYTES = 56 * 1024 * 1024
LANES = 128

COL_GM = 0
COL_GH = COL_GM + D_MODEL
COL_MQ = COL_GH + D_MODEL
COL_MK = COL_MQ + MLSTM_QK_W
COL_MV = COL_MK + MLSTM_QK_W
COL_MO = COL_MV + MLSTM_V_W
COL_HQ = COL_MO + MLSTM_V_W
COL_HF = COL_HQ + HGRN_W
COL_HI = COL_HF + HGRN_W
COL_HG = COL_HI + HGRN_W
PROJ_W = COL_HG + HGRN_W

MLSTM_L = 256
HGRN_SUB = 16
HGRN_TB = 128


def _rms(x, g):
    return x * lax.rsqrt(jnp.mean(x * x, axis=-1, keepdims=True) + EPS) * g


def _sigmoid(x):
    return 1.0 / (1.0 + jnp.exp(-x))


def _log_sigmoid(x):
    return jnp.minimum(x, 0.0) - jnp.log(1.0 + jnp.exp(-jnp.abs(x)))


def _params(*sem):
    return pltpu.CompilerParams(dimension_semantics=sem, vmem_limit_bytes=VMEM_LIMIT_BYTES)


def _ffn_kernel(*refs, final_norm):
    if final_norm:
        x_ref, g_ref, w1_ref, w3_ref, w2_ref, gf_ref, o_ref, h_sc, acc_sc = refs
    else:
        x_ref, g_ref, w1_ref, w3_ref, w2_ref, o_ref, h_sc, acc_sc = refs
    j = pl.program_id(1)

    @pl.when(j == 0)
    def _():
        h_sc[...] = _rms(x_ref[...], g_ref[...]).astype(BF16)
        acc_sc[...] = jnp.zeros_like(acc_sc)

    h = h_sc[...]
    a = jnp.dot(h, w1_ref[...], preferred_element_type=F32)
    b = jnp.dot(h, w3_ref[...], preferred_element_type=F32)
    act = (a * _sigmoid(a) * b).astype(BF16)
    acc_sc[...] += jnp.dot(act, w2_ref[...], preferred_element_type=F32)

    @pl.when(j == pl.num_programs(1) - 1)
    def _():
        y = x_ref[...] + 0.5 * acc_sc[...]
        if final_norm:
            y = _rms(y, gf_ref[...])
        o_ref[...] = y


def _ffn(x2d, g, w1, w3, w2, g_final=None, *, tm=512, tf=512):
    n, d = x2d.shape
    f = w1.shape[1]
    final_norm = g_final is not None
    in_specs = [
        pl.BlockSpec((tm, d), lambda i, j: (i, 0)),
        pl.BlockSpec((1, d), lambda i, j: (0, 0)),
        pl.BlockSpec((d, tf), lambda i, j: (0, j)),
        pl.BlockSpec((d, tf), lambda i, j: (0, j)),
        pl.BlockSpec((tf, d), lambda i, j: (j, 0)),
    ]
    args = [x2d, g.reshape(1, d), w1, w3, w2]
    if final_norm:
        in_specs.append(pl.BlockSpec((1, d), lambda i, j: (0, 0)))
        args.append(g_final.reshape(1, d))
    return pl.pallas_call(
        functools.partial(_ffn_kernel, final_norm=final_norm),
        grid=(n // tm, f // tf),
        in_specs=in_specs,
        out_specs=pl.BlockSpec((tm, d), lambda i, j: (i, 0)),
        out_shape=jax.ShapeDtypeStruct((n, d), F32),
        scratch_shapes=[pltpu.VMEM((tm, d), BF16), pltpu.VMEM((tm, d), F32)],
        compiler_params=_params("parallel", "arbitrary"),
        name="ffn",
    )(*args)


def _in_proj_kernel(x_ref, g_ref, w_ref, wg_ref, p_ref, gates_ref, h_sc):
    j = pl.program_id(1)

    @pl.when(j == 0)
    def _():
        h = _rms(x_ref[...], g_ref[...]).astype(BF16)
        h_sc[...] = h
        gates_ref[...] = jnp.dot(h, wg_ref[...], preferred_element_type=F32)

    p_ref[...] = jnp.dot(h_sc[...], w_ref[...], preferred_element_type=F32)


def _in_proj(x2d, g, w_main, w_gates, *, tm=1024, tn=1024):
    n, d = x2d.shape
    nw = w_main.shape[1]
    return pl.pallas_call(
        _in_proj_kernel,
        grid=(n // tm, nw // tn),
        in_specs=[
            pl.BlockSpec((tm, d), lambda i, j: (i, 0)),
            pl.BlockSpec((1, d), lambda i, j: (0, 0)),
            pl.BlockSpec((d, tn), lambda i, j: (0, j)),
            pl.BlockSpec((d, LANES), lambda i, j: (0, 0)),
        ],
        out_specs=[
            pl.BlockSpec((tm, tn), lambda i, j: (i, j)),
            pl.BlockSpec((tm, LANES), lambda i, j: (i, 0)),
        ],
        out_shape=[
            jax.ShapeDtypeStruct((n, nw), F32),
            jax.ShapeDtypeStruct((n, LANES), F32),
        ],
        scratch_shapes=[pltpu.VMEM((tm, d), BF16)],
        compiler_params=_params("parallel", "arbitrary"),
        name="in_proj",
    )(x2d, g.reshape(1, d), w_main, w_gates)


def _mlstm_kernel(q_ref, k_ref, v_ref, o_ref, gcol_ref, grow_ref, cw_ref, cb_ref, bcol_ref, brow_ref,
                  hn_ref, y_ref, xpad_sc, c_sc, n_sc, m_sc):
    L = MLSTM_L
    H = MLSTM_HEADS
    c = pl.program_id(1)

    @pl.when(c == 0)
    def _():
        xpad_sc[0:8, :] = jnp.zeros((8, 2 * MLSTM_QK_W), F32)
        c_sc[...] = jnp.zeros_like(c_sc)
        n_sc[...] = jnp.zeros_like(n_sc)
        m_sc[...] = jnp.zeros_like(m_sc)

    xpad_sc[8:8 + L, 0:MLSTM_QK_W] = q_ref[0]
    xpad_sc[8:8 + L, MLSTM_QK_W:2 * MLSTM_QK_W] = k_ref[0]
    acc = jnp.broadcast_to(cb_ref[...], (L, 2 * MLSTM_QK_W))
    for j in range(CONV_WIDTH):
        off = 8 - (CONV_WIDTH - 1) + j
        acc = acc + cw_ref[j:j + 1, :] * xpad_sc[off:off + L, :]
    xpad_sc[0:8, :] = xpad_sc[L:L + 8, :]
    qk = acc * _sigmoid(acc)

    gcol = gcol_ref[0] + bcol_ref[...]
    grow = grow_ref[0] + brow_ref[...]
    fcol = _log_sigmoid(gcol)
    frow = _log_sigmoid(grow)
    r_i = lax.broadcasted_iota(jnp.int32, (L, L), 0)
    c_i = lax.broadcasted_iota(jnp.int32, (L, L), 1)
    causal = r_i >= c_i
    tril = jnp.where(causal, 1.0, 0.0).astype(F32)
    triu = jnp.where(r_i <= c_i, 1.0, 0.0).astype(F32)
    bcol = jnp.dot(tril, fcol, preferred_element_type=F32, precision=lax.Precision.HIGHEST)
    brow = jnp.dot(frow, triu, preferred_element_type=F32, precision=lax.Precision.HIGHEST)

    for h in range(H):
        b_c = bcol[:, H + h:H + h + 1]
        i_c = gcol[:, h:h + 1]
        b_r = brow[H + h:H + h + 1, :]
        i_r = grow[h:h + 1, :]
        m_prev = m_sc[h:h + 1, 0:1]

        d_log = jnp.where(causal, b_c - b_r + i_r, -jnp.inf)
        inter_log = b_c + m_prev
        m_t = jnp.maximum(jnp.max(d_log, axis=1, keepdims=True), inter_log)

        q_h = qk[:, h * MLSTM_QK:(h + 1) * MLSTM_QK]
        k_h = qk[:, MLSTM_QK_W + h * MLSTM_QK:MLSTM_QK_W + (h + 1) * MLSTM_QK] * (MLSTM_QK ** -0.5)
        v_h = v_ref[0, :, h * MLSTM_V:(h + 1) * MLSTM_V].astype(BF16)
        q_b = q_h.astype(BF16)

        s = lax.dot_general(q_b, k_h.astype(BF16), (((1,), (1,)), ((), ())), preferred_element_type=F32)
        s = s * jnp.exp(d_log - m_t)
        w_inter = jnp.exp(inter_log - m_t)
        c_h = c_sc[h]
        n_h = n_sc[h:h + 1, :]
        num = (jnp.dot(s.astype(BF16), v_h, preferred_element_type=F32)
               + w_inter * jnp.dot(q_b, c_h.astype(BF16), preferred_element_type=F32))
        den = jnp.sum(s, axis=1, keepdims=True) + w_inter * jnp.sum(q_h * n_h, axis=1, keepdims=True)
        hh = num / jnp.maximum(jnp.abs(den), jnp.exp(-m_t))

        hn = hh * lax.rsqrt(jnp.mean(hh * hh, axis=-1, keepdims=True) + EPS)
        hn = hn * hn_ref[:, h * MLSTM_V:(h + 1) * MLSTM_V]
        y = hn * _sigmoid(o_ref[0, :, h * MLSTM_V:(h + 1) * MLSTM_V])
        y_ref[0, :, h * MLSTM_V:(h + 1) * MLSTM_V] = y.astype(y_ref.dtype)

        b_last = b_c[L - 1:L, :]
        a_log = b_last - b_c + i_c
        m_new = jnp.maximum(b_last + m_prev, jnp.max(a_log, axis=0, keepdims=True))
        w_a = jnp.exp(a_log - m_new)
        decay = jnp.exp(b_last + m_prev - m_new)
        kw = k_h * w_a
        c_sc[h] = decay * c_h + lax.dot_general(kw.astype(BF16), v_h, (((0,), (0,)), ((), ())),
                                                preferred_element_type=F32)
        n_sc[h:h + 1, :] = decay * n_h + jnp.sum(kw, axis=0, keepdims=True)
        m_sc[h:h + 1, :] = jnp.broadcast_to(m_new, (1, LANES))


def _mlstm(proj, gates_col, gates_row, conv_w, conv_b, bias_col, bias_row, head_norm):
    b, t, _ = proj.shape
    L = MLSTM_L
    qw = MLSTM_QK_W
    vw = MLSTM_V_W
    return pl.pallas_call(
        _mlstm_kernel,
        grid=(b, t // L),
        in_specs=[
            pl.BlockSpec((1, L, qw), lambda i, c: (i, c, COL_MQ // qw)),
            pl.BlockSpec((1, L, qw), lambda i, c: (i, c, COL_MK // qw)),
            pl.BlockSpec((1, L, vw), lambda i, c: (i, c, COL_MV // vw)),
            pl.BlockSpec((1, L, vw), lambda i, c: (i, c, COL_MO // vw)),
            pl.BlockSpec((1, L, LANES), lambda i, c: (i, c, 0)),
            pl.BlockSpec((1, 8, L), lambda i, c: (i, 0, c)),
            pl.BlockSpec((CONV_WIDTH, 2 * qw), lambda i, c: (0, 0)),
            pl.BlockSpec((1, 2 * qw), lambda i, c: (0, 0)),
            pl.BlockSpec((1, LANES), lambda i, c: (0, 0)),
            pl.BlockSpec((8, 1), lambda i, c: (0, 0)),
            pl.BlockSpec((1, vw), lambda i, c: (0, 0)),
        ],
        out_specs=pl.BlockSpec((1, L, vw), lambda i, c: (i, c, 0)),
        out_shape=jax.ShapeDtypeStruct((b, t, vw), BF16),
        scratch_shapes=[
            pltpu.VMEM((L + 8, 2 * qw), F32),
            pltpu.VMEM((MLSTM_HEADS, MLSTM_QK, MLSTM_V), F32),
            pltpu.VMEM((8, MLSTM_QK), F32),
            pltpu.VMEM((8, LANES), F32),
        ],
        compiler_params=_params("parallel", "arbitrary"),
        name="mlstm",
    )(proj, proj, proj, proj, gates_col, gates_row, conv_w, conv_b, bias_col, bias_row, head_norm)


def _hgrn_kernel(q_ref, f_ref, v_ref, og_ref, lbl_ref, hn_ref, y_ref, g_sc, k_sc, q_sc, st_sc):
    TB = HGRN_TB
    SUB = HGRN_SUB
    NSUB = TB // SUB
    H = HGRN_HEADS
    DK = HGRN_DK
    c = pl.program_id(1)

    @pl.when(c == 0)
    def _():
        st_sc[...] = jnp.zeros_like(st_sc)

    lbl = lbl_ref[...]
    lmax = jnp.max(lbl, axis=0, keepdims=True)
    le = jnp.exp(lbl - lmax)
    lb = le[1:2, :] / jnp.sum(le, axis=0, keepdims=True)

    f = lb + (1.0 - lb) * _sigmoid(f_ref[0])
    k_sc[...] = 1.0 - f
    qraw = q_ref[0]
    q_sc[...] = qraw * _sigmoid(qraw) * (DK ** -0.5)
    r_i = lax.broadcasted_iota(jnp.int32, (TB, TB), 0)
    c_i = lax.broadcasted_iota(jnp.int32, (TB, TB), 1)
    same = (r_i - c_i) <= (r_i & (SUB - 1))
    causal = jnp.logical_and(same, r_i >= c_i)
    tri = jnp.where(causal, 1.0, 0.0).astype(F32)
    g_sc[...] = jnp.dot(tri, jnp.log(f), preferred_element_type=F32, precision=lax.Precision.HIGHEST)

    lane = lax.broadcasted_iota(jnp.int32, (SUB, TB), 1)
    row = lax.broadcasted_iota(jnp.int32, (SUB, TB), 0)

    for h in range(H):
        cols = slice(h * DK, (h + 1) * DK)
        k_blk = k_sc[:, cols].astype(BF16)
        v_blk = v_ref[0, :, cols].astype(BF16)

        a_rows = []
        for cc in range(NSUB):
            base = cc * SUB
            g_c = g_sc[base:base + SUB, cols]
            q_c = q_sc[base:base + SUB, cols]
            lhs = jnp.concatenate(
                [q_c * jnp.exp(jnp.minimum(g_c - g_sc[base + s:base + s + 1, cols], 0.0)) for s in range(SUB)],
                axis=0).astype(BF16)
            r = lax.dot_general(lhs, k_blk, (((1,), (1,)), ((), ())), preferred_element_type=F32)
            a_c = jnp.zeros((SUB, TB), F32)
            for s in range(SUB):
                a_c = a_c + jnp.where(lane == base + s, r[s * SUB:(s + 1) * SUB, :], 0.0)
            a_rows.append(jnp.where(row >= lane - base, a_c, 0.0))
        a_blk = jnp.concatenate(a_rows, axis=0).astype(BF16)
        o_intra = jnp.dot(a_blk, v_blk, preferred_element_type=F32)

        o_rows = []
        st = st_sc[h]
        for cc in range(NSUB):
            base = cc * SUB
            g_c = g_sc[base:base + SUB, cols]
            q_c = q_sc[base:base + SUB, cols]
            k_c = k_sc[base:base + SUB, cols]
            v_c = v_blk[base:base + SUB, :]
            g_last = g_c[SUB - 1:SUB, :]
            qd = (q_c * jnp.exp(g_c)).astype(BF16)
            o_rows.append(lax.dot_general(qd, st.astype(BF16), (((1,), (1,)), ((), ())),
                                          preferred_element_type=F32))
            kd = (k_c * jnp.exp(g_last - g_c)).astype(BF16)
            st = jnp.exp(g_last) * st + lax.dot_general(v_c, kd, (((0,), (0,)), ((), ())),
                                                        preferred_element_type=F32)
        st_sc[h] = st
        o = o_intra + jnp.concatenate(o_rows, axis=0)

        on = o * lax.rsqrt(jnp.mean(o * o, axis=-1, keepdims=True) + EPS) * hn_ref[:, cols]
        og = og_ref[0, :, cols]
        y_ref[0, :, cols] = (on * (og * _sigmoid(og))).astype(y_ref.dtype)


def _hgrn(proj, lb_logits, head_norm):
    b, t, _ = proj.shape
    TB = HGRN_TB
    w = HGRN_W
    return pl.pallas_call(
        _hgrn_kernel,
        grid=(b, t // TB),
        in_specs=[
            pl.BlockSpec((1, TB, w), lambda i, c: (i, c, COL_HQ // w)),
            pl.BlockSpec((1, TB, w), lambda i, c: (i, c, COL_HF // w)),
            pl.BlockSpec((1, TB, w), lambda i, c: (i, c, COL_HI // w)),
            pl.BlockSpec((1, TB, w), lambda i, c: (i, c, COL_HG // w)),
            pl.BlockSpec((2, w), lambda i, c: (0, 0)),
            pl.BlockSpec((1, w), lambda i, c: (0, 0)),
        ],
        out_specs=pl.BlockSpec((1, TB, w), lambda i, c: (i, c, 0)),
        out_shape=jax.ShapeDtypeStruct((b, t, w), BF16),
        scratch_shapes=[
            pltpu.VMEM((TB, w), F32),
            pltpu.VMEM((TB, w), F32),
            pltpu.VMEM((TB, w), F32),
            pltpu.VMEM((HGRN_HEADS, HGRN_DV, HGRN_DK), F32),
        ],
        compiler_params=_params("parallel", "arbitrary"),
        name="hgrn2",
    )(proj, proj, proj, proj, lb_logits, head_norm)


def _merge_kernel(x_ref, ym_ref, yh_ref, gm_ref, gh_ref, wm_ref, wh_ref, wo_ref, o_ref):
    pm = jnp.dot(ym_ref[...], wm_ref[...], preferred_element_type=F32)
    ph = jnp.dot(yh_ref[...], wh_ref[...], preferred_element_type=F32)
    merged = _sigmoid(gm_ref[...]) * pm + _sigmoid(gh_ref[...]) * ph
    o_ref[...] = x_ref[...] + jnp.dot(merged.astype(BF16), wo_ref[...], preferred_element_type=F32)


def _const_spec(shape):
    return pl.BlockSpec(shape, lambda *_: (0,) * len(shape), pipeline_mode=pl.Buffered(1))


def _merge(x2d, ym, yh, proj2d, w_m, w_h, w_o, *, tm=256):
    n, d = x2d.shape
    return pl.pallas_call(
        _merge_kernel,
        grid=(n // tm,),
        in_specs=[
            pl.BlockSpec((tm, d), lambda i: (i, 0)),
            pl.BlockSpec((tm, MLSTM_V_W), lambda i: (i, 0)),
            pl.BlockSpec((tm, HGRN_W), lambda i: (i, 0)),
            pl.BlockSpec((tm, d), lambda i: (i, COL_GM // d)),
            pl.BlockSpec((tm, d), lambda i: (i, COL_GH // d)),
            _const_spec(w_m.shape),
            _const_spec(w_h.shape),
            _const_spec(w_o.shape),
        ],
        out_specs=pl.BlockSpec((tm, d), lambda i: (i, 0)),
        out_shape=jax.ShapeDtypeStruct((n, d), F32),
        compiler_params=_params("parallel"),
        name="merge",
    )(x2d, ym, yh, proj2d, proj2d, w_m, w_h, w_o)


def _mem_kv_kernel(m_ref, g_ref, w_ref, o_ref):
    m = _rms(m_ref[...], g_ref[...]).astype(BF16)
    o_ref[...] = jnp.dot(m, w_ref[...], preferred_element_type=F32).astype(o_ref.dtype)


def _mem_kv(mem2d, g, w_kv, *, tn=1024):
    n, d = mem2d.shape
    nw = w_kv.shape[1]
    return pl.pallas_call(
        _mem_kv_kernel,
        grid=(nw // tn,),
        in_specs=[
            pl.BlockSpec((n, d), lambda j: (0, 0)),
            pl.BlockSpec((1, d), lambda j: (0, 0)),
            pl.BlockSpec((d, tn), lambda j: (0, j)),
        ],
        out_specs=pl.BlockSpec((n, tn), lambda j: (0, j)),
        out_shape=jax.ShapeDtypeStruct((n, nw), BF16),
        compiler_params=_params("parallel"),
        name="mem_kv",
    )(mem2d, g.reshape(1, d), w_kv)


def _xattn_kernel(x_ref, g_ref, wq_ref, k_ref, v_ref, wo_ref, o_ref):
    x = x_ref[0]
    h = _rms(x, g_ref[...]).astype(BF16)
    q = jnp.dot(h, wq_ref[...], preferred_element_type=F32)
    outs = []
    for hd in range(XATTN_HEADS):
        cols = slice(hd * XATTN_HEAD_DIM, (hd + 1) * XATTN_HEAD_DIM)
        s = lax.dot_general(q[:, cols].astype(BF16), k_ref[0, :, cols], (((1,), (1,)), ((), ())),
                            preferred_element_type=F32) * (XATTN_HEAD_DIM ** -0.5)
        s = s - jnp.max(s, axis=-1, keepdims=True)
        e = jnp.exp(s)
        p = e / jnp.sum(e, axis=-1, keepdims=True)
        outs.append(jnp.dot(p.astype(BF16), v_ref[0, :, cols], preferred_element_type=F32).astype(BF16))
    o = jnp.concatenate(outs, axis=-1)
    o_ref[0] = x + jnp.dot(o, wo_ref[...], preferred_element_type=F32)


def _xattn(x, g, w_q, kv, w_o, *, tm=512):
    b, t, d = x.shape
    m = kv.shape[1]
    return pl.pallas_call(
        _xattn_kernel,
        grid=(b, t // tm),
        in_specs=[
            pl.BlockSpec((1, tm, d), lambda i, c: (i, c, 0)),
            _const_spec((1, d)),
            _const_spec(w_q.shape),
            pl.BlockSpec((1, m, d), lambda i, c: (i, 0, 0)),
            pl.BlockSpec((1, m, d), lambda i, c: (i, 0, 1)),
            _const_spec(w_o.shape),
        ],
        out_specs=pl.BlockSpec((1, tm, d), lambda i, c: (i, c, 0)),
        out_shape=jax.ShapeDtypeStruct((b, t, d), F32),
        compiler_params=_params("parallel", "parallel"),
        name="xattn",
    )(x, g.reshape(1, d), w_q, kv, kv, w_o)


def kernel(x, mem, norm_ffn1, ffn1_w1, ffn1_w3, ffn1_w2, norm_mix, w_in, mlstm_conv_w, mlstm_conv_b,
           mlstm_ig_bias, mlstm_fg_bias, mlstm_head_norm, hgrn_lb_logits, hgrn_head_norm, w_proj_m,
           w_proj_h, w_out, norm_xattn, norm_mem, xattn_wq, xattn_wkv, xattn_wo, norm_ffn2, ffn2_w1,
           ffn2_w3, ffn2_w2, norm_final):
    b, t, d = x.shape
    depth = norm_ffn1.shape[0]
    assert depth == 1 and hgrn_lb_logits.shape[0] == 2
    n = b * t
    l = 0
    bf = lambda w: w.astype(BF16)

    n_main = MLSTM_QK_W * 2 + MLSTM_V_W * 2
    n_gate = 2 * MLSTM_HEADS
    w_l = w_in[l]
    n_mix = n_main + n_gate + 4 * HGRN_W
    w_main = bf(jnp.concatenate([w_l[:, n_mix:], w_l[:, :n_main], w_l[:, n_main + n_gate:n_mix]], axis=1))
    w_gates = bf(jnp.pad(w_l[:, n_main:n_main + n_gate], ((0, 0), (0, LANES - n_gate))))
    gate_bias = jnp.concatenate([mlstm_ig_bias[l], mlstm_fg_bias[l]]).astype(F32)
    bias_col = jnp.pad(gate_bias, (0, LANES - n_gate)).reshape(1, LANES)
    bias_row = gate_bias.reshape(n_gate, 1)

    x1 = _ffn(x.reshape(n, d), norm_ffn1[l], bf(ffn1_w1[l]), bf(ffn1_w3[l]), bf(ffn1_w2[l]))

    proj, gates = _in_proj(x1, norm_mix[l], w_main, w_gates)
    proj3 = proj.reshape(b, t, PROJ_W)
    gates3 = gates.reshape(b, t, LANES)
    gates_row = jnp.swapaxes(gates3[:, :, :n_gate], 1, 2)
    y_m = _mlstm(proj3, gates3, gates_row, mlstm_conv_w[l], mlstm_conv_b[l].reshape(1, -1),
                 bias_col, bias_row, mlstm_head_norm[l].reshape(1, -1))
    y_h = _hgrn(proj3, hgrn_lb_logits, hgrn_head_norm[l].reshape(1, -1))

    x2 = _merge(x1, y_m.reshape(n, MLSTM_V_W), y_h.reshape(n, HGRN_W), proj,
                bf(w_proj_m[l]), bf(w_proj_h[l]), bf(w_out[l]))

    m_len = mem.shape[1]
    kv = _mem_kv(mem.reshape(b * m_len, d), norm_mem[l], bf(xattn_wkv[l])).reshape(b, m_len, 2 * d)
    x3 = _xattn(x2.reshape(b, t, d), norm_xattn[l], bf(xattn_wq[l]), kv, bf(xattn_wo[l]))

    out = _ffn(x3.reshape(n, d), norm_ffn2[l], bf(ffn2_w1[l]), bf(ffn2_w3[l]), bf(ffn2_w2[l]),
               g_final=norm_final)
    return out.reshape(b, t, d)
```

```python
import functools

import jax
import jax.numpy as jnp
from jax import lax
from jax.experimental import pallas as pl
from jax.experimental.pallas import tpu as pltpu

F32 = jnp.float32
BF16 = jnp.bfloat16
EPS = 1e-6

D_MODEL = 2048
D_FF = 5632
MLSTM_HEADS = 4
MLSTM_QK = 128
MLSTM_V = 256
CONV_WIDTH = 4
HGRN_HEADS = 8
HGRN_DK = 128
HGRN_DV = 128
XATTN_HEADS = 4
XATTN_HEAD_DIM = D_MODEL // XATTN_HEADS

MLSTM_QK_W = MLSTM_HEADS * MLSTM_QK
MLSTM_V_W = MLSTM_HEADS * MLSTM_V
HGRN_W = HGRN_HEADS * HGRN_DK

VMEM_LIMIT_BYTES = 56 * 1024 * 1024
LANES = 128

COL_GM = 0
COL_GH = COL_GM + D_MODEL
COL_MQ = COL_GH + D_MODEL
COL_MK = COL_MQ + MLSTM_QK_W
COL_MV = COL_MK + MLSTM_QK_W
COL_MO = COL_MV + MLSTM_V_W
COL_HQ = COL_MO + MLSTM_V_W
COL_HF = COL_HQ + HGRN_W
COL_HI = COL_HF + HGRN_W
COL_HG = COL_HI + HGRN_W
PROJ_W = COL_HG + HGRN_W

MLSTM_L = 256
HGRN_TB = 128
HGRN_NUM_LEVELS = HGRN_TB.bit_length() - 1
HGRN_COARSE_LEVELS = tuple(1 << i for i in range(3, HGRN_NUM_LEVELS))


def _rms(x, g):
    return x * lax.rsqrt(jnp.mean(x * x, axis=-1, keepdims=True) + EPS) * g


def _sigmoid(x):
    return 1.0 / (1.0 + jnp.exp(-x))


def _log_sigmoid(x):
    return jnp.minimum(x, 0.0) - jnp.log(1.0 + jnp.exp(-jnp.abs(x)))


def _params(*sem):
    return pltpu.CompilerParams(dimension_semantics=sem, vmem_limit_bytes=VMEM_LIMIT_BYTES)


def _ffn_kernel(*refs, final_norm):
    if final_norm:
        x_ref, g_ref, w1_ref, w3_ref, w2_ref, gf_ref, o_ref, h_sc, acc_sc = refs
    else:
        x_ref, g_ref, w1_ref, w3_ref, w2_ref, o_ref, h_sc, acc_sc = refs
    j = pl.program_id(1)

    @pl.when(j == 0)
    def _():
        h_sc[...] = _rms(x_ref[...], g_ref[...]).astype(BF16)
        acc_sc[...] = jnp.zeros_like(acc_sc)

    h = h_sc[...]
    a = jnp.dot(h, w1_ref[...], preferred_element_type=F32)
    b = jnp.dot(h, w3_ref[...], preferred_element_type=F32)
    act = (a * _sigmoid(a) * b).astype(BF16)
    acc_sc[...] += jnp.dot(act, w2_ref[...], preferred_element_type=F32)

    @pl.when(j == pl.num_programs(1) - 1)
    def _():
        y = x_ref[...] + 0.5 * acc_sc[...]
        if final_norm:
            y = _rms(y, gf_ref[...])
        o_ref[...] = y


def _ffn(x2d, g, w1, w3, w2, g_final=None, *, tm=512, tf=512):
    n, d = x2d.shape
    f = w1.shape[1]
    final_norm = g_final is not None
    in_specs = [
        pl.BlockSpec((tm, d), lambda i, j: (i, 0)),
        pl.BlockSpec((1, d), lambda i, j: (0, 0)),
        pl.BlockSpec((d, tf), lambda i, j: (0, j)),
        pl.BlockSpec((d, tf), lambda i, j: (0, j)),
        pl.BlockSpec((tf, d), lambda i, j: (j, 0)),
    ]
    args = [x2d, g.reshape(1, d), w1, w3, w2]
    if final_norm:
        in_specs.append(pl.BlockSpec((1, d), lambda i, j: (0, 0)))
        args.append(g_final.reshape(1, d))
    return pl.pallas_call(
        functools.partial(_ffn_kernel, final_norm=final_norm),
        grid=(n // tm, f // tf),
        in_specs=in_specs,
        out_specs=pl.BlockSpec((tm, d), lambda i, j: (i, 0)),
        out_shape=jax.ShapeDtypeStruct((n, d), F32),
        scratch_shapes=[pltpu.VMEM((tm, d), BF16), pltpu.VMEM((tm, d), F32)],
        compiler_params=_params("parallel", "arbitrary"),
        name="ffn",
    )(*args)


def _in_proj_kernel(x_ref, g_ref, w_ref, wg_ref, p_ref, gates_ref, h_sc):
    j = pl.program_id(1)

    @pl.when(j == 0)
    def _():
        h = _rms(x_ref[...], g_ref[...]).astype(BF16)
        h_sc[...] = h
        gates_ref[...] = jnp.dot(h, wg_ref[...], preferred_element_type=F32)

    p_ref[...] = jnp.dot(h_sc[...], w_ref[...], preferred_element_type=F32)


def _in_proj(x2d, g, w_main, w_gates, *, tm=1024, tn=1024):
    n, d = x2d.shape
    nw = w_main.shape[1]
    return pl.pallas_call(
        _in_proj_kernel,
        grid=(n // tm, nw // tn),
        in_specs=[
            pl.BlockSpec((tm, d), lambda i, j: (i, 0)),
            pl.BlockSpec((1, d), lambda i, j: (0, 0)),
            pl.BlockSpec((d, tn), lambda i, j: (0, j)),
            pl.BlockSpec((d, LANES), lambda i, j: (0, 0)),
        ],
        out_specs=[
            pl.BlockSpec((tm, tn), lambda i, j: (i, j)),
            pl.BlockSpec((tm, LANES), lambda i, j: (i, 0)),
        ],
        out_shape=[
            jax.ShapeDtypeStruct((n, nw), F32),
            jax.ShapeDtypeStruct((n, LANES), F32),
        ],
        scratch_shapes=[pltpu.VMEM((tm, d), BF16)],
        compiler_params=_params("parallel", "arbitrary"),
        name="in_proj",
    )(x2d, g.reshape(1, d), w_main, w_gates)


def _mlstm_kernel(q_ref, k_ref, v_ref, o_ref, gcol_ref, grow_ref, cw_ref, cb_ref, bcol_ref, brow_ref,
                  hn_ref, y_ref, xpad_sc, c_sc, n_sc, m_sc):
    L = MLSTM_L
    H = MLSTM_HEADS
    c = pl.program_id(1)

    @pl.when(c == 0)
    def _():
        xpad_sc[0:8, :] = jnp.zeros((8, 2 * MLSTM_QK_W), F32)
        c_sc[...] = jnp.zeros_like(c_sc)
        n_sc[...] = jnp.zeros_like(n_sc)
        m_sc[...] = jnp.zeros_like(m_sc)

    xpad_sc[8:8 + L, 0:MLSTM_QK_W] = q_ref[0]
    xpad_sc[8:8 + L, MLSTM_QK_W:2 * MLSTM_QK_W] = k_ref[0]
    acc = jnp.broadcast_to(cb_ref[...], (L, 2 * MLSTM_QK_W))
    for j in range(CONV_WIDTH):
        off = 8 - (CONV_WIDTH - 1) + j
        acc = acc + cw_ref[j:j + 1, :] * xpad_sc[off:off + L, :]
    xpad_sc[0:8, :] = xpad_sc[L:L + 8, :]
    qk = acc * _sigmoid(acc)

    gcol = gcol_ref[0] + bcol_ref[...]
    grow = grow_ref[0] + brow_ref[...]
    fcol = _log_sigmoid(gcol)
    frow = _log_sigmoid(grow)
    r_i = lax.broadcasted_iota(jnp.int32, (L, L), 0)
    c_i = lax.broadcasted_iota(jnp.int32, (L, L), 1)
    causal = r_i >= c_i
    tril = jnp.where(causal, 1.0, 0.0).astype(F32)
    triu = jnp.where(r_i <= c_i, 1.0, 0.0).astype(F32)
    bcol = jnp.dot(tril, fcol, preferred_element_type=F32, precision=lax.Precision.HIGHEST)
    brow = jnp.dot(frow, triu, preferred_element_type=F32, precision=lax.Precision.HIGHEST)

    for h in range(H):
        b_c = bcol[:, H + h:H + h + 1]
        i_c = gcol[:, h:h + 1]
        b_r = brow[H + h:H + h + 1, :]
        i_r = grow[h:h + 1, :]
        m_prev = m_sc[h:h + 1, 0:1]

        d_log = jnp.where(causal, b_c - b_r + i_r, -jnp.inf)
        inter_log = b_c + m_prev
        m_t = jnp.maximum(jnp.max(d_log, axis=1, keepdims=True), inter_log)

        q_h = qk[:, h * MLSTM_QK:(h + 1) * MLSTM_QK]
        k_h = qk[:, MLSTM_QK_W + h * MLSTM_QK:MLSTM_QK_W + (h + 1) * MLSTM_QK] * (MLSTM_QK ** -0.5)
        v_h = v_ref[0, :, h * MLSTM_V:(h + 1) * MLSTM_V].astype(BF16)
        q_b = q_h.astype(BF16)

        s = lax.dot_general(q_b, k_h.astype(BF16), (((1,), (1,)), ((), ())), preferred_element_type=F32)
        s = s * jnp.exp(d_log - m_t)
        w_inter = jnp.exp(inter_log - m_t)
        c_h = c_sc[h]
        n_h = n_sc[h:h + 1, :]
        num = (jnp.dot(s.astype(BF16), v_h, preferred_element_type=F32)
               + w_inter * jnp.dot(q_b, c_h.astype(BF16), preferred_element_type=F32))
        den = jnp.sum(s, axis=1, keepdims=True) + w_inter * jnp.sum(q_h * n_h, axis=1, keepdims=True)
        hh = num / jnp.maximum(jnp.abs(den), jnp.exp(-m_t))

        hn = hh * lax.rsqrt(jnp.mean(hh * hh, axis=-1, keepdims=True) + EPS)
        hn = hn * hn_ref[:, h * MLSTM_V:(h + 1) * MLSTM_V]
        y = hn * _sigmoid(o_ref[0, :, h * MLSTM_V:(h + 1) * MLSTM_V])
        y_ref[0, :, h * MLSTM_V:(h + 1) * MLSTM_V] = y.astype(y_ref.dtype)

        b_last = b_c[L - 1:L, :]
        a_log = b_last - b_c + i_c
        m_new = jnp.maximum(b_last + m_prev, jnp.max(a_log, axis=0, keepdims=True))
        w_a = jnp.exp(a_log - m_new)
        decay = jnp.exp(b_last + m_prev - m_new)
        kw = k_h * w_a
        c_sc[h] = decay * c_h + lax.dot_general(kw.astype(BF16), v_h, (((0,), (0,)), ((), ())),
                                                preferred_element_type=F32)
        n_sc[h:h + 1, :] = decay * n_h + jnp.sum(kw, axis=0, keepdims=True)
        m_sc[h:h + 1, :] = jnp.broadcast_to(m_new, (1, LANES))


def _mlstm(proj, gates_col, gates_row, conv_w, conv_b, bias_col, bias_row, head_norm):
    b, t, _ = proj.shape
    L = MLSTM_L
    qw = MLSTM_QK_W
    vw = MLSTM_V_W
    return pl.pallas_call(
        _mlstm_kernel,
        grid=(b, t // L),
        in_specs=[
            pl.BlockSpec((1, L, qw), lambda i, c: (i, c, COL_MQ // qw)),
            pl.BlockSpec((1, L, qw), lambda i, c: (i, c, COL_MK // qw)),
            pl.BlockSpec((1, L, vw), lambda i, c: (i, c, COL_MV // vw)),
            pl.BlockSpec((1, L, vw), lambda i, c: (i, c, COL_MO // vw)),
            pl.BlockSpec((1, L, LANES), lambda i, c: (i, c, 0)),
            pl.BlockSpec((1, 8, L), lambda i, c: (i, 0, c)),
            pl.BlockSpec((CONV_WIDTH, 2 * qw), lambda i, c: (0, 0)),
            pl.BlockSpec((1, 2 * qw), lambda i, c: (0, 0)),
            pl.BlockSpec((1, LANES), lambda i, c: (0, 0)),
            pl.BlockSpec((8, 1), lambda i, c: (0, 0)),
            pl.BlockSpec((1, vw), lambda i, c: (0, 0)),
        ],
        out_specs=pl.BlockSpec((1, L, vw), lambda i, c: (i, c, 0)),
        out_shape=jax.ShapeDtypeStruct((b, t, vw), BF16),
        scratch_shapes=[
            pltpu.VMEM((L + 8, 2 * qw), F32),
            pltpu.VMEM((MLSTM_HEADS, MLSTM_QK, MLSTM_V), F32),
            pltpu.VMEM((8, MLSTM_QK), F32),
            pltpu.VMEM((8, LANES), F32),
        ],
        compiler_params=_params("parallel", "arbitrary"),
        name="mlstm",
    )(proj, proj, proj, proj, gates_col, gates_row, conv_w, conv_b, bias_col, bias_row, head_norm)


def _hgrn_kernel(q_ref, f_ref, v_ref, og_ref, lbl_ref, hn_ref, y_ref, g_sc, k_sc, q_sc, gr_sc, st_sc):
    TB = HGRN_TB
    H = HGRN_HEADS
    DK = HGRN_DK
    W = H * DK
    NT = TB // 8
    c = pl.program_id(1)

    @pl.when(c == 0)
    def _():
        st_sc[...] = jnp.zeros_like(st_sc)

    lbl = lbl_ref[...]
    lmax = jnp.max(lbl, axis=0, keepdims=True)
    le = jnp.exp(lbl - lmax)
    lb = le[1:2, :] / jnp.sum(le, axis=0, keepdims=True)

    f = lb + (1.0 - lb) * _sigmoid(f_ref[0])
    k_sc[...] = 1.0 - f
    qraw = q_ref[0]
    q_sc[...] = qraw * _sigmoid(qraw) * (DK ** -0.5)
    r_i = lax.broadcasted_iota(jnp.int32, (TB, TB), 0)
    c_i = lax.broadcasted_iota(jnp.int32, (TB, TB), 1)
    tri = jnp.where(r_i >= c_i, 1.0, 0.0).astype(F32)
    g = jnp.dot(tri, jnp.log(f), preferred_element_type=F32, precision=lax.Precision.HIGHEST)
    g_sc[...] = g

    g3 = g.reshape(NT, 8, W)
    sub = lax.broadcasted_iota(jnp.int32, (NT, 8, W), 1)
    bit0 = (sub & 1) != 0
    bit1 = (sub & 2) != 0
    bit2 = (sub & 4) != 0
    last2 = jnp.where(bit0, g3, pltpu.roll(g3, 7, 1))
    last4 = jnp.where(bit1, last2, pltpu.roll(last2, 6, 1))
    last4_r = pltpu.roll(last4, 4, 1)
    last8 = jnp.where(bit2, last4, last4_r).reshape(TB, W)
    gr_sc[0] = jnp.where(bit0, pltpu.roll(g3, 1, 1), g3).reshape(TB, W)
    gr_sc[1] = jnp.where(bit1, pltpu.roll(last2, 2, 1), last2).reshape(TB, W)
    gr_sc[2] = jnp.where(bit2, last4_r, last4).reshape(TB, W)
    for li, lvl in enumerate(HGRN_COARSE_LEVELS):
        groups = []
        for base in range(0, TB, 2 * lvl):
            src = last8[base + lvl - 8:base + lvl, :]
            groups.extend([src] * (2 * lvl // 8))
        gr_sc[3 + li] = jnp.concatenate(groups, axis=0)

    x_i = r_i ^ c_i
    nt_dims = (((1,), (1,)), ((), ()))
    for h in range(H):
        cols = slice(h * DK, (h + 1) * DK)
        q_h = q_sc[:, cols]
        k_h = k_sc[:, cols]
        g_h = g_sc[:, cols]
        v_h = v_ref[0, :, cols].astype(BF16)

        a = lax.dot_general(q_h.astype(BF16), k_h.astype(BF16), nt_dims, preferred_element_type=F32)
        for li in range(HGRN_NUM_LEVELS):
            e = jnp.exp(-jnp.abs(g_h - gr_sc[li, :, cols]))
            p = lax.dot_general((q_h * e).astype(BF16), (k_h * e).astype(BF16), nt_dims,
                                preferred_element_type=F32)
            a = jnp.where(x_i >= (1 << li), p, a)
        a = jnp.where(r_i >= c_i, a, 0.0)
        o = jnp.dot(a.astype(BF16), v_h, preferred_element_type=F32)

        st = st_sc[h]
        g_last = g_h[TB - 1:TB, :]
        qd = (q_h * jnp.exp(g_h)).astype(BF16)
        o = o + lax.dot_general(qd, st.astype(BF16), nt_dims, preferred_element_type=F32)
        kd = (k_h * jnp.exp(g_last - g_h)).astype(BF16)
        st_sc[h] = jnp.exp(g_last) * st + lax.dot_general(v_h, kd, (((0,), (0,)), ((), ())),
                                                          preferred_element_type=F32)

        on = o * lax.rsqrt(jnp.mean(o * o, axis=-1, keepdims=True) + EPS) * hn_ref[:, cols]
        og = og_ref[0, :, cols]
        y_ref[0, :, cols] = (on * (og * _sigmoid(og))).astype(y_ref.dtype)


def _hgrn(proj, lb_logits, head_norm):
    b, t, _ = proj.shape
    TB = HGRN_TB
    w = HGRN_W
    return pl.pallas_call(
        _hgrn_kernel,
        grid=(b, t // TB),
        in_specs=[
            pl.BlockSpec((1, TB, w), lambda i, c: (i, c, COL_HQ // w)),
            pl.BlockSpec((1, TB, w), lambda i, c: (i, c, COL_HF // w)),
            pl.BlockSpec((1, TB, w), lambda i, c: (i, c, COL_HI // w)),
            pl.BlockSpec((1, TB, w), lambda i, c: (i, c, COL_HG // w)),
            pl.BlockSpec((2, w), lambda i, c: (0, 0)),
            pl.BlockSpec((1, w), lambda i, c: (0, 0)),
        ],
        out_specs=pl.BlockSpec((1, TB, w), lambda i, c: (i, c, 0)),
        out_shape=jax.ShapeDtypeStruct((b, t, w), BF16),
        scratch_shapes=[
            pltpu.VMEM((TB, w), F32),
            pltpu.VMEM((TB, w), F32),
            pltpu.VMEM((TB, w), F32),
            pltpu.VMEM((HGRN_NUM_LEVELS, TB, w), F32),
            pltpu.VMEM((HGRN_HEADS, HGRN_DV, HGRN_DK), F32),
        ],
        compiler_params=_params("parallel", "arbitrary"),
        name="hgrn2",
    )(proj, proj, proj, proj, lb_logits, head_norm)


def _merge_kernel(x_ref, ym_ref, yh_ref, gm_ref, gh_ref, wm_ref, wh_ref, wo_ref, o_ref):
    pm = jnp.dot(ym_ref[...], wm_ref[...], preferred_element_type=F32)
    ph = jnp.dot(yh_ref[...], wh_ref[...], preferred_element_type=F32)
    merged = _sigmoid(gm_ref[...]) * pm + _sigmoid(gh_ref[...]) * ph
    o_ref[...] = x_ref[...] + jnp.dot(merged.astype(BF16), wo_ref[...], preferred_element_type=F32)


def _const_spec(shape):
    return pl.BlockSpec(shape, lambda *_: (0,) * len(shape), pipeline_mode=pl.Buffered(1))


def _merge(x2d, ym, yh, proj2d, w_m, w_h, w_o, *, tm=256):
    n, d = x2d.shape
    return pl.pallas_call(
        _merge_kernel,
        grid=(n // tm,),
        in_specs=[
            pl.BlockSpec((tm, d), lambda i: (i, 0)),
            pl.BlockSpec((tm, MLSTM_V_W), lambda i: (i, 0)),
            pl.BlockSpec((tm, HGRN_W), lambda i: (i, 0)),
            pl.BlockSpec((tm, d), lambda i: (i, COL_GM // d)),
            pl.BlockSpec((tm, d), lambda i: (i, COL_GH // d)),
            _const_spec(w_m.shape),
            _const_spec(w_h.shape),
            _const_spec(w_o.shape),
        ],
        out_specs=pl.BlockSpec((tm, d), lambda i: (i, 0)),
        out_shape=jax.ShapeDtypeStruct((n, d), F32),
        compiler_params=_params("parallel"),
        name="merge",
    )(x2d, ym, yh, proj2d, proj2d, w_m, w_h, w_o)


def _mem_kv_kernel(m_ref, g_ref, w_ref, o_ref):
    m = _rms(m_ref[...], g_ref[...]).astype(BF16)
    o_ref[...] = jnp.dot(m, w_ref[...], preferred_element_type=F32).astype(o_ref.dtype)


def _mem_kv(mem2d, g, w_kv, *, tn=1024):
    n, d = mem2d.shape
    nw = w_kv.shape[1]
    return pl.pallas_call(
        _mem_kv_kernel,
        grid=(nw // tn,),
        in_specs=[
            pl.BlockSpec((n, d), lambda j: (0, 0)),
            pl.BlockSpec((1, d), lambda j: (0, 0)),
            pl.BlockSpec((d, tn), lambda j: (0, j)),
        ],
        out_specs=pl.BlockSpec((n, tn), lambda j: (0, j)),
        out_shape=jax.ShapeDtypeStruct((n, nw), BF16),
        compiler_params=_params("parallel"),
        name="mem_kv",
    )(mem2d, g.reshape(1, d), w_kv)


def _xattn_kernel(x_ref, g_ref, wq_ref, k_ref, v_ref, wo_ref, o_ref):
    x = x_ref[0]
    h = _rms(x, g_ref[...]).astype(BF16)
    q = jnp.dot(h, wq_ref[...], preferred_element_type=F32)
    outs = []
    for hd in range(XATTN_HEADS):
        cols = slice(hd * XATTN_HEAD_DIM, (hd + 1) * XATTN_HEAD_DIM)
        s = lax.dot_general(q[:, cols].astype(BF16), k_ref[0, :, cols], (((1,), (1,)), ((), ())),
                            preferred_element_type=F32) * (XATTN_HEAD_DIM ** -0.5)
        s = s - jnp.max(s, axis=-1, keepdims=True)
        e = jnp.exp(s)
        p = e / jnp.sum(e, axis=-1, keepdims=True)
        outs.append(jnp.dot(p.astype(BF16), v_ref[0, :, cols], preferred_element_type=F32).astype(BF16))
    o = jnp.concatenate(outs, axis=-1)
    o_ref[0] = x + jnp.dot(o, wo_ref[...], preferred_element_type=F32)


def _xattn(x, g, w_q, kv, w_o, *, tm=512):
    b, t, d = x.shape
    m = kv.shape[1]
    return pl.pallas_call(
        _xattn_kernel,
        grid=(b, t // tm),
        in_specs=[
            pl.BlockSpec((1, tm, d), lambda i, c: (i, c, 0)),
            _const_spec((1, d)),
            _const_spec(w_q.shape),
            pl.BlockSpec((1, m, d), lambda i, c: (i, 0, 0)),
            pl.BlockSpec((1, m, d), lambda i, c: (i, 0, 1)),
            _const_spec(w_o.shape),
        ],
        out_specs=pl.BlockSpec((1, tm, d), lambda i, c: (i, c, 0)),
        out_shape=jax.ShapeDtypeStruct((b, t, d), F32),
        compiler_params=_params("parallel", "parallel"),
        name="xattn",
    )(x, g.reshape(1, d), w_q, kv, kv, w_o)


def kernel(x, mem, norm_ffn1, ffn1_w1, ffn1_w3, ffn1_w2, norm_mix, w_in, mlstm_conv_w, mlstm_conv_b,
           mlstm_ig_bias, mlstm_fg_bias, mlstm_head_norm, hgrn_lb_logits, hgrn_head_norm, w_proj_m,
           w_proj_h, w_out, norm_xattn, norm_mem, xattn_wq, xattn_wkv, xattn_wo, norm_ffn2, ffn2_w1,
           ffn2_w3, ffn2_w2, norm_final):
    b, t, d = x.shape
    depth = norm_ffn1.shape[0]
    assert depth == 1 and hgrn_lb_logits.shape[0] == 2
    n = b * t
    l = 0
    bf = lambda w: w.astype(BF16)

    n_main = MLSTM_QK_W * 2 + MLSTM_V_W * 2
    n_gate = 2 * MLSTM_HEADS
    w_l = w_in[l]
    n_mix = n_main + n_gate + 4 * HGRN_W
    w_main = bf(jnp.concatenate([w_l[:, n_mix:], w_l[:, :n_main], w_l[:, n_main + n_gate:n_mix]], axis=1))
    w_gates = bf(jnp.pad(w_l[:, n_main:n_main + n_gate], ((0, 0), (0, LANES - n_gate))))
    gate_bias = jnp.concatenate([mlstm_ig_bias[l], mlstm_fg_bias[l]]).astype(F32)
    bias_col = jnp.pad(gate_bias, (0, LANES - n_gate)).reshape(1, LANES)
    bias_row = gate_bias.reshape(n_gate, 1)

    x1 = _ffn(x.reshape(n, d), norm_ffn1[l], bf(ffn1_w1[l]), bf(ffn1_w3[l]), bf(ffn1_w2[l]))

    proj, gates = _in_proj(x1, norm_mix[l], w_main, w_gates)
    proj3 = proj.reshape(b, t, PROJ_W)
    gates3 = gates.reshape(b, t, LANES)
    gates_row = jnp.swapaxes(gates3[:, :, :n_gate], 1, 2)
    y_m = _mlstm(proj3, gates3, gates_row, mlstm_conv_w[l], mlstm_conv_b[l].reshape(1, -1),
                 bias_col, bias_row, mlstm_head_norm[l].reshape(1, -1))
    y_h = _hgrn(proj3, hgrn_lb_logits, hgrn_head_norm[l].reshape(1, -1))

    x2 = _merge(x1, y_m.reshape(n, MLSTM_V_W), y_h.reshape(n, HGRN_W), proj,
                bf(w_proj_m[l]), bf(w_proj_h[l]), bf(w_out[l]))

    m_len = mem.shape[1]
    kv = _mem_kv(mem.reshape(b * m_len, d), norm_mem[l], bf(xattn_wkv[l])).reshape(b, m_len, 2 * d)
    x3 = _xattn(x2.reshape(b, t, d), norm_xattn[l], bf(xattn_wq[l]), kv, bf(xattn_wo[l]))

    out = _ffn(x3.reshape(n, d), norm_ffn2[l], bf(ffn2_w1[l]), bf(ffn2_w3[l]), bf(ffn2_w2[l]),
               g_final=norm_final)
    return out.reshape(b, t, d)
```

```python
import functools

import jax
import jax.numpy as jnp
from jax import lax
from jax.experimental import pallas as pl
from jax.experimental.pallas import tpu as pltpu

F32 = jnp.float32
BF16 = jnp.bfloat16
EPS = 1e-6

D_MODEL = 2048
D_FF = 5632
MLSTM_HEADS = 4
MLSTM_QK = 128
MLSTM_V = 256
CONV_WIDTH = 4
HGRN_HEADS = 8
HGRN_DK = 128
HGRN_DV = 128
XATTN_HEADS = 4
XATTN_HEAD_DIM = D_MODEL // XATTN_HEADS

MLSTM_QK_W = MLSTM_HEADS * MLSTM_QK
MLSTM_V_W = MLSTM_HEADS * MLSTM_V
HGRN_W = HGRN_HEADS * HGRN_DK

VMEM_LIMIT_BYTES = 56 * 1024 * 1024
LANES = 128

COL_GM = 0
COL_GH = COL_GM + D_MODEL
COL_MQ = COL_GH + D_MODEL
COL_MK = COL_MQ + MLSTM_QK_W
COL_MV = COL_MK + MLSTM_QK_W
COL_MO = COL_MV + MLSTM_V_W
COL_HQ = COL_MO + MLSTM_V_W
COL_HF = COL_HQ + HGRN_W
COL_HI = COL_HF + HGRN_W
COL_HG = COL_HI + HGRN_W
PROJ_W = COL_HG + HGRN_W

MIX_GATE_ROW = 2 * MLSTM_QK_W + 2 * MLSTM_V_W
MIX_HGRN_ROW = MIX_GATE_ROW + 2 * MLSTM_HEADS
MIX_MERGE_ROW = MIX_HGRN_ROW + 4 * HGRN_W

MLSTM_L = 256
HGRN_TB = 128
HGRN_NUM_LEVELS = HGRN_TB.bit_length() - 1
HGRN_COARSE_LEVELS = tuple(1 << i for i in range(3, HGRN_NUM_LEVELS))


def _rms(x, g):
    return x * lax.rsqrt(jnp.mean(x * x, axis=-1, keepdims=True) + EPS) * g


def _sigmoid(x):
    return 1.0 / (1.0 + jnp.exp(-x))


def _log_sigmoid(x):
    return jnp.minimum(x, 0.0) - jnp.log(1.0 + jnp.exp(-jnp.abs(x)))


def _params(*sem):
    return pltpu.CompilerParams(dimension_semantics=sem, vmem_limit_bytes=VMEM_LIMIT_BYTES)


def _ffn_kernel(*refs, mode):
    if mode == "mix":
        (x_ref, g_ref, w1_ref, w3_ref, w2_ref, gn_ref, wg_ref,
         o_ref, hn_ref, gcol_ref, grow_ref, h_sc) = refs
    else:
        x_ref, g_ref, w1_ref, w3_ref, w2_ref, gn_ref, o_ref, h_sc = refs
    j = pl.program_id(1)

    @pl.when(j == 0)
    def _():
        h_sc[...] = _rms(x_ref[...], g_ref[...]).astype(BF16)
        o_ref[...] = jnp.zeros_like(o_ref)

    h = h_sc[...]
    a = jnp.dot(h, w1_ref[...].astype(BF16), preferred_element_type=F32)
    b = jnp.dot(h, w3_ref[...].astype(BF16), preferred_element_type=F32)
    act = (a * _sigmoid(a) * b).astype(BF16)
    o_ref[...] += jnp.dot(act, w2_ref[...].astype(BF16), preferred_element_type=F32)

    @pl.when(j == pl.num_programs(1) - 1)
    def _():
        y = x_ref[...] + 0.5 * o_ref[...]
        if mode == "final":
            o_ref[...] = _rms(y, gn_ref[...])
        else:
            o_ref[...] = y
            hn = _rms(y, gn_ref[...]).astype(BF16)
            hn_ref[...] = hn
            nt_dims = (((1,), (1,)), ((), ()))
            wg = wg_ref[...].astype(BF16)
            wg_pad = jnp.concatenate([wg, jnp.zeros((LANES - wg.shape[0], wg.shape[1]), BF16)], axis=0)
            gcol_ref[...] = lax.dot_general(hn, wg_pad, nt_dims, preferred_element_type=F32)
            grow_ref[...] = lax.dot_general(wg, hn, nt_dims, preferred_element_type=F32)


def _ffn(x2d, g, w1, w3, w2, g_next, w_gates_t=None, *, tm=1024, tf=256):
    n, d = x2d.shape
    f = w1.shape[1]
    mode = "final" if w_gates_t is None else "mix"
    in_specs = [
        pl.BlockSpec((tm, d), lambda i, j: (i, 0), pipeline_mode=pl.Buffered(1)),
        pl.BlockSpec((1, d), lambda i, j: (0, 0)),
        pl.BlockSpec((d, tf), lambda i, j: (0, j)),
        pl.BlockSpec((d, tf), lambda i, j: (0, j)),
        pl.BlockSpec((tf, d), lambda i, j: (j, 0)),
        pl.BlockSpec((1, d), lambda i, j: (0, 0)),
    ]
    args = [x2d, g.reshape(1, d), w1, w3, w2, g_next.reshape(1, d)]
    out_specs = [pl.BlockSpec((tm, d), lambda i, j: (i, 0))]
    out_shape = [jax.ShapeDtypeStruct((n, d), F32)]
    if mode == "mix":
        ng = 2 * MLSTM_HEADS
        in_specs.append(pl.BlockSpec((ng, d), lambda i, j: (MIX_GATE_ROW // ng, 0)))
        args.append(w_gates_t)
        out_specs += [
            pl.BlockSpec((tm, d), lambda i, j: (i, 0)),
            pl.BlockSpec((tm, LANES), lambda i, j: (i, 0)),
            pl.BlockSpec((ng, tm), lambda i, j: (0, i)),
        ]
        out_shape += [
            jax.ShapeDtypeStruct((n, d), BF16),
            jax.ShapeDtypeStruct((n, LANES), F32),
            jax.ShapeDtypeStruct((ng, n), F32),
        ]
    return pl.pallas_call(
        functools.partial(_ffn_kernel, mode=mode),
        grid=(n // tm, f // tf),
        in_specs=in_specs,
        out_specs=out_specs,
        out_shape=out_shape,
        scratch_shapes=[pltpu.VMEM((tm, d), BF16)],
        compiler_params=_params("parallel", "arbitrary"),
        name="ffn_" + mode,
    )(*args)


def _in_proj_kernel(h_ref, wt_ref, p_ref, w_sc):
    @pl.when(pl.program_id(1) == 0)
    def _():
        w_sc[...] = wt_ref[...].astype(BF16)

    p_ref[...] = lax.dot_general(h_ref[...], w_sc[...], (((1,), (1,)), ((), ())), preferred_element_type=F32)


def _proj_row_offset(j, tn):
    n_gate_tiles = 2 * D_MODEL // tn
    n_mlstm_tiles = MIX_GATE_ROW // tn
    off8 = jnp.where(j < n_gate_tiles, MIX_MERGE_ROW // 8 + j * (tn // 8),
                     jnp.where(j < n_gate_tiles + n_mlstm_tiles, (j - n_gate_tiles) * (tn // 8),
                               MIX_HGRN_ROW // 8 + (j - n_gate_tiles - n_mlstm_tiles) * (tn // 8)))
    return off8 * 8


def _in_proj(hn, w_t, *, tm=1024, tn=1024):
    n, d = hn.shape
    return pl.pallas_call(
        _in_proj_kernel,
        grid=(PROJ_W // tn, n // tm),
        in_specs=[
            pl.BlockSpec((tm, d), lambda j, i: (i, 0)),
            pl.BlockSpec((pl.Element(tn), pl.Element(d)), lambda j, i: (_proj_row_offset(j, tn), 0)),
        ],
        out_specs=pl.BlockSpec((tm, tn), lambda j, i: (i, j)),
        out_shape=jax.ShapeDtypeStruct((n, PROJ_W), F32),
        scratch_shapes=[pltpu.VMEM((tn, d), BF16)],
        compiler_params=_params("arbitrary", "arbitrary"),
        name="in_proj",
    )(hn, w_t)


def _mlstm_kernel(q_ref, k_ref, v_ref, o_ref, gcol_ref, grow_ref, cw_ref, cb_ref, bcol_ref, brow_ref,
                  hn_ref, y_ref, xpad_sc, c_sc, n_sc, m_sc):
    L = MLSTM_L
    H = MLSTM_HEADS
    c = pl.program_id(1)

    @pl.when(c == 0)
    def _():
        xpad_sc[0:8, :] = jnp.zeros((8, 2 * MLSTM_QK_W), F32)
        c_sc[...] = jnp.zeros_like(c_sc)
        n_sc[...] = jnp.zeros_like(n_sc)
        m_sc[...] = jnp.zeros_like(m_sc)

    xpad_sc[8:8 + L, 0:MLSTM_QK_W] = q_ref[0]
    xpad_sc[8:8 + L, MLSTM_QK_W:2 * MLSTM_QK_W] = k_ref[0]
    acc = jnp.broadcast_to(cb_ref[...], (L, 2 * MLSTM_QK_W))
    for j in range(CONV_WIDTH):
        off = 8 - (CONV_WIDTH - 1) + j
        acc = acc + cw_ref[j:j + 1, :] * xpad_sc[off:off + L, :]
    xpad_sc[0:8, :] = xpad_sc[L:L + 8, :]
    qk = acc * _sigmoid(acc)

    gcol = gcol_ref[0] + bcol_ref[...]
    grow = grow_ref[...] + brow_ref[...]
    fcol = _log_sigmoid(gcol)
    frow = _log_sigmoid(grow)
    r_i = lax.broadcasted_iota(jnp.int32, (L, L), 0)
    c_i = lax.broadcasted_iota(jnp.int32, (L, L), 1)
    causal = r_i >= c_i
    tril = jnp.where(causal, 1.0, 0.0).astype(F32)
    triu = jnp.where(r_i <= c_i, 1.0, 0.0).astype(F32)
    bcol = jnp.dot(tril, fcol, preferred_element_type=F32, precision=lax.Precision.HIGHEST)
    brow = jnp.dot(frow, triu, preferred_element_type=F32, precision=lax.Precision.HIGHEST)

    for h in range(H):
        b_c = bcol[:, H + h:H + h + 1]
        i_c = gcol[:, h:h + 1]
        b_r = brow[H + h:H + h + 1, :]
        i_r = grow[h:h + 1, :]
        m_prev = m_sc[h:h + 1, 0:1]

        d_log = jnp.where(causal, b_c - b_r + i_r, -jnp.inf)
        inter_log = b_c + m_prev
        m_t = jnp.maximum(jnp.max(d_log, axis=1, keepdims=True), inter_log)

        q_h = qk[:, h * MLSTM_QK:(h + 1) * MLSTM_QK]
        k_h = qk[:, MLSTM_QK_W + h * MLSTM_QK:MLSTM_QK_W + (h + 1) * MLSTM_QK] * (MLSTM_QK ** -0.5)
        v_h = v_ref[0, :, h * MLSTM_V:(h + 1) * MLSTM_V].astype(BF16)
        q_b = q_h.astype(BF16)

        s = lax.dot_general(q_b, k_h.astype(BF16), (((1,), (1,)), ((), ())), preferred_element_type=F32)
        s = s * jnp.exp(d_log - m_t)
        w_inter = jnp.exp(inter_log - m_t)
        c_h = c_sc[h]
        n_h = n_sc[h:h + 1, :]
        num = (jnp.dot(s.astype(BF16), v_h, preferred_element_type=F32)
               + w_inter * jnp.dot(q_b, c_h.astype(BF16), preferred_element_type=F32))
        den = jnp.sum(s, axis=1, keepdims=True) + w_inter * jnp.sum(q_h * n_h, axis=1, keepdims=True)
        hh = num / jnp.maximum(jnp.abs(den), jnp.exp(-m_t))

        hn = hh * lax.rsqrt(jnp.mean(hh * hh, axis=-1, keepdims=True) + EPS)
        hn = hn * hn_ref[:, h * MLSTM_V:(h + 1) * MLSTM_V]
        y = hn * _sigmoid(o_ref[0, :, h * MLSTM_V:(h + 1) * MLSTM_V])
        y_ref[0, :, h * MLSTM_V:(h + 1) * MLSTM_V] = y.astype(y_ref.dtype)

        b_last = b_c[L - 1:L, :]
        a_log = b_last - b_c + i_c
        m_new = jnp.maximum(b_last + m_prev, jnp.max(a_log, axis=0, keepdims=True))
        w_a = jnp.exp(a_log - m_new)
        decay = jnp.exp(b_last + m_prev - m_new)
        kw = k_h * w_a
        c_sc[h] = decay * c_h + lax.dot_general(kw.astype(BF16), v_h, (((0,), (0,)), ((), ())),
                                                preferred_element_type=F32)
        n_sc[h:h + 1, :] = decay * n_h + jnp.sum(kw, axis=0, keepdims=True)
        m_sc[h:h + 1, :] = jnp.broadcast_to(m_new, (1, LANES))


def _mlstm(proj, gates_col, gates_row, conv_w, conv_b, bias_col, bias_row, head_norm):
    b, t, _ = proj.shape
    L = MLSTM_L
    qw = MLSTM_QK_W
    vw = MLSTM_V_W
    return pl.pallas_call(
        _mlstm_kernel,
        grid=(b, t // L),
        in_specs=[
            pl.BlockSpec((1, L, qw), lambda i, c: (i, c, COL_MQ // qw)),
            pl.BlockSpec((1, L, qw), lambda i, c: (i, c, COL_MK // qw)),
            pl.BlockSpec((1, L, vw), lambda i, c: (i, c, COL_MV // vw)),
            pl.BlockSpec((1, L, vw), lambda i, c: (i, c, COL_MO // vw)),
            pl.BlockSpec((1, L, LANES), lambda i, c: (i, c, 0)),
            pl.BlockSpec((2 * MLSTM_HEADS, L), lambda i, c: (0, i * (t // L) + c)),
            pl.BlockSpec((CONV_WIDTH, 2 * qw), lambda i, c: (0, 0)),
            pl.BlockSpec((1, 2 * qw), lambda i, c: (0, 0)),
            pl.BlockSpec((1, LANES), lambda i, c: (0, 0)),
            pl.BlockSpec((8, 1), lambda i, c: (0, 0)),
            pl.BlockSpec((1, vw), lambda i, c: (0, 0)),
        ],
        out_specs=pl.BlockSpec((1, L, vw), lambda i, c: (i, c, 0)),
        out_shape=jax.ShapeDtypeStruct((b, t, vw), BF16),
        scratch_shapes=[
            pltpu.VMEM((L + 8, 2 * qw), F32),
            pltpu.VMEM((MLSTM_HEADS, MLSTM_QK, MLSTM_V), F32),
            pltpu.VMEM((8, MLSTM_QK), F32),
            pltpu.VMEM((8, LANES), F32),
        ],
        compiler_params=_params("parallel", "arbitrary"),
        name="mlstm",
    )(proj, proj, proj, proj, gates_col, gates_row, conv_w, conv_b, bias_col, bias_row, head_norm)


def _hgrn_kernel(q_ref, f_ref, v_ref, og_ref, lbl_ref, hn_ref, y_ref, g_sc, k_sc, q_sc, gr_sc, st_sc):
    TB = HGRN_TB
    H = HGRN_HEADS
    DK = HGRN_DK
    W = H * DK
    NT = TB // 8
    c = pl.program_id(1)

    @pl.when(c == 0)
    def _():
        st_sc[...] = jnp.zeros_like(st_sc)

    lbl = lbl_ref[...]
    lmax = jnp.max(lbl, axis=0, keepdims=True)
    le = jnp.exp(lbl - lmax)
    lb = le[1:2, :] / jnp.sum(le, axis=0, keepdims=True)

    f = lb + (1.0 - lb) * _sigmoid(f_ref[0])
    k_sc[...] = 1.0 - f
    qraw = q_ref[0]
    q_sc[...] = qraw * _sigmoid(qraw) * (DK ** -0.5)
    r_i = lax.broadcasted_iota(jnp.int32, (TB, TB), 0)
    c_i = lax.broadcasted_iota(jnp.int32, (TB, TB), 1)
    tri = jnp.where(r_i >= c_i, 1.0, 0.0).astype(F32)
    g = jnp.dot(tri, jnp.log(f), preferred_element_type=F32, precision=lax.Precision.HIGHEST)
    g_sc[...] = g

    g3 = g.reshape(NT, 8, W)
    sub = lax.broadcasted_iota(jnp.int32, (NT, 8, W), 1)
    bit0 = (sub & 1) != 0
    bit1 = (sub & 2) != 0
    bit2 = (sub & 4) != 0
    last2 = jnp.where(bit0, g3, pltpu.roll(g3, 7, 1))
    last4 = jnp.where(bit1, last2, pltpu.roll(last2, 6, 1))
    last4_r = pltpu.roll(last4, 4, 1)
    last8 = jnp.where(bit2, last4, last4_r).reshape(TB, W)
    gr_sc[0] = jnp.where(bit0, pltpu.roll(g3, 1, 1), g3).reshape(TB, W)
    gr_sc[1] = jnp.where(bit1, pltpu.roll(last2, 2, 1), last2).reshape(TB, W)
    gr_sc[2] = jnp.where(bit2, last4_r, last4).reshape(TB, W)
    for li, lvl in enumerate(HGRN_COARSE_LEVELS):
        groups = []
        for base in range(0, TB, 2 * lvl):
            src = last8[base + lvl - 8:base + lvl, :]
            groups.extend([src] * (2 * lvl // 8))
        gr_sc[3 + li] = jnp.concatenate(groups, axis=0)

    x_i = r_i ^ c_i
    nt_dims = (((1,), (1,)), ((), ()))
    for h in range(H):
        cols = slice(h * DK, (h + 1) * DK)
        q_h = q_sc[:, cols]
        k_h = k_sc[:, cols]
        g_h = g_sc[:, cols]
        v_h = v_ref[0, :, cols].astype(BF16)

        a = lax.dot_general(q_h.astype(BF16), k_h.astype(BF16), nt_dims, preferred_element_type=F32)
        for li in range(HGRN_NUM_LEVELS):
            e = jnp.exp(-jnp.abs(g_h - gr_sc[li, :, cols]))
            p = lax.dot_general((q_h * e).astype(BF16), (k_h * e).astype(BF16), nt_dims,
                                preferred_element_type=F32)
            a = jnp.where(x_i >= (1 << li), p, a)
        a = jnp.where(r_i >= c_i, a, 0.0)
        o = jnp.dot(a.astype(BF16), v_h, preferred_element_type=F32)

        st = st_sc[h]
        g_last = g_h[TB - 1:TB, :]
        qd = (q_h * jnp.exp(g_h)).astype(BF16)
        o = o + lax.dot_general(qd, st.astype(BF16), nt_dims, preferred_element_type=F32)
        kd = (k_h * jnp.exp(g_last - g_h)).astype(BF16)
        st_sc[h] = jnp.exp(g_last) * st + lax.dot_general(v_h, kd, (((0,), (0,)), ((), ())),
                                                          preferred_element_type=F32)

        on = o * lax.rsqrt(jnp.mean(o * o, axis=-1, keepdims=True) + EPS) * hn_ref[:, cols]
        og = og_ref[0, :, cols]
        y_ref[0, :, cols] = (on * (og * _sigmoid(og))).astype(y_ref.dtype)


def _hgrn(proj, lb_logits, head_norm):
    b, t, _ = proj.shape
    TB = HGRN_TB
    w = HGRN_W
    return pl.pallas_call(
        _hgrn_kernel,
        grid=(b, t // TB),
        in_specs=[
            pl.BlockSpec((1, TB, w), lambda i, c: (i, c, COL_HQ // w)),
            pl.BlockSpec((1, TB, w), lambda i, c: (i, c, COL_HF // w)),
            pl.BlockSpec((1, TB, w), lambda i, c: (i, c, COL_HI // w)),
            pl.BlockSpec((1, TB, w), lambda i, c: (i, c, COL_HG // w)),
            pl.BlockSpec((2, w), lambda i, c: (0, 0)),
            pl.BlockSpec((1, w), lambda i, c: (0, 0)),
        ],
        out_specs=pl.BlockSpec((1, TB, w), lambda i, c: (i, c, 0)),
        out_shape=jax.ShapeDtypeStruct((b, t, w), BF16),
        scratch_shapes=[
            pltpu.VMEM((TB, w), F32),
            pltpu.VMEM((TB, w), F32),
            pltpu.VMEM((TB, w), F32),
            pltpu.VMEM((HGRN_NUM_LEVELS, TB, w), F32),
            pltpu.VMEM((HGRN_HEADS, HGRN_DV, HGRN_DK), F32),
        ],
        compiler_params=_params("parallel", "arbitrary"),
        name="hgrn2",
    )(proj, proj, proj, proj, lb_logits, head_norm)


def _merge_kernel(x_ref, ym_ref, yh_ref, gm_ref, gh_ref, wm_ref, wh_ref, wo_ref, o_ref):
    pm = jnp.dot(ym_ref[...], wm_ref[...], preferred_element_type=F32)
    ph = jnp.dot(yh_ref[...], wh_ref[...], preferred_element_type=F32)
    merged = _sigmoid(gm_ref[...]) * pm + _sigmoid(gh_ref[...]) * ph
    o_ref[...] = x_ref[...] + jnp.dot(merged.astype(BF16), wo_ref[...], preferred_element_type=F32)


def _const_spec(shape):
    return pl.BlockSpec(shape, lambda *_: (0,) * len(shape), pipeline_mode=pl.Buffered(1))


def _merge(x2d, ym, yh, proj2d, w_m, w_h, w_o, *, tm=256):
    n, d = x2d.shape
    return pl.pallas_call(
        _merge_kernel,
        grid=(n // tm,),
        in_specs=[
            pl.BlockSpec((tm, d), lambda i: (i, 0)),
            pl.BlockSpec((tm, MLSTM_V_W), lambda i: (i, 0)),
            pl.BlockSpec((tm, HGRN_W), lambda i: (i, 0)),
            pl.BlockSpec((tm, d), lambda i: (i, COL_GM // d)),
            pl.BlockSpec((tm, d), lambda i: (i, COL_GH // d)),
            _const_spec(w_m.shape),
            _const_spec(w_h.shape),
            _const_spec(w_o.shape),
        ],
        out_specs=pl.BlockSpec((tm, d), lambda i: (i, 0)),
        out_shape=jax.ShapeDtypeStruct((n, d), F32),
        compiler_params=_params("parallel"),
        name="merge",
    )(x2d, ym, yh, proj2d, proj2d, w_m, w_h, w_o)


def _mem_kv_kernel(m_ref, g_ref, w_ref, o_ref):
    m = _rms(m_ref[...], g_ref[...]).astype(BF16)
    o_ref[...] = jnp.dot(m, w_ref[...], preferred_element_type=F32).astype(o_ref.dtype)


def _mem_kv(mem2d, g, w_kv, *, tn=1024):
    n, d = mem2d.shape
    nw = w_kv.shape[1]
    return pl.pallas_call(
        _mem_kv_kernel,
        grid=(nw // tn,),
        in_specs=[
            pl.BlockSpec((n, d), lambda j: (0, 0)),
            pl.BlockSpec((1, d), lambda j: (0, 0)),
            pl.BlockSpec((d, tn), lambda j: (0, j)),
        ],
        out_specs=pl.BlockSpec((n, tn), lambda j: (0, j)),
        out_shape=jax.ShapeDtypeStruct((n, nw), BF16),
        compiler_params=_params("parallel"),
        name="mem_kv",
    )(mem2d, g.reshape(1, d), w_kv)


def _xattn_kernel(x_ref, g_ref, wq_ref, k_ref, v_ref, wo_ref, o_ref):
    x = x_ref[0]
    h = _rms(x, g_ref[...]).astype(BF16)
    q = jnp.dot(h, wq_ref[...], preferred_element_type=F32)
    outs = []
    for hd in range(XATTN_HEADS):
        cols = slice(hd * XATTN_HEAD_DIM, (hd + 1) * XATTN_HEAD_DIM)
        s = lax.dot_general(q[:, cols].astype(BF16), k_ref[0, :, cols], (((1,), (1,)), ((), ())),
                            preferred_element_type=F32) * (XATTN_HEAD_DIM ** -0.5)
        s = s - jnp.max(s, axis=-1, keepdims=True)
        e = jnp.exp(s)
        p = e / jnp.sum(e, axis=-1, keepdims=True)
        outs.append(jnp.dot(p.astype(BF16), v_ref[0, :, cols], preferred_element_type=F32).astype(BF16))
    o = jnp.concatenate(outs, axis=-1)
    o_ref[0] = x + jnp.dot(o, wo_ref[...], preferred_element_type=F32)


def _xattn(x, g, w_q, kv, w_o, *, tm=512):
    b, t, d = x.shape
    m = kv.shape[1]
    return pl.pallas_call(
        _xattn_kernel,
        grid=(b, t // tm),
        in_specs=[
            pl.BlockSpec((1, tm, d), lambda i, c: (i, c, 0)),
            _const_spec((1, d)),
            _const_spec(w_q.shape),
            pl.BlockSpec((1, m, d), lambda i, c: (i, 0, 0)),
            pl.BlockSpec((1, m, d), lambda i, c: (i, 0, 1)),
            _const_spec(w_o.shape),
        ],
        out_specs=pl.BlockSpec((1, tm, d), lambda i, c: (i, c, 0)),
        out_shape=jax.ShapeDtypeStruct((b, t, d), F32),
        compiler_params=_params("parallel", "parallel"),
        name="xattn",
    )(x, g.reshape(1, d), w_q, kv, kv, w_o)


def kernel(x, mem, norm_ffn1, ffn1_w1, ffn1_w3, ffn1_w2, norm_mix, w_in, mlstm_conv_w, mlstm_conv_b,
           mlstm_ig_bias, mlstm_fg_bias, mlstm_head_norm, hgrn_lb_logits, hgrn_head_norm, w_proj_m,
           w_proj_h, w_out, norm_xattn, norm_mem, xattn_wq, xattn_wkv, xattn_wo, norm_ffn2, ffn2_w1,
           ffn2_w3, ffn2_w2, norm_final):
    b, t, d = x.shape
    depth = norm_ffn1.shape[0]
    assert depth == 1 and hgrn_lb_logits.shape[0] == 2
    n = b * t
    l = 0
    bf = lambda w: w.astype(BF16)

    w_in_t = jnp.swapaxes(w_in[l], 0, 1)
    n_gate = 2 * MLSTM_HEADS
    gate_bias = jnp.concatenate([mlstm_ig_bias[l], mlstm_fg_bias[l]]).astype(F32)
    bias_col = jnp.pad(gate_bias, (0, LANES - n_gate)).reshape(1, LANES)
    bias_row = gate_bias.reshape(n_gate, 1)

    x1, hn, gates_col, gates_row = _ffn(x.reshape(n, d), norm_ffn1[l], ffn1_w1[l], ffn1_w3[l], ffn1_w2[l],
                                        norm_mix[l], w_in_t)

    proj = _in_proj(hn, w_in_t)
    proj3 = proj.reshape(b, t, PROJ_W)
    y_m = _mlstm(proj3, gates_col.reshape(b, t, LANES), gates_row, mlstm_conv_w[l],
                 mlstm_conv_b[l].reshape(1, -1), bias_col, bias_row, mlstm_head_norm[l].reshape(1, -1))
    y_h = _hgrn(proj3, hgrn_lb_logits, hgrn_head_norm[l].reshape(1, -1))

    x2 = _merge(x1, y_m.reshape(n, MLSTM_V_W), y_h.reshape(n, HGRN_W), proj,
                bf(w_proj_m[l]), bf(w_proj_h[l]), bf(w_out[l]))

    m_len = mem.shape[1]
    kv = _mem_kv(mem.reshape(b * m_len, d), norm_mem[l], bf(xattn_wkv[l])).reshape(b, m_len, 2 * d)
    x3 = _xattn(x2.reshape(b, t, d), norm_xattn[l], bf(xattn_wq[l]), kv, bf(xattn_wo[l]))

    (out,) = _ffn(x3.reshape(n, d), norm_ffn2[l], ffn2_w1[l], ffn2_w3[l], ffn2_w2[l], norm_final)
    return out.reshape(b, t, d)
```

```python
import functools

import jax
import jax.numpy as jnp
from jax import lax
from jax.experimental import pallas as pl
from jax.experimental.pallas import tpu as pltpu

F32 = jnp.float32
BF16 = jnp.bfloat16
EPS = 1e-6

D_MODEL = 2048
D_FF = 5632
MLSTM_HEADS = 4
MLSTM_QK = 128
MLSTM_V = 256
CONV_WIDTH = 4
HGRN_HEADS = 8
HGRN_DK = 128
HGRN_DV = 128
XATTN_HEADS = 4
XATTN_HEAD_DIM = D_MODEL // XATTN_HEADS

MLSTM_QK_W = MLSTM_HEADS * MLSTM_QK
MLSTM_V_W = MLSTM_HEADS * MLSTM_V
HGRN_W = HGRN_HEADS * HGRN_DK

VMEM_LIMIT_BYTES = 60000 * 1024
LANES = 128

COL_GM = 0
COL_GH = COL_GM + D_MODEL
COL_MQ = COL_GH + D_MODEL
COL_MK = COL_MQ + MLSTM_QK_W
COL_MV = COL_MK + MLSTM_QK_W
COL_MO = COL_MV + MLSTM_V_W
COL_HQ = COL_MO + MLSTM_V_W
COL_HF = COL_HQ + HGRN_W
COL_HI = COL_HF + HGRN_W
COL_HG = COL_HI + HGRN_W
PROJ_W = COL_HG + HGRN_W

MIX_GATE_ROW = 2 * MLSTM_QK_W + 2 * MLSTM_V_W
MIX_HGRN_ROW = MIX_GATE_ROW + 2 * MLSTM_HEADS
MIX_MERGE_ROW = MIX_HGRN_ROW + 4 * HGRN_W

MLSTM_L = 256
HGRN_TB = 128
HGRN_NUM_LEVELS = HGRN_TB.bit_length() - 1
HGRN_COARSE_LEVELS = tuple(1 << i for i in range(3, HGRN_NUM_LEVELS))


def _rms(x, g):
    return x * lax.rsqrt(jnp.mean(x * x, axis=-1, keepdims=True) + EPS) * g


def _sigmoid(x):
    return 1.0 / (1.0 + jnp.exp(-x))


def _log_sigmoid(x):
    return jnp.minimum(x, 0.0) - jnp.log(1.0 + jnp.exp(-jnp.abs(x)))


def _params(*sem):
    return pltpu.CompilerParams(dimension_semantics=sem, vmem_limit_bytes=VMEM_LIMIT_BYTES)


def _ffn_kernel(*refs, mode):
    if mode == "mix":
        (x_ref, g_ref, w1_ref, w3_ref, w2_ref, gn_ref, wg_ref,
         o_ref, hn_ref, gcol_ref, grow_ref, h_sc) = refs
    else:
        x_ref, g_ref, w1_ref, w3_ref, w2_ref, gn_ref, o_ref, h_sc = refs
    j = pl.program_id(1)

    @pl.when(j == 0)
    def _():
        h_sc[...] = _rms(x_ref[...], g_ref[...]).astype(BF16)
        o_ref[...] = jnp.zeros_like(o_ref)

    h = h_sc[...]
    a = jnp.dot(h, w1_ref[...].astype(BF16), preferred_element_type=F32)
    b = jnp.dot(h, w3_ref[...].astype(BF16), preferred_element_type=F32)
    act = (a * _sigmoid(a) * b).astype(BF16)
    o_ref[...] += jnp.dot(act, w2_ref[...].astype(BF16), preferred_element_type=F32)

    @pl.when(j == pl.num_programs(1) - 1)
    def _():
        y = x_ref[...] + 0.5 * o_ref[...]
        if mode == "final":
            o_ref[...] = _rms(y, gn_ref[...])
        else:
            o_ref[...] = y
            hn = _rms(y, gn_ref[...]).astype(BF16)
            hn_ref[...] = hn
            nt_dims = (((1,), (1,)), ((), ()))
            wg = wg_ref[...].astype(BF16)
            wg_pad = jnp.concatenate([wg, jnp.zeros((LANES - wg.shape[0], wg.shape[1]), BF16)], axis=0)
            gcol_ref[...] = lax.dot_general(hn, wg_pad, nt_dims, preferred_element_type=F32)
            grow_ref[...] = lax.dot_general(wg, hn, nt_dims, preferred_element_type=F32)


def _ffn(x2d, g, w1, w3, w2, g_next, w_gates_t=None, *, tm=1024, tf=256):
    n, d = x2d.shape
    f = w1.shape[1]
    mode = "final" if w_gates_t is None else "mix"
    in_specs = [
        pl.BlockSpec((tm, d), lambda i, j: (i, 0), pipeline_mode=pl.Buffered(1)),
        pl.BlockSpec((1, d), lambda i, j: (0, 0)),
        pl.BlockSpec((d, tf), lambda i, j: (0, j)),
        pl.BlockSpec((d, tf), lambda i, j: (0, j)),
        pl.BlockSpec((tf, d), lambda i, j: (j, 0)),
        pl.BlockSpec((1, d), lambda i, j: (0, 0)),
    ]
    args = [x2d, g.reshape(1, d), w1, w3, w2, g_next.reshape(1, d)]
    out_mode = {} if tf <= 256 else dict(pipeline_mode=pl.Buffered(1))
    out_specs = [pl.BlockSpec((tm, d), lambda i, j: (i, 0), **out_mode)]
    out_shape = [jax.ShapeDtypeStruct((n, d), F32)]
    if mode == "mix":
        ng = 2 * MLSTM_HEADS
        in_specs.append(pl.BlockSpec((ng, d), lambda i, j: (MIX_GATE_ROW // ng, 0)))
        args.append(w_gates_t)
        out_specs += [
            pl.BlockSpec((tm, d), lambda i, j: (i, 0)),
            pl.BlockSpec((tm, LANES), lambda i, j: (i, 0)),
            pl.BlockSpec((ng, tm), lambda i, j: (0, i)),
        ]
        out_shape += [
            jax.ShapeDtypeStruct((n, d), BF16),
            jax.ShapeDtypeStruct((n, LANES), F32),
            jax.ShapeDtypeStruct((ng, n), F32),
        ]
    return pl.pallas_call(
        functools.partial(_ffn_kernel, mode=mode),
        grid=(n // tm, f // tf),
        in_specs=in_specs,
        out_specs=out_specs,
        out_shape=out_shape,
        scratch_shapes=[pltpu.VMEM((tm, d), BF16)],
        compiler_params=_params("parallel", "arbitrary"),
        name="ffn_" + mode,
    )(*args)


def _in_proj_kernel(h_ref, wt_ref, p_ref, w_sc):
    @pl.when(pl.program_id(1) == 0)
    def _():
        w_sc[...] = wt_ref[...].astype(BF16)

    p_ref[...] = lax.dot_general(h_ref[...], w_sc[...], (((1,), (1,)), ((), ())), preferred_element_type=F32)


def _proj_row_offset(j, tn):
    n_gate_tiles = 2 * D_MODEL // tn
    n_mlstm_tiles = MIX_GATE_ROW // tn
    off8 = jnp.where(j < n_gate_tiles, MIX_MERGE_ROW // 8 + j * (tn // 8),
                     jnp.where(j < n_gate_tiles + n_mlstm_tiles, (j - n_gate_tiles) * (tn // 8),
                               MIX_HGRN_ROW // 8 + (j - n_gate_tiles - n_mlstm_tiles) * (tn // 8)))
    return off8 * 8


def _in_proj(hn, w_t, *, tm=1024, tn=1024):
    n, d = hn.shape
    return pl.pallas_call(
        _in_proj_kernel,
        grid=(PROJ_W // tn, n // tm),
        in_specs=[
            pl.BlockSpec((tm, d), lambda j, i: (i, 0)),
            pl.BlockSpec((pl.Element(tn), pl.Element(d)), lambda j, i: (_proj_row_offset(j, tn), 0)),
        ],
        out_specs=pl.BlockSpec((tm, tn), lambda j, i: (i, j)),
        out_shape=jax.ShapeDtypeStruct((n, PROJ_W), F32),
        scratch_shapes=[pltpu.VMEM((tn, d), BF16)],
        compiler_params=_params("arbitrary", "arbitrary"),
        name="in_proj",
    )(hn, w_t)


def _mlstm_kernel(q_ref, k_ref, v_ref, o_ref, gcol_ref, grow_ref, cw_ref, cb_ref, bcol_ref, brow_ref,
                  hn_ref, y_ref, xpad_sc, c_sc, n_sc, m_sc):
    L = MLSTM_L
    H = MLSTM_HEADS
    c = pl.program_id(1)

    @pl.when(c == 0)
    def _():
        xpad_sc[0:8, :] = jnp.zeros((8, 2 * MLSTM_QK_W), F32)
        c_sc[...] = jnp.zeros_like(c_sc)
        n_sc[...] = jnp.zeros_like(n_sc)
        m_sc[...] = jnp.zeros_like(m_sc)

    xpad_sc[8:8 + L, 0:MLSTM_QK_W] = q_ref[0]
    xpad_sc[8:8 + L, MLSTM_QK_W:2 * MLSTM_QK_W] = k_ref[0]
    acc = jnp.broadcast_to(cb_ref[...], (L, 2 * MLSTM_QK_W))
    for j in range(CONV_WIDTH):
        off = 8 - (CONV_WIDTH - 1) + j
        acc = acc + cw_ref[j:j + 1, :] * xpad_sc[off:off + L, :]
    xpad_sc[0:8, :] = xpad_sc[L:L + 8, :]
    qk = acc * _sigmoid(acc)

    gcol = gcol_ref[0] + bcol_ref[...]
    grow = grow_ref[...] + brow_ref[...]
    fcol = _log_sigmoid(gcol)
    frow = _log_sigmoid(grow)
    r_i = lax.broadcasted_iota(jnp.int32, (L, L), 0)
    c_i = lax.broadcasted_iota(jnp.int32, (L, L), 1)
    causal = r_i >= c_i
    tril = jnp.where(causal, 1.0, 0.0).astype(F32)
    triu = jnp.where(r_i <= c_i, 1.0, 0.0).astype(F32)
    bcol = jnp.dot(tril, fcol, preferred_element_type=F32, precision=lax.Precision.HIGHEST)
    brow = jnp.dot(frow, triu, preferred_element_type=F32, precision=lax.Precision.HIGHEST)

    for h in range(H):
        b_c = bcol[:, H + h:H + h + 1]
        i_c = gcol[:, h:h + 1]
        b_r = brow[H + h:H + h + 1, :]
        i_r = grow[h:h + 1, :]
        m_prev = m_sc[h:h + 1, 0:1]

        d_log = jnp.where(causal, b_c - b_r + i_r, -jnp.inf)
        inter_log = b_c + m_prev
        m_t = jnp.maximum(jnp.max(d_log, axis=1, keepdims=True), inter_log)

        q_h = qk[:, h * MLSTM_QK:(h + 1) * MLSTM_QK]
        k_h = qk[:, MLSTM_QK_W + h * MLSTM_QK:MLSTM_QK_W + (h + 1) * MLSTM_QK] * (MLSTM_QK ** -0.5)
        v_h = v_ref[0, :, h * MLSTM_V:(h + 1) * MLSTM_V].astype(BF16)
        q_b = q_h.astype(BF16)

        s = lax.dot_general(q_b, k_h.astype(BF16), (((1,), (1,)), ((), ())), preferred_element_type=F32)
        s = s * jnp.exp(d_log - m_t)
        w_inter = jnp.exp(inter_log - m_t)
        c_h = c_sc[h]
        n_h = n_sc[h:h + 1, :]
        num = (jnp.dot(s.astype(BF16), v_h, preferred_element_type=F32)
               + w_inter * jnp.dot(q_b, c_h.astype(BF16), preferred_element_type=F32))
        den = jnp.sum(s, axis=1, keepdims=True) + w_inter * jnp.sum(q_h * n_h, axis=1, keepdims=True)
        hh = num / jnp.maximum(jnp.abs(den), jnp.exp(-m_t))

        hn = hh * lax.rsqrt(jnp.mean(hh * hh, axis=-1, keepdims=True) + EPS)
        hn = hn * hn_ref[:, h * MLSTM_V:(h + 1) * MLSTM_V]
        y = hn * _sigmoid(o_ref[0, :, h * MLSTM_V:(h + 1) * MLSTM_V])
        y_ref[0, :, h * MLSTM_V:(h + 1) * MLSTM_V] = y.astype(y_ref.dtype)

        b_last = b_c[L - 1:L, :]
        a_log = b_last - b_c + i_c
        m_new = jnp.maximum(b_last + m_prev, jnp.max(a_log, axis=0, keepdims=True))
        w_a = jnp.exp(a_log - m_new)
        decay = jnp.exp(b_last + m_prev - m_new)
        kw = k_h * w_a
        c_sc[h] = decay * c_h + lax.dot_general(kw.astype(BF16), v_h, (((0,), (0,)), ((), ())),
                                                preferred_element_type=F32)
        n_sc[h:h + 1, :] = decay * n_h + jnp.sum(kw, axis=0, keepdims=True)
        m_sc[h:h + 1, :] = jnp.broadcast_to(m_new, (1, LANES))


def _mlstm(proj, gates_col, gates_row, conv_w, conv_b, bias_col, bias_row, head_norm):
    b, t, _ = proj.shape
    L = MLSTM_L
    qw = MLSTM_QK_W
    vw = MLSTM_V_W
    return pl.pallas_call(
        _mlstm_kernel,
        grid=(b, t // L),
        in_specs=[
            pl.BlockSpec((1, L, qw), lambda i, c: (i, c, COL_MQ // qw)),
            pl.BlockSpec((1, L, qw), lambda i, c: (i, c, COL_MK // qw)),
            pl.BlockSpec((1, L, vw), lambda i, c: (i, c, COL_MV // vw)),
            pl.BlockSpec((1, L, vw), lambda i, c: (i, c, COL_MO // vw)),
            pl.BlockSpec((1, L, LANES), lambda i, c: (i, c, 0)),
            pl.BlockSpec((2 * MLSTM_HEADS, L), lambda i, c: (0, i * (t // L) + c)),
            pl.BlockSpec((CONV_WIDTH, 2 * qw), lambda i, c: (0, 0)),
            pl.BlockSpec((1, 2 * qw), lambda i, c: (0, 0)),
            pl.BlockSpec((1, LANES), lambda i, c: (0, 0)),
            pl.BlockSpec((8, 1), lambda i, c: (0, 0)),
            pl.BlockSpec((1, vw), lambda i, c: (0, 0)),
        ],
        out_specs=pl.BlockSpec((1, L, vw), lambda i, c: (i, c, 0)),
        out_shape=jax.ShapeDtypeStruct((b, t, vw), BF16),
        scratch_shapes=[
            pltpu.VMEM((L + 8, 2 * qw), F32),
            pltpu.VMEM((MLSTM_HEADS, MLSTM_QK, MLSTM_V), F32),
            pltpu.VMEM((8, MLSTM_QK), F32),
            pltpu.VMEM((8, LANES), F32),
        ],
        compiler_params=_params("parallel", "arbitrary"),
        name="mlstm",
    )(proj, proj, proj, proj, gates_col, gates_row, conv_w, conv_b, bias_col, bias_row, head_norm)


def _hgrn_kernel(q_ref, f_ref, v_ref, og_ref, lbl_ref, hn_ref, y_ref, g_sc, k_sc, q_sc, gr_sc, st_sc):
    TB = HGRN_TB
    H = HGRN_HEADS
    DK = HGRN_DK
    W = H * DK
    NT = TB // 8
    c = pl.program_id(1)

    @pl.when(c == 0)
    def _():
        st_sc[...] = jnp.zeros_like(st_sc)

    lbl = lbl_ref[...]
    lmax = jnp.max(lbl, axis=0, keepdims=True)
    le = jnp.exp(lbl - lmax)
    lb = le[1:2, :] / jnp.sum(le, axis=0, keepdims=True)

    f = lb + (1.0 - lb) * _sigmoid(f_ref[0])
    k_sc[...] = 1.0 - f
    qraw = q_ref[0]
    q_sc[...] = qraw * _sigmoid(qraw) * (DK ** -0.5)
    r_i = lax.broadcasted_iota(jnp.int32, (TB, TB), 0)
    c_i = lax.broadcasted_iota(jnp.int32, (TB, TB), 1)
    tri = jnp.where(r_i >= c_i, 1.0, 0.0).astype(F32)
    g = jnp.dot(tri, jnp.log(f), preferred_element_type=F32, precision=lax.Precision.HIGHEST)
    g_sc[...] = g

    g3 = g.reshape(NT, 8, W)
    sub = lax.broadcasted_iota(jnp.int32, (NT, 8, W), 1)
    bit0 = (sub & 1) != 0
    bit1 = (sub & 2) != 0
    bit2 = (sub & 4) != 0
    last2 = jnp.where(bit0, g3, pltpu.roll(g3, 7, 1))
    last4 = jnp.where(bit1, last2, pltpu.roll(last2, 6, 1))
    last4_r = pltpu.roll(last4, 4, 1)
    last8 = jnp.where(bit2, last4, last4_r).reshape(TB, W)
    gr_sc[0] = jnp.where(bit0, pltpu.roll(g3, 1, 1), g3).reshape(TB, W)
    gr_sc[1] = jnp.where(bit1, pltpu.roll(last2, 2, 1), last2).reshape(TB, W)
    gr_sc[2] = jnp.where(bit2, last4_r, last4).reshape(TB, W)
    for li, lvl in enumerate(HGRN_COARSE_LEVELS):
        groups = []
        for base in range(0, TB, 2 * lvl):
            src = last8[base + lvl - 8:base + lvl, :]
            groups.extend([src] * (2 * lvl // 8))
        gr_sc[3 + li] = jnp.concatenate(groups, axis=0)

    x_i = r_i ^ c_i
    nt_dims = (((1,), (1,)), ((), ()))
    for h in range(H):
        cols = slice(h * DK, (h + 1) * DK)
        q_h = q_sc[:, cols]
        k_h = k_sc[:, cols]
        g_h = g_sc[:, cols]
        v_h = v_ref[0, :, cols].astype(BF16)

        a = lax.dot_general(q_h.astype(BF16), k_h.astype(BF16), nt_dims, preferred_element_type=F32)
        for li in range(HGRN_NUM_LEVELS):
            e = jnp.exp(-jnp.abs(g_h - gr_sc[li, :, cols]))
            p = lax.dot_general((q_h * e).astype(BF16), (k_h * e).astype(BF16), nt_dims,
                                preferred_element_type=F32)
            a = jnp.where(x_i >= (1 << li), p, a)
        a = jnp.where(r_i >= c_i, a, 0.0)
        o = jnp.dot(a.astype(BF16), v_h, preferred_element_type=F32)

        st = st_sc[h]
        g_last = g_h[TB - 1:TB, :]
        qd = (q_h * jnp.exp(g_h)).astype(BF16)
        o = o + lax.dot_general(qd, st.astype(BF16), nt_dims, preferred_element_type=F32)
        kd = (k_h * jnp.exp(g_last - g_h)).astype(BF16)
        st_sc[h] = jnp.exp(g_last) * st + lax.dot_general(v_h, kd, (((0,), (0,)), ((), ())),
                                                          preferred_element_type=F32)

        on = o * lax.rsqrt(jnp.mean(o * o, axis=-1, keepdims=True) + EPS) * hn_ref[:, cols]
        og = og_ref[0, :, cols]
        y_ref[0, :, cols] = (on * (og * _sigmoid(og))).astype(y_ref.dtype)


def _hgrn(proj, lb_logits, head_norm):
    b, t, _ = proj.shape
    TB = HGRN_TB
    w = HGRN_W
    return pl.pallas_call(
        _hgrn_kernel,
        grid=(b, t // TB),
        in_specs=[
            pl.BlockSpec((1, TB, w), lambda i, c: (i, c, COL_HQ // w)),
            pl.BlockSpec((1, TB, w), lambda i, c: (i, c, COL_HF // w)),
            pl.BlockSpec((1, TB, w), lambda i, c: (i, c, COL_HI // w)),
            pl.BlockSpec((1, TB, w), lambda i, c: (i, c, COL_HG // w)),
            pl.BlockSpec((2, w), lambda i, c: (0, 0)),
            pl.BlockSpec((1, w), lambda i, c: (0, 0)),
        ],
        out_specs=pl.BlockSpec((1, TB, w), lambda i, c: (i, c, 0)),
        out_shape=jax.ShapeDtypeStruct((b, t, w), BF16),
        scratch_shapes=[
            pltpu.VMEM((TB, w), F32),
            pltpu.VMEM((TB, w), F32),
            pltpu.VMEM((TB, w), F32),
            pltpu.VMEM((HGRN_NUM_LEVELS, TB, w), F32),
            pltpu.VMEM((HGRN_HEADS, HGRN_DV, HGRN_DK), F32),
        ],
        compiler_params=_params("parallel", "arbitrary"),
        name="hgrn2",
    )(proj, proj, proj, proj, lb_logits, head_norm)


def _merge_kernel(x_ref, ym_ref, yh_ref, gm_ref, gh_ref, wm_ref, wh_ref, wo_ref, o_ref):
    pm = jnp.dot(ym_ref[...], wm_ref[...], preferred_element_type=F32)
    ph = jnp.dot(yh_ref[...], wh_ref[...], preferred_element_type=F32)
    merged = _sigmoid(gm_ref[...]) * pm + _sigmoid(gh_ref[...]) * ph
    o_ref[...] = x_ref[...] + jnp.dot(merged.astype(BF16), wo_ref[...], preferred_element_type=F32)


def _const_spec(shape):
    return pl.BlockSpec(shape, lambda *_: (0,) * len(shape), pipeline_mode=pl.Buffered(1))


def _merge(x2d, ym, yh, proj2d, w_m, w_h, w_o, *, tm=256):
    n, d = x2d.shape
    return pl.pallas_call(
        _merge_kernel,
        grid=(n // tm,),
        in_specs=[
            pl.BlockSpec((tm, d), lambda i: (i, 0)),
            pl.BlockSpec((tm, MLSTM_V_W), lambda i: (i, 0)),
            pl.BlockSpec((tm, HGRN_W), lambda i: (i, 0)),
            pl.BlockSpec((tm, d), lambda i: (i, COL_GM // d)),
            pl.BlockSpec((tm, d), lambda i: (i, COL_GH // d)),
            _const_spec(w_m.shape),
            _const_spec(w_h.shape),
            _const_spec(w_o.shape),
        ],
        out_specs=pl.BlockSpec((tm, d), lambda i: (i, 0)),
        out_shape=jax.ShapeDtypeStruct((n, d), F32),
        compiler_params=_params("parallel"),
        name="merge",
    )(x2d, ym, yh, proj2d, proj2d, w_m, w_h, w_o)


def _mem_kv_kernel(m_ref, g_ref, w_ref, o_ref):
    m = _rms(m_ref[...], g_ref[...]).astype(BF16)
    o_ref[...] = jnp.dot(m, w_ref[...], preferred_element_type=F32).astype(o_ref.dtype)


def _mem_kv(mem2d, g, w_kv, *, tn=1024):
    n, d = mem2d.shape
    nw = w_kv.shape[1]
    return pl.pallas_call(
        _mem_kv_kernel,
        grid=(nw // tn,),
        in_specs=[
            pl.BlockSpec((n, d), lambda j: (0, 0)),
            pl.BlockSpec((1, d), lambda j: (0, 0)),
            pl.BlockSpec((d, tn), lambda j: (0, j)),
        ],
        out_specs=pl.BlockSpec((n, tn), lambda j: (0, j)),
        out_shape=jax.ShapeDtypeStruct((n, nw), BF16),
        compiler_params=_params("parallel"),
        name="mem_kv",
    )(mem2d, g.reshape(1, d), w_kv)


def _xattn_kernel(x_ref, g_ref, wq_ref, k_ref, v_ref, wo_ref, o_ref):
    x = x_ref[0]
    h = _rms(x, g_ref[...]).astype(BF16)
    q = jnp.dot(h, wq_ref[...], preferred_element_type=F32)
    outs = []
    for hd in range(XATTN_HEADS):
        cols = slice(hd * XATTN_HEAD_DIM, (hd + 1) * XATTN_HEAD_DIM)
        s = lax.dot_general(q[:, cols].astype(BF16), k_ref[0, :, cols], (((1,), (1,)), ((), ())),
                            preferred_element_type=F32) * (XATTN_HEAD_DIM ** -0.5)
        s = s - jnp.max(s, axis=-1, keepdims=True)
        e = jnp.exp(s)
        p = e / jnp.sum(e, axis=-1, keepdims=True)
        outs.append(jnp.dot(p.astype(BF16), v_ref[0, :, cols], preferred_element_type=F32).astype(BF16))
    o = jnp.concatenate(outs, axis=-1)
    o_ref[0] = x + jnp.dot(o, wo_ref[...], preferred_element_type=F32)


def _xattn(x, g, w_q, kv, w_o, *, tm=512):
    b, t, d = x.shape
    m = kv.shape[1]
    return pl.pallas_call(
        _xattn_kernel,
        grid=(b, t // tm),
        in_specs=[
            pl.BlockSpec((1, tm, d), lambda i, c: (i, c, 0)),
            _const_spec((1, d)),
            _const_spec(w_q.shape),
            pl.BlockSpec((1, m, d), lambda i, c: (i, 0, 0)),
            pl.BlockSpec((1, m, d), lambda i, c: (i, 0, 1)),
            _const_spec(w_o.shape),
        ],
        out_specs=pl.BlockSpec((1, tm, d), lambda i, c: (i, c, 0)),
        out_shape=jax.ShapeDtypeStruct((b, t, d), F32),
        compiler_params=_params("parallel", "parallel"),
        name="xattn",
    )(x, g.reshape(1, d), w_q, kv, kv, w_o)


def kernel(x, mem, norm_ffn1, ffn1_w1, ffn1_w3, ffn1_w2, norm_mix, w_in, mlstm_conv_w, mlstm_conv_b,
           mlstm_ig_bias, mlstm_fg_bias, mlstm_head_norm, hgrn_lb_logits, hgrn_head_norm, w_proj_m,
           w_proj_h, w_out, norm_xattn, norm_mem, xattn_wq, xattn_wkv, xattn_wo, norm_ffn2, ffn2_w1,
           ffn2_w3, ffn2_w2, norm_final):
    b, t, d = x.shape
    depth = norm_ffn1.shape[0]
    assert depth == 1 and hgrn_lb_logits.shape[0] == 2
    n = b * t
    l = 0
    bf = lambda w: w.astype(BF16)

    w_in_t = jnp.swapaxes(w_in[l], 0, 1)
    n_gate = 2 * MLSTM_HEADS
    gate_bias = jnp.concatenate([mlstm_ig_bias[l], mlstm_fg_bias[l]]).astype(F32)
    bias_col = jnp.pad(gate_bias, (0, LANES - n_gate)).reshape(1, LANES)
    bias_row = gate_bias.reshape(n_gate, 1)

    x1, hn, gates_col, gates_row = _ffn(x.reshape(n, d), norm_ffn1[l], ffn1_w1[l], ffn1_w3[l], ffn1_w2[l],
                                        norm_mix[l], w_in_t)

    proj = _in_proj(hn, w_in_t)
    proj3 = proj.reshape(b, t, PROJ_W)
    y_m = _mlstm(proj3, gates_col.reshape(b, t, LANES), gates_row, mlstm_conv_w[l],
                 mlstm_conv_b[l].reshape(1, -1), bias_col, bias_row, mlstm_head_norm[l].reshape(1, -1))
    y_h = _hgrn(proj3, hgrn_lb_logits, hgrn_head_norm[l].reshape(1, -1))

    x2 = _merge(x1, y_m.reshape(n, MLSTM_V_W), y_h.reshape(n, HGRN_W), proj,
                bf(w_proj_m[l]), bf(w_proj_h[l]), bf(w_out[l]))

    m_len = mem.shape[1]
    kv = _mem_kv(mem.reshape(b * m_len, d), norm_mem[l], bf(xattn_wkv[l])).reshape(b, m_len, 2 * d)
    x3 = _xattn(x2.reshape(b, t, d), norm_xattn[l], bf(xattn_wq[l]), kv, bf(xattn_wo[l]))

    (out,) = _ffn(x3.reshape(n, d), norm_ffn2[l], ffn2_w1[l], ffn2_w3[l], ffn2_w2[l], norm_final, tf=512)
    return out.reshape(b, t, d)
```

```python
import functools

import jax
import jax.numpy as jnp
from jax import lax
from jax.experimental import pallas as pl
from jax.experimental.pallas import tpu as pltpu

F32 = jnp.float32
BF16 = jnp.bfloat16
EPS = 1e-6

D_MODEL = 2048
D_FF = 5632
MLSTM_HEADS = 4
MLSTM_QK = 128
MLSTM_V = 256
CONV_WIDTH = 4
HGRN_HEADS = 8
HGRN_DK = 128
HGRN_DV = 128
XATTN_HEADS = 4
XATTN_HEAD_DIM = D_MODEL // XATTN_HEADS

MLSTM_QK_W = MLSTM_HEADS * MLSTM_QK
MLSTM_V_W = MLSTM_HEADS * MLSTM_V
HGRN_W = HGRN_HEADS * HGRN_DK

VMEM_LIMIT_BYTES = 60000 * 1024
LANES = 128

COL_GM = 0
COL_GH = COL_GM + D_MODEL
COL_MQ = COL_GH + D_MODEL
COL_MK = COL_MQ + MLSTM_QK_W
COL_MV = COL_MK + MLSTM_QK_W
COL_MO = COL_MV + MLSTM_V_W
COL_HQ = COL_MO + MLSTM_V_W
COL_HF = COL_HQ + HGRN_W
COL_HI = COL_HF + HGRN_W
COL_HG = COL_HI + HGRN_W
PROJ_W = COL_HG + HGRN_W

MIX_GATE_ROW = 2 * MLSTM_QK_W + 2 * MLSTM_V_W
MIX_HGRN_ROW = MIX_GATE_ROW + 2 * MLSTM_HEADS
MIX_MERGE_ROW = MIX_HGRN_ROW + 4 * HGRN_W

MIX_TB = 128
HGRN_NUM_LEVELS = MIX_TB.bit_length() - 1
HGRN_COARSE_LEVELS = tuple(1 << i for i in range(3, HGRN_NUM_LEVELS))


def _rms(x, g):
    return x * lax.rsqrt(jnp.mean(x * x, axis=-1, keepdims=True) + EPS) * g


def _sigmoid(x):
    return 1.0 / (1.0 + jnp.exp(-x))


def _log_sigmoid(x):
    return jnp.minimum(x, 0.0) - jnp.log(1.0 + jnp.exp(-jnp.abs(x)))


def _params(*sem):
    return pltpu.CompilerParams(dimension_semantics=sem, vmem_limit_bytes=VMEM_LIMIT_BYTES)


def _ffn_kernel(*refs, mode):
    if mode == "mix":
        (x_ref, g_ref, w1_ref, w3_ref, w2_ref, gn_ref, wg_ref,
         o_ref, hn_ref, gcol_ref, grow_ref, h_sc) = refs
    else:
        x_ref, g_ref, w1_ref, w3_ref, w2_ref, gn_ref, o_ref, h_sc = refs
    j = pl.program_id(1)

    @pl.when(j == 0)
    def _():
        h_sc[...] = _rms(x_ref[...], g_ref[...]).astype(BF16)
        o_ref[...] = jnp.zeros_like(o_ref)

    h = h_sc[...]
    a = jnp.dot(h, w1_ref[...].astype(BF16), preferred_element_type=F32)
    b = jnp.dot(h, w3_ref[...].astype(BF16), preferred_element_type=F32)
    act = (a * _sigmoid(a) * b).astype(BF16)
    o_ref[...] += jnp.dot(act, w2_ref[...].astype(BF16), preferred_element_type=F32)

    @pl.when(j == pl.num_programs(1) - 1)
    def _():
        y = x_ref[...] + 0.5 * o_ref[...]
        if mode == "final":
            o_ref[...] = _rms(y, gn_ref[...])
        else:
            o_ref[...] = y
            hn = _rms(y, gn_ref[...]).astype(BF16)
            hn_ref[...] = hn
            nt_dims = (((1,), (1,)), ((), ()))
            wg = wg_ref[...].astype(BF16)
            wg_pad = jnp.concatenate([wg, jnp.zeros((LANES - wg.shape[0], wg.shape[1]), BF16)], axis=0)
            gcol_ref[...] = lax.dot_general(hn, wg_pad, nt_dims, preferred_element_type=F32)
            grow_ref[...] = lax.dot_general(wg, hn, nt_dims, preferred_element_type=F32)


def _ffn(x2d, g, w1, w3, w2, g_next, w_gates_t=None, *, tm=1024, tf=256):
    n, d = x2d.shape
    f = w1.shape[1]
    mode = "final" if w_gates_t is None else "mix"
    in_specs = [
        pl.BlockSpec((tm, d), lambda i, j: (i, 0), pipeline_mode=pl.Buffered(1)),
        pl.BlockSpec((1, d), lambda i, j: (0, 0)),
        pl.BlockSpec((d, tf), lambda i, j: (0, j)),
        pl.BlockSpec((d, tf), lambda i, j: (0, j)),
        pl.BlockSpec((tf, d), lambda i, j: (j, 0)),
        pl.BlockSpec((1, d), lambda i, j: (0, 0)),
    ]
    args = [x2d, g.reshape(1, d), w1, w3, w2, g_next.reshape(1, d)]
    out_mode = {} if tf <= 256 else dict(pipeline_mode=pl.Buffered(1))
    out_specs = [pl.BlockSpec((tm, d), lambda i, j: (i, 0), **out_mode)]
    out_shape = [jax.ShapeDtypeStruct((n, d), F32)]
    if mode == "mix":
        ng = 2 * MLSTM_HEADS
        in_specs.append(pl.BlockSpec((ng, d), lambda i, j: (MIX_GATE_ROW // ng, 0)))
        args.append(w_gates_t)
        out_specs += [
            pl.BlockSpec((tm, d), lambda i, j: (i, 0)),
            pl.BlockSpec((tm, LANES), lambda i, j: (i, 0)),
            pl.BlockSpec((ng, tm), lambda i, j: (0, i)),
        ]
        out_shape += [
            jax.ShapeDtypeStruct((n, d), BF16),
            jax.ShapeDtypeStruct((n, LANES), F32),
            jax.ShapeDtypeStruct((ng, n), F32),
        ]
    return pl.pallas_call(
        functools.partial(_ffn_kernel, mode=mode),
        grid=(n // tm, f // tf),
        in_specs=in_specs,
        out_specs=out_specs,
        out_shape=out_shape,
        scratch_shapes=[pltpu.VMEM((tm, d), BF16)],
        compiler_params=_params("parallel", "arbitrary"),
        name="ffn_" + mode,
    )(*args)


def _in_proj_kernel(h_ref, wt_ref, p_ref, w_sc):
    @pl.when(pl.program_id(1) == 0)
    def _():
        w_sc[...] = wt_ref[...].astype(BF16)

    p_ref[...] = lax.dot_general(h_ref[...], w_sc[...], (((1,), (1,)), ((), ())), preferred_element_type=F32)


def _proj_row_offset(j, tn):
    n_gate_tiles = 2 * D_MODEL // tn
    n_mlstm_tiles = MIX_GATE_ROW // tn
    off8 = jnp.where(j < n_gate_tiles, MIX_MERGE_ROW // 8 + j * (tn // 8),
                     jnp.where(j < n_gate_tiles + n_mlstm_tiles, (j - n_gate_tiles) * (tn // 8),
                               MIX_HGRN_ROW // 8 + (j - n_gate_tiles - n_mlstm_tiles) * (tn // 8)))
    return off8 * 8


def _in_proj(hn, w_t, *, tm=1024, tn=1024):
    n, d = hn.shape
    return pl.pallas_call(
        _in_proj_kernel,
        grid=(PROJ_W // tn, n // tm),
        in_specs=[
            pl.BlockSpec((tm, d), lambda j, i: (i, 0)),
            pl.BlockSpec((pl.Element(tn), pl.Element(d)), lambda j, i: (_proj_row_offset(j, tn), 0)),
        ],
        out_specs=pl.BlockSpec((tm, tn), lambda j, i: (i, j)),
        out_shape=jax.ShapeDtypeStruct((n, PROJ_W), F32),
        scratch_shapes=[pltpu.VMEM((tn, d), BF16)],
        compiler_params=_params("arbitrary", "arbitrary"),
        name="in_proj",
    )(hn, w_t)


def _mlstm_block(q_ref, k_ref, v_ref, o_ref, gcol_ref, grow_ref, cw_ref, cb_ref, bcol_ref, brow_ref,
                 hn_ref, y_ref, xpad_sc, c_sc, n_sc, m_sc):
    L = MIX_TB
    H = MLSTM_HEADS

    xpad_sc[8:8 + L, 0:MLSTM_QK_W] = q_ref[...]
    xpad_sc[8:8 + L, MLSTM_QK_W:2 * MLSTM_QK_W] = k_ref[...]
    acc = jnp.broadcast_to(cb_ref[...], (L, 2 * MLSTM_QK_W))
    for j in range(CONV_WIDTH):
        off = 8 - (CONV_WIDTH - 1) + j
        acc = acc + cw_ref[j:j + 1, :] * xpad_sc[off:off + L, :]
    xpad_sc[0:8, :] = xpad_sc[L:L + 8, :]
    qk = acc * _sigmoid(acc)

    gcol = gcol_ref[...] + bcol_ref[...]
    grow = grow_ref[...] + brow_ref[...]
    fcol = _log_sigmoid(gcol)
    frow = _log_sigmoid(grow)
    r_i = lax.broadcasted_iota(jnp.int32, (L, L), 0)
    c_i = lax.broadcasted_iota(jnp.int32, (L, L), 1)
    causal = r_i >= c_i
    tril = jnp.where(causal, 1.0, 0.0).astype(F32)
    triu = jnp.where(r_i <= c_i, 1.0, 0.0).astype(F32)
    bcol = jnp.dot(tril, fcol, preferred_element_type=F32, precision=lax.Precision.HIGHEST)
    brow = jnp.dot(frow, triu, preferred_element_type=F32, precision=lax.Precision.HIGHEST)

    for h in range(H):
        b_c = bcol[:, H + h:H + h + 1]
        i_c = gcol[:, h:h + 1]
        b_r = brow[H + h:H + h + 1, :]
        i_r = grow[h:h + 1, :]
        m_prev = m_sc[h:h + 1, 0:1]

        d_log = jnp.where(causal, b_c - (b_r - i_r), -jnp.inf)
        inter_log = b_c + m_prev
        m_t = jnp.maximum(jnp.max(d_log, axis=1, keepdims=True), inter_log)

        q_h = qk[:, h * MLSTM_QK:(h + 1) * MLSTM_QK]
        k_h = qk[:, MLSTM_QK_W + h * MLSTM_QK:MLSTM_QK_W + (h + 1) * MLSTM_QK] * (MLSTM_QK ** -0.5)
        v_h = v_ref[:, h * MLSTM_V:(h + 1) * MLSTM_V].astype(BF16)
        q_b = q_h.astype(BF16)

        s = lax.dot_general(q_b, k_h.astype(BF16), (((1,), (1,)), ((), ())), preferred_element_type=F32)
        s = s * jnp.exp(d_log - m_t)
        w_inter = jnp.exp(inter_log - m_t)
        c_h = c_sc[h]
        n_h = n_sc[h:h + 1, :]
        num = (jnp.dot(s.astype(BF16), v_h, preferred_element_type=F32)
               + w_inter * jnp.dot(q_b, c_h.astype(BF16), preferred_element_type=F32))
        den = jnp.sum(s, axis=1, keepdims=True) + w_inter * jnp.sum(q_h * n_h, axis=1, keepdims=True)
        hh = num / jnp.maximum(jnp.abs(den), jnp.exp(-m_t))

        hn = hh * lax.rsqrt(jnp.mean(hh * hh, axis=-1, keepdims=True) + EPS)
        hn = hn * hn_ref[:, h * MLSTM_V:(h + 1) * MLSTM_V]
        y = hn * _sigmoid(o_ref[:, h * MLSTM_V:(h + 1) * MLSTM_V])
        y_ref[:, h * MLSTM_V:(h + 1) * MLSTM_V] = y.astype(y_ref.dtype)

        b_last = b_c[L - 1:L, :]
        a_log = b_last - b_c + i_c
        m_new = jnp.maximum(b_last + m_prev, jnp.max(a_log, axis=0, keepdims=True))
        w_a = jnp.exp(a_log - m_new)
        decay = jnp.exp(b_last + m_prev - m_new)
        kw = k_h * w_a
        c_sc[h] = decay * c_h + lax.dot_general(kw.astype(BF16), v_h, (((0,), (0,)), ((), ())),
                                                preferred_element_type=F32)
        n_sc[h:h + 1, :] = decay * n_h + jnp.sum(kw, axis=0, keepdims=True)
        m_sc[h:h + 1, :] = jnp.broadcast_to(m_new, (1, LANES))


def _hgrn_block(q_ref, f_ref, v_ref, og_ref, lbl_ref, hn_ref, y_ref, g_sc, k_sc, q_sc, gr_sc, st_sc):
    TB = MIX_TB
    H = HGRN_HEADS
    DK = HGRN_DK
    W = H * DK
    NT = TB // 8

    lbl = lbl_ref[...]
    lmax = jnp.max(lbl, axis=0, keepdims=True)
    le = jnp.exp(lbl - lmax)
    lb = le[1:2, :] / jnp.sum(le, axis=0, keepdims=True)

    f = lb + (1.0 - lb) * _sigmoid(f_ref[...])
    k_sc[...] = 1.0 - f
    qraw = q_ref[...]
    q_sc[...] = qraw * _sigmoid(qraw) * (DK ** -0.5)
    r_i = lax.broadcasted_iota(jnp.int32, (TB, TB), 0)
    c_i = lax.broadcasted_iota(jnp.int32, (TB, TB), 1)
    tri = jnp.where(r_i >= c_i, 1.0, 0.0).astype(F32)
    g = jnp.dot(tri, jnp.log(f), preferred_element_type=F32, precision=lax.Precision.HIGHEST)
    g_sc[...] = g

    g3 = g.reshape(NT, 8, W)
    sub = lax.broadcasted_iota(jnp.int32, (NT, 8, W), 1)
    bit0 = (sub & 1) != 0
    bit1 = (sub & 2) != 0
    bit2 = (sub & 4) != 0
    last2 = jnp.where(bit0, g3, pltpu.roll(g3, 7, 1))
    last4 = jnp.where(bit1, last2, pltpu.roll(last2, 6, 1))
    last4_r = pltpu.roll(last4, 4, 1)
    last8 = jnp.where(bit2, last4, last4_r).reshape(TB, W)
    gr_sc[0] = jnp.where(bit0, pltpu.roll(g3, 1, 1), g3).reshape(TB, W)
    gr_sc[1] = jnp.where(bit1, pltpu.roll(last2, 2, 1), last2).reshape(TB, W)
    gr_sc[2] = jnp.where(bit2, last4_r, last4).reshape(TB, W)
    for li, lvl in enumerate(HGRN_COARSE_LEVELS):
        groups = []
        for base in range(0, TB, 2 * lvl):
            src = last8[base + lvl - 8:base + lvl, :]
            groups.extend([src] * (2 * lvl // 8))
        gr_sc[3 + li] = jnp.concatenate(groups, axis=0)

    x_i = r_i ^ c_i
    nt_dims = (((1,), (1,)), ((), ()))
    for h in range(H):
        cols = slice(h * DK, (h + 1) * DK)
        q_h = q_sc[:, cols]
        k_h = k_sc[:, cols]
        g_h = g_sc[:, cols]
        v_h = v_ref[:, cols].astype(BF16)

        a = lax.dot_general(q_h.astype(BF16), k_h.astype(BF16), nt_dims, preferred_element_type=F32)
        for li in range(HGRN_NUM_LEVELS):
            e = jnp.exp(-jnp.abs(g_h - gr_sc[li, :, cols]))
            p = lax.dot_general((q_h * e).astype(BF16), (k_h * e).astype(BF16), nt_dims,
                                preferred_element_type=F32)
            a = jnp.where(x_i >= (1 << li), p, a)
        a = jnp.where(r_i >= c_i, a, 0.0)
        o = jnp.dot(a.astype(BF16), v_h, preferred_element_type=F32)

        st = st_sc[h]
        g_last = g_h[TB - 1:TB, :]
        qd = (q_h * jnp.exp(g_h)).astype(BF16)
        o = o + lax.dot_general(qd, st.astype(BF16), nt_dims, preferred_element_type=F32)
        kd = (k_h * jnp.exp(g_last - g_h)).astype(BF16)
        st_sc[h] = jnp.exp(g_last) * st + lax.dot_general(v_h, kd, (((0,), (0,)), ((), ())),
                                                          preferred_element_type=F32)

        on = o * lax.rsqrt(jnp.mean(o * o, axis=-1, keepdims=True) + EPS) * hn_ref[:, cols]
        og = og_ref[:, cols]
        y_ref[:, cols] = (on * (og * _sigmoid(og))).astype(y_ref.dtype)


def _mix_kernel(mq_ref, mk_ref, mv_ref, mo_ref, gcol_ref, grow_ref, cw_ref, cb_ref, bcol_ref, brow_ref,
                mhn_ref, hq_ref, hf_ref, hi_ref, hg_ref, lbl_ref, hhn_ref,
                x_ref, gm_ref, gh_ref, wm_ref, wh_ref, wo_ref,
                out_ref,
                ym_sc, yh_sc, xpad_sc, c_sc, n_sc, m_sc, g_sc, k_sc, q_sc, gr_sc, st_sc,
                *, blocks_per_seq):
    s = pl.program_id(0)

    @pl.when(s == 0)
    def _():
        ym_sc[...] = jnp.zeros_like(ym_sc)
        yh_sc[...] = jnp.zeros_like(yh_sc)

    @pl.when(s % blocks_per_seq == 0)
    def _():
        xpad_sc[0:8, :] = jnp.zeros((8, 2 * MLSTM_QK_W), F32)
        c_sc[...] = jnp.zeros_like(c_sc)
        n_sc[...] = jnp.zeros_like(n_sc)
        m_sc[...] = jnp.zeros_like(m_sc)
        st_sc[...] = jnp.zeros_like(st_sc)

    pm = jnp.dot(ym_sc[...], wm_ref[...], preferred_element_type=F32)
    ph = jnp.dot(yh_sc[...], wh_ref[...], preferred_element_type=F32)
    merged = _sigmoid(gm_ref[...]) * pm + _sigmoid(gh_ref[...]) * ph
    out_ref[...] = x_ref[...] + jnp.dot(merged.astype(BF16), wo_ref[...], preferred_element_type=F32)

    _mlstm_block(mq_ref, mk_ref, mv_ref, mo_ref, gcol_ref, grow_ref, cw_ref, cb_ref, bcol_ref, brow_ref,
                 mhn_ref, ym_sc, xpad_sc, c_sc, n_sc, m_sc)
    _hgrn_block(hq_ref, hf_ref, hi_ref, hg_ref, lbl_ref, hhn_ref, yh_sc, g_sc, k_sc, q_sc, gr_sc, st_sc)


def _const_spec(shape):
    return pl.BlockSpec(shape, lambda *_: (0,) * len(shape), pipeline_mode=pl.Buffered(1))


def _mix(x2d, proj, gates_col, gates_row, conv_w, conv_b, bias_col, bias_row, m_head_norm,
         lb_logits, h_head_norm, w_m, w_h, w_o, *, seq_len):
    n, d = x2d.shape
    tb = MIX_TB
    nblk = n // tb
    qw, vw, hw = MLSTM_QK_W, MLSTM_V_W, HGRN_W

    def cur(col_block):
        return lambda s: (jnp.minimum(s, nblk - 1), col_block)

    def prev(col_block):
        return lambda s: (jnp.maximum(s - 1, 0), col_block)

    return pl.pallas_call(
        functools.partial(_mix_kernel, blocks_per_seq=seq_len // tb),
        grid=(nblk + 1,),
        in_specs=[
            pl.BlockSpec((tb, qw), cur(COL_MQ // qw)),
            pl.BlockSpec((tb, qw), cur(COL_MK // qw)),
            pl.BlockSpec((tb, vw), cur(COL_MV // vw)),
            pl.BlockSpec((tb, vw), cur(COL_MO // vw)),
            pl.BlockSpec((tb, LANES), cur(0)),
            pl.BlockSpec((2 * MLSTM_HEADS, tb), lambda s: (0, jnp.minimum(s, nblk - 1))),
            _const_spec((CONV_WIDTH, 2 * qw)),
            _const_spec((1, 2 * qw)),
            _const_spec((1, LANES)),
            _const_spec((2 * MLSTM_HEADS, 1)),
            _const_spec((1, vw)),
            pl.BlockSpec((tb, hw), cur(COL_HQ // hw)),
            pl.BlockSpec((tb, hw), cur(COL_HF // hw)),
            pl.BlockSpec((tb, hw), cur(COL_HI // hw)),
            pl.BlockSpec((tb, hw), cur(COL_HG // hw)),
            _const_spec((2, hw)),
            _const_spec((1, hw)),
            pl.BlockSpec((tb, d), prev(0)),
            pl.BlockSpec((tb, d), prev(COL_GM // d)),
            pl.BlockSpec((tb, d), prev(COL_GH // d)),
            _const_spec(w_m.shape),
            _const_spec(w_h.shape),
            _const_spec(w_o.shape),
        ],
        out_specs=pl.BlockSpec((tb, d), prev(0)),
        out_shape=jax.ShapeDtypeStruct((n, d), F32),
        scratch_shapes=[
            pltpu.VMEM((tb, vw), BF16),
            pltpu.VMEM((tb, hw), BF16),
            pltpu.VMEM((tb + 8, 2 * qw), F32),
            pltpu.VMEM((MLSTM_HEADS, MLSTM_QK, MLSTM_V), F32),
            pltpu.VMEM((8, MLSTM_QK), F32),
            pltpu.VMEM((8, LANES), F32),
            pltpu.VMEM((tb, hw), F32),
            pltpu.VMEM((tb, hw), F32),
            pltpu.VMEM((tb, hw), F32),
            pltpu.VMEM((HGRN_NUM_LEVELS, tb, hw), F32),
            pltpu.VMEM((HGRN_HEADS, HGRN_DV, HGRN_DK), F32),
        ],
        compiler_params=_params("arbitrary"),
        name="mix",
    )(proj, proj, proj, proj, gates_col, gates_row, conv_w, conv_b, bias_col, bias_row, m_head_norm,
      proj, proj, proj, proj, lb_logits, h_head_norm, x2d, proj, proj, w_m, w_h, w_o)


def _mem_kv_kernel(m_ref, g_ref, w_ref, o_ref):
    m = _rms(m_ref[...], g_ref[...]).astype(BF16)
    o_ref[...] = jnp.dot(m, w_ref[...], preferred_element_type=F32).astype(o_ref.dtype)


def _mem_kv(mem2d, g, w_kv, *, tn=1024):
    n, d = mem2d.shape
    nw = w_kv.shape[1]
    return pl.pallas_call(
        _mem_kv_kernel,
        grid=(nw // tn,),
        in_specs=[
            pl.BlockSpec((n, d), lambda j: (0, 0)),
            pl.BlockSpec((1, d), lambda j: (0, 0)),
            pl.BlockSpec((d, tn), lambda j: (0, j)),
        ],
        out_specs=pl.BlockSpec((n, tn), lambda j: (0, j)),
        out_shape=jax.ShapeDtypeStruct((n, nw), BF16),
        compiler_params=_params("parallel"),
        name="mem_kv",
    )(mem2d, g.reshape(1, d), w_kv)


def _xattn_kernel(x_ref, g_ref, wq_ref, k_ref, v_ref, wo_ref, o_ref):
    x = x_ref[0]
    h = _rms(x, g_ref[...]).astype(BF16)
    q = jnp.dot(h, wq_ref[...], preferred_element_type=F32)
    outs = []
    for hd in range(XATTN_HEADS):
        cols = slice(hd * XATTN_HEAD_DIM, (hd + 1) * XATTN_HEAD_DIM)
        s = lax.dot_general(q[:, cols].astype(BF16), k_ref[0, :, cols], (((1,), (1,)), ((), ())),
                            preferred_element_type=F32) * (XATTN_HEAD_DIM ** -0.5)
        s = s - jnp.max(s, axis=-1, keepdims=True)
        e = jnp.exp(s)
        p = e / jnp.sum(e, axis=-1, keepdims=True)
        outs.append(jnp.dot(p.astype(BF16), v_ref[0, :, cols], preferred_element_type=F32).astype(BF16))
    o = jnp.concatenate(outs, axis=-1)
    o_ref[0] = x + jnp.dot(o, wo_ref[...], preferred_element_type=F32)


def _xattn(x, g, w_q, kv, w_o, *, tm=512):
    b, t, d = x.shape
    m = kv.shape[1]
    return pl.pallas_call(
        _xattn_kernel,
        grid=(b, t // tm),
        in_specs=[
            pl.BlockSpec((1, tm, d), lambda i, c: (i, c, 0)),
            _const_spec((1, d)),
            _const_spec(w_q.shape),
            pl.BlockSpec((1, m, d), lambda i, c: (i, 0, 0)),
            pl.BlockSpec((1, m, d), lambda i, c: (i, 0, 1)),
            _const_spec(w_o.shape),
        ],
        out_specs=pl.BlockSpec((1, tm, d), lambda i, c: (i, c, 0)),
        out_shape=jax.ShapeDtypeStruct((b, t, d), F32),
        compiler_params=_params("parallel", "parallel"),
        name="xattn",
    )(x, g.reshape(1, d), w_q, kv, kv, w_o)


def kernel(x, mem, norm_ffn1, ffn1_w1, ffn1_w3, ffn1_w2, norm_mix, w_in, mlstm_conv_w, mlstm_conv_b,
           mlstm_ig_bias, mlstm_fg_bias, mlstm_head_norm, hgrn_lb_logits, hgrn_head_norm, w_proj_m,
           w_proj_h, w_out, norm_xattn, norm_mem, xattn_wq, xattn_wkv, xattn_wo, norm_ffn2, ffn2_w1,
           ffn2_w3, ffn2_w2, norm_final):
    b, t, d = x.shape
    depth = norm_ffn1.shape[0]
    assert depth == 1 and hgrn_lb_logits.shape[0] == 2
    n = b * t
    l = 0
    bf = lambda w: w.astype(BF16)

    w_in_t = jnp.swapaxes(w_in[l], 0, 1)
    n_gate = 2 * MLSTM_HEADS
    gate_bias = jnp.concatenate([mlstm_ig_bias[l], mlstm_fg_bias[l]]).astype(F32)
    bias_col = jnp.pad(gate_bias, (0, LANES - n_gate)).reshape(1, LANES)
    bias_row = gate_bias.reshape(n_gate, 1)

    x1, hn, gates_col, gates_row = _ffn(x.reshape(n, d), norm_ffn1[l], ffn1_w1[l], ffn1_w3[l], ffn1_w2[l],
                                        norm_mix[l], w_in_t)

    proj = _in_proj(hn, w_in_t)
    x2 = _mix(x1, proj, gates_col, gates_row, mlstm_conv_w[l], mlstm_conv_b[l].reshape(1, -1),
              bias_col, bias_row, mlstm_head_norm[l].reshape(1, -1), hgrn_lb_logits,
              hgrn_head_norm[l].reshape(1, -1), bf(w_proj_m[l]), bf(w_proj_h[l]), bf(w_out[l]), seq_len=t)

    m_len = mem.shape[1]
    kv = _mem_kv(mem.reshape(b * m_len, d), norm_mem[l], bf(xattn_wkv[l])).reshape(b, m_len, 2 * d)
    x3 = _xattn(x2.reshape(b, t, d), norm_xattn[l], bf(xattn_wq[l]), kv, bf(xattn_wo[l]))

    (out,) = _ffn(x3.reshape(n, d), norm_ffn2[l], ffn2_w1[l], ffn2_w3[l], ffn2_w2[l], norm_final, tf=512)
    return out.reshape(b, t, d)
```

```python
import functools

import jax
import jax.numpy as jnp
from jax import lax
from jax.experimental import pallas as pl
from jax.experimental.pallas import tpu as pltpu

F32 = jnp.float32
BF16 = jnp.bfloat16
EPS = 1e-6

D_MODEL = 2048
D_FF = 5632
MLSTM_HEADS = 4
MLSTM_QK = 128
MLSTM_V = 256
CONV_WIDTH = 4
HGRN_HEADS = 8
HGRN_DK = 128
HGRN_DV = 128
XATTN_HEADS = 4
XATTN_HEAD_DIM = D_MODEL // XATTN_HEADS

MLSTM_QK_W = MLSTM_HEADS * MLSTM_QK
MLSTM_V_W = MLSTM_HEADS * MLSTM_V
HGRN_W = HGRN_HEADS * HGRN_DK

VMEM_LIMIT_BYTES = 60000 * 1024
LANES = 128

COL_GM = 0
COL_GH = COL_GM + D_MODEL
COL_MQ = COL_GH + D_MODEL
COL_MK = COL_MQ + MLSTM_QK_W
COL_MV = COL_MK + MLSTM_QK_W
COL_MO = COL_MV + MLSTM_V_W
COL_HQ = COL_MO + MLSTM_V_W
COL_HF = COL_HQ + HGRN_W
COL_HI = COL_HF + HGRN_W
COL_HG = COL_HI + HGRN_W
PROJ_W = COL_HG + HGRN_W

MIX_GATE_ROW = 2 * MLSTM_QK_W + 2 * MLSTM_V_W
MIX_HGRN_ROW = MIX_GATE_ROW + 2 * MLSTM_HEADS
MIX_MERGE_ROW = MIX_HGRN_ROW + 4 * HGRN_W

MIX_TB = 128
HGRN_NUM_LEVELS = MIX_TB.bit_length() - 1
MIX_SUBBLOCKS = 2
MERGE_GATE_COLS = 512
MERGE_OUT_COLS = 256
HGRN_COARSE_LEVELS = tuple(1 << i for i in range(3, HGRN_NUM_LEVELS))


def _rms(x, g):
    return x * lax.rsqrt(jnp.mean(x * x, axis=-1, keepdims=True) + EPS) * g


def _sigmoid(x):
    return 1.0 / (1.0 + jnp.exp(-x))


def _log_sigmoid(x):
    return jnp.minimum(x, 0.0) - jnp.log(1.0 + jnp.exp(-jnp.abs(x)))


def _params(*sem):
    return pltpu.CompilerParams(dimension_semantics=sem, vmem_limit_bytes=VMEM_LIMIT_BYTES)


def _ffn_kernel(*refs, mode):
    if mode == "mix":
        (x_ref, g_ref, w1_ref, w3_ref, w2_ref, gn_ref, wg_ref,
         o_ref, hn_ref, gcol_ref, grow_ref, h_sc) = refs
    else:
        x_ref, g_ref, w1_ref, w3_ref, w2_ref, gn_ref, o_ref, h_sc = refs
    j = pl.program_id(1)

    @pl.when(j == 0)
    def _():
        h_sc[...] = _rms(x_ref[...], g_ref[...]).astype(BF16)
        o_ref[...] = jnp.zeros_like(o_ref)

    h = h_sc[...]
    a = jnp.dot(h, w1_ref[...].astype(BF16), preferred_element_type=F32)
    b = jnp.dot(h, w3_ref[...].astype(BF16), preferred_element_type=F32)
    act = (a * _sigmoid(a) * b).astype(BF16)
    o_ref[...] += jnp.dot(act, w2_ref[...].astype(BF16), preferred_element_type=F32)

    @pl.when(j == pl.num_programs(1) - 1)
    def _():
        y = x_ref[...] + 0.5 * o_ref[...]
        if mode == "final":
            o_ref[...] = _rms(y, gn_ref[...])
        else:
            o_ref[...] = y
            hn = _rms(y, gn_ref[...]).astype(BF16)
            hn_ref[...] = hn
            nt_dims = (((1,), (1,)), ((), ()))
            wg = wg_ref[...].astype(BF16)
            wg_pad = jnp.concatenate([wg, jnp.zeros((LANES - wg.shape[0], wg.shape[1]), BF16)], axis=0)
            gcol_ref[...] = lax.dot_general(hn, wg_pad, nt_dims, preferred_element_type=F32)
            grow_ref[...] = lax.dot_general(wg, hn, nt_dims, preferred_element_type=F32)


def _ffn(x2d, g, w1, w3, w2, g_next, w_gates_t=None, *, tm=1024, tf=256):
    n, d = x2d.shape
    f = w1.shape[1]
    mode = "final" if w_gates_t is None else "mix"
    in_specs = [
        pl.BlockSpec((tm, d), lambda i, j: (i, 0), pipeline_mode=pl.Buffered(1)),
        pl.BlockSpec((1, d), lambda i, j: (0, 0)),
        pl.BlockSpec((d, tf), lambda i, j: (0, j)),
        pl.BlockSpec((d, tf), lambda i, j: (0, j)),
        pl.BlockSpec((tf, d), lambda i, j: (j, 0)),
        pl.BlockSpec((1, d), lambda i, j: (0, 0)),
    ]
    args = [x2d, g.reshape(1, d), w1, w3, w2, g_next.reshape(1, d)]
    out_mode = {} if tf <= 256 else dict(pipeline_mode=pl.Buffered(1))
    out_specs = [pl.BlockSpec((tm, d), lambda i, j: (i, 0), **out_mode)]
    out_shape = [jax.ShapeDtypeStruct((n, d), F32)]
    if mode == "mix":
        ng = 2 * MLSTM_HEADS
        in_specs.append(pl.BlockSpec((ng, d), lambda i, j: (MIX_GATE_ROW // ng, 0)))
        args.append(w_gates_t)
        out_specs += [
            pl.BlockSpec((tm, d), lambda i, j: (i, 0)),
            pl.BlockSpec((tm, LANES), lambda i, j: (i, 0)),
            pl.BlockSpec((ng, tm), lambda i, j: (0, i)),
        ]
        out_shape += [
            jax.ShapeDtypeStruct((n, d), BF16),
            jax.ShapeDtypeStruct((n, LANES), F32),
            jax.ShapeDtypeStruct((ng, n), F32),
        ]
    return pl.pallas_call(
        functools.partial(_ffn_kernel, mode=mode),
        grid=(n // tm, f // tf),
        in_specs=in_specs,
        out_specs=out_specs,
        out_shape=out_shape,
        scratch_shapes=[pltpu.VMEM((tm, d), BF16)],
        compiler_params=_params("parallel", "arbitrary"),
        name="ffn_" + mode,
    )(*args)


def _in_proj_kernel(h_ref, wt_ref, p_ref, w_sc):
    @pl.when(pl.program_id(1) == 0)
    def _():
        w_sc[...] = wt_ref[...].astype(BF16)

    p_ref[...] = lax.dot_general(h_ref[...], w_sc[...], (((1,), (1,)), ((), ())), preferred_element_type=F32)


def _proj_row_offset(j, tn):
    n_gate_tiles = 2 * D_MODEL // tn
    n_mlstm_tiles = MIX_GATE_ROW // tn
    off8 = jnp.where(j < n_gate_tiles, MIX_MERGE_ROW // 8 + j * (tn // 8),
                     jnp.where(j < n_gate_tiles + n_mlstm_tiles, (j - n_gate_tiles) * (tn // 8),
                               MIX_HGRN_ROW // 8 + (j - n_gate_tiles - n_mlstm_tiles) * (tn // 8)))
    return off8 * 8


def _in_proj(hn, w_t, *, tm=1024, tn=1024):
    n, d = hn.shape
    return pl.pallas_call(
        _in_proj_kernel,
        grid=(PROJ_W // tn, n // tm),
        in_specs=[
            pl.BlockSpec((tm, d), lambda j, i: (i, 0)),
            pl.BlockSpec((pl.Element(tn), pl.Element(d)), lambda j, i: (_proj_row_offset(j, tn), 0)),
        ],
        out_specs=pl.BlockSpec((tm, tn), lambda j, i: (i, j)),
        out_shape=jax.ShapeDtypeStruct((n, PROJ_W), F32),
        scratch_shapes=[pltpu.VMEM((tn, d), BF16)],
        compiler_params=_params("arbitrary", "arbitrary"),
        name="in_proj",
    )(hn, w_t)


def _mlstm_block(q_ref, k_ref, v_ref, o_ref, gcol_ref, grow_ref, cw_ref, cb_ref, bcol_ref, brow_ref,
                 hn_ref, y_ref, xpad_sc, c_sc, n_sc, m_sc, before_head):
    L = MIX_TB
    H = MLSTM_HEADS

    xpad_sc[8:8 + L, 0:MLSTM_QK_W] = q_ref[...]
    xpad_sc[8:8 + L, MLSTM_QK_W:2 * MLSTM_QK_W] = k_ref[...]
    acc = jnp.broadcast_to(cb_ref[...], (L, 2 * MLSTM_QK_W))
    for j in range(CONV_WIDTH):
        off = 8 - (CONV_WIDTH - 1) + j
        acc = acc + cw_ref[j:j + 1, :] * xpad_sc[off:off + L, :]
    xpad_sc[0:8, :] = xpad_sc[L:L + 8, :]
    qk = acc * _sigmoid(acc)

    gcol = gcol_ref[...] + bcol_ref[...]
    grow = grow_ref[...] + brow_ref[...]
    fcol = _log_sigmoid(gcol)
    frow = _log_sigmoid(grow)
    r_i = lax.broadcasted_iota(jnp.int32, (L, L), 0)
    c_i = lax.broadcasted_iota(jnp.int32, (L, L), 1)
    causal = r_i >= c_i
    tril = jnp.where(causal, 1.0, 0.0).astype(F32)
    triu = jnp.where(r_i <= c_i, 1.0, 0.0).astype(F32)
    bcol = jnp.dot(tril, fcol, preferred_element_type=F32, precision=lax.Precision.HIGHEST)
    brow = jnp.dot(frow, triu, preferred_element_type=F32, precision=lax.Precision.HIGHEST)

    for h in range(H):
        before_head[h]()
        b_c = bcol[:, H + h:H + h + 1]
        i_c = gcol[:, h:h + 1]
        b_r = brow[H + h:H + h + 1, :]
        i_r = grow[h:h + 1, :]
        m_prev = m_sc[h:h + 1, 0:1]

        d_log = jnp.where(causal, b_c - (b_r - i_r), -jnp.inf)
        inter_log = b_c + m_prev
        m_t = jnp.maximum(jnp.max(d_log, axis=1, keepdims=True), inter_log)

        q_h = qk[:, h * MLSTM_QK:(h + 1) * MLSTM_QK]
        k_h = qk[:, MLSTM_QK_W + h * MLSTM_QK:MLSTM_QK_W + (h + 1) * MLSTM_QK] * (MLSTM_QK ** -0.5)
        v_h = v_ref[:, h * MLSTM_V:(h + 1) * MLSTM_V].astype(BF16)
        q_b = q_h.astype(BF16)

        s = lax.dot_general(q_b, k_h.astype(BF16), (((1,), (1,)), ((), ())), preferred_element_type=F32)
        s = s * jnp.exp(d_log - m_t)
        w_inter = jnp.exp(inter_log - m_t)
        c_h = c_sc[h]
        n_h = n_sc[h:h + 1, :]
        num = (jnp.dot(s.astype(BF16), v_h, preferred_element_type=F32)
               + w_inter * jnp.dot(q_b, c_h.astype(BF16), preferred_element_type=F32))
        den = jnp.sum(s, axis=1, keepdims=True) + w_inter * jnp.sum(q_h * n_h, axis=1, keepdims=True)
        hh = num / jnp.maximum(jnp.abs(den), jnp.exp(-m_t))

        hn = hh * lax.rsqrt(jnp.mean(hh * hh, axis=-1, keepdims=True) + EPS)
        hn = hn * hn_ref[:, h * MLSTM_V:(h + 1) * MLSTM_V]
        y = hn * _sigmoid(o_ref[:, h * MLSTM_V:(h + 1) * MLSTM_V])
        y_ref[:, h * MLSTM_V:(h + 1) * MLSTM_V] = y.astype(y_ref.dtype)

        b_last = b_c[L - 1:L, :]
        a_log = b_last - b_c + i_c
        m_new = jnp.maximum(b_last + m_prev, jnp.max(a_log, axis=0, keepdims=True))
        w_a = jnp.exp(a_log - m_new)
        decay = jnp.exp(b_last + m_prev - m_new)
        kw = k_h * w_a
        c_sc[h] = decay * c_h + lax.dot_general(kw.astype(BF16), v_h, (((0,), (0,)), ((), ())),
                                                preferred_element_type=F32)
        n_sc[h:h + 1, :] = decay * n_h + jnp.sum(kw, axis=0, keepdims=True)
        m_sc[h:h + 1, :] = jnp.broadcast_to(m_new, (1, LANES))


def _hgrn_block(q_ref, f_ref, v_ref, og_ref, lbl_ref, hn_ref, y_ref, g_sc, k_sc, q_sc, gr_sc, st_sc,
                before_head):
    TB = MIX_TB
    H = HGRN_HEADS
    DK = HGRN_DK
    W = H * DK
    NT = TB // 8

    lbl = lbl_ref[...]
    lmax = jnp.max(lbl, axis=0, keepdims=True)
    le = jnp.exp(lbl - lmax)
    lb = le[1:2, :] / jnp.sum(le, axis=0, keepdims=True)

    f = lb + (1.0 - lb) * _sigmoid(f_ref[...])
    k_sc[...] = 1.0 - f
    qraw = q_ref[...]
    q_sc[...] = qraw * _sigmoid(qraw) * (DK ** -0.5)
    r_i = lax.broadcasted_iota(jnp.int32, (TB, TB), 0)
    c_i = lax.broadcasted_iota(jnp.int32, (TB, TB), 1)
    tri = jnp.where(r_i >= c_i, 1.0, 0.0).astype(F32)
    g = jnp.dot(tri, jnp.log(f), preferred_element_type=F32, precision=lax.Precision.HIGHEST)
    g_sc[...] = g

    g3 = g.reshape(NT, 8, W)
    sub = lax.broadcasted_iota(jnp.int32, (NT, 8, W), 1)
    bit0 = (sub & 1) != 0
    bit1 = (sub & 2) != 0
    bit2 = (sub & 4) != 0
    last2 = jnp.where(bit0, g3, pltpu.roll(g3, 7, 1))
    last4 = jnp.where(bit1, last2, pltpu.roll(last2, 6, 1))
    last4_r = pltpu.roll(last4, 4, 1)
    last8 = jnp.where(bit2, last4, last4_r).reshape(TB, W)
    gr_sc[0] = jnp.where(bit0, pltpu.roll(g3, 1, 1), g3).reshape(TB, W)
    gr_sc[1] = jnp.where(bit1, pltpu.roll(last2, 2, 1), last2).reshape(TB, W)
    gr_sc[2] = jnp.where(bit2, last4_r, last4).reshape(TB, W)
    for li, lvl in enumerate(HGRN_COARSE_LEVELS):
        groups = []
        for base in range(0, TB, 2 * lvl):
            src = last8[base + lvl - 8:base + lvl, :]
            groups.extend([src] * (2 * lvl // 8))
        gr_sc[3 + li] = jnp.concatenate(groups, axis=0)

    x_i = r_i ^ c_i
    nt_dims = (((1,), (1,)), ((), ()))
    for h in range(H):
        before_head[h]()
        cols = slice(h * DK, (h + 1) * DK)
        q_h = q_sc[:, cols]
        k_h = k_sc[:, cols]
        g_h = g_sc[:, cols]
        v_h = v_ref[:, cols].astype(BF16)

        a = lax.dot_general(q_h.astype(BF16), k_h.astype(BF16), nt_dims, preferred_element_type=F32)
        for li in range(HGRN_NUM_LEVELS):
            e = jnp.exp(-jnp.abs(g_h - gr_sc[li, :, cols]))
            p = lax.dot_general((q_h * e).astype(BF16), (k_h * e).astype(BF16), nt_dims,
                                preferred_element_type=F32)
            a = jnp.where(x_i >= (1 << li), p, a)
        a = jnp.where(r_i >= c_i, a, 0.0)
        o = jnp.dot(a.astype(BF16), v_h, preferred_element_type=F32)

        st = st_sc[h]
        g_last = g_h[TB - 1:TB, :]
        qd = (q_h * jnp.exp(g_h)).astype(BF16)
        o = o + lax.dot_general(qd, st.astype(BF16), nt_dims, preferred_element_type=F32)
        kd = (k_h * jnp.exp(g_last - g_h)).astype(BF16)
        st_sc[h] = jnp.exp(g_last) * st + lax.dot_general(v_h, kd, (((0,), (0,)), ((), ())),
                                                          preferred_element_type=F32)

        on = o * lax.rsqrt(jnp.mean(o * o, axis=-1, keepdims=True) + EPS) * hn_ref[:, cols]
        og = og_ref[:, cols]
        y_ref[:, cols] = (on * (og * _sigmoid(og))).astype(y_ref.dtype)


def _mix_kernel(mq_ref, mk_ref, mv_ref, mo_ref, gcol_ref, grow_ref, cw_ref, cb_ref, bcol_ref, brow_ref,
                mhn_ref, hq_ref, hf_ref, hi_ref, hg_ref, lbl_ref, hhn_ref,
                x_ref, gm_ref, gh_ref, wm_ref, wh_ref, wo_ref,
                out_ref,
                ym_sc, yh_sc, merged_sc, xpad_sc, c_sc, n_sc, m_sc, g_sc, k_sc, q_sc, gr_sc, st_sc,
                *, blocks_per_seq):
    s = pl.program_id(0)
    cur = s % 2
    prv = 1 - cur

    @pl.when(s == 0)
    def _():
        ym_sc[...] = jnp.zeros_like(ym_sc)
        yh_sc[...] = jnp.zeros_like(yh_sc)

    @pl.when(s % blocks_per_seq == 0)
    def _():
        xpad_sc[0:8, :] = jnp.zeros((8, 2 * MLSTM_QK_W), F32)
        c_sc[...] = jnp.zeros_like(c_sc)
        n_sc[...] = jnp.zeros_like(n_sc)
        m_sc[...] = jnp.zeros_like(m_sc)
        st_sc[...] = jnp.zeros_like(st_sc)

    def gate_piece(c):
        def run():
            cols = slice(c * MERGE_GATE_COLS, (c + 1) * MERGE_GATE_COLS)
            pm = jnp.dot(ym_sc[prv], wm_ref[:, cols], preferred_element_type=F32)
            ph = jnp.dot(yh_sc[prv], wh_ref[:, cols], preferred_element_type=F32)
            merged = _sigmoid(gm_ref[:, cols]) * pm + _sigmoid(gh_ref[:, cols]) * ph
            merged_sc[:, cols] = merged.astype(BF16)
        return run

    def out_piece(c):
        def run():
            cols = slice(c * MERGE_OUT_COLS, (c + 1) * MERGE_OUT_COLS)
            out_ref[:, cols] = x_ref[:, cols] + jnp.dot(merged_sc[...], wo_ref[:, cols],
                                                        preferred_element_type=F32)
        return run

    pieces = ([gate_piece(c) for c in range(D_MODEL // MERGE_GATE_COLS)]
              + [out_piece(c) for c in range(D_MODEL // MERGE_OUT_COLS)])
    n_heads = MLSTM_HEADS + HGRN_HEADS
    n_slots = MIX_SUBBLOCKS * n_heads
    stride = n_slots // len(pieces)
    hooks = [pieces[i // stride] if i % stride == 0 else (lambda: None) for i in range(n_slots)]

    for sub in range(MIX_SUBBLOCKS):
        rows = pl.ds(sub * MIX_TB, MIX_TB)
        sub_hooks = hooks[sub * n_heads:(sub + 1) * n_heads]
        _mlstm_block(mq_ref.at[rows], mk_ref.at[rows], mv_ref.at[rows], mo_ref.at[rows], gcol_ref.at[rows],
                     grow_ref.at[:, rows], cw_ref, cb_ref, bcol_ref, brow_ref, mhn_ref,
                     ym_sc.at[cur, rows], xpad_sc, c_sc, n_sc, m_sc, sub_hooks[:MLSTM_HEADS])
        _hgrn_block(hq_ref.at[rows], hf_ref.at[rows], hi_ref.at[rows], hg_ref.at[rows], lbl_ref, hhn_ref,
                    yh_sc.at[cur, rows], g_sc, k_sc, q_sc, gr_sc, st_sc, sub_hooks[MLSTM_HEADS:])


def _const_spec(shape):
    return pl.BlockSpec(shape, lambda *_: (0,) * len(shape), pipeline_mode=pl.Buffered(1))


def _mix(x2d, proj, gates_col, gates_row, conv_w, conv_b, bias_col, bias_row, m_head_norm,
         lb_logits, h_head_norm, w_m, w_h, w_o, *, seq_len):
    n, d = x2d.shape
    tb = MIX_TB * MIX_SUBBLOCKS
    nblk = n // tb
    qw, vw, hw = MLSTM_QK_W, MLSTM_V_W, HGRN_W

    def cur(col_block):
        return lambda s: (jnp.minimum(s, nblk - 1), col_block)

    def prev(col_block):
        return lambda s: (jnp.maximum(s - 1, 0), col_block)

    return pl.pallas_call(
        functools.partial(_mix_kernel, blocks_per_seq=seq_len // tb),
        grid=(nblk + 1,),
        in_specs=[
            pl.BlockSpec((tb, qw), cur(COL_MQ // qw)),
            pl.BlockSpec((tb, qw), cur(COL_MK // qw)),
            pl.BlockSpec((tb, vw), cur(COL_MV // vw)),
            pl.BlockSpec((tb, vw), cur(COL_MO // vw)),
            pl.BlockSpec((tb, LANES), cur(0)),
            pl.BlockSpec((2 * MLSTM_HEADS, tb), lambda s: (0, jnp.minimum(s, nblk - 1))),
            _const_spec((CONV_WIDTH, 2 * qw)),
            _const_spec((1, 2 * qw)),
            _const_spec((1, LANES)),
            _const_spec((2 * MLSTM_HEADS, 1)),
            _const_spec((1, vw)),
            pl.BlockSpec((tb, hw), cur(COL_HQ // hw)),
            pl.BlockSpec((tb, hw), cur(COL_HF // hw)),
            pl.BlockSpec((tb, hw), cur(COL_HI // hw)),
            pl.BlockSpec((tb, hw), cur(COL_HG // hw)),
            _const_spec((2, hw)),
            _const_spec((1, hw)),
            pl.BlockSpec((tb, d), prev(0)),
            pl.BlockSpec((tb, d), prev(COL_GM // d)),
            pl.BlockSpec((tb, d), prev(COL_GH // d)),
            _const_spec(w_m.shape),
            _const_spec(w_h.shape),
            _const_spec(w_o.shape),
        ],
        out_specs=pl.BlockSpec((tb, d), prev(0)),
        out_shape=jax.ShapeDtypeStruct((n, d), F32),
        scratch_shapes=[
            pltpu.VMEM((2, tb, vw), BF16),
            pltpu.VMEM((2, tb, hw), BF16),
            pltpu.VMEM((tb, d), BF16),
            pltpu.VMEM((MIX_TB + 8, 2 * qw), F32),
            pltpu.VMEM((MLSTM_HEADS, MLSTM_QK, MLSTM_V), F32),
            pltpu.VMEM((8, MLSTM_QK), F32),
            pltpu.VMEM((8, LANES), F32),
            pltpu.VMEM((MIX_TB, hw), F32),
            pltpu.VMEM((MIX_TB, hw), F32),
            pltpu.VMEM((MIX_TB, hw), F32),
            pltpu.VMEM((HGRN_NUM_LEVELS, MIX_TB, hw), F32),
            pltpu.VMEM((HGRN_HEADS, HGRN_DV, HGRN_DK), F32),
        ],
        compiler_params=_params("arbitrary"),
        name="mix",
    )(proj, proj, proj, proj, gates_col, gates_row, conv_w, conv_b, bias_col, bias_row, m_head_norm,
      proj, proj, proj, proj, lb_logits, h_head_norm, x2d, proj, proj, w_m, w_h, w_o)


def _mem_kv_kernel(m_ref, g_ref, w_ref, o_ref):
    m = _rms(m_ref[...], g_ref[...]).astype(BF16)
    o_ref[...] = jnp.dot(m, w_ref[...], preferred_element_type=F32).astype(o_ref.dtype)


def _mem_kv(mem2d, g, w_kv, *, tn=1024):
    n, d = mem2d.shape
    nw = w_kv.shape[1]
    return pl.pallas_call(
        _mem_kv_kernel,
        grid=(nw // tn,),
        in_specs=[
            pl.BlockSpec((n, d), lambda j: (0, 0)),
            pl.BlockSpec((1, d), lambda j: (0, 0)),
            pl.BlockSpec((d, tn), lambda j: (0, j)),
        ],
        out_specs=pl.BlockSpec((n, tn), lambda j: (0, j)),
        out_shape=jax.ShapeDtypeStruct((n, nw), BF16),
        compiler_params=_params("parallel"),
        name="mem_kv",
    )(mem2d, g.reshape(1, d), w_kv)


def _xattn_kernel(x_ref, g_ref, wq_ref, k_ref, v_ref, wo_ref, o_ref):
    x = x_ref[0]
    h = _rms(x, g_ref[...]).astype(BF16)
    q = jnp.dot(h, wq_ref[...], preferred_element_type=F32)
    outs = []
    for hd in range(XATTN_HEADS):
        cols = slice(hd * XATTN_HEAD_DIM, (hd + 1) * XATTN_HEAD_DIM)
        s = lax.dot_general(q[:, cols].astype(BF16), k_ref[0, :, cols], (((1,), (1,)), ((), ())),
                            preferred_element_type=F32) * (XATTN_HEAD_DIM ** -0.5)
        s = s - jnp.max(s, axis=-1, keepdims=True)
        e = jnp.exp(s)
        p = e / jnp.sum(e, axis=-1, keepdims=True)
        outs.append(jnp.dot(p.astype(BF16), v_ref[0, :, cols], preferred_element_type=F32).astype(BF16))
    o = jnp.concatenate(outs, axis=-1)
    o_ref[0] = x + jnp.dot(o, wo_ref[...], preferred_element_type=F32)


def _xattn(x, g, w_q, kv, w_o, *, tm=512):
    b, t, d = x.shape
    m = kv.shape[1]
    return pl.pallas_call(
        _xattn_kernel,
        grid=(b, t // tm),
        in_specs=[
            pl.BlockSpec((1, tm, d), lambda i, c: (i, c, 0)),
            _const_spec((1, d)),
            _const_spec(w_q.shape),
            pl.BlockSpec((1, m, d), lambda i, c: (i, 0, 0)),
            pl.BlockSpec((1, m, d), lambda i, c: (i, 0, 1)),
            _const_spec(w_o.shape),
        ],
        out_specs=pl.BlockSpec((1, tm, d), lambda i, c: (i, c, 0)),
        out_shape=jax.ShapeDtypeStruct((b, t, d), F32),
        compiler_params=_params("parallel", "parallel"),
        name="xattn",
    )(x, g.reshape(1, d), w_q, kv, kv, w_o)


def kernel(x, mem, norm_ffn1, ffn1_w1, ffn1_w3, ffn1_w2, norm_mix, w_in, mlstm_conv_w, mlstm_conv_b,
           mlstm_ig_bias, mlstm_fg_bias, mlstm_head_norm, hgrn_lb_logits, hgrn_head_norm, w_proj_m,
           w_proj_h, w_out, norm_xattn, norm_mem, xattn_wq, xattn_wkv, xattn_wo, norm_ffn2, ffn2_w1,
           ffn2_w3, ffn2_w2, norm_final):
    b, t, d = x.shape
    depth = norm_ffn1.shape[0]
    assert depth == 1 and hgrn_lb_logits.shape[0] == 2
    n = b * t
    l = 0
    bf = lambda w: w.astype(BF16)

    w_in_t = jnp.swapaxes(w_in[l], 0, 1)
    n_gate = 2 * MLSTM_HEADS
    gate_bias = jnp.concatenate([mlstm_ig_bias[l], mlstm_fg_bias[l]]).astype(F32)
    bias_col = jnp.pad(gate_bias, (0, LANES - n_gate)).reshape(1, LANES)
    bias_row = gate_bias.reshape(n_gate, 1)

    x1, hn, gates_col, gates_row = _ffn(x.reshape(n, d), norm_ffn1[l], ffn1_w1[l], ffn1_w3[l], ffn1_w2[l],
                                        norm_mix[l], w_in_t)

    proj = _in_proj(hn, w_in_t)
    x2 = _mix(x1, proj, gates_col, gates_row, mlstm_conv_w[l], mlstm_conv_b[l].reshape(1, -1),
              bias_col, bias_row, mlstm_head_norm[l].reshape(1, -1), hgrn_lb_logits,
              hgrn_head_norm[l].reshape(1, -1), bf(w_proj_m[l]), bf(w_proj_h[l]), bf(w_out[l]), seq_len=t)

    m_len = mem.shape[1]
    kv = _mem_kv(mem.reshape(b * m_len, d), norm_mem[l], bf(xattn_wkv[l])).reshape(b, m_len, 2 * d)
    x3 = _xattn(x2.reshape(b, t, d), norm_xattn[l], bf(xattn_wq[l]), kv, bf(xattn_wo[l]))

    (out,) = _ffn(x3.reshape(n, d), norm_ffn2[l], ffn2_w1[l], ffn2_w3[l], ffn2_w2[l], norm_final, tf=512)
    return out.reshape(b, t, d)
```

```python
import functools

import jax
import jax.numpy as jnp
from jax import lax
from jax.experimental import pallas as pl
from jax.experimental.pallas import tpu as pltpu

F32 = jnp.float32
BF16 = jnp.bfloat16
EPS = 1e-6

D_MODEL = 2048
D_FF = 5632
MLSTM_HEADS = 4
MLSTM_QK = 128
MLSTM_V = 256
CONV_WIDTH = 4
HGRN_HEADS = 8
HGRN_DK = 128
HGRN_DV = 128
XATTN_HEADS = 4
XATTN_HEAD_DIM = D_MODEL // XATTN_HEADS

MLSTM_QK_W = MLSTM_HEADS * MLSTM_QK
MLSTM_V_W = MLSTM_HEADS * MLSTM_V
HGRN_W = HGRN_HEADS * HGRN_DK

VMEM_LIMIT_BYTES = 60000 * 1024
LANES = 128

COL_GM = 0
COL_GH = COL_GM + D_MODEL
COL_MQ = COL_GH + D_MODEL
COL_MK = COL_MQ + MLSTM_QK_W
COL_MV = COL_MK + MLSTM_QK_W
COL_MO = COL_MV + MLSTM_V_W
COL_HQ = COL_MO + MLSTM_V_W
COL_HF = COL_HQ + HGRN_W
COL_HI = COL_HF + HGRN_W
COL_HG = COL_HI + HGRN_W
PROJ_W = COL_HG + HGRN_W

MIX_GATE_ROW = 2 * MLSTM_QK_W + 2 * MLSTM_V_W
MIX_HGRN_ROW = MIX_GATE_ROW + 2 * MLSTM_HEADS
MIX_MERGE_ROW = MIX_HGRN_ROW + 4 * HGRN_W

FFN_SUB_COLS = 256

MIX_TB = 128
HGRN_NUM_LEVELS = MIX_TB.bit_length() - 1
MIX_SUBBLOCKS = 2
MERGE_GATE_COLS = 512
MERGE_OUT_COLS = 256
HGRN_COARSE_LEVELS = tuple(1 << i for i in range(3, HGRN_NUM_LEVELS))


def _rms(x, g):
    return x * lax.rsqrt(jnp.mean(x * x, axis=-1, keepdims=True) + EPS) * g


def _sigmoid(x):
    return 1.0 / (1.0 + jnp.exp(-x))


def _log_sigmoid(x):
    return jnp.minimum(x, 0.0) - jnp.log(1.0 + jnp.exp(-jnp.abs(x)))


def _params(*sem):
    return pltpu.CompilerParams(dimension_semantics=sem, vmem_limit_bytes=VMEM_LIMIT_BYTES)


def _ffn_kernel(*refs, mode):
    if mode == "mix":
        (x_ref, g_ref, w1_ref, w3_ref, w2_ref, gn_ref, wg_ref,
         o_ref, hn_ref, gcol_ref, grow_ref, h_sc) = refs
    else:
        x_ref, g_ref, w1_ref, w3_ref, w2_ref, gn_ref, o_ref, h_sc = refs
    j = pl.program_id(1)

    @pl.when(j == 0)
    def _():
        h_sc[...] = _rms(x_ref[...], g_ref[...]).astype(BF16)
        o_ref[...] = jnp.zeros_like(o_ref)

    h = h_sc[...]
    for c in range(w1_ref.shape[1] // FFN_SUB_COLS):
        cols = slice(c * FFN_SUB_COLS, (c + 1) * FFN_SUB_COLS)
        a = jnp.dot(h, w1_ref[:, cols].astype(BF16), preferred_element_type=F32)
        b = jnp.dot(h, w3_ref[:, cols].astype(BF16), preferred_element_type=F32)
        act = (a * _sigmoid(a) * b).astype(BF16)
        o_ref[...] += jnp.dot(act, w2_ref[cols, :].astype(BF16), preferred_element_type=F32)

    @pl.when(j == pl.num_programs(1) - 1)
    def _():
        y = x_ref[...] + 0.5 * o_ref[...]
        if mode == "final":
            o_ref[...] = _rms(y, gn_ref[...])
        else:
            o_ref[...] = y
            hn = _rms(y, gn_ref[...]).astype(BF16)
            hn_ref[...] = hn
            nt_dims = (((1,), (1,)), ((), ()))
            wg = wg_ref[...].astype(BF16)
            wg_pad = jnp.concatenate([wg, jnp.zeros((LANES - wg.shape[0], wg.shape[1]), BF16)], axis=0)
            gcol_ref[...] = lax.dot_general(hn, wg_pad, nt_dims, preferred_element_type=F32)
            grow_ref[...] = lax.dot_general(wg, hn, nt_dims, preferred_element_type=F32)


def _ffn(x2d, g, w1, w3, w2, g_next, w_gates_t=None, *, tm=1024, tf=512):
    n, d = x2d.shape
    f = w1.shape[1]
    mode = "final" if w_gates_t is None else "mix"
    in_specs = [
        pl.BlockSpec((tm, d), lambda i, j: (i, 0), pipeline_mode=pl.Buffered(1)),
        pl.BlockSpec((1, d), lambda i, j: (0, 0)),
        pl.BlockSpec((d, tf), lambda i, j: (0, j)),
        pl.BlockSpec((d, tf), lambda i, j: (0, j)),
        pl.BlockSpec((tf, d), lambda i, j: (j, 0)),
        pl.BlockSpec((1, d), lambda i, j: (0, 0)),
    ]
    args = [x2d, g.reshape(1, d), w1, w3, w2, g_next.reshape(1, d)]
    single = dict(pipeline_mode=pl.Buffered(1))
    out_specs = [pl.BlockSpec((tm, d), lambda i, j: (i, 0), **single)]
    out_shape = [jax.ShapeDtypeStruct((n, d), F32)]
    if mode == "mix":
        ng = 2 * MLSTM_HEADS
        in_specs.append(pl.BlockSpec((ng, d), lambda i, j: (MIX_GATE_ROW // ng, 0)))
        args.append(w_gates_t)
        out_specs += [
            pl.BlockSpec((tm, d), lambda i, j: (i, 0), **single),
            pl.BlockSpec((tm, LANES), lambda i, j: (i, 0)),
            pl.BlockSpec((ng, tm), lambda i, j: (0, i)),
        ]
        out_shape += [
            jax.ShapeDtypeStruct((n, d), BF16),
            jax.ShapeDtypeStruct((n, LANES), F32),
            jax.ShapeDtypeStruct((ng, n), F32),
        ]
    return pl.pallas_call(
        functools.partial(_ffn_kernel, mode=mode),
        grid=(n // tm, f // tf),
        in_specs=in_specs,
        out_specs=out_specs,
        out_shape=out_shape,
        scratch_shapes=[pltpu.VMEM((tm, d), BF16)],
        compiler_params=_params("parallel", "arbitrary"),
        name="ffn_" + mode,
    )(*args)


def _in_proj_kernel(h_ref, wt_ref, p_ref, w_sc):
    @pl.when(pl.program_id(1) == 0)
    def _():
        w_sc[...] = wt_ref[...].astype(BF16)

    p_ref[...] = lax.dot_general(h_ref[...], w_sc[...], (((1,), (1,)), ((), ())), preferred_element_type=F32)


def _proj_row_offset(j, tn):
    n_gate_tiles = 2 * D_MODEL // tn
    n_mlstm_tiles = MIX_GATE_ROW // tn
    off8 = jnp.where(j < n_gate_tiles, MIX_MERGE_ROW // 8 + j * (tn // 8),
                     jnp.where(j < n_gate_tiles + n_mlstm_tiles, (j - n_gate_tiles) * (tn // 8),
                               MIX_HGRN_ROW // 8 + (j - n_gate_tiles - n_mlstm_tiles) * (tn // 8)))
    return off8 * 8


def _in_proj(hn, w_t, *, tm=1024, tn=1024):
    n, d = hn.shape
    return pl.pallas_call(
        _in_proj_kernel,
        grid=(PROJ_W // tn, n // tm),
        in_specs=[
            pl.BlockSpec((tm, d), lambda j, i: (i, 0)),
            pl.BlockSpec((pl.Element(tn), pl.Element(d)), lambda j, i: (_proj_row_offset(j, tn), 0)),
        ],
        out_specs=pl.BlockSpec((tm, tn), lambda j, i: (i, j)),
        out_shape=jax.ShapeDtypeStruct((n, PROJ_W), F32),
        scratch_shapes=[pltpu.VMEM((tn, d), BF16)],
        compiler_params=_params("arbitrary", "arbitrary"),
        name="in_proj",
    )(hn, w_t)


def _mlstm_block(q_ref, k_ref, v_ref, o_ref, gcol_ref, grow_ref, cw_ref, cb_ref, bcol_ref, brow_ref,
                 hn_ref, y_ref, xpad_sc, c_sc, n_sc, m_sc, before_head):
    L = MIX_TB
    H = MLSTM_HEADS

    xpad_sc[8:8 + L, 0:MLSTM_QK_W] = q_ref[...]
    xpad_sc[8:8 + L, MLSTM_QK_W:2 * MLSTM_QK_W] = k_ref[...]
    acc = jnp.broadcast_to(cb_ref[...], (L, 2 * MLSTM_QK_W))
    for j in range(CONV_WIDTH):
        off = 8 - (CONV_WIDTH - 1) + j
        acc = acc + cw_ref[j:j + 1, :] * xpad_sc[off:off + L, :]
    xpad_sc[0:8, :] = xpad_sc[L:L + 8, :]
    qk = acc * _sigmoid(acc)

    gcol = gcol_ref[...] + bcol_ref[...]
    grow = grow_ref[...] + brow_ref[...]
    fcol = _log_sigmoid(gcol)
    frow = _log_sigmoid(grow)
    r_i = lax.broadcasted_iota(jnp.int32, (L, L), 0)
    c_i = lax.broadcasted_iota(jnp.int32, (L, L), 1)
    causal = r_i >= c_i
    tril = jnp.where(causal, 1.0, 0.0).astype(F32)
    triu = jnp.where(r_i <= c_i, 1.0, 0.0).astype(F32)
    bcol = jnp.dot(tril, fcol, preferred_element_type=F32, precision=lax.Precision.HIGHEST)
    brow = jnp.dot(frow, triu, preferred_element_type=F32, precision=lax.Precision.HIGHEST)

    for h in range(H):
        before_head[h]()
        b_c = bcol[:, H + h:H + h + 1]
        i_c = gcol[:, h:h + 1]
        b_r = brow[H + h:H + h + 1, :]
        i_r = grow[h:h + 1, :]
        m_prev = m_sc[h:h + 1, 0:1]

        d_log = jnp.where(causal, b_c - (b_r - i_r), -jnp.inf)
        inter_log = b_c + m_prev
        m_t = jnp.maximum(jnp.max(d_log, axis=1, keepdims=True), inter_log)

        q_h = qk[:, h * MLSTM_QK:(h + 1) * MLSTM_QK]
        k_h = qk[:, MLSTM_QK_W + h * MLSTM_QK:MLSTM_QK_W + (h + 1) * MLSTM_QK] * (MLSTM_QK ** -0.5)
        v_h = v_ref[:, h * MLSTM_V:(h + 1) * MLSTM_V].astype(BF16)
        q_b = q_h.astype(BF16)

        s = lax.dot_general(q_b, k_h.astype(BF16), (((1,), (1,)), ((), ())), preferred_element_type=F32)
        s = s * jnp.exp(d_log - m_t)
        w_inter = jnp.exp(inter_log - m_t)
        c_h = c_sc[h]
        n_h = n_sc[h:h + 1, :]
        num = (jnp.dot(s.astype(BF16), v_h, preferred_element_type=F32)
               + w_inter * jnp.dot(q_b, c_h.astype(BF16), preferred_element_type=F32))
        den = jnp.sum(s, axis=1, keepdims=True) + w_inter * jnp.sum(q_h * n_h, axis=1, keepdims=True)
        hh = num / jnp.maximum(jnp.abs(den), jnp.exp(-m_t))

        hn = hh * lax.rsqrt(jnp.mean(hh * hh, axis=-1, keepdims=True) + EPS)
        hn = hn * hn_ref[:, h * MLSTM_V:(h + 1) * MLSTM_V]
        y = hn * _sigmoid(o_ref[:, h * MLSTM_V:(h + 1) * MLSTM_V])
        y_ref[:, h * MLSTM_V:(h + 1) * MLSTM_V] = y.astype(y_ref.dtype)

        b_last = b_c[L - 1:L, :]
        a_log = b_last - b_c + i_c
        m_new = jnp.maximum(b_last + m_prev, jnp.max(a_log, axis=0, keepdims=True))
        w_a = jnp.exp(a_log - m_new)
        decay = jnp.exp(b_last + m_prev - m_new)
        kw = k_h * w_a
        c_sc[h] = decay * c_h + lax.dot_general(kw.astype(BF16), v_h, (((0,), (0,)), ((), ())),
                                                preferred_element_type=F32)
        n_sc[h:h + 1, :] = decay * n_h + jnp.sum(kw, axis=0, keepdims=True)
        m_sc[h:h + 1, :] = jnp.broadcast_to(m_new, (1, LANES))


def _hgrn_block(q_ref, f_ref, v_ref, og_ref, lbl_ref, hn_ref, y_ref, g_sc, k_sc, q_sc, gr_sc, st_sc,
                before_head):
    TB = MIX_TB
    H = HGRN_HEADS
    DK = HGRN_DK
    W = H * DK
    NT = TB // 8

    lbl = lbl_ref[...]
    lmax = jnp.max(lbl, axis=0, keepdims=True)
    le = jnp.exp(lbl - lmax)
    lb = le[1:2, :] / jnp.sum(le, axis=0, keepdims=True)

    f = lb + (1.0 - lb) * _sigmoid(f_ref[...])
    k_sc[...] = 1.0 - f
    qraw = q_ref[...]
    q_sc[...] = qraw * _sigmoid(qraw) * (DK ** -0.5)
    r_i = lax.broadcasted_iota(jnp.int32, (TB, TB), 0)
    c_i = lax.broadcasted_iota(jnp.int32, (TB, TB), 1)
    tri = jnp.where(r_i >= c_i, 1.0, 0.0).astype(F32)
    g = jnp.dot(tri, jnp.log(f), preferred_element_type=F32, precision=lax.Precision.HIGHEST)
    g_sc[...] = g

    g3 = g.reshape(NT, 8, W)
    sub = lax.broadcasted_iota(jnp.int32, (NT, 8, W), 1)
    bit0 = (sub & 1) != 0
    bit1 = (sub & 2) != 0
    bit2 = (sub & 4) != 0
    last2 = jnp.where(bit0, g3, pltpu.roll(g3, 7, 1))
    last4 = jnp.where(bit1, last2, pltpu.roll(last2, 6, 1))
    last4_r = pltpu.roll(last4, 4, 1)
    last8 = jnp.where(bit2, last4, last4_r).reshape(TB, W)
    gr_sc[0] = jnp.where(bit0, pltpu.roll(g3, 1, 1), g3).reshape(TB, W)
    gr_sc[1] = jnp.where(bit1, pltpu.roll(last2, 2, 1), last2).reshape(TB, W)
    gr_sc[2] = jnp.where(bit2, last4_r, last4).reshape(TB, W)
    for li, lvl in enumerate(HGRN_COARSE_LEVELS):
        groups = []
        for base in range(0, TB, 2 * lvl):
            src = last8[base + lvl - 8:base + lvl, :]
            groups.extend([src] * (2 * lvl // 8))
        gr_sc[3 + li] = jnp.concatenate(groups, axis=0)

    x_i = r_i ^ c_i
    nt_dims = (((1,), (1,)), ((), ()))
    for h in range(H):
        before_head[h]()
        cols = slice(h * DK, (h + 1) * DK)
        q_h = q_sc[:, cols]
        k_h = k_sc[:, cols]
        g_h = g_sc[:, cols]
        v_h = v_ref[:, cols].astype(BF16)

        a = lax.dot_general(q_h.astype(BF16), k_h.astype(BF16), nt_dims, preferred_element_type=F32)
        for li in range(HGRN_NUM_LEVELS):
            e = jnp.exp(-jnp.abs(g_h - gr_sc[li, :, cols]))
            p = lax.dot_general((q_h * e).astype(BF16), (k_h * e).astype(BF16), nt_dims,
                                preferred_element_type=F32)
            a = jnp.where(x_i >= (1 << li), p, a)
        a = jnp.where(r_i >= c_i, a, 0.0)
        o = jnp.dot(a.astype(BF16), v_h, preferred_element_type=F32)

        st = st_sc[h]
        g_last = g_h[TB - 1:TB, :]
        qd = (q_h * jnp.exp(g_h)).astype(BF16)
        o = o + lax.dot_general(qd, st.astype(BF16), nt_dims, preferred_element_type=F32)
        kd = (k_h * jnp.exp(g_last - g_h)).astype(BF16)
        st_sc[h] = jnp.exp(g_last) * st + lax.dot_general(v_h, kd, (((0,), (0,)), ((), ())),
                                                          preferred_element_type=F32)

        on = o * lax.rsqrt(jnp.mean(o * o, axis=-1, keepdims=True) + EPS) * hn_ref[:, cols]
        og = og_ref[:, cols]
        y_ref[:, cols] = (on * (og * _sigmoid(og))).astype(y_ref.dtype)


def _mix_kernel(mq_ref, mk_ref, mv_ref, mo_ref, gcol_ref, grow_ref, cw_ref, cb_ref, bcol_ref, brow_ref,
                mhn_ref, hq_ref, hf_ref, hi_ref, hg_ref, lbl_ref, hhn_ref,
                x_ref, gm_ref, gh_ref, wm_ref, wh_ref, wo_ref,
                out_ref,
                ym_sc, yh_sc, merged_sc, xpad_sc, c_sc, n_sc, m_sc, g_sc, k_sc, q_sc, gr_sc, st_sc,
                *, blocks_per_seq):
    s = pl.program_id(0)
    cur = s % 2
    prv = 1 - cur

    @pl.when(s == 0)
    def _():
        ym_sc[...] = jnp.zeros_like(ym_sc)
        yh_sc[...] = jnp.zeros_like(yh_sc)

    @pl.when(s % blocks_per_seq == 0)
    def _():
        xpad_sc[0:8, :] = jnp.zeros((8, 2 * MLSTM_QK_W), F32)
        c_sc[...] = jnp.zeros_like(c_sc)
        n_sc[...] = jnp.zeros_like(n_sc)
        m_sc[...] = jnp.zeros_like(m_sc)
        st_sc[...] = jnp.zeros_like(st_sc)

    def gate_piece(c):
        def run():
            cols = slice(c * MERGE_GATE_COLS, (c + 1) * MERGE_GATE_COLS)
            pm = jnp.dot(ym_sc[prv], wm_ref[:, cols], preferred_element_type=F32)
            ph = jnp.dot(yh_sc[prv], wh_ref[:, cols], preferred_element_type=F32)
            merged = _sigmoid(gm_ref[:, cols]) * pm + _sigmoid(gh_ref[:, cols]) * ph
            merged_sc[:, cols] = merged.astype(BF16)
        return run

    def out_piece(c):
        def run():
            cols = slice(c * MERGE_OUT_COLS, (c + 1) * MERGE_OUT_COLS)
            out_ref[:, cols] = x_ref[:, cols] + jnp.dot(merged_sc[...], wo_ref[:, cols],
                                                        preferred_element_type=F32)
        return run

    pieces = ([gate_piece(c) for c in range(D_MODEL // MERGE_GATE_COLS)]
              + [out_piece(c) for c in range(D_MODEL // MERGE_OUT_COLS)])
    n_heads = MLSTM_HEADS + HGRN_HEADS
    n_slots = MIX_SUBBLOCKS * n_heads
    stride = n_slots // len(pieces)
    hooks = [pieces[i // stride] if i % stride == 0 else (lambda: None) for i in range(n_slots)]

    for sub in range(MIX_SUBBLOCKS):
        rows = pl.ds(sub * MIX_TB, MIX_TB)
        sub_hooks = hooks[sub * n_heads:(sub + 1) * n_heads]
        _mlstm_block(mq_ref.at[rows], mk_ref.at[rows], mv_ref.at[rows], mo_ref.at[rows], gcol_ref.at[rows],
                     grow_ref.at[:, rows], cw_ref, cb_ref, bcol_ref, brow_ref, mhn_ref,
                     ym_sc.at[cur, rows], xpad_sc, c_sc, n_sc, m_sc, sub_hooks[:MLSTM_HEADS])
        _hgrn_block(hq_ref.at[rows], hf_ref.at[rows], hi_ref.at[rows], hg_ref.at[rows], lbl_ref, hhn_ref,
                    yh_sc.at[cur, rows], g_sc, k_sc, q_sc, gr_sc, st_sc, sub_hooks[MLSTM_HEADS:])


def _const_spec(shape):
    return pl.BlockSpec(shape, lambda *_: (0,) * len(shape), pipeline_mode=pl.Buffered(1))


def _mix(x2d, proj, gates_col, gates_row, conv_w, conv_b, bias_col, bias_row, m_head_norm,
         lb_logits, h_head_norm, w_m, w_h, w_o, *, seq_len):
    n, d = x2d.shape
    tb = MIX_TB * MIX_SUBBLOCKS
    nblk = n // tb
    qw, vw, hw = MLSTM_QK_W, MLSTM_V_W, HGRN_W

    def cur(col_block):
        return lambda s: (jnp.minimum(s, nblk - 1), col_block)

    def prev(col_block):
        return lambda s: (jnp.maximum(s - 1, 0), col_block)

    return pl.pallas_call(
        functools.partial(_mix_kernel, blocks_per_seq=seq_len // tb),
        grid=(nblk + 1,),
        in_specs=[
            pl.BlockSpec((tb, qw), cur(COL_MQ // qw)),
            pl.BlockSpec((tb, qw), cur(COL_MK // qw)),
            pl.BlockSpec((tb, vw), cur(COL_MV // vw)),
            pl.BlockSpec((tb, vw), cur(COL_MO // vw)),
            pl.BlockSpec((tb, LANES), cur(0)),
            pl.BlockSpec((2 * MLSTM_HEADS, tb), lambda s: (0, jnp.minimum(s, nblk - 1))),
            _const_spec((CONV_WIDTH, 2 * qw)),
            _const_spec((1, 2 * qw)),
            _const_spec((1, LANES)),
            _const_spec((2 * MLSTM_HEADS, 1)),
            _const_spec((1, vw)),
            pl.BlockSpec((tb, hw), cur(COL_HQ // hw)),
            pl.BlockSpec((tb, hw), cur(COL_HF // hw)),
            pl.BlockSpec((tb, hw), cur(COL_HI // hw)),
            pl.BlockSpec((tb, hw), cur(COL_HG // hw)),
            _const_spec((2, hw)),
            _const_spec((1, hw)),
            pl.BlockSpec((tb, d), prev(0)),
            pl.BlockSpec((tb, d), prev(COL_GM // d)),
            pl.BlockSpec((tb, d), prev(COL_GH // d)),
            _const_spec(w_m.shape),
            _const_spec(w_h.shape),
            _const_spec(w_o.shape),
        ],
        out_specs=pl.BlockSpec((tb, d), prev(0)),
        out_shape=jax.ShapeDtypeStruct((n, d), F32),
        scratch_shapes=[
            pltpu.VMEM((2, tb, vw), BF16),
            pltpu.VMEM((2, tb, hw), BF16),
            pltpu.VMEM((tb, d), BF16),
            pltpu.VMEM((MIX_TB + 8, 2 * qw), F32),
            pltpu.VMEM((MLSTM_HEADS, MLSTM_QK, MLSTM_V), F32),
            pltpu.VMEM((8, MLSTM_QK), F32),
            pltpu.VMEM((8, LANES), F32),
            pltpu.VMEM((MIX_TB, hw), F32),
            pltpu.VMEM((MIX_TB, hw), F32),
            pltpu.VMEM((MIX_TB, hw), F32),
            pltpu.VMEM((HGRN_NUM_LEVELS, MIX_TB, hw), F32),
            pltpu.VMEM((HGRN_HEADS, HGRN_DV, HGRN_DK), F32),
        ],
        compiler_params=_params("arbitrary"),
        name="mix",
    )(proj, proj, proj, proj, gates_col, gates_row, conv_w, conv_b, bias_col, bias_row, m_head_norm,
      proj, proj, proj, proj, lb_logits, h_head_norm, x2d, proj, proj, w_m, w_h, w_o)


def _mem_kv_kernel(m_ref, g_ref, w_ref, o_ref):
    m = _rms(m_ref[...], g_ref[...]).astype(BF16)
    o_ref[...] = jnp.dot(m, w_ref[...], preferred_element_type=F32).astype(o_ref.dtype)


def _mem_kv(mem2d, g, w_kv, *, tn=1024):
    n, d = mem2d.shape
    nw = w_kv.shape[1]
    return pl.pallas_call(
        _mem_kv_kernel,
        grid=(nw // tn,),
        in_specs=[
            pl.BlockSpec((n, d), lambda j: (0, 0)),
            pl.BlockSpec((1, d), lambda j: (0, 0)),
            pl.BlockSpec((d, tn), lambda j: (0, j)),
        ],
        out_specs=pl.BlockSpec((n, tn), lambda j: (0, j)),
        out_shape=jax.ShapeDtypeStruct((n, nw), BF16),
        compiler_params=_params("parallel"),
        name="mem_kv",
    )(mem2d, g.reshape(1, d), w_kv)


def _xattn_kernel(x_ref, g_ref, wq_ref, k_ref, v_ref, wo_ref, o_ref):
    x = x_ref[0]
    h = _rms(x, g_ref[...]).astype(BF16)
    q = jnp.dot(h, wq_ref[...], preferred_element_type=F32)
    outs = []
    for hd in range(XATTN_HEADS):
        cols = slice(hd * XATTN_HEAD_DIM, (hd + 1) * XATTN_HEAD_DIM)
        s = lax.dot_general(q[:, cols].astype(BF16), k_ref[0, :, cols], (((1,), (1,)), ((), ())),
                            preferred_element_type=F32) * (XATTN_HEAD_DIM ** -0.5)
        s = s - jnp.max(s, axis=-1, keepdims=True)
        e = jnp.exp(s)
        p = e / jnp.sum(e, axis=-1, keepdims=True)
        outs.append(jnp.dot(p.astype(BF16), v_ref[0, :, cols], preferred_element_type=F32).astype(BF16))
    o = jnp.concatenate(outs, axis=-1)
    o_ref[0] = x + jnp.dot(o, wo_ref[...], preferred_element_type=F32)


def _xattn(x, g, w_q, kv, w_o, *, tm=512):
    b, t, d = x.shape
    m = kv.shape[1]
    return pl.pallas_call(
        _xattn_kernel,
        grid=(b, t // tm),
        in_specs=[
            pl.BlockSpec((1, tm, d), lambda i, c: (i, c, 0)),
            _const_spec((1, d)),
            _const_spec(w_q.shape),
            pl.BlockSpec((1, m, d), lambda i, c: (i, 0, 0)),
            pl.BlockSpec((1, m, d), lambda i, c: (i, 0, 1)),
            _const_spec(w_o.shape),
        ],
        out_specs=pl.BlockSpec((1, tm, d), lambda i, c: (i, c, 0)),
        out_shape=jax.ShapeDtypeStruct((b, t, d), F32),
        compiler_params=_params("parallel", "parallel"),
        name="xattn",
    )(x, g.reshape(1, d), w_q, kv, kv, w_o)


def kernel(x, mem, norm_ffn1, ffn1_w1, ffn1_w3, ffn1_w2, norm_mix, w_in, mlstm_conv_w, mlstm_conv_b,
           mlstm_ig_bias, mlstm_fg_bias, mlstm_head_norm, hgrn_lb_logits, hgrn_head_norm, w_proj_m,
           w_proj_h, w_out, norm_xattn, norm_mem, xattn_wq, xattn_wkv, xattn_wo, norm_ffn2, ffn2_w1,
           ffn2_w3, ffn2_w2, norm_final):
    b, t, d = x.shape
    depth = norm_ffn1.shape[0]
    assert depth == 1 and hgrn_lb_logits.shape[0] == 2
    n = b * t
    l = 0
    bf = lambda w: w.astype(BF16)

    w_in_t = jnp.swapaxes(w_in[l], 0, 1)
    n_gate = 2 * MLSTM_HEADS
    gate_bias = jnp.concatenate([mlstm_ig_bias[l], mlstm_fg_bias[l]]).astype(F32)
    bias_col = jnp.pad(gate_bias, (0, LANES - n_gate)).reshape(1, LANES)
    bias_row = gate_bias.reshape(n_gate, 1)

    x1, hn, gates_col, gates_row = _ffn(x.reshape(n, d), norm_ffn1[l], ffn1_w1[l], ffn1_w3[l], ffn1_w2[l],
                                        norm_mix[l], w_in_t)

    proj = _in_proj(hn, w_in_t)
    x2 = _mix(x1, proj, gates_col, gates_row, mlstm_conv_w[l], mlstm_conv_b[l].reshape(1, -1),
              bias_col, bias_row, mlstm_head_norm[l].reshape(1, -1), hgrn_lb_logits,
              hgrn_head_norm[l].reshape(1, -1), bf(w_proj_m[l]), bf(w_proj_h[l]), bf(w_out[l]), seq_len=t)

    m_len = mem.shape[1]
    kv = _mem_kv(mem.reshape(b * m_len, d), norm_mem[l], bf(xattn_wkv[l])).reshape(b, m_len, 2 * d)
    x3 = _xattn(x2.reshape(b, t, d), norm_xattn[l], bf(xattn_wq[l]), kv, bf(xattn_wo[l]))

    (out,) = _ffn(x3.reshape(n, d), norm_ffn2[l], ffn2_w1[l], ffn2_w3[l], ffn2_w2[l], norm_final)
    return out.reshape(b, t, d)
```

```python
import functools

import jax
import jax.numpy as jnp
from jax import lax
from jax.experimental import pallas as pl
from jax.experimental.pallas import tpu as pltpu

F32 = jnp.float32
BF16 = jnp.bfloat16
EPS = 1e-6

D_MODEL = 2048
D_FF = 5632
MLSTM_HEADS = 4
MLSTM_QK = 128
MLSTM_V = 256
CONV_WIDTH = 4
HGRN_HEADS = 8
HGRN_DK = 128
HGRN_DV = 128
XATTN_HEADS = 4
XATTN_HEAD_DIM = D_MODEL // XATTN_HEADS

MLSTM_QK_W = MLSTM_HEADS * MLSTM_QK
MLSTM_V_W = MLSTM_HEADS * MLSTM_V
HGRN_W = HGRN_HEADS * HGRN_DK

VMEM_LIMIT_BYTES = 60000 * 1024
LANES = 128

COL_GM = 0
COL_GH = COL_GM + D_MODEL
COL_MQ = COL_GH + D_MODEL
COL_MK = COL_MQ + MLSTM_QK_W
COL_MV = COL_MK + MLSTM_QK_W
COL_MO = COL_MV + MLSTM_V_W
COL_HQ = COL_MO + MLSTM_V_W
COL_HF = COL_HQ + HGRN_W
COL_HI = COL_HF + HGRN_W
COL_HG = COL_HI + HGRN_W
PROJ_W = COL_HG + HGRN_W

MIX_GATE_ROW = 2 * MLSTM_QK_W + 2 * MLSTM_V_W
MIX_HGRN_ROW = MIX_GATE_ROW + 2 * MLSTM_HEADS
MIX_MERGE_ROW = MIX_HGRN_ROW + 4 * HGRN_W

FFN_SUB_COLS = 256

MIX_TB = 128
HGRN_NUM_LEVELS = MIX_TB.bit_length() - 1
MIX_SUBBLOCKS = 2
MERGE_GATE_COLS = 512
MERGE_OUT_COLS = 256
HGRN_COARSE_LEVELS = tuple(1 << i for i in range(3, HGRN_NUM_LEVELS))


def _rms(x, g):
    return x * lax.rsqrt(jnp.mean(x * x, axis=-1, keepdims=True) + EPS) * g


def _sigmoid(x):
    return 1.0 / (1.0 + jnp.exp(-x))


def _log_sigmoid(x):
    return jnp.minimum(x, 0.0) - jnp.log(1.0 + jnp.exp(-jnp.abs(x)))


def _params(*sem):
    return pltpu.CompilerParams(dimension_semantics=sem, vmem_limit_bytes=VMEM_LIMIT_BYTES)


def _ffn_kernel(*refs, mode):
    if mode == "mix":
        (x_ref, g_ref, w1_ref, w3_ref, w2_ref, gn_ref, wg_ref,
         o_ref, hn_ref, gcol_ref, grow_ref, h_sc) = refs
    else:
        x_ref, g_ref, w1_ref, w3_ref, w2_ref, gn_ref, o_ref, h_sc = refs
    j = pl.program_id(1)

    @pl.when(j == 0)
    def _():
        h_sc[...] = _rms(x_ref[...], g_ref[...]).astype(BF16)
        o_ref[...] = jnp.zeros_like(o_ref)

    h = h_sc[...]
    for c in range(w1_ref.shape[1] // FFN_SUB_COLS):
        cols = slice(c * FFN_SUB_COLS, (c + 1) * FFN_SUB_COLS)
        a = jnp.dot(h, w1_ref[:, cols].astype(BF16), preferred_element_type=F32)
        b = jnp.dot(h, w3_ref[:, cols].astype(BF16), preferred_element_type=F32)
        act = (a * _sigmoid(a) * b).astype(BF16)
        o_ref[...] += jnp.dot(act, w2_ref[cols, :].astype(BF16), preferred_element_type=F32)

    @pl.when(j == pl.num_programs(1) - 1)
    def _():
        y = x_ref[...] + 0.5 * o_ref[...]
        if mode == "final":
            o_ref[...] = _rms(y, gn_ref[...])
        else:
            o_ref[...] = y
            hn = _rms(y, gn_ref[...]).astype(BF16)
            hn_ref[...] = hn
            nt_dims = (((1,), (1,)), ((), ()))
            wg = wg_ref[...].astype(BF16)
            wg_pad = jnp.concatenate([wg, jnp.zeros((LANES - wg.shape[0], wg.shape[1]), BF16)], axis=0)
            gcol_ref[...] = lax.dot_general(hn, wg_pad, nt_dims, preferred_element_type=F32)
            grow_ref[...] = lax.dot_general(wg, hn, nt_dims, preferred_element_type=F32)


def _ffn(x2d, g, w1, w3, w2, g_next, w_gates_t=None, *, tm=1024, tf=512):
    n, d = x2d.shape
    f = w1.shape[1]
    mode = "final" if w_gates_t is None else "mix"
    in_specs = [
        pl.BlockSpec((tm, d), lambda i, j: (i, 0), pipeline_mode=pl.Buffered(1)),
        pl.BlockSpec((1, d), lambda i, j: (0, 0)),
        pl.BlockSpec((d, tf), lambda i, j: (0, j)),
        pl.BlockSpec((d, tf), lambda i, j: (0, j)),
        pl.BlockSpec((tf, d), lambda i, j: (j, 0)),
        pl.BlockSpec((1, d), lambda i, j: (0, 0)),
    ]
    args = [x2d, g.reshape(1, d), w1, w3, w2, g_next.reshape(1, d)]
    single = dict(pipeline_mode=pl.Buffered(1))
    out_specs = [pl.BlockSpec((tm, d), lambda i, j: (i, 0), **single)]
    out_shape = [jax.ShapeDtypeStruct((n, d), F32)]
    if mode == "mix":
        ng = 2 * MLSTM_HEADS
        in_specs.append(pl.BlockSpec((ng, d), lambda i, j: (MIX_GATE_ROW // ng, 0)))
        args.append(w_gates_t)
        out_specs += [
            pl.BlockSpec((tm, d), lambda i, j: (i, 0), **single),
            pl.BlockSpec((tm, LANES), lambda i, j: (i, 0)),
            pl.BlockSpec((ng, tm), lambda i, j: (0, i)),
        ]
        out_shape += [
            jax.ShapeDtypeStruct((n, d), BF16),
            jax.ShapeDtypeStruct((n, LANES), F32),
            jax.ShapeDtypeStruct((ng, n), F32),
        ]
    return pl.pallas_call(
        functools.partial(_ffn_kernel, mode=mode),
        grid=(n // tm, f // tf),
        in_specs=in_specs,
        out_specs=out_specs,
        out_shape=out_shape,
        scratch_shapes=[pltpu.VMEM((tm, d), BF16)],
        compiler_params=_params("parallel", "arbitrary"),
        name="ffn_" + mode,
    )(*args)


def _in_proj_kernel(h_ref, wt_ref, p_ref, w_sc):
    @pl.when(pl.program_id(1) == 0)
    def _():
        w_sc[...] = wt_ref[...].astype(BF16)

    p_ref[...] = lax.dot_general(h_ref[...], w_sc[...], (((1,), (1,)), ((), ())), preferred_element_type=F32)


def _proj_row_offset(j, tn):
    n_gate_tiles = 2 * D_MODEL // tn
    n_mlstm_tiles = MIX_GATE_ROW // tn
    off8 = jnp.where(j < n_gate_tiles, MIX_MERGE_ROW // 8 + j * (tn // 8),
                     jnp.where(j < n_gate_tiles + n_mlstm_tiles, (j - n_gate_tiles) * (tn // 8),
                               MIX_HGRN_ROW // 8 + (j - n_gate_tiles - n_mlstm_tiles) * (tn // 8)))
    return off8 * 8


def _in_proj(hn, w_t, *, tm=1024, tn=1024):
    n, d = hn.shape
    return pl.pallas_call(
        _in_proj_kernel,
        grid=(PROJ_W // tn, n // tm),
        in_specs=[
            pl.BlockSpec((tm, d), lambda j, i: (i, 0)),
            pl.BlockSpec((pl.Element(tn), pl.Element(d)), lambda j, i: (_proj_row_offset(j, tn), 0)),
        ],
        out_specs=pl.BlockSpec((tm, tn), lambda j, i: (i, j)),
        out_shape=jax.ShapeDtypeStruct((n, PROJ_W), F32),
        scratch_shapes=[pltpu.VMEM((tn, d), BF16)],
        compiler_params=_params("arbitrary", "arbitrary"),
        name="in_proj",
    )(hn, w_t)


def _mlstm_block(q_ref, k_ref, v_ref, o_ref, gcol_ref, grow_ref, cw_ref, cb_ref, bcol_ref, brow_ref,
                 hn_ref, y_ref, xpad_sc, c_sc, n_sc, m_sc, before_head):
    L = MIX_TB
    H = MLSTM_HEADS

    xpad_sc[8:8 + L, 0:MLSTM_QK_W] = q_ref[...]
    xpad_sc[8:8 + L, MLSTM_QK_W:2 * MLSTM_QK_W] = k_ref[...]
    acc = jnp.broadcast_to(cb_ref[...], (L, 2 * MLSTM_QK_W))
    for j in range(CONV_WIDTH):
        off = 8 - (CONV_WIDTH - 1) + j
        acc = acc + cw_ref[j:j + 1, :] * xpad_sc[off:off + L, :]
    xpad_sc[0:8, :] = xpad_sc[L:L + 8, :]
    qk = acc * _sigmoid(acc)

    gcol = gcol_ref[...] + bcol_ref[...]
    grow = grow_ref[...] + brow_ref[...]
    fcol = _log_sigmoid(gcol)
    frow = _log_sigmoid(grow)
    r_i = lax.broadcasted_iota(jnp.int32, (L, L), 0)
    c_i = lax.broadcasted_iota(jnp.int32, (L, L), 1)
    causal = r_i >= c_i
    tril = jnp.where(causal, 1.0, 0.0).astype(F32)
    triu = jnp.where(r_i <= c_i, 1.0, 0.0).astype(F32)
    bcol = jnp.dot(tril, fcol, preferred_element_type=F32, precision=lax.Precision.HIGHEST)
    brow = jnp.dot(frow, triu, preferred_element_type=F32, precision=lax.Precision.HIGHEST)

    for h in range(H):
        before_head[h]()
        b_c = bcol[:, H + h:H + h + 1]
        i_c = gcol[:, h:h + 1]
        b_r = brow[H + h:H + h + 1, :]
        i_r = grow[h:h + 1, :]
        m_prev = m_sc[h:h + 1, 0:1]

        d_log = jnp.where(causal, b_c - (b_r - i_r), -jnp.inf)
        inter_log = b_c + m_prev
        m_t = jnp.maximum(jnp.max(d_log, axis=1, keepdims=True), inter_log)

        q_h = qk[:, h * MLSTM_QK:(h + 1) * MLSTM_QK]
        k_h = qk[:, MLSTM_QK_W + h * MLSTM_QK:MLSTM_QK_W + (h + 1) * MLSTM_QK] * (MLSTM_QK ** -0.5)
        v_h = v_ref[:, h * MLSTM_V:(h + 1) * MLSTM_V].astype(BF16)
        q_b = q_h.astype(BF16)

        s = lax.dot_general(q_b, k_h.astype(BF16), (((1,), (1,)), ((), ())), preferred_element_type=F32)
        s = s * jnp.exp(d_log - m_t)
        w_inter = jnp.exp(inter_log - m_t)
        c_h = c_sc[h]
        n_h = n_sc[h:h + 1, :]
        num = (jnp.dot(s.astype(BF16), v_h, preferred_element_type=F32)
               + w_inter * jnp.dot(q_b, c_h.astype(BF16), preferred_element_type=F32))
        den = jnp.sum(s, axis=1, keepdims=True) + w_inter * jnp.sum(q_h * n_h, axis=1, keepdims=True)
        hh = num / jnp.maximum(jnp.abs(den), jnp.exp(-m_t))

        hn = hh * lax.rsqrt(jnp.mean(hh * hh, axis=-1, keepdims=True) + EPS)
        hn = hn * hn_ref[:, h * MLSTM_V:(h + 1) * MLSTM_V]
        y = hn * _sigmoid(o_ref[:, h * MLSTM_V:(h + 1) * MLSTM_V])
        y_ref[:, h * MLSTM_V:(h + 1) * MLSTM_V] = y.astype(y_ref.dtype)

        b_last = b_c[L - 1:L, :]
        a_log = b_last - b_c + i_c
        m_new = jnp.maximum(b_last + m_prev, jnp.max(a_log, axis=0, keepdims=True))
        w_a = jnp.exp(a_log - m_new)
        decay = jnp.exp(b_last + m_prev - m_new)
        kw = k_h * w_a
        c_sc[h] = decay * c_h + lax.dot_general(kw.astype(BF16), v_h, (((0,), (0,)), ((), ())),
                                                preferred_element_type=F32)
        n_sc[h:h + 1, :] = decay * n_h + jnp.sum(kw, axis=0, keepdims=True)
        m_sc[h:h + 1, :] = jnp.broadcast_to(m_new, (1, LANES))


def _hgrn_block(q_ref, f_ref, v_ref, og_ref, lbl_ref, hn_ref, y_ref, g_sc, k_sc, q_sc, gr_sc, st_sc,
                before_head):
    TB = MIX_TB
    H = HGRN_HEADS
    DK = HGRN_DK
    W = H * DK
    NT = TB // 8

    lbl = lbl_ref[...]
    lmax = jnp.max(lbl, axis=0, keepdims=True)
    le = jnp.exp(lbl - lmax)
    lb = le[1:2, :] / jnp.sum(le, axis=0, keepdims=True)

    f = lb + (1.0 - lb) * _sigmoid(f_ref[...])
    k_sc[...] = 1.0 - f
    qraw = q_ref[...]
    q_sc[...] = qraw * _sigmoid(qraw) * (DK ** -0.5)
    r_i = lax.broadcasted_iota(jnp.int32, (TB, TB), 0)
    c_i = lax.broadcasted_iota(jnp.int32, (TB, TB), 1)
    tri = jnp.where(r_i >= c_i, 1.0, 0.0).astype(F32)
    g = jnp.dot(tri, jnp.log(f), preferred_element_type=F32, precision=lax.Precision.HIGHEST)
    g_sc[...] = g

    g3 = g.reshape(NT, 8, W)
    sub = lax.broadcasted_iota(jnp.int32, (NT, 8, W), 1)
    bit0 = (sub & 1) != 0
    bit1 = (sub & 2) != 0
    bit2 = (sub & 4) != 0
    last2 = jnp.where(bit0, g3, pltpu.roll(g3, 7, 1))
    last4 = jnp.where(bit1, last2, pltpu.roll(last2, 6, 1))
    last4_r = pltpu.roll(last4, 4, 1)
    last8 = jnp.where(bit2, last4, last4_r).reshape(TB, W)
    gr_sc[0] = jnp.where(bit0, pltpu.roll(g3, 1, 1), g3).reshape(TB, W)
    gr_sc[1] = jnp.where(bit1, pltpu.roll(last2, 2, 1), last2).reshape(TB, W)
    gr_sc[2] = jnp.where(bit2, last4_r, last4).reshape(TB, W)
    for li, lvl in enumerate(HGRN_COARSE_LEVELS):
        groups = []
        for base in range(0, TB, 2 * lvl):
            src = last8[base + lvl - 8:base + lvl, :]
            groups.extend([src] * (2 * lvl // 8))
        gr_sc[3 + li] = jnp.concatenate(groups, axis=0)

    x_i = r_i ^ c_i
    nt_dims = (((1,), (1,)), ((), ()))
    for h in range(H):
        before_head[h]()
        cols = slice(h * DK, (h + 1) * DK)
        q_h = q_sc[:, cols]
        k_h = k_sc[:, cols]
        g_h = g_sc[:, cols]
        v_h = v_ref[:, cols].astype(BF16)

        a = lax.dot_general(q_h.astype(BF16), k_h.astype(BF16), nt_dims, preferred_element_type=F32)
        for li in range(HGRN_NUM_LEVELS):
            e = jnp.exp(-jnp.abs(g_h - gr_sc[li, :, cols]))
            p = lax.dot_general((q_h * e).astype(BF16), (k_h * e).astype(BF16), nt_dims,
                                preferred_element_type=F32)
            a = jnp.where(x_i >= (1 << li), p, a)
        a = jnp.where(r_i >= c_i, a, 0.0)
        o = jnp.dot(a.astype(BF16), v_h, preferred_element_type=F32)

        st = st_sc[h]
        g_last = g_h[TB - 1:TB, :]
        qd = (q_h * jnp.exp(g_h)).astype(BF16)
        o = o + lax.dot_general(qd, st.astype(BF16), nt_dims, preferred_element_type=F32)
        kd = (k_h * jnp.exp(g_last - g_h)).astype(BF16)
        st_sc[h] = jnp.exp(g_last) * st + lax.dot_general(v_h, kd, (((0,), (0,)), ((), ())),
                                                          preferred_element_type=F32)

        on = o * lax.rsqrt(jnp.mean(o * o, axis=-1, keepdims=True) + EPS) * hn_ref[:, cols]
        og = og_ref[:, cols]
        y_ref[:, cols] = (on * (og * _sigmoid(og))).astype(y_ref.dtype)


def _mix_kernel(mq_ref, mk_ref, mv_ref, mo_ref, gcol_ref, grow_ref, cw_ref, cb_ref, bcol_ref, brow_ref,
                mhn_ref, hq_ref, hf_ref, hi_ref, hg_ref, lbl_ref, hhn_ref,
                x_ref, gm_ref, gh_ref, wm_ref, wh_ref, wo_ref,
                out_ref,
                ym_sc, yh_sc, merged_sc, xpad_sc, c_sc, n_sc, m_sc, g_sc, k_sc, q_sc, gr_sc, st_sc,
                *, blocks_per_seq):
    s = pl.program_id(0)
    cur = s % 2
    prv = 1 - cur

    @pl.when(s == 0)
    def _():
        ym_sc[...] = jnp.zeros_like(ym_sc)
        yh_sc[...] = jnp.zeros_like(yh_sc)

    @pl.when(s % blocks_per_seq == 0)
    def _():
        xpad_sc[0:8, :] = jnp.zeros((8, 2 * MLSTM_QK_W), F32)
        c_sc[...] = jnp.zeros_like(c_sc)
        n_sc[...] = jnp.zeros_like(n_sc)
        m_sc[...] = jnp.zeros_like(m_sc)
        st_sc[...] = jnp.zeros_like(st_sc)

    def gate_piece(c):
        def run():
            cols = slice(c * MERGE_GATE_COLS, (c + 1) * MERGE_GATE_COLS)
            pm = jnp.dot(ym_sc[prv], wm_ref[:, cols], preferred_element_type=F32)
            ph = jnp.dot(yh_sc[prv], wh_ref[:, cols], preferred_element_type=F32)
            merged = _sigmoid(gm_ref[:, cols]) * pm + _sigmoid(gh_ref[:, cols]) * ph
            merged_sc[:, cols] = merged.astype(BF16)
        return run

    def out_piece(c):
        def run():
            cols = slice(c * MERGE_OUT_COLS, (c + 1) * MERGE_OUT_COLS)
            out_ref[:, cols] = x_ref[:, cols] + jnp.dot(merged_sc[...], wo_ref[:, cols],
                                                        preferred_element_type=F32)
        return run

    pieces = ([gate_piece(c) for c in range(D_MODEL // MERGE_GATE_COLS)]
              + [out_piece(c) for c in range(D_MODEL // MERGE_OUT_COLS)])
    n_heads = MLSTM_HEADS + HGRN_HEADS
    n_slots = MIX_SUBBLOCKS * n_heads
    stride = n_slots // len(pieces)
    hooks = [pieces[i // stride] if i % stride == 0 else (lambda: None) for i in range(n_slots)]

    for sub in range(MIX_SUBBLOCKS):
        rows = pl.ds(sub * MIX_TB, MIX_TB)
        sub_hooks = hooks[sub * n_heads:(sub + 1) * n_heads]
        _mlstm_block(mq_ref.at[rows], mk_ref.at[rows], mv_ref.at[rows], mo_ref.at[rows], gcol_ref.at[rows],
                     grow_ref.at[:, rows], cw_ref, cb_ref, bcol_ref, brow_ref, mhn_ref,
                     ym_sc.at[cur, rows], xpad_sc, c_sc, n_sc, m_sc, sub_hooks[:MLSTM_HEADS])
        _hgrn_block(hq_ref.at[rows], hf_ref.at[rows], hi_ref.at[rows], hg_ref.at[rows], lbl_ref, hhn_ref,
                    yh_sc.at[cur, rows], g_sc, k_sc, q_sc, gr_sc, st_sc, sub_hooks[MLSTM_HEADS:])


def _const_spec(shape):
    return pl.BlockSpec(shape, lambda *_: (0,) * len(shape), pipeline_mode=pl.Buffered(1))


def _mix(x2d, proj, gates_col, gates_row, conv_w, conv_b, bias_col, bias_row, m_head_norm,
         lb_logits, h_head_norm, w_m, w_h, w_o, *, seq_len):
    n, d = x2d.shape
    tb = MIX_TB * MIX_SUBBLOCKS
    nblk = n // tb
    qw, vw, hw = MLSTM_QK_W, MLSTM_V_W, HGRN_W

    def cur(col_block):
        return lambda s: (jnp.minimum(s, nblk - 1), col_block)

    def prev(col_block):
        return lambda s: (jnp.maximum(s - 1, 0), col_block)

    return pl.pallas_call(
        functools.partial(_mix_kernel, blocks_per_seq=seq_len // tb),
        grid=(nblk + 1,),
        in_specs=[
            pl.BlockSpec((tb, qw), cur(COL_MQ // qw)),
            pl.BlockSpec((tb, qw), cur(COL_MK // qw)),
            pl.BlockSpec((tb, vw), cur(COL_MV // vw)),
            pl.BlockSpec((tb, vw), cur(COL_MO // vw)),
            pl.BlockSpec((tb, LANES), cur(0)),
            pl.BlockSpec((2 * MLSTM_HEADS, tb), lambda s: (0, jnp.minimum(s, nblk - 1))),
            _const_spec((CONV_WIDTH, 2 * qw)),
            _const_spec((1, 2 * qw)),
            _const_spec((1, LANES)),
            _const_spec((2 * MLSTM_HEADS, 1)),
            _const_spec((1, vw)),
            pl.BlockSpec((tb, hw), cur(COL_HQ // hw)),
            pl.BlockSpec((tb, hw), cur(COL_HF // hw)),
            pl.BlockSpec((tb, hw), cur(COL_HI // hw)),
            pl.BlockSpec((tb, hw), cur(COL_HG // hw)),
            _const_spec((2, hw)),
            _const_spec((1, hw)),
            pl.BlockSpec((tb, d), prev(0)),
            pl.BlockSpec((tb, d), prev(COL_GM // d)),
            pl.BlockSpec((tb, d), prev(COL_GH // d)),
            _const_spec(w_m.shape),
            _const_spec(w_h.shape),
            _const_spec(w_o.shape),
        ],
        out_specs=pl.BlockSpec((tb, d), prev(0)),
        out_shape=jax.ShapeDtypeStruct((n, d), F32),
        scratch_shapes=[
            pltpu.VMEM((2, tb, vw), BF16),
            pltpu.VMEM((2, tb, hw), BF16),
            pltpu.VMEM((tb, d), BF16),
            pltpu.VMEM((MIX_TB + 8, 2 * qw), F32),
            pltpu.VMEM((MLSTM_HEADS, MLSTM_QK, MLSTM_V), F32),
            pltpu.VMEM((8, MLSTM_QK), F32),
            pltpu.VMEM((8, LANES), F32),
            pltpu.VMEM((MIX_TB, hw), F32),
            pltpu.VMEM((MIX_TB, hw), F32),
            pltpu.VMEM((MIX_TB, hw), F32),
            pltpu.VMEM((HGRN_NUM_LEVELS, MIX_TB, hw), F32),
            pltpu.VMEM((HGRN_HEADS, HGRN_DV, HGRN_DK), F32),
        ],
        compiler_params=_params("arbitrary"),
        name="mix",
    )(proj, proj, proj, proj, gates_col, gates_row, conv_w, conv_b, bias_col, bias_row, m_head_norm,
      proj, proj, proj, proj, lb_logits, h_head_norm, x2d, proj, proj, w_m, w_h, w_o)


def _xattn_prep_kernel(m_ref, g_ref, wk_ref, wv_ref, wq_ref, wo_ref, qk_ref, vo_ref):
    m = _rms(m_ref[0], g_ref[...]).astype(BF16)
    k = jnp.dot(m, wk_ref[...].astype(BF16), preferred_element_type=F32).astype(BF16)
    v = jnp.dot(m, wv_ref[...].astype(BF16), preferred_element_type=F32).astype(BF16)
    qk = lax.dot_general(wq_ref[...].astype(BF16), k, (((1,), (1,)), ((), ())),
                         preferred_element_type=F32)
    qk_ref[0] = (qk * (XATTN_HEAD_DIM ** -0.5)).astype(qk_ref.dtype)
    vo_ref[0] = jnp.dot(v, wo_ref[...].astype(BF16), preferred_element_type=F32).astype(vo_ref.dtype)


def _xattn_prep(mem, g, w_kv, w_q, w_o):
    b, m, d = mem.shape
    hd = XATTN_HEAD_DIM
    nh = XATTN_HEADS
    return pl.pallas_call(
        _xattn_prep_kernel,
        grid=(nh, b),
        in_specs=[
            pl.BlockSpec((1, m, d), lambda h, i: (i, 0, 0)),
            pl.BlockSpec((1, d), lambda h, i: (0, 0)),
            pl.BlockSpec((d, hd), lambda h, i: (0, h)),
            pl.BlockSpec((d, hd), lambda h, i: (0, nh + h)),
            pl.BlockSpec((d, hd), lambda h, i: (0, h)),
            pl.BlockSpec((hd, d), lambda h, i: (h, 0)),
        ],
        out_specs=[
            pl.BlockSpec((1, d, m), lambda h, i: (i, 0, h)),
            pl.BlockSpec((1, m, d), lambda h, i: (i, h, 0)),
        ],
        out_shape=[
            jax.ShapeDtypeStruct((b, d, nh * m), BF16),
            jax.ShapeDtypeStruct((b, nh * m, d), BF16),
        ],
        compiler_params=_params("arbitrary", "arbitrary"),
        name="xattn_prep",
    )(mem, g.reshape(1, d), w_kv, w_kv, w_q, w_o)


def _xattn_kernel(x_ref, g_ref, qk_ref, vo_ref, o_ref, *, mem_len):
    x = x_ref[0]
    h = _rms(x, g_ref[...]).astype(BF16)
    s = jnp.dot(h, qk_ref[0], preferred_element_type=F32)
    probs = []
    for hd in range(XATTN_HEADS):
        s_h = s[:, hd * mem_len:(hd + 1) * mem_len]
        e = jnp.exp(s_h - jnp.max(s_h, axis=-1, keepdims=True))
        probs.append((e / jnp.sum(e, axis=-1, keepdims=True)).astype(BF16))
    p = jnp.concatenate(probs, axis=-1)
    o_ref[0] = x + jnp.dot(p, vo_ref[0], preferred_element_type=F32)


def _xattn(x, g, qk, vo, *, tm=512):
    b, t, d = x.shape
    hm = qk.shape[2]
    return pl.pallas_call(
        functools.partial(_xattn_kernel, mem_len=hm // XATTN_HEADS),
        grid=(b, t // tm),
        in_specs=[
            pl.BlockSpec((1, tm, d), lambda i, c: (i, c, 0)),
            _const_spec((1, d)),
            pl.BlockSpec((1, d, hm), lambda i, c: (i, 0, 0)),
            pl.BlockSpec((1, hm, d), lambda i, c: (i, 0, 0)),
        ],
        out_specs=pl.BlockSpec((1, tm, d), lambda i, c: (i, c, 0)),
        out_shape=jax.ShapeDtypeStruct((b, t, d), F32),
        compiler_params=_params("parallel", "parallel"),
        name="xattn",
    )(x, g.reshape(1, d), qk, vo)


def kernel(x, mem, norm_ffn1, ffn1_w1, ffn1_w3, ffn1_w2, norm_mix, w_in, mlstm_conv_w, mlstm_conv_b,
           mlstm_ig_bias, mlstm_fg_bias, mlstm_head_norm, hgrn_lb_logits, hgrn_head_norm, w_proj_m,
           w_proj_h, w_out, norm_xattn, norm_mem, xattn_wq, xattn_wkv, xattn_wo, norm_ffn2, ffn2_w1,
           ffn2_w3, ffn2_w2, norm_final):
    b, t, d = x.shape
    depth = norm_ffn1.shape[0]
    assert depth == 1 and hgrn_lb_logits.shape[0] == 2
    n = b * t
    l = 0
    bf = lambda w: w.astype(BF16)

    w_in_t = jnp.swapaxes(w_in[l], 0, 1)
    n_gate = 2 * MLSTM_HEADS
    gate_bias = jnp.concatenate([mlstm_ig_bias[l], mlstm_fg_bias[l]]).astype(F32)
    bias_col = jnp.pad(gate_bias, (0, LANES - n_gate)).reshape(1, LANES)
    bias_row = gate_bias.reshape(n_gate, 1)

    x1, hn, gates_col, gates_row = _ffn(x.reshape(n, d), norm_ffn1[l], ffn1_w1[l], ffn1_w3[l], ffn1_w2[l],
                                        norm_mix[l], w_in_t)

    proj = _in_proj(hn, w_in_t)
    x2 = _mix(x1, proj, gates_col, gates_row, mlstm_conv_w[l], mlstm_conv_b[l].reshape(1, -1),
              bias_col, bias_row, mlstm_head_norm[l].reshape(1, -1), hgrn_lb_logits,
              hgrn_head_norm[l].reshape(1, -1), bf(w_proj_m[l]), bf(w_proj_h[l]), bf(w_out[l]), seq_len=t)

    qk, vo = _xattn_prep(mem, norm_mem[l], xattn_wkv[l], xattn_wq[l], xattn_wo[l])
    x3 = _xattn(x2.reshape(b, t, d), norm_xattn[l], qk, vo)

    (out,) = _ffn(x3.reshape(n, d), norm_ffn2[l], ffn2_w1[l], ffn2_w3[l], ffn2_w2[l], norm_final)
    return out.reshape(b, t, d)
```

```python
import functools

import jax
import jax.numpy as jnp
from jax import lax
from jax.experimental import pallas as pl
from jax.experimental.pallas import tpu as pltpu

F32 = jnp.float32
BF16 = jnp.bfloat16
EPS = 1e-6
LOG2_E = 1.4426950408889634

D_MODEL = 2048
D_FF = 5632
MLSTM_HEADS = 4
MLSTM_QK = 128
MLSTM_V = 256
CONV_WIDTH = 4
HGRN_HEADS = 8
HGRN_DK = 128
HGRN_DV = 128
XATTN_HEADS = 4
XATTN_HEAD_DIM = D_MODEL // XATTN_HEADS

MLSTM_QK_W = MLSTM_HEADS * MLSTM_QK
MLSTM_V_W = MLSTM_HEADS * MLSTM_V
HGRN_W = HGRN_HEADS * HGRN_DK

VMEM_LIMIT_BYTES = 60000 * 1024
LANES = 128

COL_GM = 0
COL_GH = COL_GM + D_MODEL
COL_MQ = COL_GH + D_MODEL
COL_MK = COL_MQ + MLSTM_QK_W
COL_MV = COL_MK + MLSTM_QK_W
COL_MO = COL_MV + MLSTM_V_W
COL_HQ = COL_MO + MLSTM_V_W
COL_HF = COL_HQ + HGRN_W
COL_HI = COL_HF + HGRN_W
COL_HG = COL_HI + HGRN_W
PROJ_W = COL_HG + HGRN_W

MIX_GATE_ROW = 2 * MLSTM_QK_W + 2 * MLSTM_V_W
MIX_HGRN_ROW = MIX_GATE_ROW + 2 * MLSTM_HEADS
MIX_MERGE_ROW = MIX_HGRN_ROW + 4 * HGRN_W

PROJ_SUB_ROWS = 1024
FFN_SUB_COLS = 256

MIX_TB = 128
HGRN_NUM_LEVELS = MIX_TB.bit_length() - 1
MIX_SUBBLOCKS = 2
MERGE_GATE_COLS = 512
MERGE_OUT_COLS = 256
HGRN_COARSE_LEVELS = tuple(1 << i for i in range(3, HGRN_NUM_LEVELS))


def _rms(x, g):
    return x * lax.rsqrt(jnp.mean(x * x, axis=-1, keepdims=True) + EPS) * g


def _sigmoid(x):
    return 1.0 / (1.0 + jnp.exp(-x))


def _log_sigmoid(x):
    return jnp.minimum(x, 0.0) - jnp.log(1.0 + jnp.exp(-jnp.abs(x)))


def _neg_abs(x):
    return pltpu.bitcast(pltpu.bitcast(x, jnp.uint32) | jnp.uint32(0x80000000), F32)


def _params(*sem):
    return pltpu.CompilerParams(dimension_semantics=sem, vmem_limit_bytes=VMEM_LIMIT_BYTES)


def _ffn_kernel(*refs, mode):
    if mode == "mix":
        (x_ref, g_ref, w1_ref, w3_ref, w2_ref, gn_ref, wg_ref,
         o_ref, hn_ref, gcol_ref, grow_ref, h_sc) = refs
    else:
        x_ref, g_ref, w1_ref, w3_ref, w2_ref, gn_ref, o_ref, h_sc = refs
    j = pl.program_id(1)

    @pl.when(j == 0)
    def _():
        h_sc[...] = _rms(x_ref[...], g_ref[...]).astype(BF16)
        o_ref[...] = jnp.zeros_like(o_ref)

    h = h_sc[...]
    for c in range(w1_ref.shape[1] // FFN_SUB_COLS):
        cols = slice(c * FFN_SUB_COLS, (c + 1) * FFN_SUB_COLS)
        a = jnp.dot(h, w1_ref[:, cols].astype(BF16), preferred_element_type=F32)
        b = jnp.dot(h, w3_ref[:, cols].astype(BF16), preferred_element_type=F32)
        act = (a * _sigmoid(a) * b).astype(BF16)
        o_ref[...] += jnp.dot(act, w2_ref[cols, :].astype(BF16), preferred_element_type=F32)

    @pl.when(j == pl.num_programs(1) - 1)
    def _():
        y = x_ref[...] + 0.5 * o_ref[...]
        if mode == "final":
            o_ref[...] = _rms(y, gn_ref[...])
        else:
            o_ref[...] = y
            hn = _rms(y, gn_ref[...]).astype(BF16)
            hn_ref[...] = hn
            nt_dims = (((1,), (1,)), ((), ()))
            wg = wg_ref[...].astype(BF16)
            wg_pad = jnp.concatenate([wg, jnp.zeros((LANES - wg.shape[0], wg.shape[1]), BF16)], axis=0)
            gcol_ref[...] = lax.dot_general(hn, wg_pad, nt_dims, preferred_element_type=F32)
            grow_ref[...] = lax.dot_general(wg, hn, nt_dims, preferred_element_type=F32)


def _ffn(x2d, g, w1, w3, w2, g_next, w_gates_t=None, *, tm=1024, tf=512):
    n, d = x2d.shape
    f = w1.shape[1]
    mode = "final" if w_gates_t is None else "mix"
    in_specs = [
        pl.BlockSpec((tm, d), lambda i, j: (i, 0), pipeline_mode=pl.Buffered(1)),
        pl.BlockSpec((1, d), lambda i, j: (0, 0)),
        pl.BlockSpec((d, tf), lambda i, j: (0, j)),
        pl.BlockSpec((d, tf), lambda i, j: (0, j)),
        pl.BlockSpec((tf, d), lambda i, j: (j, 0)),
        pl.BlockSpec((1, d), lambda i, j: (0, 0)),
    ]
    args = [x2d, g.reshape(1, d), w1, w3, w2, g_next.reshape(1, d)]
    single = dict(pipeline_mode=pl.Buffered(1))
    out_specs = [pl.BlockSpec((tm, d), lambda i, j: (i, 0), **single)]
    out_shape = [jax.ShapeDtypeStruct((n, d), F32)]
    if mode == "mix":
        ng = 2 * MLSTM_HEADS
        in_specs.append(pl.BlockSpec((ng, d), lambda i, j: (MIX_GATE_ROW // ng, 0)))
        args.append(w_gates_t)
        out_specs += [
            pl.BlockSpec((tm, d), lambda i, j: (i, 0), **single),
            pl.BlockSpec((tm, LANES), lambda i, j: (i, 0)),
            pl.BlockSpec((ng, tm), lambda i, j: (0, i)),
        ]
        out_shape += [
            jax.ShapeDtypeStruct((n, d), BF16),
            jax.ShapeDtypeStruct((n, LANES), F32),
            jax.ShapeDtypeStruct((ng, n), F32),
        ]
    return pl.pallas_call(
        functools.partial(_ffn_kernel, mode=mode),
        grid=(n // tm, f // tf),
        in_specs=in_specs,
        out_specs=out_specs,
        out_shape=out_shape,
        scratch_shapes=[pltpu.VMEM((tm, d), BF16)],
        compiler_params=_params("parallel", "arbitrary"),
        name="ffn_" + mode,
    )(*args)


def _in_proj_kernel(h_ref, wt_ref, p_ref, w_sc):
    @pl.when(pl.program_id(1) == 0)
    def _():
        w_sc[...] = wt_ref[...].astype(BF16)

    for r in range(0, h_ref.shape[0], PROJ_SUB_ROWS):
        rows = slice(r, r + PROJ_SUB_ROWS)
        p_ref[rows, :] = lax.dot_general(h_ref[rows, :], w_sc[...], (((1,), (1,)), ((), ())),
                                         preferred_element_type=F32)


def _proj_row_offset(j, tn):
    n_gate_tiles = 2 * D_MODEL // tn
    n_mlstm_tiles = MIX_GATE_ROW // tn
    off8 = jnp.where(j < n_gate_tiles, MIX_MERGE_ROW // 8 + j * (tn // 8),
                     jnp.where(j < n_gate_tiles + n_mlstm_tiles, (j - n_gate_tiles) * (tn // 8),
                               MIX_HGRN_ROW // 8 + (j - n_gate_tiles - n_mlstm_tiles) * (tn // 8)))
    return off8 * 8


def _in_proj(hn, w_t, *, tm=2048, tn=1024):
    n, d = hn.shape
    return pl.pallas_call(
        _in_proj_kernel,
        grid=(PROJ_W // tn, n // tm),
        in_specs=[
            pl.BlockSpec((tm, d), lambda j, i: (i, 0)),
            pl.BlockSpec((pl.Element(tn), pl.Element(d)), lambda j, i: (_proj_row_offset(j, tn), 0)),
        ],
        out_specs=pl.BlockSpec((tm, tn), lambda j, i: (i, j)),
        out_shape=jax.ShapeDtypeStruct((n, PROJ_W), F32),
        scratch_shapes=[pltpu.VMEM((tn, d), BF16)],
        compiler_params=_params("arbitrary", "arbitrary"),
        name="in_proj",
    )(hn, w_t)


def _mlstm_block(q_ref, k_ref, v_ref, o_ref, gcol_ref, grow_ref, cw_ref, cb_ref, bcol_ref, brow_ref,
                 hn_ref, y_ref, xpad_sc, c_sc, n_sc, m_sc, before_head):
    L = MIX_TB
    H = MLSTM_HEADS

    xpad_sc[8:8 + L, 0:MLSTM_QK_W] = q_ref[...]
    xpad_sc[8:8 + L, MLSTM_QK_W:2 * MLSTM_QK_W] = k_ref[...]
    acc = jnp.broadcast_to(cb_ref[...], (L, 2 * MLSTM_QK_W))
    for j in range(CONV_WIDTH):
        off = 8 - (CONV_WIDTH - 1) + j
        acc = acc + cw_ref[j:j + 1, :] * xpad_sc[off:off + L, :]
    xpad_sc[0:8, :] = xpad_sc[L:L + 8, :]
    qk = acc * _sigmoid(acc)

    gcol = gcol_ref[...] + bcol_ref[...]
    grow = grow_ref[...] + brow_ref[...]
    fcol = _log_sigmoid(gcol)
    frow = _log_sigmoid(grow)
    r_i = lax.broadcasted_iota(jnp.int32, (L, L), 0)
    c_i = lax.broadcasted_iota(jnp.int32, (L, L), 1)
    causal = r_i >= c_i
    tril = jnp.where(causal, 1.0, 0.0).astype(F32)
    triu = jnp.where(r_i <= c_i, 1.0, 0.0).astype(F32)
    bcol = jnp.dot(tril, fcol, preferred_element_type=F32, precision=lax.Precision.HIGHEST)
    brow = jnp.dot(frow, triu, preferred_element_type=F32, precision=lax.Precision.HIGHEST)

    for h in range(H):
        before_head[h]()
        b_c = bcol[:, H + h:H + h + 1]
        i_c = gcol[:, h:h + 1]
        b_r = brow[H + h:H + h + 1, :]
        i_r = grow[h:h + 1, :]
        m_prev = m_sc[h:h + 1, 0:1]

        d_log = jnp.where(causal, b_c - (b_r - i_r), -jnp.inf)
        inter_log = b_c + m_prev
        m_t = jnp.maximum(jnp.max(d_log, axis=1, keepdims=True), inter_log)

        q_h = qk[:, h * MLSTM_QK:(h + 1) * MLSTM_QK]
        k_h = qk[:, MLSTM_QK_W + h * MLSTM_QK:MLSTM_QK_W + (h + 1) * MLSTM_QK] * (MLSTM_QK ** -0.5)
        v_h = v_ref[:, h * MLSTM_V:(h + 1) * MLSTM_V].astype(BF16)
        q_b = q_h.astype(BF16)

        s = lax.dot_general(q_b, k_h.astype(BF16), (((1,), (1,)), ((), ())), preferred_element_type=F32)
        s = s * jnp.exp(d_log - m_t)
        w_inter = jnp.exp(inter_log - m_t)
        c_h = c_sc[h]
        n_h = n_sc[h:h + 1, :]
        num = (jnp.dot(s.astype(BF16), v_h, preferred_element_type=F32)
               + w_inter * jnp.dot(q_b, c_h.astype(BF16), preferred_element_type=F32))
        den = jnp.sum(s, axis=1, keepdims=True) + w_inter * jnp.sum(q_h * n_h, axis=1, keepdims=True)
        hh = num / jnp.maximum(jnp.abs(den), jnp.exp(-m_t))

        hn = hh * lax.rsqrt(jnp.mean(hh * hh, axis=-1, keepdims=True) + EPS)
        hn = hn * hn_ref[:, h * MLSTM_V:(h + 1) * MLSTM_V]
        y = hn * _sigmoid(o_ref[:, h * MLSTM_V:(h + 1) * MLSTM_V])
        y_ref[:, h * MLSTM_V:(h + 1) * MLSTM_V] = y.astype(y_ref.dtype)

        b_last = b_c[L - 1:L, :]
        a_log = b_last - b_c + i_c
        m_new = jnp.maximum(b_last + m_prev, jnp.max(a_log, axis=0, keepdims=True))
        w_a = jnp.exp(a_log - m_new)
        decay = jnp.exp(b_last + m_prev - m_new)
        kw = k_h * w_a
        c_sc[h] = decay * c_h + lax.dot_general(kw.astype(BF16), v_h, (((0,), (0,)), ((), ())),
                                                preferred_element_type=F32)
        n_sc[h:h + 1, :] = decay * n_h + jnp.sum(kw, axis=0, keepdims=True)
        m_sc[h:h + 1, :] = jnp.broadcast_to(m_new, (1, LANES))


def _hgrn_block(q_ref, f_ref, v_ref, og_ref, lbl_ref, hn_ref, y_ref, g_sc, k_sc, q_sc, gr_sc, st_sc,
                before_head):
    TB = MIX_TB
    H = HGRN_HEADS
    DK = HGRN_DK
    W = H * DK
    NT = TB // 8

    lbl = lbl_ref[...]
    lmax = jnp.max(lbl, axis=0, keepdims=True)
    le = jnp.exp(lbl - lmax)
    lb = le[1:2, :] / jnp.sum(le, axis=0, keepdims=True)

    f = lb + (1.0 - lb) * _sigmoid(f_ref[...])
    k_sc[...] = 1.0 - f
    qraw = q_ref[...]
    q_sc[...] = qraw * _sigmoid(qraw) * (DK ** -0.5)
    r_i = lax.broadcasted_iota(jnp.int32, (TB, TB), 0)
    c_i = lax.broadcasted_iota(jnp.int32, (TB, TB), 1)
    tri = jnp.where(r_i >= c_i, 1.0, 0.0).astype(F32)
    g = jnp.dot(tri, jnp.log(f) * LOG2_E, preferred_element_type=F32, precision=lax.Precision.HIGHEST)
    g_sc[...] = g

    g3 = g.reshape(NT, 8, W)
    sub = lax.broadcasted_iota(jnp.int32, (NT, 8, W), 1)
    bit0 = (sub & 1) != 0
    bit1 = (sub & 2) != 0
    bit2 = (sub & 4) != 0
    last2 = jnp.where(bit0, g3, pltpu.roll(g3, 7, 1))
    last4 = jnp.where(bit1, last2, pltpu.roll(last2, 6, 1))
    last4_r = pltpu.roll(last4, 4, 1)
    last8 = jnp.where(bit2, last4, last4_r).reshape(TB, W)
    gr_sc[0] = jnp.where(bit0, pltpu.roll(g3, 1, 1), g3).reshape(TB, W)
    gr_sc[1] = jnp.where(bit1, pltpu.roll(last2, 2, 1), last2).reshape(TB, W)
    gr_sc[2] = jnp.where(bit2, last4_r, last4).reshape(TB, W)
    for li, lvl in enumerate(HGRN_COARSE_LEVELS):
        groups = []
        for base in range(0, TB, 2 * lvl):
            src = last8[base + lvl - 8:base + lvl, :]
            groups.extend([src] * (2 * lvl // 8))
        gr_sc[3 + li] = jnp.concatenate(groups, axis=0)

    x_i = r_i ^ c_i
    nt_dims = (((1,), (1,)), ((), ()))
    for h in range(H):
        before_head[h]()
        cols = slice(h * DK, (h + 1) * DK)
        q_h = q_sc[:, cols]
        k_h = k_sc[:, cols]
        g_h = g_sc[:, cols]
        v_h = v_ref[:, cols].astype(BF16)

        q_b = q_h.astype(BF16)
        k_b = k_h.astype(BF16)
        a = lax.dot_general(q_b, k_b, nt_dims, preferred_element_type=F32)
        for li in range(HGRN_NUM_LEVELS):
            e = jnp.exp2(_neg_abs(g_h - gr_sc[li, :, cols])).astype(BF16)
            p = lax.dot_general(q_b * e, k_b * e, nt_dims, preferred_element_type=F32)
            a = jnp.where(x_i >= (1 << li), p, a)
        a = jnp.where(r_i >= c_i, a, 0.0)
        o = jnp.dot(a.astype(BF16), v_h, preferred_element_type=F32)

        st = st_sc[h]
        g_last = g_h[TB - 1:TB, :]
        qd = (q_h * jnp.exp2(g_h)).astype(BF16)
        o = o + lax.dot_general(qd, st.astype(BF16), nt_dims, preferred_element_type=F32)
        kd = (k_h * jnp.exp2(g_last - g_h)).astype(BF16)
        st_sc[h] = jnp.exp2(g_last) * st + lax.dot_general(v_h, kd, (((0,), (0,)), ((), ())),
                                                           preferred_element_type=F32)

        on = o * lax.rsqrt(jnp.mean(o * o, axis=-1, keepdims=True) + EPS) * hn_ref[:, cols]
        og = og_ref[:, cols]
        y_ref[:, cols] = (on * (og * _sigmoid(og))).astype(y_ref.dtype)


def _mix_kernel(mq_ref, mk_ref, mv_ref, mo_ref, gcol_ref, grow_ref, cw_ref, cb_ref, bcol_ref, brow_ref,
                mhn_ref, hq_ref, hf_ref, hi_ref, hg_ref, lbl_ref, hhn_ref,
                x_ref, gm_ref, gh_ref, wm_ref, wh_ref, wo_ref,
                out_ref,
                ym_sc, yh_sc, merged_sc, xpad_sc, c_sc, n_sc, m_sc, g_sc, k_sc, q_sc, gr_sc, st_sc,
                *, blocks_per_seq):
    s = pl.program_id(0)
    cur = s % 2
    prv = 1 - cur

    @pl.when(s == 0)
    def _():
        ym_sc[...] = jnp.zeros_like(ym_sc)
        yh_sc[...] = jnp.zeros_like(yh_sc)

    @pl.when(s % blocks_per_seq == 0)
    def _():
        xpad_sc[0:8, :] = jnp.zeros((8, 2 * MLSTM_QK_W), F32)
        c_sc[...] = jnp.zeros_like(c_sc)
        n_sc[...] = jnp.zeros_like(n_sc)
        m_sc[...] = jnp.zeros_like(m_sc)
        st_sc[...] = jnp.zeros_like(st_sc)

    def gate_piece(c):
        def run():
            cols = slice(c * MERGE_GATE_COLS, (c + 1) * MERGE_GATE_COLS)
            pm = jnp.dot(ym_sc[prv], wm_ref[:, cols], preferred_element_type=F32)
            ph = jnp.dot(yh_sc[prv], wh_ref[:, cols], preferred_element_type=F32)
            merged = _sigmoid(gm_ref[:, cols]) * pm + _sigmoid(gh_ref[:, cols]) * ph
            merged_sc[:, cols] = merged.astype(BF16)
        return run

    def out_piece(c):
        def run():
            cols = slice(c * MERGE_OUT_COLS, (c + 1) * MERGE_OUT_COLS)
            out_ref[:, cols] = x_ref[:, cols] + jnp.dot(merged_sc[...], wo_ref[:, cols],
                                                        preferred_element_type=F32)
        return run

    pieces = ([gate_piece(c) for c in range(D_MODEL // MERGE_GATE_COLS)]
              + [out_piece(c) for c in range(D_MODEL // MERGE_OUT_COLS)])
    n_heads = MLSTM_HEADS + HGRN_HEADS
    n_slots = MIX_SUBBLOCKS * n_heads
    slot_of = [(i * n_slots) // len(pieces) for i in range(len(pieces))]
    hooks = [(pieces[slot_of.index(i)] if i in slot_of else (lambda: None)) for i in range(n_slots)]

    for sub in range(MIX_SUBBLOCKS):
        rows = pl.ds(sub * MIX_TB, MIX_TB)
        sub_hooks = hooks[sub * n_heads:(sub + 1) * n_heads]
        _mlstm_block(mq_ref.at[rows], mk_ref.at[rows], mv_ref.at[rows], mo_ref.at[rows], gcol_ref.at[rows],
                     grow_ref.at[:, rows], cw_ref, cb_ref, bcol_ref, brow_ref, mhn_ref,
                     ym_sc.at[cur, rows], xpad_sc, c_sc, n_sc, m_sc, sub_hooks[:MLSTM_HEADS])
        _hgrn_block(hq_ref.at[rows], hf_ref.at[rows], hi_ref.at[rows], hg_ref.at[rows], lbl_ref, hhn_ref,
                    yh_sc.at[cur, rows], g_sc, k_sc, q_sc, gr_sc, st_sc, sub_hooks[MLSTM_HEADS:])


def _const_spec(shape):
    return pl.BlockSpec(shape, lambda *_: (0,) * len(shape), pipeline_mode=pl.Buffered(1))


def _mix(x2d, proj, gates_col, gates_row, conv_w, conv_b, bias_col, bias_row, m_head_norm,
         lb_logits, h_head_norm, w_m, w_h, w_o, *, seq_len):
    n, d = x2d.shape
    tb = MIX_TB * MIX_SUBBLOCKS
    nblk = n // tb
    qw, vw, hw = MLSTM_QK_W, MLSTM_V_W, HGRN_W

    def cur(col_block):
        return lambda s: (jnp.minimum(s, nblk - 1), col_block)

    def prev(col_block):
        return lambda s: (jnp.maximum(s - 1, 0), col_block)

    return pl.pallas_call(
        functools.partial(_mix_kernel, blocks_per_seq=seq_len // tb),
        grid=(nblk + 1,),
        in_specs=[
            pl.BlockSpec((tb, qw), cur(COL_MQ // qw)),
            pl.BlockSpec((tb, qw), cur(COL_MK // qw)),
            pl.BlockSpec((tb, vw), cur(COL_MV // vw)),
            pl.BlockSpec((tb, vw), cur(COL_MO // vw)),
            pl.BlockSpec((tb, LANES), cur(0)),
            pl.BlockSpec((2 * MLSTM_HEADS, tb), lambda s: (0, jnp.minimum(s, nblk - 1))),
            _const_spec((CONV_WIDTH, 2 * qw)),
            _const_spec((1, 2 * qw)),
            _const_spec((1, LANES)),
            _const_spec((2 * MLSTM_HEADS, 1)),
            _const_spec((1, vw)),
            pl.BlockSpec((tb, hw), cur(COL_HQ // hw)),
            pl.BlockSpec((tb, hw), cur(COL_HF // hw)),
            pl.BlockSpec((tb, hw), cur(COL_HI // hw)),
            pl.BlockSpec((tb, hw), cur(COL_HG // hw)),
            _const_spec((2, hw)),
            _const_spec((1, hw)),
            pl.BlockSpec((tb, d), prev(0)),
            pl.BlockSpec((tb, d), prev(COL_GM // d)),
            pl.BlockSpec((tb, d), prev(COL_GH // d)),
            _const_spec(w_m.shape),
            _const_spec(w_h.shape),
            _const_spec(w_o.shape),
        ],
        out_specs=pl.BlockSpec((tb, d), prev(0)),
        out_shape=jax.ShapeDtypeStruct((n, d), F32),
        scratch_shapes=[
            pltpu.VMEM((2, tb, vw), BF16),
            pltpu.VMEM((2, tb, hw), BF16),
            pltpu.VMEM((tb, d), BF16),
            pltpu.VMEM((MIX_TB + 8, 2 * qw), F32),
            pltpu.VMEM((MLSTM_HEADS, MLSTM_QK, MLSTM_V), F32),
            pltpu.VMEM((8, MLSTM_QK), F32),
            pltpu.VMEM((8, LANES), F32),
            pltpu.VMEM((MIX_TB, hw), F32),
            pltpu.VMEM((MIX_TB, hw), F32),
            pltpu.VMEM((MIX_TB, hw), F32),
            pltpu.VMEM((HGRN_NUM_LEVELS, MIX_TB, hw), F32),
            pltpu.VMEM((HGRN_HEADS, HGRN_DV, HGRN_DK), F32),
        ],
        compiler_params=_params("arbitrary"),
        name="mix",
    )(proj, proj, proj, proj, gates_col, gates_row, conv_w, conv_b, bias_col, bias_row, m_head_norm,
      proj, proj, proj, proj, lb_logits, h_head_norm, x2d, proj, proj, w_m, w_h, w_o)


def _xattn_prep_kernel(m_ref, g_ref, wk_ref, wv_ref, wq_ref, wo_ref, qk_ref, vo_ref):
    m = _rms(m_ref[0], g_ref[...]).astype(BF16)
    k = jnp.dot(m, wk_ref[...].astype(BF16), preferred_element_type=F32).astype(BF16)
    v = jnp.dot(m, wv_ref[...].astype(BF16), preferred_element_type=F32).astype(BF16)
    qk = lax.dot_general(wq_ref[...].astype(BF16), k, (((1,), (1,)), ((), ())),
                         preferred_element_type=F32)
    qk_ref[0] = (qk * (XATTN_HEAD_DIM ** -0.5)).astype(qk_ref.dtype)
    vo_ref[0] = jnp.dot(v, wo_ref[...].astype(BF16), preferred_element_type=F32).astype(vo_ref.dtype)


def _xattn_prep(mem, g, w_kv, w_q, w_o):
    b, m, d = mem.shape
    hd = XATTN_HEAD_DIM
    nh = XATTN_HEADS
    return pl.pallas_call(
        _xattn_prep_kernel,
        grid=(nh, b),
        in_specs=[
            pl.BlockSpec((1, m, d), lambda h, i: (i, 0, 0)),
            pl.BlockSpec((1, d), lambda h, i: (0, 0)),
            pl.BlockSpec((d, hd), lambda h, i: (0, h)),
            pl.BlockSpec((d, hd), lambda h, i: (0, nh + h)),
            pl.BlockSpec((d, hd), lambda h, i: (0, h)),
            pl.BlockSpec((hd, d), lambda h, i: (h, 0)),
        ],
        out_specs=[
            pl.BlockSpec((1, d, m), lambda h, i: (i, 0, h)),
            pl.BlockSpec((1, m, d), lambda h, i: (i, h, 0)),
        ],
        out_shape=[
            jax.ShapeDtypeStruct((b, d, nh * m), BF16),
            jax.ShapeDtypeStruct((b, nh * m, d), BF16),
        ],
        compiler_params=_params("arbitrary", "arbitrary"),
        name="xattn_prep",
    )(mem, g.reshape(1, d), w_kv, w_kv, w_q, w_o)


def _xattn_kernel(x_ref, g_ref, qk_ref, vo_ref, o_ref, *, mem_len):
    x = x_ref[0]
    h = _rms(x, g_ref[...]).astype(BF16)
    s = jnp.dot(h, qk_ref[0], preferred_element_type=F32)
    probs = []
    for hd in range(XATTN_HEADS):
        s_h = s[:, hd * mem_len:(hd + 1) * mem_len]
        e = jnp.exp(s_h - jnp.max(s_h, axis=-1, keepdims=True))
        probs.append((e / jnp.sum(e, axis=-1, keepdims=True)).astype(BF16))
    p = jnp.concatenate(probs, axis=-1)
    o_ref[0] = x + jnp.dot(p, vo_ref[0], preferred_element_type=F32)


def _xattn(x, g, qk, vo, *, tm=512):
    b, t, d = x.shape
    hm = qk.shape[2]
    return pl.pallas_call(
        functools.partial(_xattn_kernel, mem_len=hm // XATTN_HEADS),
        grid=(b, t // tm),
        in_specs=[
            pl.BlockSpec((1, tm, d), lambda i, c: (i, c, 0)),
            _const_spec((1, d)),
            pl.BlockSpec((1, d, hm), lambda i, c: (i, 0, 0)),
            pl.BlockSpec((1, hm, d), lambda i, c: (i, 0, 0)),
        ],
        out_specs=pl.BlockSpec((1, tm, d), lambda i, c: (i, c, 0)),
        out_shape=jax.ShapeDtypeStruct((b, t, d), F32),
        compiler_params=_params("parallel", "parallel"),
        name="xattn",
    )(x, g.reshape(1, d), qk, vo)


def kernel(x, mem, norm_ffn1, ffn1_w1, ffn1_w3, ffn1_w2, norm_mix, w_in, mlstm_conv_w, mlstm_conv_b,
           mlstm_ig_bias, mlstm_fg_bias, mlstm_head_norm, hgrn_lb_logits, hgrn_head_norm, w_proj_m,
           w_proj_h, w_out, norm_xattn, norm_mem, xattn_wq, xattn_wkv, xattn_wo, norm_ffn2, ffn2_w1,
           ffn2_w3, ffn2_w2, norm_final):
    b, t, d = x.shape
    depth = norm_ffn1.shape[0]
    assert depth == 1 and hgrn_lb_logits.shape[0] == 2
    n = b * t
    l = 0
    bf = lambda w: w.astype(BF16)

    w_in_t = jnp.swapaxes(w_in[l], 0, 1)
    n_gate = 2 * MLSTM_HEADS
    gate_bias = jnp.concatenate([mlstm_ig_bias[l], mlstm_fg_bias[l]]).astype(F32)
    bias_col = jnp.pad(gate_bias, (0, LANES - n_gate)).reshape(1, LANES)
    bias_row = gate_bias.reshape(n_gate, 1)

    x1, hn, gates_col, gates_row = _ffn(x.reshape(n, d), norm_ffn1[l], ffn1_w1[l], ffn1_w3[l], ffn1_w2[l],
                                        norm_mix[l], w_in_t)

    proj = _in_proj(hn, w_in_t)
    x2 = _mix(x1, proj, gates_col, gates_row, mlstm_conv_w[l], mlstm_conv_b[l].reshape(1, -1),
              bias_col, bias_row, mlstm_head_norm[l].reshape(1, -1), hgrn_lb_logits,
              hgrn_head_norm[l].reshape(1, -1), bf(w_proj_m[l]), bf(w_proj_h[l]), bf(w_out[l]), seq_len=t)

    qk, vo = _xattn_prep(mem, norm_mem[l], xattn_wkv[l], xattn_wq[l], xattn_wo[l])
    x3 = _xattn(x2.reshape(b, t, d), norm_xattn[l], qk, vo)

    (out,) = _ffn(x3.reshape(n, d), norm_ffn2[l], ffn2_w1[l], ffn2_w3[l], ffn2_w2[l], norm_final)
    return out.reshape(b, t, d)
```

```python
import functools

import jax
import jax.numpy as jnp
from jax import lax
from jax.experimental import pallas as pl
from jax.experimental.pallas import tpu as pltpu

F32 = jnp.float32
BF16 = jnp.bfloat16
EPS = 1e-6
LOG2_E = 1.4426950408889634

D_MODEL = 2048
D_FF = 5632
MLSTM_HEADS = 4
MLSTM_QK = 128
MLSTM_V = 256
CONV_WIDTH = 4
HGRN_HEADS = 8
HGRN_DK = 128
HGRN_DV = 128
XATTN_HEADS = 4
XATTN_HEAD_DIM = D_MODEL // XATTN_HEADS

MLSTM_QK_W = MLSTM_HEADS * MLSTM_QK
MLSTM_V_W = MLSTM_HEADS * MLSTM_V
HGRN_W = HGRN_HEADS * HGRN_DK

VMEM_LIMIT_BYTES = 60000 * 1024
LANES = 128

COL_GM = 0
COL_GH = COL_GM + D_MODEL
COL_MQ = COL_GH + D_MODEL
COL_MK = COL_MQ + MLSTM_QK_W
COL_MV = COL_MK + MLSTM_QK_W
COL_MO = COL_MV + MLSTM_V_W
COL_HQ = COL_MO + MLSTM_V_W
COL_HF = COL_HQ + HGRN_W
COL_HI = COL_HF + HGRN_W
COL_HG = COL_HI + HGRN_W
PROJ_W = COL_HG + HGRN_W

MIX_GATE_ROW = 2 * MLSTM_QK_W + 2 * MLSTM_V_W
MIX_HGRN_ROW = MIX_GATE_ROW + 2 * MLSTM_HEADS
MIX_MERGE_ROW = MIX_HGRN_ROW + 4 * HGRN_W

PROJ_SUB_ROWS = 1024
FFN_NORM_ROWS = 64
FFN_SUB_COLS = 256

MIX_TB = 128
HGRN_NUM_LEVELS = MIX_TB.bit_length() - 1
MIX_SUBBLOCKS = 2
MERGE_GATE_COLS = 512
MERGE_OUT_COLS = 256
HGRN_COARSE_LEVELS = tuple(1 << i for i in range(3, HGRN_NUM_LEVELS))


def _rms(x, g):
    return x * lax.rsqrt(jnp.mean(x * x, axis=-1, keepdims=True) + EPS) * g


def _sigmoid(x):
    return 1.0 / (1.0 + jnp.exp(-x))


def _log_sigmoid(x):
    return jnp.minimum(x, 0.0) - jnp.log(1.0 + jnp.exp(-jnp.abs(x)))


def _neg_abs(x):
    return pltpu.bitcast(pltpu.bitcast(x, jnp.uint32) | jnp.uint32(0x80000000), F32)


def _params(*sem):
    return pltpu.CompilerParams(dimension_semantics=sem, vmem_limit_bytes=VMEM_LIMIT_BYTES)


def _ffn_kernel(*refs, mode):
    if mode == "mix":
        (x_ref, g_ref, w1_ref, w3_ref, w2_ref, gn_ref, wg_ref,
         o_ref, hn_ref, gcol_ref, grow_ref, h_sc) = refs
    else:
        x_ref, g_ref, w1_ref, w3_ref, w2_ref, gn_ref, o_ref, h_sc = refs
    j = pl.program_id(1)

    n_row_chunks = x_ref.shape[0] // FFN_NORM_ROWS

    def row_chunk(r):
        return pl.ds(pl.multiple_of(r * FFN_NORM_ROWS, FFN_NORM_ROWS), FFN_NORM_ROWS)

    @pl.when(j == 0)
    def _():
        def norm_rows(r, carry):
            rows = row_chunk(r)
            h_sc[rows, :] = _rms(x_ref[rows, :], g_ref[...]).astype(BF16)
            return carry
        lax.fori_loop(0, n_row_chunks, norm_rows, 0)
        o_ref[...] = jnp.zeros_like(o_ref)

    h = h_sc[...]
    for c in range(w1_ref.shape[1] // FFN_SUB_COLS):
        cols = slice(c * FFN_SUB_COLS, (c + 1) * FFN_SUB_COLS)
        a = jnp.dot(h, w1_ref[:, cols].astype(BF16), preferred_element_type=F32)
        b = jnp.dot(h, w3_ref[:, cols].astype(BF16), preferred_element_type=F32)
        act = (a * _sigmoid(a) * b).astype(BF16)
        o_ref[...] += jnp.dot(act, w2_ref[cols, :].astype(BF16), preferred_element_type=F32)

    @pl.when(j == pl.num_programs(1) - 1)
    def _():
        def finish_rows(r, carry):
            rows = row_chunk(r)
            y = x_ref[rows, :] + 0.5 * o_ref[rows, :]
            if mode == "final":
                o_ref[rows, :] = _rms(y, gn_ref[...])
            else:
                o_ref[rows, :] = y
                hn_ref[rows, :] = _rms(y, gn_ref[...]).astype(BF16)
            return carry
        lax.fori_loop(0, n_row_chunks, finish_rows, 0)

        if mode == "mix":
            nt_dims = (((1,), (1,)), ((), ()))
            hn = hn_ref[...]
            wg = wg_ref[...].astype(BF16)
            wg_pad = jnp.concatenate([wg, jnp.zeros((LANES - wg.shape[0], wg.shape[1]), BF16)], axis=0)
            gcol = lax.dot_general(hn, wg_pad, nt_dims, preferred_element_type=F32)
            gcol_ref[...] = gcol
            grow_ref[...] = gcol.T[0:wg.shape[0], :]


def _ffn(x2d, g, w1, w3, w2, g_next, w_gates_t=None, *, tm=1024, tf=512):
    n, d = x2d.shape
    f = w1.shape[1]
    mode = "final" if w_gates_t is None else "mix"
    x_mode = {} if mode == "final" else dict(pipeline_mode=pl.Buffered(1))
    in_specs = [
        pl.BlockSpec((tm, d), lambda i, j: (i, 0), **x_mode),
        pl.BlockSpec((1, d), lambda i, j: (0, 0)),
        pl.BlockSpec((d, tf), lambda i, j: (0, j)),
        pl.BlockSpec((d, tf), lambda i, j: (0, j)),
        pl.BlockSpec((tf, d), lambda i, j: (j, 0)),
        pl.BlockSpec((1, d), lambda i, j: (0, 0)),
    ]
    args = [x2d, g.reshape(1, d), w1, w3, w2, g_next.reshape(1, d)]
    single = dict(pipeline_mode=pl.Buffered(1))
    out_specs = [pl.BlockSpec((tm, d), lambda i, j: (i, 0), **single)]
    out_shape = [jax.ShapeDtypeStruct((n, d), F32)]
    if mode == "mix":
        ng = 2 * MLSTM_HEADS
        in_specs.append(pl.BlockSpec((ng, d), lambda i, j: (MIX_GATE_ROW // ng, 0)))
        args.append(w_gates_t)
        out_specs += [
            pl.BlockSpec((tm, d), lambda i, j: (i, 0), **single),
            pl.BlockSpec((tm, LANES), lambda i, j: (i, 0)),
            pl.BlockSpec((ng, tm), lambda i, j: (0, i)),
        ]
        out_shape += [
            jax.ShapeDtypeStruct((n, d), BF16),
            jax.ShapeDtypeStruct((n, LANES), F32),
            jax.ShapeDtypeStruct((ng, n), F32),
        ]
    return pl.pallas_call(
        functools.partial(_ffn_kernel, mode=mode),
        grid=(n // tm, f // tf),
        in_specs=in_specs,
        out_specs=out_specs,
        out_shape=out_shape,
        scratch_shapes=[pltpu.VMEM((tm, d), BF16)],
        compiler_params=_params("parallel", "arbitrary"),
        name="ffn_" + mode,
    )(*args)


def _in_proj_kernel(h_ref, wt_ref, p_ref, w_sc):
    @pl.when(pl.program_id(1) == 0)
    def _():
        w_sc[...] = wt_ref[...].astype(BF16)

    for r in range(0, h_ref.shape[0], PROJ_SUB_ROWS):
        rows = slice(r, r + PROJ_SUB_ROWS)
        p_ref[rows, :] = lax.dot_general(h_ref[rows, :], w_sc[...], (((1,), (1,)), ((), ())),
                                         preferred_element_type=F32)


def _proj_row_offset(j, tn):
    n_gate_tiles = 2 * D_MODEL // tn
    n_mlstm_tiles = MIX_GATE_ROW // tn
    off8 = jnp.where(j < n_gate_tiles, MIX_MERGE_ROW // 8 + j * (tn // 8),
                     jnp.where(j < n_gate_tiles + n_mlstm_tiles, (j - n_gate_tiles) * (tn // 8),
                               MIX_HGRN_ROW // 8 + (j - n_gate_tiles - n_mlstm_tiles) * (tn // 8)))
    return off8 * 8


def _in_proj(hn, w_t, *, tm=2048, tn=1024):
    n, d = hn.shape
    return pl.pallas_call(
        _in_proj_kernel,
        grid=(PROJ_W // tn, n // tm),
        in_specs=[
            pl.BlockSpec((tm, d), lambda j, i: (i, 0)),
            pl.BlockSpec((pl.Element(tn), pl.Element(d)), lambda j, i: (_proj_row_offset(j, tn), 0)),
        ],
        out_specs=pl.BlockSpec((tm, tn), lambda j, i: (i, j)),
        out_shape=jax.ShapeDtypeStruct((n, PROJ_W), F32),
        scratch_shapes=[pltpu.VMEM((tn, d), BF16)],
        compiler_params=_params("arbitrary", "arbitrary"),
        name="in_proj",
    )(hn, w_t)


def _mlstm_block(q_ref, k_ref, v_ref, o_ref, gcol_ref, grow_ref, cw_ref, cb_ref, bcol_ref, brow_ref,
                 hn_ref, y_ref, xpad_sc, c_sc, n_sc, m_sc, before_head):
    L = MIX_TB
    H = MLSTM_HEADS

    xpad_sc[8:8 + L, 0:MLSTM_QK_W] = q_ref[...]
    xpad_sc[8:8 + L, MLSTM_QK_W:2 * MLSTM_QK_W] = k_ref[...]
    acc = jnp.broadcast_to(cb_ref[...], (L, 2 * MLSTM_QK_W))
    for j in range(CONV_WIDTH):
        off = 8 - (CONV_WIDTH - 1) + j
        acc = acc + cw_ref[j:j + 1, :] * xpad_sc[off:off + L, :]
    xpad_sc[0:8, :] = xpad_sc[L:L + 8, :]
    qk = acc * _sigmoid(acc)

    gcol = gcol_ref[...] + bcol_ref[...]
    grow = grow_ref[...] + brow_ref[...]
    fcol = _log_sigmoid(gcol)
    frow = _log_sigmoid(grow)
    r_i = lax.broadcasted_iota(jnp.int32, (L, L), 0)
    c_i = lax.broadcasted_iota(jnp.int32, (L, L), 1)
    causal = r_i >= c_i
    tril = jnp.where(causal, 1.0, 0.0).astype(F32)
    triu = jnp.where(r_i <= c_i, 1.0, 0.0).astype(F32)
    bcol = jnp.dot(tril, fcol, preferred_element_type=F32, precision=lax.Precision.HIGHEST)
    brow = jnp.dot(frow, triu, preferred_element_type=F32, precision=lax.Precision.HIGHEST)

    for h in range(H):
        before_head[h]()
        b_c = bcol[:, H + h:H + h + 1]
        i_c = gcol[:, h:h + 1]
        b_r = brow[H + h:H + h + 1, :]
        i_r = grow[h:h + 1, :]
        m_prev = m_sc[h:h + 1, 0:1]

        d_log = jnp.where(causal, b_c - (b_r - i_r), -jnp.inf)
        inter_log = b_c + m_prev
        m_t = jnp.maximum(jnp.max(d_log, axis=1, keepdims=True), inter_log)

        q_h = qk[:, h * MLSTM_QK:(h + 1) * MLSTM_QK]
        k_h = qk[:, MLSTM_QK_W + h * MLSTM_QK:MLSTM_QK_W + (h + 1) * MLSTM_QK] * (MLSTM_QK ** -0.5)
        v_h = v_ref[:, h * MLSTM_V:(h + 1) * MLSTM_V].astype(BF16)
        q_b = q_h.astype(BF16)

        s = lax.dot_general(q_b, k_h.astype(BF16), (((1,), (1,)), ((), ())), preferred_element_type=F32)
        s = s * jnp.exp(d_log - m_t)
        w_inter = jnp.exp(inter_log - m_t)
        c_h = c_sc[h]
        n_h = n_sc[h:h + 1, :]
        num = (jnp.dot(s.astype(BF16), v_h, preferred_element_type=F32)
               + w_inter * jnp.dot(q_b, c_h.astype(BF16), preferred_element_type=F32))
        den = jnp.sum(s, axis=1, keepdims=True) + w_inter * jnp.sum(q_h * n_h, axis=1, keepdims=True)
        hh = num / jnp.maximum(jnp.abs(den), jnp.exp(-m_t))

        hn = hh * lax.rsqrt(jnp.mean(hh * hh, axis=-1, keepdims=True) + EPS)
        hn = hn * hn_ref[:, h * MLSTM_V:(h + 1) * MLSTM_V]
        y = hn * _sigmoid(o_ref[:, h * MLSTM_V:(h + 1) * MLSTM_V])
        y_ref[:, h * MLSTM_V:(h + 1) * MLSTM_V] = y.astype(y_ref.dtype)

        b_last = b_c[L - 1:L, :]
        a_log = b_last - b_c + i_c
        m_new = jnp.maximum(b_last + m_prev, jnp.max(a_log, axis=0, keepdims=True))
        w_a = jnp.exp(a_log - m_new)
        decay = jnp.exp(b_last + m_prev - m_new)
        kw = k_h * w_a
        c_sc[h] = decay * c_h + lax.dot_general(kw.astype(BF16), v_h, (((0,), (0,)), ((), ())),
                                                preferred_element_type=F32)
        n_sc[h:h + 1, :] = decay * n_h + jnp.sum(kw, axis=0, keepdims=True)
        m_sc[h:h + 1, :] = jnp.broadcast_to(m_new, (1, LANES))


def _hgrn_block(q_ref, f_ref, v_ref, og_ref, lbl_ref, hn_ref, y_ref, g_sc, k_sc, q_sc, gr_sc, st_sc,
                before_head):
    TB = MIX_TB
    H = HGRN_HEADS
    DK = HGRN_DK
    W = H * DK
    NT = TB // 8

    lbl = lbl_ref[...]
    lmax = jnp.max(lbl, axis=0, keepdims=True)
    le = jnp.exp(lbl - lmax)
    lb = le[1:2, :] / jnp.sum(le, axis=0, keepdims=True)

    f = lb + (1.0 - lb) * _sigmoid(f_ref[...])
    k_sc[...] = 1.0 - f
    qraw = q_ref[...]
    q_sc[...] = qraw * _sigmoid(qraw) * (DK ** -0.5)
    r_i = lax.broadcasted_iota(jnp.int32, (TB, TB), 0)
    c_i = lax.broadcasted_iota(jnp.int32, (TB, TB), 1)
    tri = jnp.where(r_i >= c_i, 1.0, 0.0).astype(F32)
    g = jnp.dot(tri, jnp.log(f) * LOG2_E, preferred_element_type=F32, precision=lax.Precision.HIGHEST)
    g_sc[...] = g

    g3 = g.reshape(NT, 8, W)
    sub = lax.broadcasted_iota(jnp.int32, (NT, 8, W), 1)
    bit0 = (sub & 1) != 0
    bit1 = (sub & 2) != 0
    bit2 = (sub & 4) != 0
    last2 = jnp.where(bit0, g3, pltpu.roll(g3, 7, 1))
    last4 = jnp.where(bit1, last2, pltpu.roll(last2, 6, 1))
    last4_r = pltpu.roll(last4, 4, 1)
    last8 = jnp.where(bit2, last4, last4_r).reshape(TB, W)
    gr_sc[0] = jnp.where(bit0, pltpu.roll(g3, 1, 1), g3).reshape(TB, W)
    gr_sc[1] = jnp.where(bit1, pltpu.roll(last2, 2, 1), last2).reshape(TB, W)
    gr_sc[2] = jnp.where(bit2, last4_r, last4).reshape(TB, W)
    for li, lvl in enumerate(HGRN_COARSE_LEVELS):
        groups = []
        for base in range(0, TB, 2 * lvl):
            src = last8[base + lvl - 8:base + lvl, :]
            groups.extend([src] * (2 * lvl // 8))
        gr_sc[3 + li] = jnp.concatenate(groups, axis=0)

    x_i = r_i ^ c_i
    nt_dims = (((1,), (1,)), ((), ()))
    for h in range(H):
        before_head[h]()
        cols = slice(h * DK, (h + 1) * DK)
        q_h = q_sc[:, cols]
        k_h = k_sc[:, cols]
        g_h = g_sc[:, cols]
        v_h = v_ref[:, cols].astype(BF16)

        q_b = q_h.astype(BF16)
        k_b = k_h.astype(BF16)
        a = lax.dot_general(q_b, k_b, nt_dims, preferred_element_type=F32)
        for li in range(HGRN_NUM_LEVELS):
            e = jnp.exp2(_neg_abs(g_h - gr_sc[li, :, cols])).astype(BF16)
            p = lax.dot_general(q_b * e, k_b * e, nt_dims, preferred_element_type=F32)
            a = jnp.where(x_i >= (1 << li), p, a)
        a = jnp.where(r_i >= c_i, a, 0.0)
        o = jnp.dot(a.astype(BF16), v_h, preferred_element_type=F32)

        st = st_sc[h]
        g_last = g_h[TB - 1:TB, :]
        qd = (q_h * jnp.exp2(g_h)).astype(BF16)
        o = o + lax.dot_general(qd, st.astype(BF16), nt_dims, preferred_element_type=F32)
        kd = (k_h * jnp.exp2(g_last - g_h)).astype(BF16)
        st_sc[h] = jnp.exp2(g_last) * st + lax.dot_general(v_h, kd, (((0,), (0,)), ((), ())),
                                                           preferred_element_type=F32)

        on = o * lax.rsqrt(jnp.mean(o * o, axis=-1, keepdims=True) + EPS) * hn_ref[:, cols]
        og = og_ref[:, cols]
        y_ref[:, cols] = (on * (og * _sigmoid(og))).astype(y_ref.dtype)


def _mix_kernel(mq_ref, mk_ref, mv_ref, mo_ref, gcol_ref, grow_ref, cw_ref, cb_ref, bcol_ref, brow_ref,
                mhn_ref, hq_ref, hf_ref, hi_ref, hg_ref, lbl_ref, hhn_ref,
                x_ref, gm_ref, gh_ref, wm_ref, wh_ref, wo_ref,
                out_ref,
                ym_sc, yh_sc, merged_sc, xpad_sc, c_sc, n_sc, m_sc, g_sc, k_sc, q_sc, gr_sc, st_sc,
                *, blocks_per_seq):
    s = pl.program_id(0)
    cur = s % 2
    prv = 1 - cur

    @pl.when(s == 0)
    def _():
        ym_sc[...] = jnp.zeros_like(ym_sc)
        yh_sc[...] = jnp.zeros_like(yh_sc)

    @pl.when(s % blocks_per_seq == 0)
    def _():
        xpad_sc[0:8, :] = jnp.zeros((8, 2 * MLSTM_QK_W), F32)
        c_sc[...] = jnp.zeros_like(c_sc)
        n_sc[...] = jnp.zeros_like(n_sc)
        m_sc[...] = jnp.zeros_like(m_sc)
        st_sc[...] = jnp.zeros_like(st_sc)

    def gate_piece(c):
        def run():
            cols = slice(c * MERGE_GATE_COLS, (c + 1) * MERGE_GATE_COLS)
            pm = jnp.dot(ym_sc[prv], wm_ref[:, cols], preferred_element_type=F32)
            ph = jnp.dot(yh_sc[prv], wh_ref[:, cols], preferred_element_type=F32)
            merged = _sigmoid(gm_ref[:, cols]) * pm + _sigmoid(gh_ref[:, cols]) * ph
            merged_sc[:, cols] = merged.astype(BF16)
        return run

    def out_piece(c):
        def run():
            cols = slice(c * MERGE_OUT_COLS, (c + 1) * MERGE_OUT_COLS)
            out_ref[:, cols] = x_ref[:, cols] + jnp.dot(merged_sc[...], wo_ref[:, cols],
                                                        preferred_element_type=F32)
        return run

    pieces = ([gate_piece(c) for c in range(D_MODEL // MERGE_GATE_COLS)]
              + [out_piece(c) for c in range(D_MODEL // MERGE_OUT_COLS)])
    n_heads = MLSTM_HEADS + HGRN_HEADS
    n_slots = MIX_SUBBLOCKS * n_heads
    slot_of = [(i * n_slots) // len(pieces) for i in range(len(pieces))]
    hooks = [(pieces[slot_of.index(i)] if i in slot_of else (lambda: None)) for i in range(n_slots)]

    for sub in range(MIX_SUBBLOCKS):
        rows = pl.ds(sub * MIX_TB, MIX_TB)
        sub_hooks = hooks[sub * n_heads:(sub + 1) * n_heads]
        _mlstm_block(mq_ref.at[rows], mk_ref.at[rows], mv_ref.at[rows], mo_ref.at[rows], gcol_ref.at[rows],
                     grow_ref.at[:, rows], cw_ref, cb_ref, bcol_ref, brow_ref, mhn_ref,
                     ym_sc.at[cur, rows], xpad_sc, c_sc, n_sc, m_sc, sub_hooks[:MLSTM_HEADS])
        _hgrn_block(hq_ref.at[rows], hf_ref.at[rows], hi_ref.at[rows], hg_ref.at[rows], lbl_ref, hhn_ref,
                    yh_sc.at[cur, rows], g_sc, k_sc, q_sc, gr_sc, st_sc, sub_hooks[MLSTM_HEADS:])


def _const_spec(shape):
    return pl.BlockSpec(shape, lambda *_: (0,) * len(shape), pipeline_mode=pl.Buffered(1))


def _mix(x2d, proj, gates_col, gates_row, conv_w, conv_b, bias_col, bias_row, m_head_norm,
         lb_logits, h_head_norm, w_m, w_h, w_o, *, seq_len):
    n, d = x2d.shape
    tb = MIX_TB * MIX_SUBBLOCKS
    nblk = n // tb
    qw, vw, hw = MLSTM_QK_W, MLSTM_V_W, HGRN_W

    def cur(col_block):
        return lambda s: (jnp.minimum(s, nblk - 1), col_block)

    def prev(col_block):
        return lambda s: (jnp.maximum(s - 1, 0), col_block)

    return pl.pallas_call(
        functools.partial(_mix_kernel, blocks_per_seq=seq_len // tb),
        grid=(nblk + 1,),
        in_specs=[
            pl.BlockSpec((tb, qw), cur(COL_MQ // qw)),
            pl.BlockSpec((tb, qw), cur(COL_MK // qw)),
            pl.BlockSpec((tb, vw), cur(COL_MV // vw)),
            pl.BlockSpec((tb, vw), cur(COL_MO // vw)),
            pl.BlockSpec((tb, LANES), cur(0)),
            pl.BlockSpec((2 * MLSTM_HEADS, tb), lambda s: (0, jnp.minimum(s, nblk - 1))),
            _const_spec((CONV_WIDTH, 2 * qw)),
            _const_spec((1, 2 * qw)),
            _const_spec((1, LANES)),
            _const_spec((2 * MLSTM_HEADS, 1)),
            _const_spec((1, vw)),
            pl.BlockSpec((tb, hw), cur(COL_HQ // hw)),
            pl.BlockSpec((tb, hw), cur(COL_HF // hw)),
            pl.BlockSpec((tb, hw), cur(COL_HI // hw)),
            pl.BlockSpec((tb, hw), cur(COL_HG // hw)),
            _const_spec((2, hw)),
            _const_spec((1, hw)),
            pl.BlockSpec((tb, d), prev(0)),
            pl.BlockSpec((tb, d), prev(COL_GM // d)),
            pl.BlockSpec((tb, d), prev(COL_GH // d)),
            _const_spec(w_m.shape),
            _const_spec(w_h.shape),
            _const_spec(w_o.shape),
        ],
        out_specs=pl.BlockSpec((tb, d), prev(0)),
        out_shape=jax.ShapeDtypeStruct((n, d), F32),
        scratch_shapes=[
            pltpu.VMEM((2, tb, vw), BF16),
            pltpu.VMEM((2, tb, hw), BF16),
            pltpu.VMEM((tb, d), BF16),
            pltpu.VMEM((MIX_TB + 8, 2 * qw), F32),
            pltpu.VMEM((MLSTM_HEADS, MLSTM_QK, MLSTM_V), F32),
            pltpu.VMEM((8, MLSTM_QK), F32),
            pltpu.VMEM((8, LANES), F32),
            pltpu.VMEM((MIX_TB, hw), F32),
            pltpu.VMEM((MIX_TB, hw), F32),
            pltpu.VMEM((MIX_TB, hw), F32),
            pltpu.VMEM((HGRN_NUM_LEVELS, MIX_TB, hw), F32),
            pltpu.VMEM((HGRN_HEADS, HGRN_DV, HGRN_DK), F32),
        ],
        compiler_params=_params("arbitrary"),
        name="mix",
    )(proj, proj, proj, proj, gates_col, gates_row, conv_w, conv_b, bias_col, bias_row, m_head_norm,
      proj, proj, proj, proj, lb_logits, h_head_norm, x2d, proj, proj, w_m, w_h, w_o)


def _xattn_prep_kernel(m_ref, g_ref, wk_ref, wv_ref, wq_ref, wo_ref, qk_ref, vo_ref):
    m = _rms(m_ref[0], g_ref[...]).astype(BF16)
    k = jnp.dot(m, wk_ref[...].astype(BF16), preferred_element_type=F32).astype(BF16)
    v = jnp.dot(m, wv_ref[...].astype(BF16), preferred_element_type=F32).astype(BF16)
    qk = lax.dot_general(wq_ref[...].astype(BF16), k, (((1,), (1,)), ((), ())),
                         preferred_element_type=F32)
    qk_ref[0] = (qk * (XATTN_HEAD_DIM ** -0.5)).astype(qk_ref.dtype)
    vo_ref[0] = jnp.dot(v, wo_ref[...].astype(BF16), preferred_element_type=F32).astype(vo_ref.dtype)


def _xattn_prep(mem, g, w_kv, w_q, w_o):
    b, m, d = mem.shape
    hd = XATTN_HEAD_DIM
    nh = XATTN_HEADS
    return pl.pallas_call(
        _xattn_prep_kernel,
        grid=(nh, b),
        in_specs=[
            pl.BlockSpec((1, m, d), lambda h, i: (i, 0, 0)),
            pl.BlockSpec((1, d), lambda h, i: (0, 0)),
            pl.BlockSpec((d, hd), lambda h, i: (0, h)),
            pl.BlockSpec((d, hd), lambda h, i: (0, nh + h)),
            pl.BlockSpec((d, hd), lambda h, i: (0, h)),
            pl.BlockSpec((hd, d), lambda h, i: (h, 0)),
        ],
        out_specs=[
            pl.BlockSpec((1, d, m), lambda h, i: (i, 0, h)),
            pl.BlockSpec((1, m, d), lambda h, i: (i, h, 0)),
        ],
        out_shape=[
            jax.ShapeDtypeStruct((b, d, nh * m), BF16),
            jax.ShapeDtypeStruct((b, nh * m, d), BF16),
        ],
        compiler_params=_params("arbitrary", "arbitrary"),
        name="xattn_prep",
    )(mem, g.reshape(1, d), w_kv, w_kv, w_q, w_o)


def _xattn_kernel(x_ref, g_ref, qk_ref, vo_ref, o_ref, *, mem_len):
    x = x_ref[0]
    h = _rms(x, g_ref[...]).astype(BF16)
    s = jnp.dot(h, qk_ref[0], preferred_element_type=F32)
    probs = []
    for hd in range(XATTN_HEADS):
        s_h = s[:, hd * mem_len:(hd + 1) * mem_len]
        e = jnp.exp(s_h - jnp.max(s_h, axis=-1, keepdims=True))
        probs.append((e / jnp.sum(e, axis=-1, keepdims=True)).astype(BF16))
    p = jnp.concatenate(probs, axis=-1)
    o_ref[0] = x + jnp.dot(p, vo_ref[0], preferred_element_type=F32)


def _xattn(x, g, qk, vo, *, tm=512):
    b, t, d = x.shape
    hm = qk.shape[2]
    return pl.pallas_call(
        functools.partial(_xattn_kernel, mem_len=hm // XATTN_HEADS),
        grid=(b, t // tm),
        in_specs=[
            pl.BlockSpec((1, tm, d), lambda i, c: (i, c, 0)),
            _const_spec((1, d)),
            pl.BlockSpec((1, d, hm), lambda i, c: (i, 0, 0)),
            pl.BlockSpec((1, hm, d), lambda i, c: (i, 0, 0)),
        ],
        out_specs=pl.BlockSpec((1, tm, d), lambda i, c: (i, c, 0)),
        out_shape=jax.ShapeDtypeStruct((b, t, d), F32),
        compiler_params=_params("parallel", "parallel"),
        name="xattn",
    )(x, g.reshape(1, d), qk, vo)


def kernel(x, mem, norm_ffn1, ffn1_w1, ffn1_w3, ffn1_w2, norm_mix, w_in, mlstm_conv_w, mlstm_conv_b,
           mlstm_ig_bias, mlstm_fg_bias, mlstm_head_norm, hgrn_lb_logits, hgrn_head_norm, w_proj_m,
           w_proj_h, w_out, norm_xattn, norm_mem, xattn_wq, xattn_wkv, xattn_wo, norm_ffn2, ffn2_w1,
           ffn2_w3, ffn2_w2, norm_final):
    b, t, d = x.shape
    depth = norm_ffn1.shape[0]
    assert depth == 1 and hgrn_lb_logits.shape[0] == 2
    n = b * t
    l = 0
    bf = lambda w: w.astype(BF16)

    w_in_t = jnp.swapaxes(w_in[l], 0, 1)
    n_gate = 2 * MLSTM_HEADS
    gate_bias = jnp.concatenate([mlstm_ig_bias[l], mlstm_fg_bias[l]]).astype(F32)
    bias_col = jnp.pad(gate_bias, (0, LANES - n_gate)).reshape(1, LANES)
    bias_row = gate_bias.reshape(n_gate, 1)

    x1, hn, gates_col, gates_row = _ffn(x.reshape(n, d), norm_ffn1[l], ffn1_w1[l], ffn1_w3[l], ffn1_w2[l],
                                        norm_mix[l], w_in_t)

    proj = _in_proj(hn, w_in_t)
    x2 = _mix(x1, proj, gates_col, gates_row, mlstm_conv_w[l], mlstm_conv_b[l].reshape(1, -1),
              bias_col, bias_row, mlstm_head_norm[l].reshape(1, -1), hgrn_lb_logits,
              hgrn_head_norm[l].reshape(1, -1), bf(w_proj_m[l]), bf(w_proj_h[l]), bf(w_out[l]), seq_len=t)

    qk, vo = _xattn_prep(mem, norm_mem[l], xattn_wkv[l], xattn_wq[l], xattn_wo[l])
    x3 = _xattn(x2.reshape(b, t, d), norm_xattn[l], qk, vo)

    (out,) = _ffn(x3.reshape(n, d), norm_ffn2[l], ffn2_w1[l], ffn2_w3[l], ffn2_w2[l], norm_final)
    return out.reshape(b, t, d)
```

```python
import functools

import jax
import jax.numpy as jnp
from jax import lax
from jax.experimental import pallas as pl
from jax.experimental.pallas import tpu as pltpu

F32 = jnp.float32
BF16 = jnp.bfloat16
EPS = 1e-6
LOG2_E = 1.4426950408889634

D_MODEL = 2048
D_FF = 5632
MLSTM_HEADS = 4
MLSTM_QK = 128
MLSTM_V = 256
CONV_WIDTH = 4
HGRN_HEADS = 8
HGRN_DK = 128
HGRN_DV = 128
XATTN_HEADS = 4
XATTN_HEAD_DIM = D_MODEL // XATTN_HEADS

MLSTM_QK_W = MLSTM_HEADS * MLSTM_QK
MLSTM_V_W = MLSTM_HEADS * MLSTM_V
HGRN_W = HGRN_HEADS * HGRN_DK

VMEM_LIMIT_BYTES = 60000 * 1024
LANES = 128

COL_GM = 0
COL_GH = COL_GM + D_MODEL
COL_MQ = COL_GH + D_MODEL
COL_MK = COL_MQ + MLSTM_QK_W
COL_MV = COL_MK + MLSTM_QK_W
COL_MO = COL_MV + MLSTM_V_W
COL_HQ = COL_MO + MLSTM_V_W
COL_HF = COL_HQ + HGRN_W
COL_HI = COL_HF + HGRN_W
COL_HG = COL_HI + HGRN_W
PROJ_W = COL_HG + HGRN_W

MIX_GATE_ROW = 2 * MLSTM_QK_W + 2 * MLSTM_V_W
MIX_HGRN_ROW = MIX_GATE_ROW + 2 * MLSTM_HEADS
MIX_MERGE_ROW = MIX_HGRN_ROW + 4 * HGRN_W

PROJ_SUB_ROWS = 1024
FFN_NORM_ROWS = 128
FFN_FINISH_ROWS = 64
FFN_SUB_COLS = 256

MIX_TB = 128
HGRN_NUM_LEVELS = MIX_TB.bit_length() - 1
MIX_SUBBLOCKS = 2
MERGE_GATE_COLS = 512
MERGE_OUT_COLS = 256
HGRN_COARSE_LEVELS = tuple(1 << i for i in range(3, HGRN_NUM_LEVELS))


def _rms(x, g):
    return x * lax.rsqrt(jnp.mean(x * x, axis=-1, keepdims=True) + EPS) * g


def _sigmoid(x):
    return 1.0 / (1.0 + jnp.exp(-x))


def _log_sigmoid(x):
    return jnp.minimum(x, 0.0) - jnp.log(1.0 + jnp.exp(-jnp.abs(x)))


def _neg_abs(x):
    return pltpu.bitcast(pltpu.bitcast(x, jnp.uint32) | jnp.uint32(0x80000000), F32)


def _params(*sem):
    return pltpu.CompilerParams(dimension_semantics=sem, vmem_limit_bytes=VMEM_LIMIT_BYTES)


def _ffn_kernel(*refs, mode):
    if mode == "mix":
        (x_ref, g_ref, w1_ref, w3_ref, w2_ref, gn_ref, wg_ref,
         o_ref, hn_ref, gcol_ref, grow_ref, h_sc) = refs
    else:
        x_ref, g_ref, w1_ref, w3_ref, w2_ref, gn_ref, o_ref, h_sc = refs
    j = pl.program_id(1)

    def row_chunk(r, size):
        return pl.ds(pl.multiple_of(r * size, size), size)

    @pl.when(j == 0)
    def _():
        if mode == "final":
            def norm_rows(r, carry):
                rows = row_chunk(r, FFN_NORM_ROWS)
                h_sc[rows, :] = _rms(x_ref[rows, :], g_ref[...]).astype(BF16)
                return carry
            lax.fori_loop(0, x_ref.shape[0] // FFN_NORM_ROWS, norm_rows, 0)
        else:
            h_sc[...] = _rms(x_ref[...], g_ref[...]).astype(BF16)
        o_ref[...] = jnp.zeros_like(o_ref)

    h = h_sc[...]
    for c in range(w1_ref.shape[1] // FFN_SUB_COLS):
        cols = slice(c * FFN_SUB_COLS, (c + 1) * FFN_SUB_COLS)
        a = jnp.dot(h, w1_ref[:, cols].astype(BF16), preferred_element_type=F32)
        b = jnp.dot(h, w3_ref[:, cols].astype(BF16), preferred_element_type=F32)
        act = (a * _sigmoid(a) * b).astype(BF16)
        o_ref[...] += jnp.dot(act, w2_ref[cols, :].astype(BF16), preferred_element_type=F32)

    @pl.when(j == pl.num_programs(1) - 1)
    def _():
        if mode == "final":
            def finish_rows(r, carry):
                rows = row_chunk(r, FFN_FINISH_ROWS)
                y = x_ref[rows, :] + 0.5 * o_ref[rows, :]
                o_ref[rows, :] = _rms(y, gn_ref[...])
                return carry
            lax.fori_loop(0, x_ref.shape[0] // FFN_FINISH_ROWS, finish_rows, 0)
        else:
            y = x_ref[...] + 0.5 * o_ref[...]
            o_ref[...] = y
            hn = _rms(y, gn_ref[...]).astype(BF16)
            hn_ref[...] = hn
            nt_dims = (((1,), (1,)), ((), ()))
            wg = wg_ref[...].astype(BF16)
            wg_pad = jnp.concatenate([wg, jnp.zeros((LANES - wg.shape[0], wg.shape[1]), BF16)], axis=0)
            gcol_ref[...] = lax.dot_general(hn, wg_pad, nt_dims, preferred_element_type=F32)
            grow_ref[...] = lax.dot_general(wg, hn, nt_dims, preferred_element_type=F32)


def _ffn(x2d, g, w1, w3, w2, g_next, w_gates_t=None, *, tm=1024, tf=512):
    n, d = x2d.shape
    f = w1.shape[1]
    mode = "final" if w_gates_t is None else "mix"
    x_mode = {} if mode == "final" else dict(pipeline_mode=pl.Buffered(1))
    in_specs = [
        pl.BlockSpec((tm, d), lambda i, j: (i, 0), **x_mode),
        pl.BlockSpec((1, d), lambda i, j: (0, 0)),
        pl.BlockSpec((d, tf), lambda i, j: (0, j)),
        pl.BlockSpec((d, tf), lambda i, j: (0, j)),
        pl.BlockSpec((tf, d), lambda i, j: (j, 0)),
        pl.BlockSpec((1, d), lambda i, j: (0, 0)),
    ]
    args = [x2d, g.reshape(1, d), w1, w3, w2, g_next.reshape(1, d)]
    single = dict(pipeline_mode=pl.Buffered(1))
    out_specs = [pl.BlockSpec((tm, d), lambda i, j: (i, 0), **single)]
    out_shape = [jax.ShapeDtypeStruct((n, d), F32)]
    if mode == "mix":
        ng = 2 * MLSTM_HEADS
        in_specs.append(pl.BlockSpec((ng, d), lambda i, j: (MIX_GATE_ROW // ng, 0)))
        args.append(w_gates_t)
        out_specs += [
            pl.BlockSpec((tm, d), lambda i, j: (i, 0), **single),
            pl.BlockSpec((tm, LANES), lambda i, j: (i, 0)),
            pl.BlockSpec((ng, tm), lambda i, j: (0, i)),
        ]
        out_shape += [
            jax.ShapeDtypeStruct((n, d), BF16),
            jax.ShapeDtypeStruct((n, LANES), F32),
            jax.ShapeDtypeStruct((ng, n), F32),
        ]
    return pl.pallas_call(
        functools.partial(_ffn_kernel, mode=mode),
        grid=(n // tm, f // tf),
        in_specs=in_specs,
        out_specs=out_specs,
        out_shape=out_shape,
        scratch_shapes=[pltpu.VMEM((tm, d), BF16)],
        compiler_params=_params("parallel", "arbitrary"),
        name="ffn_" + mode,
    )(*args)


def _in_proj_kernel(h_ref, wt_ref, p_ref, w_sc):
    @pl.when(pl.program_id(1) == 0)
    def _():
        w_sc[...] = wt_ref[...].astype(BF16)

    for r in range(0, h_ref.shape[0], PROJ_SUB_ROWS):
        rows = slice(r, r + PROJ_SUB_ROWS)
        p_ref[rows, :] = lax.dot_general(h_ref[rows, :], w_sc[...], (((1,), (1,)), ((), ())),
                                         preferred_element_type=F32)


def _proj_row_offset(j, tn):
    n_gate_tiles = 2 * D_MODEL // tn
    n_mlstm_tiles = MIX_GATE_ROW // tn
    off8 = jnp.where(j < n_gate_tiles, MIX_MERGE_ROW // 8 + j * (tn // 8),
                     jnp.where(j < n_gate_tiles + n_mlstm_tiles, (j - n_gate_tiles) * (tn // 8),
                               MIX_HGRN_ROW // 8 + (j - n_gate_tiles - n_mlstm_tiles) * (tn // 8)))
    return off8 * 8


def _in_proj(hn, w_t, *, tm=2048, tn=1024):
    n, d = hn.shape
    return pl.pallas_call(
        _in_proj_kernel,
        grid=(PROJ_W // tn, n // tm),
        in_specs=[
            pl.BlockSpec((tm, d), lambda j, i: (i, 0)),
            pl.BlockSpec((pl.Element(tn), pl.Element(d)), lambda j, i: (_proj_row_offset(j, tn), 0)),
        ],
        out_specs=pl.BlockSpec((tm, tn), lambda j, i: (i, j)),
        out_shape=jax.ShapeDtypeStruct((n, PROJ_W), F32),
        scratch_shapes=[pltpu.VMEM((tn, d), BF16)],
        compiler_params=_params("arbitrary", "arbitrary"),
        name="in_proj",
    )(hn, w_t)


def _mlstm_block(q_ref, k_ref, v_ref, o_ref, gcol_ref, grow_ref, cw_ref, cb_ref, bcol_ref, brow_ref,
                 hn_ref, y_ref, xpad_sc, c_sc, n_sc, m_sc, before_head):
    L = MIX_TB
    H = MLSTM_HEADS

    xpad_sc[8:8 + L, 0:MLSTM_QK_W] = q_ref[...]
    xpad_sc[8:8 + L, MLSTM_QK_W:2 * MLSTM_QK_W] = k_ref[...]
    acc = jnp.broadcast_to(cb_ref[...], (L, 2 * MLSTM_QK_W))
    for j in range(CONV_WIDTH):
        off = 8 - (CONV_WIDTH - 1) + j
        acc = acc + cw_ref[j:j + 1, :] * xpad_sc[off:off + L, :]
    xpad_sc[0:8, :] = xpad_sc[L:L + 8, :]
    qk = acc * _sigmoid(acc)

    gcol = gcol_ref[...] + bcol_ref[...]
    grow = grow_ref[...] + brow_ref[...]
    fcol = _log_sigmoid(gcol)
    frow = _log_sigmoid(grow)
    r_i = lax.broadcasted_iota(jnp.int32, (L, L), 0)
    c_i = lax.broadcasted_iota(jnp.int32, (L, L), 1)
    causal = r_i >= c_i
    tril = jnp.where(causal, 1.0, 0.0).astype(F32)
    triu = jnp.where(r_i <= c_i, 1.0, 0.0).astype(F32)
    bcol = jnp.dot(tril, fcol, preferred_element_type=F32, precision=lax.Precision.HIGHEST)
    brow = jnp.dot(frow, triu, preferred_element_type=F32, precision=lax.Precision.HIGHEST)

    for h in range(H):
        before_head[h]()
        b_c = bcol[:, H + h:H + h + 1]
        i_c = gcol[:, h:h + 1]
        b_r = brow[H + h:H + h + 1, :]
        i_r = grow[h:h + 1, :]
        m_prev = m_sc[h:h + 1, 0:1]

        d_log = jnp.where(causal, b_c - (b_r - i_r), -jnp.inf)
        inter_log = b_c + m_prev
        m_t = jnp.maximum(jnp.max(d_log, axis=1, keepdims=True), inter_log)

        q_h = qk[:, h * MLSTM_QK:(h + 1) * MLSTM_QK]
        k_h = qk[:, MLSTM_QK_W + h * MLSTM_QK:MLSTM_QK_W + (h + 1) * MLSTM_QK] * (MLSTM_QK ** -0.5)
        v_h = v_ref[:, h * MLSTM_V:(h + 1) * MLSTM_V].astype(BF16)
        q_b = q_h.astype(BF16)

        s = lax.dot_general(q_b, k_h.astype(BF16), (((1,), (1,)), ((), ())), preferred_element_type=F32)
        s = s * jnp.exp(d_log - m_t)
        w_inter = jnp.exp(inter_log - m_t)
        c_h = c_sc[h]
        n_h = n_sc[h:h + 1, :]
        num = (jnp.dot(s.astype(BF16), v_h, preferred_element_type=F32)
               + w_inter * jnp.dot(q_b, c_h.astype(BF16), preferred_element_type=F32))
        den = jnp.sum(s, axis=1, keepdims=True) + w_inter * jnp.sum(q_h * n_h, axis=1, keepdims=True)
        hh = num / jnp.maximum(jnp.abs(den), jnp.exp(-m_t))

        hn = hh * lax.rsqrt(jnp.mean(hh * hh, axis=-1, keepdims=True) + EPS)
        hn = hn * hn_ref[:, h * MLSTM_V:(h + 1) * MLSTM_V]
        y = hn * _sigmoid(o_ref[:, h * MLSTM_V:(h + 1) * MLSTM_V])
        y_ref[:, h * MLSTM_V:(h + 1) * MLSTM_V] = y.astype(y_ref.dtype)

        b_last = b_c[L - 1:L, :]
        a_log = b_last - b_c + i_c
        m_new = jnp.maximum(b_last + m_prev, jnp.max(a_log, axis=0, keepdims=True))
        w_a = jnp.exp(a_log - m_new)
        decay = jnp.exp(b_last + m_prev - m_new)
        kw = k_h * w_a
        c_sc[h] = decay * c_h + lax.dot_general(kw.astype(BF16), v_h, (((0,), (0,)), ((), ())),
                                                preferred_element_type=F32)
        n_sc[h:h + 1, :] = decay * n_h + jnp.sum(kw, axis=0, keepdims=True)
        m_sc[h:h + 1, :] = jnp.broadcast_to(m_new, (1, LANES))


def _hgrn_block(q_ref, f_ref, v_ref, og_ref, lbl_ref, hn_ref, y_ref, g_sc, k_sc, q_sc, gr_sc, st_sc,
                before_head):
    TB = MIX_TB
    H = HGRN_HEADS
    DK = HGRN_DK
    W = H * DK
    NT = TB // 8

    lbl = lbl_ref[...]
    lmax = jnp.max(lbl, axis=0, keepdims=True)
    le = jnp.exp(lbl - lmax)
    lb = le[1:2, :] / jnp.sum(le, axis=0, keepdims=True)

    f = lb + (1.0 - lb) * _sigmoid(f_ref[...])
    k_sc[...] = 1.0 - f
    qraw = q_ref[...]
    q_sc[...] = qraw * _sigmoid(qraw) * (DK ** -0.5)
    r_i = lax.broadcasted_iota(jnp.int32, (TB, TB), 0)
    c_i = lax.broadcasted_iota(jnp.int32, (TB, TB), 1)
    tri = jnp.where(r_i >= c_i, 1.0, 0.0).astype(F32)
    g = jnp.dot(tri, jnp.log(f) * LOG2_E, preferred_element_type=F32, precision=lax.Precision.HIGHEST)
    g_sc[...] = g

    g3 = g.reshape(NT, 8, W)
    sub = lax.broadcasted_iota(jnp.int32, (NT, 8, W), 1)
    bit0 = (sub & 1) != 0
    bit1 = (sub & 2) != 0
    bit2 = (sub & 4) != 0
    last2 = jnp.where(bit0, g3, pltpu.roll(g3, 7, 1))
    last4 = jnp.where(bit1, last2, pltpu.roll(last2, 6, 1))
    last4_r = pltpu.roll(last4, 4, 1)
    last8 = jnp.where(bit2, last4, last4_r).reshape(TB, W)
    gr_sc[0] = jnp.where(bit0, pltpu.roll(g3, 1, 1), g3).reshape(TB, W)
    gr_sc[1] = jnp.where(bit1, pltpu.roll(last2, 2, 1), last2).reshape(TB, W)
    gr_sc[2] = jnp.where(bit2, last4_r, last4).reshape(TB, W)
    for li, lvl in enumerate(HGRN_COARSE_LEVELS):
        groups = []
        for base in range(0, TB, 2 * lvl):
            src = last8[base + lvl - 8:base + lvl, :]
            groups.extend([src] * (2 * lvl // 8))
        gr_sc[3 + li] = jnp.concatenate(groups, axis=0)

    x_i = r_i ^ c_i
    nt_dims = (((1,), (1,)), ((), ()))
    for h in range(H):
        before_head[h]()
        cols = slice(h * DK, (h + 1) * DK)
        q_h = q_sc[:, cols]
        k_h = k_sc[:, cols]
        g_h = g_sc[:, cols]
        v_h = v_ref[:, cols].astype(BF16)

        q_b = q_h.astype(BF16)
        k_b = k_h.astype(BF16)
        a = lax.dot_general(q_b, k_b, nt_dims, preferred_element_type=F32)
        for li in range(HGRN_NUM_LEVELS):
            e = jnp.exp2(_neg_abs(g_h - gr_sc[li, :, cols])).astype(BF16)
            p = lax.dot_general(q_b * e, k_b * e, nt_dims, preferred_element_type=F32)
            a = jnp.where(x_i >= (1 << li), p, a)
        a = jnp.where(r_i >= c_i, a, 0.0)
        o = jnp.dot(a.astype(BF16), v_h, preferred_element_type=F32)

        st = st_sc[h]
        g_last = g_h[TB - 1:TB, :]
        qd = (q_h * jnp.exp2(g_h)).astype(BF16)
        o = o + lax.dot_general(qd, st.astype(BF16), nt_dims, preferred_element_type=F32)
        kd = (k_h * jnp.exp2(g_last - g_h)).astype(BF16)
        st_sc[h] = jnp.exp2(g_last) * st + lax.dot_general(v_h, kd, (((0,), (0,)), ((), ())),
                                                           preferred_element_type=F32)

        on = o * lax.rsqrt(jnp.mean(o * o, axis=-1, keepdims=True) + EPS) * hn_ref[:, cols]
        og = og_ref[:, cols]
        y_ref[:, cols] = (on * (og * _sigmoid(og))).astype(y_ref.dtype)


def _mix_kernel(mq_ref, mk_ref, mv_ref, mo_ref, gcol_ref, grow_ref, cw_ref, cb_ref, bcol_ref, brow_ref,
                mhn_ref, hq_ref, hf_ref, hi_ref, hg_ref, lbl_ref, hhn_ref,
                x_ref, gm_ref, gh_ref, wm_ref, wh_ref, wo_ref,
                out_ref,
                ym_sc, yh_sc, merged_sc, xpad_sc, c_sc, n_sc, m_sc, g_sc, k_sc, q_sc, gr_sc, st_sc,
                *, blocks_per_seq):
    s = pl.program_id(0)
    cur = s % 2
    prv = 1 - cur

    @pl.when(s == 0)
    def _():
        ym_sc[...] = jnp.zeros_like(ym_sc)
        yh_sc[...] = jnp.zeros_like(yh_sc)

    @pl.when(s % blocks_per_seq == 0)
    def _():
        xpad_sc[0:8, :] = jnp.zeros((8, 2 * MLSTM_QK_W), F32)
        c_sc[...] = jnp.zeros_like(c_sc)
        n_sc[...] = jnp.zeros_like(n_sc)
        m_sc[...] = jnp.zeros_like(m_sc)
        st_sc[...] = jnp.zeros_like(st_sc)

    def gate_piece(c):
        def run():
            cols = slice(c * MERGE_GATE_COLS, (c + 1) * MERGE_GATE_COLS)
            pm = jnp.dot(ym_sc[prv], wm_ref[:, cols], preferred_element_type=F32)
            ph = jnp.dot(yh_sc[prv], wh_ref[:, cols], preferred_element_type=F32)
            merged = _sigmoid(gm_ref[:, cols]) * pm + _sigmoid(gh_ref[:, cols]) * ph
            merged_sc[:, cols] = merged.astype(BF16)
        return run

    def out_piece(c):
        def run():
            cols = slice(c * MERGE_OUT_COLS, (c + 1) * MERGE_OUT_COLS)
            out_ref[:, cols] = x_ref[:, cols] + jnp.dot(merged_sc[...], wo_ref[:, cols],
                                                        preferred_element_type=F32)
        return run

    pieces = ([gate_piece(c) for c in range(D_MODEL // MERGE_GATE_COLS)]
              + [out_piece(c) for c in range(D_MODEL // MERGE_OUT_COLS)])
    n_heads = MLSTM_HEADS + HGRN_HEADS
    n_slots = MIX_SUBBLOCKS * n_heads
    slot_of = [(i * n_slots) // len(pieces) for i in range(len(pieces))]
    hooks = [(pieces[slot_of.index(i)] if i in slot_of else (lambda: None)) for i in range(n_slots)]

    for sub in range(MIX_SUBBLOCKS):
        rows = pl.ds(sub * MIX_TB, MIX_TB)
        sub_hooks = hooks[sub * n_heads:(sub + 1) * n_heads]
        _mlstm_block(mq_ref.at[rows], mk_ref.at[rows], mv_ref.at[rows], mo_ref.at[rows], gcol_ref.at[rows],
                     grow_ref.at[:, rows], cw_ref, cb_ref, bcol_ref, brow_ref, mhn_ref,
                     ym_sc.at[cur, rows], xpad_sc, c_sc, n_sc, m_sc, sub_hooks[:MLSTM_HEADS])
        _hgrn_block(hq_ref.at[rows], hf_ref.at[rows], hi_ref.at[rows], hg_ref.at[rows], lbl_ref, hhn_ref,
                    yh_sc.at[cur, rows], g_sc, k_sc, q_sc, gr_sc, st_sc, sub_hooks[MLSTM_HEADS:])


def _const_spec(shape):
    return pl.BlockSpec(shape, lambda *_: (0,) * len(shape), pipeline_mode=pl.Buffered(1))


def _mix(x2d, proj, gates_col, gates_row, conv_w, conv_b, bias_col, bias_row, m_head_norm,
         lb_logits, h_head_norm, w_m, w_h, w_o, *, seq_len):
    n, d = x2d.shape
    tb = MIX_TB * MIX_SUBBLOCKS
    nblk = n // tb
    qw, vw, hw = MLSTM_QK_W, MLSTM_V_W, HGRN_W

    def cur(col_block):
        return lambda s: (jnp.minimum(s, nblk - 1), col_block)

    def prev(col_block):
        return lambda s: (jnp.maximum(s - 1, 0), col_block)

    return pl.pallas_call(
        functools.partial(_mix_kernel, blocks_per_seq=seq_len // tb),
        grid=(nblk + 1,),
        in_specs=[
            pl.BlockSpec((tb, qw), cur(COL_MQ // qw)),
            pl.BlockSpec((tb, qw), cur(COL_MK // qw)),
            pl.BlockSpec((tb, vw), cur(COL_MV // vw)),
            pl.BlockSpec((tb, vw), cur(COL_MO // vw)),
            pl.BlockSpec((tb, LANES), cur(0)),
            pl.BlockSpec((2 * MLSTM_HEADS, tb), lambda s: (0, jnp.minimum(s, nblk - 1))),
            _const_spec((CONV_WIDTH, 2 * qw)),
            _const_spec((1, 2 * qw)),
            _const_spec((1, LANES)),
            _const_spec((2 * MLSTM_HEADS, 1)),
            _const_spec((1, vw)),
            pl.BlockSpec((tb, hw), cur(COL_HQ // hw)),
            pl.BlockSpec((tb, hw), cur(COL_HF // hw)),
            pl.BlockSpec((tb, hw), cur(COL_HI // hw)),
            pl.BlockSpec((tb, hw), cur(COL_HG // hw)),
            _const_spec((2, hw)),
            _const_spec((1, hw)),
            pl.BlockSpec((tb, d), prev(0)),
            pl.BlockSpec((tb, d), prev(COL_GM // d)),
            pl.BlockSpec((tb, d), prev(COL_GH // d)),
            _const_spec(w_m.shape),
            _const_spec(w_h.shape),
            _const_spec(w_o.shape),
        ],
        out_specs=pl.BlockSpec((tb, d), prev(0)),
        out_shape=jax.ShapeDtypeStruct((n, d), F32),
        scratch_shapes=[
            pltpu.VMEM((2, tb, vw), BF16),
            pltpu.VMEM((2, tb, hw), BF16),
            pltpu.VMEM((tb, d), BF16),
            pltpu.VMEM((MIX_TB + 8, 2 * qw), F32),
            pltpu.VMEM((MLSTM_HEADS, MLSTM_QK, MLSTM_V), F32),
            pltpu.VMEM((8, MLSTM_QK), F32),
            pltpu.VMEM((8, LANES), F32),
            pltpu.VMEM((MIX_TB, hw), F32),
            pltpu.VMEM((MIX_TB, hw), F32),
            pltpu.VMEM((MIX_TB, hw), F32),
            pltpu.VMEM((HGRN_NUM_LEVELS, MIX_TB, hw), F32),
            pltpu.VMEM((HGRN_HEADS, HGRN_DV, HGRN_DK), F32),
        ],
        compiler_params=_params("arbitrary"),
        name="mix",
    )(proj, proj, proj, proj, gates_col, gates_row, conv_w, conv_b, bias_col, bias_row, m_head_norm,
      proj, proj, proj, proj, lb_logits, h_head_norm, x2d, proj, proj, w_m, w_h, w_o)


def _xattn_prep_kernel(m_ref, g_ref, wk_ref, wv_ref, wq_ref, wo_ref, qk_ref, vo_ref):
    m = _rms(m_ref[0], g_ref[...]).astype(BF16)
    k = jnp.dot(m, wk_ref[...].astype(BF16), preferred_element_type=F32).astype(BF16)
    v = jnp.dot(m, wv_ref[...].astype(BF16), preferred_element_type=F32).astype(BF16)
    qk = lax.dot_general(wq_ref[...].astype(BF16), k, (((1,), (1,)), ((), ())),
                         preferred_element_type=F32)
    qk_ref[0] = (qk * (XATTN_HEAD_DIM ** -0.5)).astype(qk_ref.dtype)
    vo_ref[0] = jnp.dot(v, wo_ref[...].astype(BF16), preferred_element_type=F32).astype(vo_ref.dtype)


def _xattn_prep(mem, g, w_kv, w_q, w_o):
    b, m, d = mem.shape
    hd = XATTN_HEAD_DIM
    nh = XATTN_HEADS
    return pl.pallas_call(
        _xattn_prep_kernel,
        grid=(nh, b),
        in_specs=[
            pl.BlockSpec((1, m, d), lambda h, i: (i, 0, 0)),
            pl.BlockSpec((1, d), lambda h, i: (0, 0)),
            pl.BlockSpec((d, hd), lambda h, i: (0, h)),
            pl.BlockSpec((d, hd), lambda h, i: (0, nh + h)),
            pl.BlockSpec((d, hd), lambda h, i: (0, h)),
            pl.BlockSpec((hd, d), lambda h, i: (h, 0)),
        ],
        out_specs=[
            pl.BlockSpec((1, d, m), lambda h, i: (i, 0, h)),
            pl.BlockSpec((1, m, d), lambda h, i: (i, h, 0)),
        ],
        out_shape=[
            jax.ShapeDtypeStruct((b, d, nh * m), BF16),
            jax.ShapeDtypeStruct((b, nh * m, d), BF16),
        ],
        compiler_params=_params("arbitrary", "arbitrary"),
        name="xattn_prep",
    )(mem, g.reshape(1, d), w_kv, w_kv, w_q, w_o)


def _xattn_kernel(x_ref, g_ref, qk_ref, vo_ref, o_ref, *, mem_len):
    x = x_ref[0]
    h = _rms(x, g_ref[...]).astype(BF16)
    s = jnp.dot(h, qk_ref[0], preferred_element_type=F32)
    probs = []
    for hd in range(XATTN_HEADS):
        s_h = s[:, hd * mem_len:(hd + 1) * mem_len]
        e = jnp.exp(s_h - jnp.max(s_h, axis=-1, keepdims=True))
        probs.append((e / jnp.sum(e, axis=-1, keepdims=True)).astype(BF16))
    p = jnp.concatenate(probs, axis=-1)
    o_ref[0] = x + jnp.dot(p, vo_ref[0], preferred_element_type=F32)


def _xattn(x, g, qk, vo, *, tm=512):
    b, t, d = x.shape
    hm = qk.shape[2]
    return pl.pallas_call(
        functools.partial(_xattn_kernel, mem_len=hm // XATTN_HEADS),
        grid=(b, t // tm),
        in_specs=[
            pl.BlockSpec((1, tm, d), lambda i, c: (i, c, 0)),
            _const_spec((1, d)),
            pl.BlockSpec((1, d, hm), lambda i, c: (i, 0, 0)),
            pl.BlockSpec((1, hm, d), lambda i, c: (i, 0, 0)),
        ],
        out_specs=pl.BlockSpec((1, tm, d), lambda i, c: (i, c, 0)),
        out_shape=jax.ShapeDtypeStruct((b, t, d), F32),
        compiler_params=_params("parallel", "parallel"),
        name="xattn",
    )(x, g.reshape(1, d), qk, vo)


def kernel(x, mem, norm_ffn1, ffn1_w1, ffn1_w3, ffn1_w2, norm_mix, w_in, mlstm_conv_w, mlstm_conv_b,
           mlstm_ig_bias, mlstm_fg_bias, mlstm_head_norm, hgrn_lb_logits, hgrn_head_norm, w_proj_m,
           w_proj_h, w_out, norm_xattn, norm_mem, xattn_wq, xattn_wkv, xattn_wo, norm_ffn2, ffn2_w1,
           ffn2_w3, ffn2_w2, norm_final):
    b, t, d = x.shape
    depth = norm_ffn1.shape[0]
    assert depth == 1 and hgrn_lb_logits.shape[0] == 2
    n = b * t
    l = 0
    bf = lambda w: w.astype(BF16)

    w_in_t = jnp.swapaxes(w_in[l], 0, 1)
    n_gate = 2 * MLSTM_HEADS
    gate_bias = jnp.concatenate([mlstm_ig_bias[l], mlstm_fg_bias[l]]).astype(F32)
    bias_col = jnp.pad(gate_bias, (0, LANES - n_gate)).reshape(1, LANES)
    bias_row = gate_bias.reshape(n_gate, 1)

    x1, hn, gates_col, gates_row = _ffn(x.reshape(n, d), norm_ffn1[l], ffn1_w1[l], ffn1_w3[l], ffn1_w2[l],
                                        norm_mix[l], w_in_t)

    proj = _in_proj(hn, w_in_t)
    x2 = _mix(x1, proj, gates_col, gates_row, mlstm_conv_w[l], mlstm_conv_b[l].reshape(1, -1),
              bias_col, bias_row, mlstm_head_norm[l].reshape(1, -1), hgrn_lb_logits,
              hgrn_head_norm[l].reshape(1, -1), bf(w_proj_m[l]), bf(w_proj_h[l]), bf(w_out[l]), seq_len=t)

    qk, vo = _xattn_prep(mem, norm_mem[l], xattn_wkv[l], xattn_wq[l], xattn_wo[l])
    x3 = _xattn(x2.reshape(b, t, d), norm_xattn[l], qk, vo)

    (out,) = _ffn(x3.reshape(n, d), norm_ffn2[l], ffn2_w1[l], ffn2_w3[l], ffn2_w2[l], norm_final)
    return out.reshape(b, t, d)
```

```python
import functools

import jax
import jax.numpy as jnp
from jax import lax
from jax.experimental import pallas as pl
from jax.experimental.pallas import tpu as pltpu

F32 = jnp.float32
BF16 = jnp.bfloat16
EPS = 1e-6
LOG2_E = 1.4426950408889634

D_MODEL = 2048
D_FF = 5632
MLSTM_HEADS = 4
MLSTM_QK = 128
MLSTM_V = 256
CONV_WIDTH = 4
HGRN_HEADS = 8
HGRN_DK = 128
HGRN_DV = 128
XATTN_HEADS = 4
XATTN_HEAD_DIM = D_MODEL // XATTN_HEADS

MLSTM_QK_W = MLSTM_HEADS * MLSTM_QK
MLSTM_V_W = MLSTM_HEADS * MLSTM_V
HGRN_W = HGRN_HEADS * HGRN_DK

VMEM_LIMIT_BYTES = 60000 * 1024
LANES = 128

COL_GM = 0
COL_GH = COL_GM + D_MODEL
COL_MQ = COL_GH + D_MODEL
COL_MK = COL_MQ + MLSTM_QK_W
COL_MV = COL_MK + MLSTM_QK_W
COL_MO = COL_MV + MLSTM_V_W
COL_HQ = COL_MO + MLSTM_V_W
COL_HF = COL_HQ + HGRN_W
COL_HI = COL_HF + HGRN_W
COL_HG = COL_HI + HGRN_W
PROJ_W = COL_HG + HGRN_W

MIX_GATE_ROW = 2 * MLSTM_QK_W + 2 * MLSTM_V_W
MIX_HGRN_ROW = MIX_GATE_ROW + 2 * MLSTM_HEADS
MIX_MERGE_ROW = MIX_HGRN_ROW + 4 * HGRN_W

PROJ_SUB_ROWS = 1024
FFN_NORM_ROWS = 64
FFN_FINISH_ROWS = 64
FFN_SUB_COLS = 256

MIX_TB = 128
HGRN_NUM_LEVELS = MIX_TB.bit_length() - 1
MIX_SUBBLOCKS = 2
MERGE_GATE_COLS = 512
MERGE_OUT_COLS = 256
HGRN_COARSE_LEVELS = tuple(1 << i for i in range(3, HGRN_NUM_LEVELS))


def _rms(x, g):
    return x * lax.rsqrt(jnp.mean(x * x, axis=-1, keepdims=True) + EPS) * g


def _sigmoid(x):
    return 1.0 / (1.0 + jnp.exp(-x))


def _log_sigmoid(x):
    return jnp.minimum(x, 0.0) - jnp.log(1.0 + jnp.exp(-jnp.abs(x)))


def _neg_abs(x):
    return pltpu.bitcast(pltpu.bitcast(x, jnp.uint32) | jnp.uint32(0x80000000), F32)


def _params(*sem):
    return pltpu.CompilerParams(dimension_semantics=sem, vmem_limit_bytes=VMEM_LIMIT_BYTES)


def _ffn_kernel(*refs, mode):
    if mode == "mix":
        (x_hbm, g_ref, w1_ref, w3_ref, w2_ref, gn_ref, wg_ref,
         o_ref, hn_ref, gcol_ref, grow_ref, h_sc, x_buf, x_sem) = refs
    else:
        x_hbm, g_ref, w1_ref, w3_ref, w2_ref, gn_ref, o_ref, h_sc, x_buf, x_sem = refs
    i = pl.program_id(0)
    j = pl.program_id(1)
    tm = x_buf.shape[0]

    def x_copy(tile):
        rows = pl.ds(pl.multiple_of(tile * tm, tm), tm)
        return pltpu.make_async_copy(x_hbm.at[rows, :], x_buf, x_sem)

    def row_chunk(r, size):
        return pl.ds(pl.multiple_of(r * size, size), size)

    @pl.when(j == 0)
    def _():
        @pl.when(i == 0)
        def _():
            x_copy(0).start()

        x_copy(i).wait()

        def norm_rows(r, carry):
            rows = row_chunk(r, FFN_NORM_ROWS)
            xr = x_buf[rows, :]
            h_sc[rows, :] = _rms(xr, g_ref[...]).astype(BF16)
            o_ref[rows, :] = 2.0 * xr
            return carry
        lax.fori_loop(0, tm // FFN_NORM_ROWS, norm_rows, 0)

    @pl.when(jnp.logical_and(j == 1, i + 1 < pl.num_programs(0)))
    def _():
        x_copy(i + 1).start()

    h = h_sc[...]
    for c in range(w1_ref.shape[1] // FFN_SUB_COLS):
        cols = slice(c * FFN_SUB_COLS, (c + 1) * FFN_SUB_COLS)
        a = jnp.dot(h, w1_ref[:, cols].astype(BF16), preferred_element_type=F32)
        b = jnp.dot(h, w3_ref[:, cols].astype(BF16), preferred_element_type=F32)
        act = (a * _sigmoid(a) * b).astype(BF16)
        o_ref[...] += jnp.dot(act, w2_ref[cols, :].astype(BF16), preferred_element_type=F32)

    @pl.when(j == pl.num_programs(1) - 1)
    def _():
        if mode == "final":
            def finish_rows(r, carry):
                rows = row_chunk(r, FFN_FINISH_ROWS)
                o_ref[rows, :] = _rms(0.5 * o_ref[rows, :], gn_ref[...])
                return carry
            lax.fori_loop(0, tm // FFN_FINISH_ROWS, finish_rows, 0)
        else:
            y = 0.5 * o_ref[...]
            o_ref[...] = y
            hn = _rms(y, gn_ref[...]).astype(BF16)
            hn_ref[...] = hn
            nt_dims = (((1,), (1,)), ((), ()))
            wg = wg_ref[...].astype(BF16)
            wg_pad = jnp.concatenate([wg, jnp.zeros((LANES - wg.shape[0], wg.shape[1]), BF16)], axis=0)
            gcol_ref[...] = lax.dot_general(hn, wg_pad, nt_dims, preferred_element_type=F32)
            grow_ref[...] = lax.dot_general(wg, hn, nt_dims, preferred_element_type=F32)


def _ffn(x2d, g, w1, w3, w2, g_next, w_gates_t=None, *, tm=1024, tf=512):
    n, d = x2d.shape
    f = w1.shape[1]
    mode = "final" if w_gates_t is None else "mix"
    in_specs = [
        pl.BlockSpec(memory_space=pl.ANY),
        pl.BlockSpec((1, d), lambda i, j: (0, 0)),
        pl.BlockSpec((d, tf), lambda i, j: (0, j)),
        pl.BlockSpec((d, tf), lambda i, j: (0, j)),
        pl.BlockSpec((tf, d), lambda i, j: (j, 0)),
        pl.BlockSpec((1, d), lambda i, j: (0, 0)),
    ]
    args = [x2d, g.reshape(1, d), w1, w3, w2, g_next.reshape(1, d)]
    single = dict(pipeline_mode=pl.Buffered(1))
    out_mode = {} if mode == "final" else single
    out_specs = [pl.BlockSpec((tm, d), lambda i, j: (i, 0), **out_mode)]
    out_shape = [jax.ShapeDtypeStruct((n, d), F32)]
    if mode == "mix":
        ng = 2 * MLSTM_HEADS
        in_specs.append(pl.BlockSpec((ng, d), lambda i, j: (MIX_GATE_ROW // ng, 0)))
        args.append(w_gates_t)
        out_specs += [
            pl.BlockSpec((tm, d), lambda i, j: (i, 0), **single),
            pl.BlockSpec((tm, LANES), lambda i, j: (i, 0)),
            pl.BlockSpec((ng, tm), lambda i, j: (0, i)),
        ]
        out_shape += [
            jax.ShapeDtypeStruct((n, d), BF16),
            jax.ShapeDtypeStruct((n, LANES), F32),
            jax.ShapeDtypeStruct((ng, n), F32),
        ]
    return pl.pallas_call(
        functools.partial(_ffn_kernel, mode=mode),
        grid=(n // tm, f // tf),
        in_specs=in_specs,
        out_specs=out_specs,
        out_shape=out_shape,
        scratch_shapes=[
            pltpu.VMEM((tm, d), BF16),
            pltpu.VMEM((tm, d), F32),
            pltpu.SemaphoreType.DMA(()),
        ],
        compiler_params=_params("arbitrary", "arbitrary"),
        name="ffn_" + mode,
    )(*args)


def _in_proj_kernel(h_ref, wt_ref, p_ref, w_sc):
    @pl.when(pl.program_id(1) == 0)
    def _():
        w_sc[...] = wt_ref[...].astype(BF16)

    for r in range(0, h_ref.shape[0], PROJ_SUB_ROWS):
        rows = slice(r, r + PROJ_SUB_ROWS)
        p_ref[rows, :] = lax.dot_general(h_ref[rows, :], w_sc[...], (((1,), (1,)), ((), ())),
                                         preferred_element_type=F32)


def _proj_row_offset(j, tn):
    n_gate_tiles = 2 * D_MODEL // tn
    n_mlstm_tiles = MIX_GATE_ROW // tn
    off8 = jnp.where(j < n_gate_tiles, MIX_MERGE_ROW // 8 + j * (tn // 8),
                     jnp.where(j < n_gate_tiles + n_mlstm_tiles, (j - n_gate_tiles) * (tn // 8),
                               MIX_HGRN_ROW // 8 + (j - n_gate_tiles - n_mlstm_tiles) * (tn // 8)))
    return off8 * 8


def _in_proj(hn, w_t, *, tm=2048, tn=1024):
    n, d = hn.shape
    return pl.pallas_call(
        _in_proj_kernel,
        grid=(PROJ_W // tn, n // tm),
        in_specs=[
            pl.BlockSpec((tm, d), lambda j, i: (i, 0)),
            pl.BlockSpec((pl.Element(tn), pl.Element(d)), lambda j, i: (_proj_row_offset(j, tn), 0)),
        ],
        out_specs=pl.BlockSpec((tm, tn), lambda j, i: (i, j)),
        out_shape=jax.ShapeDtypeStruct((n, PROJ_W), F32),
        scratch_shapes=[pltpu.VMEM((tn, d), BF16)],
        compiler_params=_params("arbitrary", "arbitrary"),
        name="in_proj",
    )(hn, w_t)


def _mlstm_block(q_ref, k_ref, v_ref, o_ref, gcol_ref, grow_ref, cw_ref, cb_ref, bcol_ref, brow_ref,
                 hn_ref, y_ref, xpad_sc, c_sc, n_sc, m_sc, before_head):
    L = MIX_TB
    H = MLSTM_HEADS

    xpad_sc[8:8 + L, 0:MLSTM_QK_W] = q_ref[...]
    xpad_sc[8:8 + L, MLSTM_QK_W:2 * MLSTM_QK_W] = k_ref[...]
    acc = jnp.broadcast_to(cb_ref[...], (L, 2 * MLSTM_QK_W))
    for j in range(CONV_WIDTH):
        off = 8 - (CONV_WIDTH - 1) + j
        acc = acc + cw_ref[j:j + 1, :] * xpad_sc[off:off + L, :]
    xpad_sc[0:8, :] = xpad_sc[L:L + 8, :]
    qk = acc * _sigmoid(acc)

    gcol = gcol_ref[...] + bcol_ref[...]
    grow = grow_ref[...] + brow_ref[...]
    fcol = _log_sigmoid(gcol)
    frow = _log_sigmoid(grow)
    r_i = lax.broadcasted_iota(jnp.int32, (L, L), 0)
    c_i = lax.broadcasted_iota(jnp.int32, (L, L), 1)
    causal = r_i >= c_i
    tril = jnp.where(causal, 1.0, 0.0).astype(F32)
    triu = jnp.where(r_i <= c_i, 1.0, 0.0).astype(F32)
    bcol = jnp.dot(tril, fcol, preferred_element_type=F32, precision=lax.Precision.HIGHEST)
    brow = jnp.dot(frow, triu, preferred_element_type=F32, precision=lax.Precision.HIGHEST)

    for h in range(H):
        before_head[h]()
        b_c = bcol[:, H + h:H + h + 1]
        i_c = gcol[:, h:h + 1]
        b_r = brow[H + h:H + h + 1, :]
        i_r = grow[h:h + 1, :]
        m_prev = m_sc[h:h + 1, 0:1]

        d_log = jnp.where(causal, b_c - (b_r - i_r), -jnp.inf)
        inter_log = b_c + m_prev
        m_t = jnp.maximum(jnp.max(d_log, axis=1, keepdims=True), inter_log)

        q_h = qk[:, h * MLSTM_QK:(h + 1) * MLSTM_QK]
        k_h = qk[:, MLSTM_QK_W + h * MLSTM_QK:MLSTM_QK_W + (h + 1) * MLSTM_QK] * (MLSTM_QK ** -0.5)
        v_h = v_ref[:, h * MLSTM_V:(h + 1) * MLSTM_V].astype(BF16)
        q_b = q_h.astype(BF16)

        s = lax.dot_general(q_b, k_h.astype(BF16), (((1,), (1,)), ((), ())), preferred_element_type=F32)
        s = s * jnp.exp(d_log - m_t)
        w_inter = jnp.exp(inter_log - m_t)
        c_h = c_sc[h]
        n_h = n_sc[h:h + 1, :]
        num = (jnp.dot(s.astype(BF16), v_h, preferred_element_type=F32)
               + w_inter * jnp.dot(q_b, c_h.astype(BF16), preferred_element_type=F32))
        den = jnp.sum(s, axis=1, keepdims=True) + w_inter * jnp.sum(q_h * n_h, axis=1, keepdims=True)
        hh = num / jnp.maximum(jnp.abs(den), jnp.exp(-m_t))

        hn = hh * lax.rsqrt(jnp.mean(hh * hh, axis=-1, keepdims=True) + EPS)
        hn = hn * hn_ref[:, h * MLSTM_V:(h + 1) * MLSTM_V]
        y = hn * _sigmoid(o_ref[:, h * MLSTM_V:(h + 1) * MLSTM_V])
        y_ref[:, h * MLSTM_V:(h + 1) * MLSTM_V] = y.astype(y_ref.dtype)

        b_last = b_c[L - 1:L, :]
        a_log = b_last - b_c + i_c
        m_new = jnp.maximum(b_last + m_prev, jnp.max(a_log, axis=0, keepdims=True))
        w_a = jnp.exp(a_log - m_new)
        decay = jnp.exp(b_last + m_prev - m_new)
        kw = k_h * w_a
        c_sc[h] = decay * c_h + lax.dot_general(kw.astype(BF16), v_h, (((0,), (0,)), ((), ())),
                                                preferred_element_type=F32)
        n_sc[h:h + 1, :] = decay * n_h + jnp.sum(kw, axis=0, keepdims=True)
        m_sc[h:h + 1, :] = jnp.broadcast_to(m_new, (1, LANES))


def _hgrn_block(q_ref, f_ref, v_ref, og_ref, lbl_ref, hn_ref, y_ref, g_sc, k_sc, q_sc, gr_sc, st_sc,
                before_head):
    TB = MIX_TB
    H = HGRN_HEADS
    DK = HGRN_DK
    W = H * DK
    NT = TB // 8

    lbl = lbl_ref[...]
    lmax = jnp.max(lbl, axis=0, keepdims=True)
    le = jnp.exp(lbl - lmax)
    lb = le[1:2, :] / jnp.sum(le, axis=0, keepdims=True)

    f = lb + (1.0 - lb) * _sigmoid(f_ref[...])
    k_sc[...] = 1.0 - f
    qraw = q_ref[...]
    q_sc[...] = qraw * _sigmoid(qraw) * (DK ** -0.5)
    r_i = lax.broadcasted_iota(jnp.int32, (TB, TB), 0)
    c_i = lax.broadcasted_iota(jnp.int32, (TB, TB), 1)
    tri = jnp.where(r_i >= c_i, 1.0, 0.0).astype(F32)
    g = jnp.dot(tri, jnp.log(f) * LOG2_E, preferred_element_type=F32, precision=lax.Precision.HIGHEST)
    g_sc[...] = g

    g3 = g.reshape(NT, 8, W)
    sub = lax.broadcasted_iota(jnp.int32, (NT, 8, W), 1)
    bit0 = (sub & 1) != 0
    bit1 = (sub & 2) != 0
    bit2 = (sub & 4) != 0
    last2 = jnp.where(bit0, g3, pltpu.roll(g3, 7, 1))
    last4 = jnp.where(bit1, last2, pltpu.roll(last2, 6, 1))
    last4_r = pltpu.roll(last4, 4, 1)
    last8 = jnp.where(bit2, last4, last4_r).reshape(TB, W)
    gr_sc[0] = jnp.where(bit0, pltpu.roll(g3, 1, 1), g3).reshape(TB, W)
    gr_sc[1] = jnp.where(bit1, pltpu.roll(last2, 2, 1), last2).reshape(TB, W)
    gr_sc[2] = jnp.where(bit2, last4_r, last4).reshape(TB, W)
    for li, lvl in enumerate(HGRN_COARSE_LEVELS):
        groups = []
        for base in range(0, TB, 2 * lvl):
            src = last8[base + lvl - 8:base + lvl, :]
            groups.extend([src] * (2 * lvl // 8))
        gr_sc[3 + li] = jnp.concatenate(groups, axis=0)

    x_i = r_i ^ c_i
    nt_dims = (((1,), (1,)), ((), ()))
    for h in range(H):
        before_head[h]()
        cols = slice(h * DK, (h + 1) * DK)
        q_h = q_sc[:, cols]
        k_h = k_sc[:, cols]
        g_h = g_sc[:, cols]
        v_h = v_ref[:, cols].astype(BF16)

        q_b = q_h.astype(BF16)
        k_b = k_h.astype(BF16)
        a = lax.dot_general(q_b, k_b, nt_dims, preferred_element_type=F32)
        for li in range(HGRN_NUM_LEVELS):
            e = jnp.exp2(_neg_abs(g_h - gr_sc[li, :, cols])).astype(BF16)
            p = lax.dot_general(q_b * e, k_b * e, nt_dims, preferred_element_type=F32)
            a = jnp.where(x_i >= (1 << li), p, a)
        a = jnp.where(r_i >= c_i, a, 0.0)
        o = jnp.dot(a.astype(BF16), v_h, preferred_element_type=F32)

        st = st_sc[h]
        g_last = g_h[TB - 1:TB, :]
        qd = (q_h * jnp.exp2(g_h)).astype(BF16)
        o = o + lax.dot_general(qd, st.astype(BF16), nt_dims, preferred_element_type=F32)
        kd = (k_h * jnp.exp2(g_last - g_h)).astype(BF16)
        st_sc[h] = jnp.exp2(g_last) * st + lax.dot_general(v_h, kd, (((0,), (0,)), ((), ())),
                                                           preferred_element_type=F32)

        on = o * lax.rsqrt(jnp.mean(o * o, axis=-1, keepdims=True) + EPS) * hn_ref[:, cols]
        og = og_ref[:, cols]
        y_ref[:, cols] = (on * (og * _sigmoid(og))).astype(y_ref.dtype)


def _mix_kernel(mq_ref, mk_ref, mv_ref, mo_ref, gcol_ref, grow_ref, cw_ref, cb_ref, bcol_ref, brow_ref,
                mhn_ref, hq_ref, hf_ref, hi_ref, hg_ref, lbl_ref, hhn_ref,
                x_ref, gm_ref, gh_ref, wm_ref, wh_ref, wo_ref,
                out_ref,
                ym_sc, yh_sc, merged_sc, xpad_sc, c_sc, n_sc, m_sc, g_sc, k_sc, q_sc, gr_sc, st_sc,
                *, blocks_per_seq):
    s = pl.program_id(0)
    cur = s % 2
    prv = 1 - cur

    @pl.when(s == 0)
    def _():
        ym_sc[...] = jnp.zeros_like(ym_sc)
        yh_sc[...] = jnp.zeros_like(yh_sc)

    @pl.when(s % blocks_per_seq == 0)
    def _():
        xpad_sc[0:8, :] = jnp.zeros((8, 2 * MLSTM_QK_W), F32)
        c_sc[...] = jnp.zeros_like(c_sc)
        n_sc[...] = jnp.zeros_like(n_sc)
        m_sc[...] = jnp.zeros_like(m_sc)
        st_sc[...] = jnp.zeros_like(st_sc)

    def gate_piece(c):
        def run():
            cols = slice(c * MERGE_GATE_COLS, (c + 1) * MERGE_GATE_COLS)
            pm = jnp.dot(ym_sc[prv], wm_ref[:, cols], preferred_element_type=F32)
            ph = jnp.dot(yh_sc[prv], wh_ref[:, cols], preferred_element_type=F32)
            merged = _sigmoid(gm_ref[:, cols]) * pm + _sigmoid(gh_ref[:, cols]) * ph
            merged_sc[:, cols] = merged.astype(BF16)
        return run

    def out_piece(c):
        def run():
            cols = slice(c * MERGE_OUT_COLS, (c + 1) * MERGE_OUT_COLS)
            out_ref[:, cols] = x_ref[:, cols] + jnp.dot(merged_sc[...], wo_ref[:, cols],
                                                        preferred_element_type=F32)
        return run

    pieces = ([gate_piece(c) for c in range(D_MODEL // MERGE_GATE_COLS)]
              + [out_piece(c) for c in range(D_MODEL // MERGE_OUT_COLS)])
    n_heads = MLSTM_HEADS + HGRN_HEADS
    n_slots = MIX_SUBBLOCKS * n_heads
    slot_of = [(i * n_slots) // len(pieces) for i in range(len(pieces))]
    hooks = [(pieces[slot_of.index(i)] if i in slot_of else (lambda: None)) for i in range(n_slots)]

    for sub in range(MIX_SUBBLOCKS):
        rows = pl.ds(sub * MIX_TB, MIX_TB)
        sub_hooks = hooks[sub * n_heads:(sub + 1) * n_heads]
        _mlstm_block(mq_ref.at[rows], mk_ref.at[rows], mv_ref.at[rows], mo_ref.at[rows], gcol_ref.at[rows],
                     grow_ref.at[:, rows], cw_ref, cb_ref, bcol_ref, brow_ref, mhn_ref,
                     ym_sc.at[cur, rows], xpad_sc, c_sc, n_sc, m_sc, sub_hooks[:MLSTM_HEADS])
        _hgrn_block(hq_ref.at[rows], hf_ref.at[rows], hi_ref.at[rows], hg_ref.at[rows], lbl_ref, hhn_ref,
                    yh_sc.at[cur, rows], g_sc, k_sc, q_sc, gr_sc, st_sc, sub_hooks[MLSTM_HEADS:])


def _const_spec(shape):
    return pl.BlockSpec(shape, lambda *_: (0,) * len(shape), pipeline_mode=pl.Buffered(1))


def _mix(x2d, proj, gates_col, gates_row, conv_w, conv_b, bias_col, bias_row, m_head_norm,
         lb_logits, h_head_norm, w_m, w_h, w_o, *, seq_len):
    n, d = x2d.shape
    tb = MIX_TB * MIX_SUBBLOCKS
    nblk = n // tb
    qw, vw, hw = MLSTM_QK_W, MLSTM_V_W, HGRN_W

    def cur(col_block):
        return lambda s: (jnp.minimum(s, nblk - 1), col_block)

    def prev(col_block):
        return lambda s: (jnp.maximum(s - 1, 0), col_block)

    return pl.pallas_call(
        functools.partial(_mix_kernel, blocks_per_seq=seq_len // tb),
        grid=(nblk + 1,),
        in_specs=[
            pl.BlockSpec((tb, qw), cur(COL_MQ // qw)),
            pl.BlockSpec((tb, qw), cur(COL_MK // qw)),
            pl.BlockSpec((tb, vw), cur(COL_MV // vw)),
            pl.BlockSpec((tb, vw), cur(COL_MO // vw)),
            pl.BlockSpec((tb, LANES), cur(0)),
            pl.BlockSpec((2 * MLSTM_HEADS, tb), lambda s: (0, jnp.minimum(s, nblk - 1))),
            _const_spec((CONV_WIDTH, 2 * qw)),
            _const_spec((1, 2 * qw)),
            _const_spec((1, LANES)),
            _const_spec((2 * MLSTM_HEADS, 1)),
            _const_spec((1, vw)),
            pl.BlockSpec((tb, hw), cur(COL_HQ // hw)),
            pl.BlockSpec((tb, hw), cur(COL_HF // hw)),
            pl.BlockSpec((tb, hw), cur(COL_HI // hw)),
            pl.BlockSpec((tb, hw), cur(COL_HG // hw)),
            _const_spec((2, hw)),
            _const_spec((1, hw)),
            pl.BlockSpec((tb, d), prev(0)),
            pl.BlockSpec((tb, d), prev(COL_GM // d)),
            pl.BlockSpec((tb, d), prev(COL_GH // d)),
            _const_spec(w_m.shape),
            _const_spec(w_h.shape),
            _const_spec(w_o.shape),
        ],
        out_specs=pl.BlockSpec((tb, d), prev(0)),
        out_shape=jax.ShapeDtypeStruct((n, d), F32),
        scratch_shapes=[
            pltpu.VMEM((2, tb, vw), BF16),
            pltpu.VMEM((2, tb, hw), BF16),
            pltpu.VMEM((tb, d), BF16),
            pltpu.VMEM((MIX_TB + 8, 2 * qw), F32),
            pltpu.VMEM((MLSTM_HEADS, MLSTM_QK, MLSTM_V), F32),
            pltpu.VMEM((8, MLSTM_QK), F32),
            pltpu.VMEM((8, LANES), F32),
            pltpu.VMEM((MIX_TB, hw), F32),
            pltpu.VMEM((MIX_TB, hw), F32),
            pltpu.VMEM((MIX_TB, hw), F32),
            pltpu.VMEM((HGRN_NUM_LEVELS, MIX_TB, hw), F32),
            pltpu.VMEM((HGRN_HEADS, HGRN_DV, HGRN_DK), F32),
        ],
        compiler_params=_params("arbitrary"),
        name="mix",
    )(proj, proj, proj, proj, gates_col, gates_row, conv_w, conv_b, bias_col, bias_row, m_head_norm,
      proj, proj, proj, proj, lb_logits, h_head_norm, x2d, proj, proj, w_m, w_h, w_o)


def _xattn_prep_kernel(m_ref, g_ref, wk_ref, wv_ref, wq_ref, wo_ref, qk_ref, vo_ref):
    m = _rms(m_ref[0], g_ref[...]).astype(BF16)
    k = jnp.dot(m, wk_ref[...].astype(BF16), preferred_element_type=F32).astype(BF16)
    v = jnp.dot(m, wv_ref[...].astype(BF16), preferred_element_type=F32).astype(BF16)
    qk = lax.dot_general(wq_ref[...].astype(BF16), k, (((1,), (1,)), ((), ())),
                         preferred_element_type=F32)
    qk_ref[0] = (qk * (XATTN_HEAD_DIM ** -0.5)).astype(qk_ref.dtype)
    vo_ref[0] = jnp.dot(v, wo_ref[...].astype(BF16), preferred_element_type=F32).astype(vo_ref.dtype)


def _xattn_prep(mem, g, w_kv, w_q, w_o):
    b, m, d = mem.shape
    hd = XATTN_HEAD_DIM
    nh = XATTN_HEADS
    return pl.pallas_call(
        _xattn_prep_kernel,
        grid=(nh, b),
        in_specs=[
            pl.BlockSpec((1, m, d), lambda h, i: (i, 0, 0)),
            pl.BlockSpec((1, d), lambda h, i: (0, 0)),
            pl.BlockSpec((d, hd), lambda h, i: (0, h)),
            pl.BlockSpec((d, hd), lambda h, i: (0, nh + h)),
            pl.BlockSpec((d, hd), lambda h, i: (0, h)),
            pl.BlockSpec((hd, d), lambda h, i: (h, 0)),
        ],
        out_specs=[
            pl.BlockSpec((1, d, m), lambda h, i: (i, 0, h)),
            pl.BlockSpec((1, m, d), lambda h, i: (i, h, 0)),
        ],
        out_shape=[
            jax.ShapeDtypeStruct((b, d, nh * m), BF16),
            jax.ShapeDtypeStruct((b, nh * m, d), BF16),
        ],
        compiler_params=_params("arbitrary", "arbitrary"),
        name="xattn_prep",
    )(mem, g.reshape(1, d), w_kv, w_kv, w_q, w_o)


def _xattn_kernel(x_ref, g_ref, qk_ref, vo_ref, o_ref, *, mem_len):
    x = x_ref[0]
    h = _rms(x, g_ref[...]).astype(BF16)
    s = jnp.dot(h, qk_ref[0], preferred_element_type=F32)
    probs = []
    for hd in range(XATTN_HEADS):
        s_h = s[:, hd * mem_len:(hd + 1) * mem_len]
        e = jnp.exp(s_h - jnp.max(s_h, axis=-1, keepdims=True))
        probs.append((e / jnp.sum(e, axis=-1, keepdims=True)).astype(BF16))
    p = jnp.concatenate(probs, axis=-1)
    o_ref[0] = x + jnp.dot(p, vo_ref[0], preferred_element_type=F32)


def _xattn(x, g, qk, vo, *, tm=512):
    b, t, d = x.shape
    hm = qk.shape[2]
    return pl.pallas_call(
        functools.partial(_xattn_kernel, mem_len=hm // XATTN_HEADS),
        grid=(b, t // tm),
        in_specs=[
            pl.BlockSpec((1, tm, d), lambda i, c: (i, c, 0)),
            _const_spec((1, d)),
            pl.BlockSpec((1, d, hm), lambda i, c: (i, 0, 0)),
            pl.BlockSpec((1, hm, d), lambda i, c: (i, 0, 0)),
        ],
        out_specs=pl.BlockSpec((1, tm, d), lambda i, c: (i, c, 0)),
        out_shape=jax.ShapeDtypeStruct((b, t, d), F32),
        compiler_params=_params("parallel", "parallel"),
        name="xattn",
    )(x, g.reshape(1, d), qk, vo)


def kernel(x, mem, norm_ffn1, ffn1_w1, ffn1_w3, ffn1_w2, norm_mix, w_in, mlstm_conv_w, mlstm_conv_b,
           mlstm_ig_bias, mlstm_fg_bias, mlstm_head_norm, hgrn_lb_logits, hgrn_head_norm, w_proj_m,
           w_proj_h, w_out, norm_xattn, norm_mem, xattn_wq, xattn_wkv, xattn_wo, norm_ffn2, ffn2_w1,
           ffn2_w3, ffn2_w2, norm_final):
    b, t, d = x.shape
    depth = norm_ffn1.shape[0]
    assert depth == 1 and hgrn_lb_logits.shape[0] == 2
    n = b * t
    l = 0
    bf = lambda w: w.astype(BF16)

    w_in_t = jnp.swapaxes(w_in[l], 0, 1)
    n_gate = 2 * MLSTM_HEADS
    gate_bias = jnp.concatenate([mlstm_ig_bias[l], mlstm_fg_bias[l]]).astype(F32)
    bias_col = jnp.pad(gate_bias, (0, LANES - n_gate)).reshape(1, LANES)
    bias_row = gate_bias.reshape(n_gate, 1)

    x1, hn, gates_col, gates_row = _ffn(x.reshape(n, d), norm_ffn1[l], ffn1_w1[l], ffn1_w3[l], ffn1_w2[l],
                                        norm_mix[l], w_in_t)

    proj = _in_proj(hn, w_in_t)
    x2 = _mix(x1, proj, gates_col, gates_row, mlstm_conv_w[l], mlstm_conv_b[l].reshape(1, -1),
              bias_col, bias_row, mlstm_head_norm[l].reshape(1, -1), hgrn_lb_logits,
              hgrn_head_norm[l].reshape(1, -1), bf(w_proj_m[l]), bf(w_proj_h[l]), bf(w_out[l]), seq_len=t)

    qk, vo = _xattn_prep(mem, norm_mem[l], xattn_wkv[l], xattn_wq[l], xattn_wo[l])
    x3 = _xattn(x2.reshape(b, t, d), norm_xattn[l], qk, vo)

    (out,) = _ffn(x3.reshape(n, d), norm_ffn2[l], ffn2_w1[l], ffn2_w3[l], ffn2_w2[l], norm_final)
    return out.reshape(b, t, d)
```

```python
import functools

import jax
import jax.numpy as jnp
from jax import lax
from jax.experimental import pallas as pl
from jax.experimental.pallas import tpu as pltpu

F32 = jnp.float32
BF16 = jnp.bfloat16
EPS = 1e-6
LOG2_E = 1.4426950408889634

D_MODEL = 2048
D_FF = 5632
MLSTM_HEADS = 4
MLSTM_QK = 128
MLSTM_V = 256
CONV_WIDTH = 4
HGRN_HEADS = 8
HGRN_DK = 128
HGRN_DV = 128
XATTN_HEADS = 4
XATTN_HEAD_DIM = D_MODEL // XATTN_HEADS

MLSTM_QK_W = MLSTM_HEADS * MLSTM_QK
MLSTM_V_W = MLSTM_HEADS * MLSTM_V
HGRN_W = HGRN_HEADS * HGRN_DK

VMEM_LIMIT_BYTES = 60000 * 1024
LANES = 128

COL_GM = 0
COL_GH = COL_GM + D_MODEL
COL_MQ = COL_GH + D_MODEL
COL_MK = COL_MQ + MLSTM_QK_W
COL_MV = COL_MK + MLSTM_QK_W
COL_MO = COL_MV + MLSTM_V_W
COL_HQ = COL_MO + MLSTM_V_W
COL_HF = COL_HQ + HGRN_W
COL_HI = COL_HF + HGRN_W
COL_HG = COL_HI + HGRN_W
PROJ_W = COL_HG + HGRN_W

MIX_GATE_ROW = 2 * MLSTM_QK_W + 2 * MLSTM_V_W
MIX_HGRN_ROW = MIX_GATE_ROW + 2 * MLSTM_HEADS
MIX_MERGE_ROW = MIX_HGRN_ROW + 4 * HGRN_W

PROJ_SUB_ROWS = 1024
FFN_NORM_ROWS = 64
FFN_FINISH_ROWS = 64
FFN_SUB_COLS = 256

MIX_TB = 128
HGRN_NUM_LEVELS = MIX_TB.bit_length() - 1
MIX_SUBBLOCKS = 2
MERGE_GATE_COLS = 512
MERGE_OUT_COLS = 256
HGRN_COARSE_LEVELS = tuple(1 << i for i in range(3, HGRN_NUM_LEVELS))


def _rms(x, g):
    return x * lax.rsqrt(jnp.mean(x * x, axis=-1, keepdims=True) + EPS) * g


def _sigmoid(x):
    return 1.0 / (1.0 + jnp.exp(-x))


def _log_sigmoid(x):
    return jnp.minimum(x, 0.0) - jnp.log(1.0 + jnp.exp(-jnp.abs(x)))


def _neg_abs(x):
    return pltpu.bitcast(pltpu.bitcast(x, jnp.uint32) | jnp.uint32(0x80000000), F32)


def _params(*sem):
    return pltpu.CompilerParams(dimension_semantics=sem, vmem_limit_bytes=VMEM_LIMIT_BYTES)


def _ffn_kernel(*refs, mode):
    if mode == "mix":
        (x_hbm, g_ref, w1_ref, w3_ref, w2_ref, gn_ref, wg_ref,
         o_ref, hn_ref, gcol_ref, grow_ref, h_sc, x_buf, x_sem) = refs
    else:
        x_hbm, g_ref, w1_ref, w3_ref, w2_ref, gn_ref, o_ref, h_sc, x_buf, x_sem = refs
    i = pl.program_id(0)
    j = pl.program_id(1)
    tm = x_buf.shape[0]

    def x_copy(tile):
        rows = pl.ds(pl.multiple_of(tile * tm, tm), tm)
        return pltpu.make_async_copy(x_hbm.at[rows, :], x_buf, x_sem)

    def row_chunk(r, size):
        return pl.ds(pl.multiple_of(r * size, size), size)

    @pl.when(j == 0)
    def _():
        @pl.when(i == 0)
        def _():
            x_copy(0).start()

        x_copy(i).wait()

        def norm_rows(r, carry):
            rows = row_chunk(r, FFN_NORM_ROWS)
            xr = x_buf[rows, :]
            h_sc[rows, :] = _rms(xr, g_ref[...]).astype(BF16)
            o_ref[rows, :] = 2.0 * xr
            return carry
        lax.fori_loop(0, tm // FFN_NORM_ROWS, norm_rows, 0)

    @pl.when(jnp.logical_and(j == 1, i + 1 < pl.num_programs(0)))
    def _():
        x_copy(i + 1).start()

    h = h_sc[...]
    for c in range(w1_ref.shape[1] // FFN_SUB_COLS):
        cols = slice(c * FFN_SUB_COLS, (c + 1) * FFN_SUB_COLS)
        a = jnp.dot(h, w1_ref[:, cols].astype(BF16), preferred_element_type=F32)
        b = jnp.dot(h, w3_ref[:, cols].astype(BF16), preferred_element_type=F32)
        act = (a * _sigmoid(a) * b).astype(BF16)
        o_ref[...] += jnp.dot(act, w2_ref[cols, :].astype(BF16), preferred_element_type=F32)

    @pl.when(j == pl.num_programs(1) - 1)
    def _():
        if mode == "final":
            def finish_rows(r, carry):
                rows = row_chunk(r, FFN_FINISH_ROWS)
                o_ref[rows, :] = _rms(0.5 * o_ref[rows, :], gn_ref[...])
                return carry
            lax.fori_loop(0, tm // FFN_FINISH_ROWS, finish_rows, 0)
        else:
            y = 0.5 * o_ref[...]
            o_ref[...] = y
            hn = _rms(y, gn_ref[...]).astype(BF16)
            hn_ref[...] = hn
            nt_dims = (((1,), (1,)), ((), ()))
            wg = wg_ref[...].astype(BF16)
            wg_pad = jnp.concatenate([wg, jnp.zeros((LANES - wg.shape[0], wg.shape[1]), BF16)], axis=0)
            gcol_ref[...] = lax.dot_general(hn, wg_pad, nt_dims, preferred_element_type=F32)
            grow_ref[...] = lax.dot_general(wg, hn, nt_dims, preferred_element_type=F32)


def _ffn(x2d, g, w1, w3, w2, g_next, w_gates_t=None, *, tm=1024, tf=512):
    n, d = x2d.shape
    f = w1.shape[1]
    mode = "final" if w_gates_t is None else "mix"
    in_specs = [
        pl.BlockSpec(memory_space=pl.ANY),
        pl.BlockSpec((1, d), lambda i, j: (0, 0)),
        pl.BlockSpec((d, tf), lambda i, j: (0, j)),
        pl.BlockSpec((d, tf), lambda i, j: (0, j)),
        pl.BlockSpec((tf, d), lambda i, j: (j, 0)),
        pl.BlockSpec((1, d), lambda i, j: (0, 0)),
    ]
    args = [x2d, g.reshape(1, d), w1, w3, w2, g_next.reshape(1, d)]
    single = dict(pipeline_mode=pl.Buffered(1))
    out_mode = {} if mode == "final" else single
    out_specs = [pl.BlockSpec((tm, d), lambda i, j: (i, 0), **out_mode)]
    out_shape = [jax.ShapeDtypeStruct((n, d), F32)]
    if mode == "mix":
        ng = 2 * MLSTM_HEADS
        in_specs.append(pl.BlockSpec((ng, d), lambda i, j: (MIX_GATE_ROW // ng, 0)))
        args.append(w_gates_t)
        out_specs += [
            pl.BlockSpec((tm, d), lambda i, j: (i, 0), **single),
            pl.BlockSpec((tm, LANES), lambda i, j: (i, 0)),
            pl.BlockSpec((ng, tm), lambda i, j: (0, i)),
        ]
        out_shape += [
            jax.ShapeDtypeStruct((n, d), BF16),
            jax.ShapeDtypeStruct((n, LANES), F32),
            jax.ShapeDtypeStruct((ng, n), F32),
        ]
    return pl.pallas_call(
        functools.partial(_ffn_kernel, mode=mode),
        grid=(n // tm, f // tf),
        in_specs=in_specs,
        out_specs=out_specs,
        out_shape=out_shape,
        scratch_shapes=[
            pltpu.VMEM((tm, d), BF16),
            pltpu.VMEM((tm, d), F32),
            pltpu.SemaphoreType.DMA(()),
        ],
        compiler_params=_params("arbitrary", "arbitrary"),
        name="ffn_" + mode,
    )(*args)


def _in_proj_kernel(h_ref, wt_ref, p_ref, w_sc):
    @pl.when(pl.program_id(1) == 0)
    def _():
        w_sc[...] = wt_ref[...].astype(BF16)

    for r in range(0, h_ref.shape[0], PROJ_SUB_ROWS):
        rows = slice(r, r + PROJ_SUB_ROWS)
        p_ref[rows, :] = lax.dot_general(h_ref[rows, :], w_sc[...], (((1,), (1,)), ((), ())),
                                         preferred_element_type=F32)


def _proj_row_offset(j, tn):
    n_gate_tiles = 2 * D_MODEL // tn
    n_mlstm_tiles = MIX_GATE_ROW // tn
    off8 = jnp.where(j < n_gate_tiles, MIX_MERGE_ROW // 8 + j * (tn // 8),
                     jnp.where(j < n_gate_tiles + n_mlstm_tiles, (j - n_gate_tiles) * (tn // 8),
                               MIX_HGRN_ROW // 8 + (j - n_gate_tiles - n_mlstm_tiles) * (tn // 8)))
    return off8 * 8


def _in_proj(hn, w_t, *, tm=2048, tn=1024):
    n, d = hn.shape
    return pl.pallas_call(
        _in_proj_kernel,
        grid=(PROJ_W // tn, n // tm),
        in_specs=[
            pl.BlockSpec((tm, d), lambda j, i: (i, 0)),
            pl.BlockSpec((pl.Element(tn), pl.Element(d)), lambda j, i: (_proj_row_offset(j, tn), 0)),
        ],
        out_specs=pl.BlockSpec((tm, tn), lambda j, i: (i, j)),
        out_shape=jax.ShapeDtypeStruct((n, PROJ_W), F32),
        scratch_shapes=[pltpu.VMEM((tn, d), BF16)],
        compiler_params=_params("arbitrary", "arbitrary"),
        name="in_proj",
    )(hn, w_t)


def _mlstm_block(q_ref, k_ref, v_ref, o_ref, gcol_ref, grow_ref, cw_ref, cb_ref, bcol_ref, brow_ref,
                 hn_ref, y_ref, xpad_sc, c_sc, n_sc, m_sc, before_head):
    L = MIX_TB
    H = MLSTM_HEADS

    xpad_sc[8:8 + L, 0:MLSTM_QK_W] = q_ref[...]
    xpad_sc[8:8 + L, MLSTM_QK_W:2 * MLSTM_QK_W] = k_ref[...]
    acc = jnp.broadcast_to(cb_ref[...], (L, 2 * MLSTM_QK_W))
    for j in range(CONV_WIDTH):
        off = 8 - (CONV_WIDTH - 1) + j
        acc = acc + cw_ref[j:j + 1, :] * xpad_sc[off:off + L, :]
    xpad_sc[0:8, :] = xpad_sc[L:L + 8, :]
    qk = acc * _sigmoid(acc)

    gcol = gcol_ref[...] + bcol_ref[...]
    grow = grow_ref[...] + brow_ref[...]
    fcol = _log_sigmoid(gcol)
    frow = _log_sigmoid(grow)
    r_i = lax.broadcasted_iota(jnp.int32, (L, L), 0)
    c_i = lax.broadcasted_iota(jnp.int32, (L, L), 1)
    causal = r_i >= c_i
    tril = jnp.where(causal, 1.0, 0.0).astype(F32)
    triu = jnp.where(r_i <= c_i, 1.0, 0.0).astype(F32)
    bcol = jnp.dot(tril, fcol, preferred_element_type=F32, precision=lax.Precision.HIGHEST)
    brow = jnp.dot(frow, triu, preferred_element_type=F32, precision=lax.Precision.HIGHEST)

    for h in range(H):
        before_head[h]()
        b_c = bcol[:, H + h:H + h + 1]
        i_c = gcol[:, h:h + 1]
        b_r = brow[H + h:H + h + 1, :]
        i_r = grow[h:h + 1, :]
        m_prev = m_sc[h:h + 1, 0:1]

        d_log = jnp.where(causal, b_c - (b_r - i_r), -jnp.inf)
        inter_log = b_c + m_prev
        m_t = jnp.maximum(jnp.max(d_log, axis=1, keepdims=True), inter_log)

        q_h = qk[:, h * MLSTM_QK:(h + 1) * MLSTM_QK]
        k_h = qk[:, MLSTM_QK_W + h * MLSTM_QK:MLSTM_QK_W + (h + 1) * MLSTM_QK] * (MLSTM_QK ** -0.5)
        v_h = v_ref[:, h * MLSTM_V:(h + 1) * MLSTM_V].astype(BF16)
        q_b = q_h.astype(BF16)

        s = lax.dot_general(q_b, k_h.astype(BF16), (((1,), (1,)), ((), ())), preferred_element_type=F32)
        s = s * jnp.exp(d_log - m_t)
        w_inter = jnp.exp(inter_log - m_t)
        c_h = c_sc[h]
        n_h = n_sc[h:h + 1, :]
        num = (jnp.dot(s.astype(BF16), v_h, preferred_element_type=F32)
               + w_inter * jnp.dot(q_b, c_h.astype(BF16), preferred_element_type=F32))
        den = jnp.sum(s, axis=1, keepdims=True) + w_inter * jnp.sum(q_h * n_h, axis=1, keepdims=True)
        hh = num / jnp.maximum(jnp.abs(den), jnp.exp(-m_t))

        hn = hh * lax.rsqrt(jnp.mean(hh * hh, axis=-1, keepdims=True) + EPS)
        hn = hn * hn_ref[:, h * MLSTM_V:(h + 1) * MLSTM_V]
        y = hn * _sigmoid(o_ref[:, h * MLSTM_V:(h + 1) * MLSTM_V])
        y_ref[:, h * MLSTM_V:(h + 1) * MLSTM_V] = y.astype(y_ref.dtype)

        b_last = b_c[L - 1:L, :]
        a_log = b_last - b_c + i_c
        m_new = jnp.maximum(b_last + m_prev, jnp.max(a_log, axis=0, keepdims=True))
        w_a = jnp.exp(a_log - m_new)
        decay = jnp.exp(b_last + m_prev - m_new)
        kw = k_h * w_a
        c_sc[h] = decay * c_h + lax.dot_general(kw.astype(BF16), v_h, (((0,), (0,)), ((), ())),
                                                preferred_element_type=F32)
        n_sc[h:h + 1, :] = decay * n_h + jnp.sum(kw, axis=0, keepdims=True)
        m_sc[h:h + 1, :] = jnp.broadcast_to(m_new, (1, LANES))


def _hgrn_block(q_ref, f_ref, v_ref, og_ref, lbl_ref, hn_ref, y_ref, g_sc, k_sc, q_sc, gr_sc, st_sc,
                before_head):
    TB = MIX_TB
    H = HGRN_HEADS
    DK = HGRN_DK
    W = H * DK
    NT = TB // 8

    lbl = lbl_ref[...]
    lmax = jnp.max(lbl, axis=0, keepdims=True)
    le = jnp.exp(lbl - lmax)
    lb = le[1:2, :] / jnp.sum(le, axis=0, keepdims=True)

    f = lb + (1.0 - lb) * _sigmoid(f_ref[...])
    k_sc[...] = 1.0 - f
    qraw = q_ref[...]
    q_sc[...] = qraw * _sigmoid(qraw) * (DK ** -0.5)
    r_i = lax.broadcasted_iota(jnp.int32, (TB, TB), 0)
    c_i = lax.broadcasted_iota(jnp.int32, (TB, TB), 1)
    tri = jnp.where(r_i >= c_i, 1.0, 0.0).astype(F32)
    g = jnp.dot(tri, jnp.log(f) * LOG2_E, preferred_element_type=F32, precision=lax.Precision.HIGHEST)
    g_sc[...] = g

    g3 = g.reshape(NT, 8, W)
    sub = lax.broadcasted_iota(jnp.int32, (NT, 8, W), 1)
    bit0 = (sub & 1) != 0
    bit1 = (sub & 2) != 0
    bit2 = (sub & 4) != 0
    last2 = jnp.where(bit0, g3, pltpu.roll(g3, 7, 1))
    last4 = jnp.where(bit1, last2, pltpu.roll(last2, 6, 1))
    last4_r = pltpu.roll(last4, 4, 1)
    last8 = jnp.where(bit2, last4, last4_r).reshape(TB, W)
    gr_sc[0] = jnp.where(bit0, pltpu.roll(g3, 1, 1), g3).reshape(TB, W)
    gr_sc[1] = jnp.where(bit1, pltpu.roll(last2, 2, 1), last2).reshape(TB, W)
    gr_sc[2] = jnp.where(bit2, last4_r, last4).reshape(TB, W)
    for li, lvl in enumerate(HGRN_COARSE_LEVELS):
        groups = []
        for base in range(0, TB, 2 * lvl):
            src = last8[base + lvl - 8:base + lvl, :]
            groups.extend([src] * (2 * lvl // 8))
        gr_sc[3 + li] = jnp.concatenate(groups, axis=0)

    x_i = r_i ^ c_i
    nt_dims = (((1,), (1,)), ((), ()))
    for h in range(H):
        before_head[h]()
        cols = slice(h * DK, (h + 1) * DK)
        q_h = q_sc[:, cols]
        k_h = k_sc[:, cols]
        g_h = g_sc[:, cols]
        v_h = v_ref[:, cols].astype(BF16)

        q_b = q_h.astype(BF16)
        k_b = k_h.astype(BF16)
        a = lax.dot_general(q_b, k_b, nt_dims, preferred_element_type=F32)
        for li in range(HGRN_NUM_LEVELS):
            e = jnp.exp2(_neg_abs(g_h - gr_sc[li, :, cols])).astype(BF16)
            p = lax.dot_general(q_b * e, k_b * e, nt_dims, preferred_element_type=F32)
            a = jnp.where(x_i >= (1 << li), p, a)
        a = jnp.where(r_i >= c_i, a, 0.0)
        o = jnp.dot(a.astype(BF16), v_h, preferred_element_type=F32)

        st = st_sc[h]
        g_last = g_h[TB - 1:TB, :]
        qd = (q_h * jnp.exp2(g_h)).astype(BF16)
        o = o + lax.dot_general(qd, st.astype(BF16), nt_dims, preferred_element_type=F32)
        kd = (k_h * jnp.exp2(g_last - g_h)).astype(BF16)
        st_sc[h] = jnp.exp2(g_last) * st + lax.dot_general(v_h, kd, (((0,), (0,)), ((), ())),
                                                           preferred_element_type=F32)

        on = o * lax.rsqrt(jnp.mean(o * o, axis=-1, keepdims=True) + EPS) * hn_ref[:, cols]
        og = og_ref[:, cols]
        y_ref[:, cols] = (on * (og * _sigmoid(og))).astype(y_ref.dtype)


def _mix_kernel(mq_ref, mk_ref, mv_ref, mo_ref, gcol_ref, grow_ref, cw_ref, cb_ref, bcol_ref, brow_ref,
                mhn_ref, hq_ref, hf_ref, hi_ref, hg_ref, lbl_ref, hhn_ref,
                x_ref, gm_ref, gh_ref, wm_ref, wh_ref, wo_ref,
                out_ref,
                ym_sc, yh_sc, merged_sc, xpad_sc, c_sc, n_sc, m_sc, g_sc, k_sc, q_sc, gr_sc, st_sc,
                *, blocks_per_seq):
    s = pl.program_id(0)
    cur = s % 2
    prv = 1 - cur

    @pl.when(s == 0)
    def _():
        ym_sc[...] = jnp.zeros_like(ym_sc)
        yh_sc[...] = jnp.zeros_like(yh_sc)

    @pl.when(s % blocks_per_seq == 0)
    def _():
        xpad_sc[0:8, :] = jnp.zeros((8, 2 * MLSTM_QK_W), F32)
        c_sc[...] = jnp.zeros_like(c_sc)
        n_sc[...] = jnp.zeros_like(n_sc)
        m_sc[...] = jnp.zeros_like(m_sc)
        st_sc[...] = jnp.zeros_like(st_sc)

    def gate_piece(c):
        def run():
            cols = slice(c * MERGE_GATE_COLS, (c + 1) * MERGE_GATE_COLS)
            pm = jnp.dot(ym_sc[prv], wm_ref[:, cols], preferred_element_type=F32)
            ph = jnp.dot(yh_sc[prv], wh_ref[:, cols], preferred_element_type=F32)
            merged = _sigmoid(gm_ref[:, cols]) * pm + _sigmoid(gh_ref[:, cols]) * ph
            merged_sc[:, cols] = merged.astype(BF16)
        return run

    def out_piece(c):
        def run():
            cols = slice(c * MERGE_OUT_COLS, (c + 1) * MERGE_OUT_COLS)
            out_ref[:, cols] = x_ref[:, cols] + jnp.dot(merged_sc[...], wo_ref[:, cols],
                                                        preferred_element_type=F32)
        return run

    pieces = ([gate_piece(c) for c in range(D_MODEL // MERGE_GATE_COLS)]
              + [out_piece(c) for c in range(D_MODEL // MERGE_OUT_COLS)])
    n_heads = MLSTM_HEADS + HGRN_HEADS
    n_slots = MIX_SUBBLOCKS * n_heads
    slot_of = [(i * n_slots) // len(pieces) for i in range(len(pieces))]
    hooks = [(pieces[slot_of.index(i)] if i in slot_of else (lambda: None)) for i in range(n_slots)]

    for sub in range(MIX_SUBBLOCKS):
        rows = pl.ds(sub * MIX_TB, MIX_TB)
        sub_hooks = hooks[sub * n_heads:(sub + 1) * n_heads]
        _mlstm_block(mq_ref.at[rows], mk_ref.at[rows], mv_ref.at[rows], mo_ref.at[rows], gcol_ref.at[rows],
                     grow_ref.at[:, rows], cw_ref, cb_ref, bcol_ref, brow_ref, mhn_ref,
                     ym_sc.at[cur, rows], xpad_sc, c_sc, n_sc, m_sc, sub_hooks[:MLSTM_HEADS])
        _hgrn_block(hq_ref.at[rows], hf_ref.at[rows], hi_ref.at[rows], hg_ref.at[rows], lbl_ref, hhn_ref,
                    yh_sc.at[cur, rows], g_sc, k_sc, q_sc, gr_sc, st_sc, sub_hooks[MLSTM_HEADS:])


def _const_spec(shape):
    return pl.BlockSpec(shape, lambda *_: (0,) * len(shape), pipeline_mode=pl.Buffered(1))


def _mix(x2d, proj, gates_col, gates_row, conv_w, conv_b, bias_col, bias_row, m_head_norm,
         lb_logits, h_head_norm, w_m, w_h, w_o, *, seq_len):
    n, d = x2d.shape
    tb = MIX_TB * MIX_SUBBLOCKS
    nblk = n // tb
    qw, vw, hw = MLSTM_QK_W, MLSTM_V_W, HGRN_W

    def cur(col_block):
        return lambda s: (jnp.minimum(s, nblk - 1), col_block)

    def prev(col_block):
        return lambda s: (jnp.maximum(s - 1, 0), col_block)

    return pl.pallas_call(
        functools.partial(_mix_kernel, blocks_per_seq=seq_len // tb),
        grid=(nblk + 1,),
        in_specs=[
            pl.BlockSpec((tb, qw), cur(COL_MQ // qw)),
            pl.BlockSpec((tb, qw), cur(COL_MK // qw)),
            pl.BlockSpec((tb, vw), cur(COL_MV // vw)),
            pl.BlockSpec((tb, vw), cur(COL_MO // vw)),
            pl.BlockSpec((tb, LANES), cur(0)),
            pl.BlockSpec((2 * MLSTM_HEADS, tb), lambda s: (0, jnp.minimum(s, nblk - 1))),
            _const_spec((CONV_WIDTH, 2 * qw)),
            _const_spec((1, 2 * qw)),
            _const_spec((1, LANES)),
            _const_spec((2 * MLSTM_HEADS, 1)),
            _const_spec((1, vw)),
            pl.BlockSpec((tb, hw), cur(COL_HQ // hw)),
            pl.BlockSpec((tb, hw), cur(COL_HF // hw)),
            pl.BlockSpec((tb, hw), cur(COL_HI // hw)),
            pl.BlockSpec((tb, hw), cur(COL_HG // hw)),
            _const_spec((2, hw)),
            _const_spec((1, hw)),
            pl.BlockSpec((tb, d), prev(0)),
            pl.BlockSpec((tb, d), prev(COL_GM // d)),
            pl.BlockSpec((tb, d), prev(COL_GH // d)),
            _const_spec(w_m.shape),
            _const_spec(w_h.shape),
            _const_spec(w_o.shape),
        ],
        out_specs=pl.BlockSpec((tb, d), prev(0)),
        out_shape=jax.ShapeDtypeStruct((n, d), F32),
        scratch_shapes=[
            pltpu.VMEM((2, tb, vw), BF16),
            pltpu.VMEM((2, tb, hw), BF16),
            pltpu.VMEM((tb, d), BF16),
            pltpu.VMEM((MIX_TB + 8, 2 * qw), F32),
            pltpu.VMEM((MLSTM_HEADS, MLSTM_QK, MLSTM_V), F32),
            pltpu.VMEM((8, MLSTM_QK), F32),
            pltpu.VMEM((8, LANES), F32),
            pltpu.VMEM((MIX_TB, hw), F32),
            pltpu.VMEM((MIX_TB, hw), F32),
            pltpu.VMEM((MIX_TB, hw), F32),
            pltpu.VMEM((HGRN_NUM_LEVELS, MIX_TB, hw), F32),
            pltpu.VMEM((HGRN_HEADS, HGRN_DV, HGRN_DK), F32),
        ],
        compiler_params=_params("arbitrary"),
        name="mix",
    )(proj, proj, proj, proj, gates_col, gates_row, conv_w, conv_b, bias_col, bias_row, m_head_norm,
      proj, proj, proj, proj, lb_logits, h_head_norm, x2d, proj, proj, w_m, w_h, w_o)


def _xattn_prep_kernel(m_ref, g_ref, wk_ref, wv_ref, wq_ref, wo_ref, qk_ref, vo_ref):
    wk = wk_ref[...].astype(BF16)
    wv = wv_ref[...].astype(BF16)
    wq = wq_ref[...].astype(BF16)
    wo = wo_ref[...].astype(BF16)
    for i in range(m_ref.shape[0]):
        m = _rms(m_ref[i], g_ref[...]).astype(BF16)
        k = jnp.dot(m, wk, preferred_element_type=F32).astype(BF16)
        v = jnp.dot(m, wv, preferred_element_type=F32).astype(BF16)
        qk = lax.dot_general(wq, k, (((1,), (1,)), ((), ())), preferred_element_type=F32)
        qk_ref[i] = (qk * (XATTN_HEAD_DIM ** -0.5)).astype(qk_ref.dtype)
        vo_ref[i] = jnp.dot(v, wo, preferred_element_type=F32).astype(vo_ref.dtype)


def _xattn_prep(mem, g, w_kv, w_q, w_o):
    b, m, d = mem.shape
    hd = XATTN_HEAD_DIM
    nh = XATTN_HEADS
    return pl.pallas_call(
        _xattn_prep_kernel,
        grid=(nh,),
        in_specs=[
            pl.BlockSpec((b, m, d), lambda h: (0, 0, 0)),
            pl.BlockSpec((1, d), lambda h: (0, 0)),
            pl.BlockSpec((d, hd), lambda h: (0, h)),
            pl.BlockSpec((d, hd), lambda h: (0, nh + h)),
            pl.BlockSpec((d, hd), lambda h: (0, h)),
            pl.BlockSpec((hd, d), lambda h: (h, 0)),
        ],
        out_specs=[
            pl.BlockSpec((b, d, m), lambda h: (0, 0, h)),
            pl.BlockSpec((b, m, d), lambda h: (0, h, 0)),
        ],
        out_shape=[
            jax.ShapeDtypeStruct((b, d, nh * m), BF16),
            jax.ShapeDtypeStruct((b, nh * m, d), BF16),
        ],
        compiler_params=_params("arbitrary"),
        name="xattn_prep",
    )(mem, g.reshape(1, d), w_kv, w_kv, w_q, w_o)


def _xattn_kernel(x_ref, g_ref, qk_ref, vo_ref, o_ref, *, mem_len):
    x = x_ref[0]
    h = _rms(x, g_ref[...]).astype(BF16)
    s = jnp.dot(h, qk_ref[0], preferred_element_type=F32)
    probs = []
    for hd in range(XATTN_HEADS):
        s_h = s[:, hd * mem_len:(hd + 1) * mem_len]
        e = jnp.exp(s_h - jnp.max(s_h, axis=-1, keepdims=True))
        probs.append((e / jnp.sum(e, axis=-1, keepdims=True)).astype(BF16))
    p = jnp.concatenate(probs, axis=-1)
    o_ref[0] = x + jnp.dot(p, vo_ref[0], preferred_element_type=F32)


def _xattn(x, g, qk, vo, *, tm=512):
    b, t, d = x.shape
    hm = qk.shape[2]
    return pl.pallas_call(
        functools.partial(_xattn_kernel, mem_len=hm // XATTN_HEADS),
        grid=(b, t // tm),
        in_specs=[
            pl.BlockSpec((1, tm, d), lambda i, c: (i, c, 0)),
            _const_spec((1, d)),
            pl.BlockSpec((1, d, hm), lambda i, c: (i, 0, 0)),
            pl.BlockSpec((1, hm, d), lambda i, c: (i, 0, 0)),
        ],
        out_specs=pl.BlockSpec((1, tm, d), lambda i, c: (i, c, 0)),
        out_shape=jax.ShapeDtypeStruct((b, t, d), F32),
        compiler_params=_params("parallel", "parallel"),
        name="xattn",
    )(x, g.reshape(1, d), qk, vo)


def kernel(x, mem, norm_ffn1, ffn1_w1, ffn1_w3, ffn1_w2, norm_mix, w_in, mlstm_conv_w, mlstm_conv_b,
           mlstm_ig_bias, mlstm_fg_bias, mlstm_head_norm, hgrn_lb_logits, hgrn_head_norm, w_proj_m,
           w_proj_h, w_out, norm_xattn, norm_mem, xattn_wq, xattn_wkv, xattn_wo, norm_ffn2, ffn2_w1,
           ffn2_w3, ffn2_w2, norm_final):
    b, t, d = x.shape
    depth = norm_ffn1.shape[0]
    assert depth == 1 and hgrn_lb_logits.shape[0] == 2
    n = b * t
    l = 0
    bf = lambda w: w.astype(BF16)

    w_in_t = jnp.swapaxes(w_in[l], 0, 1)
    n_gate = 2 * MLSTM_HEADS
    gate_bias = jnp.concatenate([mlstm_ig_bias[l], mlstm_fg_bias[l]]).astype(F32)
    bias_col = jnp.pad(gate_bias, (0, LANES - n_gate)).reshape(1, LANES)
    bias_row = gate_bias.reshape(n_gate, 1)

    x1, hn, gates_col, gates_row = _ffn(x.reshape(n, d), norm_ffn1[l], ffn1_w1[l], ffn1_w3[l], ffn1_w2[l],
                                        norm_mix[l], w_in_t)

    proj = _in_proj(hn, w_in_t)
    x2 = _mix(x1, proj, gates_col, gates_row, mlstm_conv_w[l], mlstm_conv_b[l].reshape(1, -1),
              bias_col, bias_row, mlstm_head_norm[l].reshape(1, -1), hgrn_lb_logits,
              hgrn_head_norm[l].reshape(1, -1), bf(w_proj_m[l]), bf(w_proj_h[l]), bf(w_out[l]), seq_len=t)

    qk, vo = _xattn_prep(mem, norm_mem[l], xattn_wkv[l], xattn_wq[l], xattn_wo[l])
    x3 = _xattn(x2.reshape(b, t, d), norm_xattn[l], qk, vo)

    (out,) = _ffn(x3.reshape(n, d), norm_ffn2[l], ffn2_w1[l], ffn2_w3[l], ffn2_w2[l], norm_final)
    return out.reshape(b, t, d)
```

```python
import functools

import jax
import jax.numpy as jnp
from jax import lax
from jax.experimental import pallas as pl
from jax.experimental.pallas import tpu as pltpu

F32 = jnp.float32
BF16 = jnp.bfloat16
EPS = 1e-6
LOG2_E = 1.4426950408889634

D_MODEL = 2048
D_FF = 5632
MLSTM_HEADS = 4
MLSTM_QK = 128
MLSTM_V = 256
CONV_WIDTH = 4
HGRN_HEADS = 8
HGRN_DK = 128
HGRN_DV = 128
XATTN_HEADS = 4
XATTN_HEAD_DIM = D_MODEL // XATTN_HEADS

MLSTM_QK_W = MLSTM_HEADS * MLSTM_QK
MLSTM_V_W = MLSTM_HEADS * MLSTM_V
HGRN_W = HGRN_HEADS * HGRN_DK

VMEM_LIMIT_BYTES = 60000 * 1024
LANES = 128

COL_GM = 0
COL_GH = COL_GM + D_MODEL
COL_MQ = COL_GH + D_MODEL
COL_MK = COL_MQ + MLSTM_QK_W
COL_MV = COL_MK + MLSTM_QK_W
COL_MO = COL_MV + MLSTM_V_W
COL_HQ = COL_MO + MLSTM_V_W
COL_HF = COL_HQ + HGRN_W
COL_HI = COL_HF + HGRN_W
COL_HG = COL_HI + HGRN_W
PROJ_W = COL_HG + HGRN_W

MIX_GATE_ROW = 2 * MLSTM_QK_W + 2 * MLSTM_V_W
MIX_HGRN_ROW = MIX_GATE_ROW + 2 * MLSTM_HEADS
MIX_MERGE_ROW = MIX_HGRN_ROW + 4 * HGRN_W

PROJ_SUB_ROWS = 1024
FFN_NORM_ROWS = 64
FFN_FINISH_ROWS = 64
FFN_SUB_COLS = 256

MIX_TB = 128
HGRN_NUM_LEVELS = MIX_TB.bit_length() - 1
MIX_SUBBLOCKS = 2
MERGE_GATE_COLS = 512
MERGE_OUT_COLS = 256
HGRN_COARSE_LEVELS = tuple(1 << i for i in range(3, HGRN_NUM_LEVELS))


def _rms(x, g):
    return x * lax.rsqrt(jnp.mean(x * x, axis=-1, keepdims=True) + EPS) * g


def _sigmoid(x):
    return 1.0 / (1.0 + jnp.exp(-x))


def _log_sigmoid(x):
    return jnp.minimum(x, 0.0) - jnp.log(1.0 + jnp.exp(-jnp.abs(x)))


def _neg_abs(x):
    return pltpu.bitcast(pltpu.bitcast(x, jnp.uint32) | jnp.uint32(0x80000000), F32)


def _params(*sem):
    return pltpu.CompilerParams(dimension_semantics=sem, vmem_limit_bytes=VMEM_LIMIT_BYTES)


def _ffn_kernel(*refs, mode):
    if mode == "mix":
        (x_hbm, g_ref, w1_ref, w3_ref, w2_ref, gn_ref, wg_ref,
         o_ref, hn_ref, gcol_ref, grow_ref, h_sc, x_buf, x_sem) = refs
    else:
        x_hbm, g_ref, w1_ref, w3_ref, w2_ref, gn_ref, o_ref, h_sc, x_buf, x_sem = refs
    i = pl.program_id(0)
    j = pl.program_id(1)
    tm = x_buf.shape[0]

    def x_copy(tile):
        rows = pl.ds(pl.multiple_of(tile * tm, tm), tm)
        return pltpu.make_async_copy(x_hbm.at[rows, :], x_buf, x_sem)

    def row_chunk(r, size):
        return pl.ds(pl.multiple_of(r * size, size), size)

    @pl.when(j == 0)
    def _():
        @pl.when(i == 0)
        def _():
            x_copy(0).start()

        x_copy(i).wait()

        def norm_rows(r, carry):
            rows = row_chunk(r, FFN_NORM_ROWS)
            xr = x_buf[rows, :]
            h_sc[rows, :] = _rms(xr, g_ref[...]).astype(BF16)
            o_ref[rows, :] = 2.0 * xr
            return carry
        lax.fori_loop(0, tm // FFN_NORM_ROWS, norm_rows, 0)

    @pl.when(jnp.logical_and(j == 1, i + 1 < pl.num_programs(0)))
    def _():
        x_copy(i + 1).start()

    h = h_sc[...]
    for c in range(w1_ref.shape[1] // FFN_SUB_COLS):
        cols = slice(c * FFN_SUB_COLS, (c + 1) * FFN_SUB_COLS)
        a = jnp.dot(h, w1_ref[:, cols].astype(BF16), preferred_element_type=F32)
        b = jnp.dot(h, w3_ref[:, cols].astype(BF16), preferred_element_type=F32)
        act = (a * _sigmoid(a) * b).astype(BF16)
        o_ref[...] += jnp.dot(act, w2_ref[cols, :].astype(BF16), preferred_element_type=F32)

    @pl.when(j == pl.num_programs(1) - 1)
    def _():
        if mode == "final":
            def finish_rows(r, carry):
                rows = row_chunk(r, FFN_FINISH_ROWS)
                o_ref[rows, :] = _rms(0.5 * o_ref[rows, :], gn_ref[...])
                return carry
            lax.fori_loop(0, tm // FFN_FINISH_ROWS, finish_rows, 0)
        else:
            y = 0.5 * o_ref[...]
            o_ref[...] = y
            hn = _rms(y, gn_ref[...]).astype(BF16)
            hn_ref[...] = hn
            nt_dims = (((1,), (1,)), ((), ()))
            wg = wg_ref[...].astype(BF16)
            wg_pad = jnp.concatenate([wg, jnp.zeros((LANES - wg.shape[0], wg.shape[1]), BF16)], axis=0)
            gcol = lax.dot_general(hn, wg_pad, nt_dims, preferred_element_type=F32)
            gcol_ref[...] = gcol
            grow_ref[...] = gcol.T[0:wg.shape[0], :]


def _ffn(x2d, g, w1, w3, w2, g_next, w_gates_t=None, *, tm=1024, tf=512):
    n, d = x2d.shape
    f = w1.shape[1]
    mode = "final" if w_gates_t is None else "mix"
    in_specs = [
        pl.BlockSpec(memory_space=pl.ANY),
        pl.BlockSpec((1, d), lambda i, j: (0, 0)),
        pl.BlockSpec((d, tf), lambda i, j: (0, j)),
        pl.BlockSpec((d, tf), lambda i, j: (0, j)),
        pl.BlockSpec((tf, d), lambda i, j: (j, 0)),
        pl.BlockSpec((1, d), lambda i, j: (0, 0)),
    ]
    args = [x2d, g.reshape(1, d), w1, w3, w2, g_next.reshape(1, d)]
    single = dict(pipeline_mode=pl.Buffered(1))
    out_mode = {} if mode == "final" else single
    out_specs = [pl.BlockSpec((tm, d), lambda i, j: (i, 0), **out_mode)]
    out_shape = [jax.ShapeDtypeStruct((n, d), F32)]
    if mode == "mix":
        ng = 2 * MLSTM_HEADS
        in_specs.append(pl.BlockSpec((ng, d), lambda i, j: (MIX_GATE_ROW // ng, 0)))
        args.append(w_gates_t)
        out_specs += [
            pl.BlockSpec((tm, d), lambda i, j: (i, 0), **single),
            pl.BlockSpec((tm, LANES), lambda i, j: (i, 0)),
            pl.BlockSpec((ng, tm), lambda i, j: (0, i)),
        ]
        out_shape += [
            jax.ShapeDtypeStruct((n, d), BF16),
            jax.ShapeDtypeStruct((n, LANES), F32),
            jax.ShapeDtypeStruct((ng, n), F32),
        ]
    return pl.pallas_call(
        functools.partial(_ffn_kernel, mode=mode),
        grid=(n // tm, f // tf),
        in_specs=in_specs,
        out_specs=out_specs,
        out_shape=out_shape,
        scratch_shapes=[
            pltpu.VMEM((tm, d), BF16),
            pltpu.VMEM((tm, d), F32),
            pltpu.SemaphoreType.DMA(()),
        ],
        compiler_params=_params("arbitrary", "arbitrary"),
        name="ffn_" + mode,
    )(*args)


def _in_proj_kernel(h_ref, wt_ref, p_ref, w_sc):
    @pl.when(pl.program_id(1) == 0)
    def _():
        w_sc[...] = wt_ref[...].astype(BF16)

    for r in range(0, h_ref.shape[0], PROJ_SUB_ROWS):
        rows = slice(r, r + PROJ_SUB_ROWS)
        p_ref[rows, :] = lax.dot_general(h_ref[rows, :], w_sc[...], (((1,), (1,)), ((), ())),
                                         preferred_element_type=F32)


def _proj_row_offset(j, tn):
    n_gate_tiles = 2 * D_MODEL // tn
    n_mlstm_tiles = MIX_GATE_ROW // tn
    off8 = jnp.where(j < n_gate_tiles, MIX_MERGE_ROW // 8 + j * (tn // 8),
                     jnp.where(j < n_gate_tiles + n_mlstm_tiles, (j - n_gate_tiles) * (tn // 8),
                               MIX_HGRN_ROW // 8 + (j - n_gate_tiles - n_mlstm_tiles) * (tn // 8)))
    return off8 * 8


def _in_proj(hn, w_t, *, tm=2048, tn=1024):
    n, d = hn.shape
    return pl.pallas_call(
        _in_proj_kernel,
        grid=(PROJ_W // tn, n // tm),
        in_specs=[
            pl.BlockSpec((tm, d), lambda j, i: (i, 0)),
            pl.BlockSpec((pl.Element(tn), pl.Element(d)), lambda j, i: (_proj_row_offset(j, tn), 0)),
        ],
        out_specs=pl.BlockSpec((tm, tn), lambda j, i: (i, j)),
        out_shape=jax.ShapeDtypeStruct((n, PROJ_W), F32),
        scratch_shapes=[pltpu.VMEM((tn, d), BF16)],
        compiler_params=_params("arbitrary", "arbitrary"),
        name="in_proj",
    )(hn, w_t)


def _mlstm_block(q_ref, k_ref, v_ref, o_ref, gcol_ref, grow_ref, cw_ref, cb_ref, bcol_ref, brow_ref,
                 hn_ref, y_ref, xpad_sc, c_sc, n_sc, m_sc, before_head):
    L = MIX_TB
    H = MLSTM_HEADS

    xpad_sc[8:8 + L, 0:MLSTM_QK_W] = q_ref[...]
    xpad_sc[8:8 + L, MLSTM_QK_W:2 * MLSTM_QK_W] = k_ref[...]
    acc = jnp.broadcast_to(cb_ref[...], (L, 2 * MLSTM_QK_W))
    for j in range(CONV_WIDTH):
        off = 8 - (CONV_WIDTH - 1) + j
        acc = acc + cw_ref[j:j + 1, :] * xpad_sc[off:off + L, :]
    xpad_sc[0:8, :] = xpad_sc[L:L + 8, :]
    qk = acc * _sigmoid(acc)

    gcol = gcol_ref[...] + bcol_ref[...]
    grow = grow_ref[...] + brow_ref[...]
    fcol = _log_sigmoid(gcol)
    frow = _log_sigmoid(grow)
    r_i = lax.broadcasted_iota(jnp.int32, (L, L), 0)
    c_i = lax.broadcasted_iota(jnp.int32, (L, L), 1)
    causal = r_i >= c_i
    tril = jnp.where(causal, 1.0, 0.0).astype(F32)
    triu = jnp.where(r_i <= c_i, 1.0, 0.0).astype(F32)
    bcol = jnp.dot(tril, fcol, preferred_element_type=F32, precision=lax.Precision.HIGHEST)
    brow = jnp.dot(frow, triu, preferred_element_type=F32, precision=lax.Precision.HIGHEST)

    for h in range(H):
        before_head[h]()
        b_c = bcol[:, H + h:H + h + 1]
        i_c = gcol[:, h:h + 1]
        b_r = brow[H + h:H + h + 1, :]
        i_r = grow[h:h + 1, :]
        m_prev = m_sc[h:h + 1, 0:1]

        d_log = jnp.where(causal, b_c - (b_r - i_r), -jnp.inf)
        inter_log = b_c + m_prev
        m_t = jnp.maximum(jnp.max(d_log, axis=1, keepdims=True), inter_log)

        q_h = qk[:, h * MLSTM_QK:(h + 1) * MLSTM_QK]
        k_h = qk[:, MLSTM_QK_W + h * MLSTM_QK:MLSTM_QK_W + (h + 1) * MLSTM_QK] * (MLSTM_QK ** -0.5)
        v_h = v_ref[:, h * MLSTM_V:(h + 1) * MLSTM_V].astype(BF16)
        q_b = q_h.astype(BF16)

        s = lax.dot_general(q_b, k_h.astype(BF16), (((1,), (1,)), ((), ())), preferred_element_type=F32)
        s = s * jnp.exp(d_log - m_t)
        w_inter = jnp.exp(inter_log - m_t)
        c_h = c_sc[h]
        n_h = n_sc[h:h + 1, :]
        num = (jnp.dot(s.astype(BF16), v_h, preferred_element_type=F32)
               + w_inter * jnp.dot(q_b, c_h.astype(BF16), preferred_element_type=F32))
        den = jnp.sum(s, axis=1, keepdims=True) + w_inter * jnp.sum(q_h * n_h, axis=1, keepdims=True)
        hh = num / jnp.maximum(jnp.abs(den), jnp.exp(-m_t))

        hn = hh * lax.rsqrt(jnp.mean(hh * hh, axis=-1, keepdims=True) + EPS)
        hn = hn * hn_ref[:, h * MLSTM_V:(h + 1) * MLSTM_V]
        y = hn * _sigmoid(o_ref[:, h * MLSTM_V:(h + 1) * MLSTM_V])
        y_ref[:, h * MLSTM_V:(h + 1) * MLSTM_V] = y.astype(y_ref.dtype)

        b_last = b_c[L - 1:L, :]
        a_log = b_last - b_c + i_c
        m_new = jnp.maximum(b_last + m_prev, jnp.max(a_log, axis=0, keepdims=True))
        w_a = jnp.exp(a_log - m_new)
        decay = jnp.exp(b_last + m_prev - m_new)
        kw = k_h * w_a
        c_sc[h] = decay * c_h + lax.dot_general(kw.astype(BF16), v_h, (((0,), (0,)), ((), ())),
                                                preferred_element_type=F32)
        n_sc[h:h + 1, :] = decay * n_h + jnp.sum(kw, axis=0, keepdims=True)
        m_sc[h:h + 1, :] = jnp.broadcast_to(m_new, (1, LANES))


def _hgrn_block(q_ref, f_ref, v_ref, og_ref, lbl_ref, hn_ref, y_ref, g_sc, k_sc, q_sc, gr_sc, st_sc,
                before_head):
    TB = MIX_TB
    H = HGRN_HEADS
    DK = HGRN_DK
    W = H * DK
    NT = TB // 8

    lbl = lbl_ref[...]
    lmax = jnp.max(lbl, axis=0, keepdims=True)
    le = jnp.exp(lbl - lmax)
    lb = le[1:2, :] / jnp.sum(le, axis=0, keepdims=True)

    f = lb + (1.0 - lb) * _sigmoid(f_ref[...])
    k_sc[...] = 1.0 - f
    qraw = q_ref[...]
    q_sc[...] = qraw * _sigmoid(qraw) * (DK ** -0.5)
    r_i = lax.broadcasted_iota(jnp.int32, (TB, TB), 0)
    c_i = lax.broadcasted_iota(jnp.int32, (TB, TB), 1)
    tri = jnp.where(r_i >= c_i, 1.0, 0.0).astype(F32)
    g = jnp.dot(tri, jnp.log(f) * LOG2_E, preferred_element_type=F32, precision=lax.Precision.HIGHEST)
    g_sc[...] = g

    g3 = g.reshape(NT, 8, W)
    sub = lax.broadcasted_iota(jnp.int32, (NT, 8, W), 1)
    bit0 = (sub & 1) != 0
    bit1 = (sub & 2) != 0
    bit2 = (sub & 4) != 0
    last2 = jnp.where(bit0, g3, pltpu.roll(g3, 7, 1))
    last4 = jnp.where(bit1, last2, pltpu.roll(last2, 6, 1))
    last4_r = pltpu.roll(last4, 4, 1)
    last8 = jnp.where(bit2, last4, last4_r).reshape(TB, W)
    gr_sc[0] = jnp.where(bit0, pltpu.roll(g3, 1, 1), g3).reshape(TB, W)
    gr_sc[1] = jnp.where(bit1, pltpu.roll(last2, 2, 1), last2).reshape(TB, W)
    gr_sc[2] = jnp.where(bit2, last4_r, last4).reshape(TB, W)
    for li, lvl in enumerate(HGRN_COARSE_LEVELS):
        groups = []
        for base in range(0, TB, 2 * lvl):
            src = last8[base + lvl - 8:base + lvl, :]
            groups.extend([src] * (2 * lvl // 8))
        gr_sc[3 + li] = jnp.concatenate(groups, axis=0)

    x_i = r_i ^ c_i
    nt_dims = (((1,), (1,)), ((), ()))
    for h in range(H):
        before_head[h]()
        cols = slice(h * DK, (h + 1) * DK)
        q_h = q_sc[:, cols]
        k_h = k_sc[:, cols]
        g_h = g_sc[:, cols]
        v_h = v_ref[:, cols].astype(BF16)

        q_b = q_h.astype(BF16)
        k_b = k_h.astype(BF16)
        a = lax.dot_general(q_b, k_b, nt_dims, preferred_element_type=F32)
        for li in range(HGRN_NUM_LEVELS):
            e = jnp.exp2(_neg_abs(g_h - gr_sc[li, :, cols])).astype(BF16)
            p = lax.dot_general(q_b * e, k_b * e, nt_dims, preferred_element_type=F32)
            a = jnp.where(x_i >= (1 << li), p, a)
        a = jnp.where(r_i >= c_i, a, 0.0)
        o = jnp.dot(a.astype(BF16), v_h, preferred_element_type=F32)

        st = st_sc[h]
        g_last = g_h[TB - 1:TB, :]
        qd = (q_h * jnp.exp2(g_h)).astype(BF16)
        o = o + lax.dot_general(qd, st.astype(BF16), nt_dims, preferred_element_type=F32)
        kd = (k_h * jnp.exp2(g_last - g_h)).astype(BF16)
        st_sc[h] = jnp.exp2(g_last) * st + lax.dot_general(v_h, kd, (((0,), (0,)), ((), ())),
                                                           preferred_element_type=F32)

        on = o * lax.rsqrt(jnp.mean(o * o, axis=-1, keepdims=True) + EPS) * hn_ref[:, cols]
        og = og_ref[:, cols]
        y_ref[:, cols] = (on * (og * _sigmoid(og))).astype(y_ref.dtype)


def _mix_kernel(mq_ref, mk_ref, mv_ref, mo_ref, gcol_ref, grow_ref, cw_ref, cb_ref, bcol_ref, brow_ref,
                mhn_ref, hq_ref, hf_ref, hi_ref, hg_ref, lbl_ref, hhn_ref,
                x_ref, gm_ref, gh_ref, wm_ref, wh_ref, wo_ref,
                out_ref,
                ym_sc, yh_sc, merged_sc, xpad_sc, c_sc, n_sc, m_sc, g_sc, k_sc, q_sc, gr_sc, st_sc,
                *, blocks_per_seq):
    s = pl.program_id(0)
    cur = s % 2
    prv = 1 - cur

    @pl.when(s == 0)
    def _():
        ym_sc[...] = jnp.zeros_like(ym_sc)
        yh_sc[...] = jnp.zeros_like(yh_sc)

    @pl.when(s % blocks_per_seq == 0)
    def _():
        xpad_sc[0:8, :] = jnp.zeros((8, 2 * MLSTM_QK_W), F32)
        c_sc[...] = jnp.zeros_like(c_sc)
        n_sc[...] = jnp.zeros_like(n_sc)
        m_sc[...] = jnp.zeros_like(m_sc)
        st_sc[...] = jnp.zeros_like(st_sc)

    def gate_piece(c):
        def run():
            cols = slice(c * MERGE_GATE_COLS, (c + 1) * MERGE_GATE_COLS)
            pm = jnp.dot(ym_sc[prv], wm_ref[:, cols], preferred_element_type=F32)
            ph = jnp.dot(yh_sc[prv], wh_ref[:, cols], preferred_element_type=F32)
            merged = _sigmoid(gm_ref[:, cols]) * pm + _sigmoid(gh_ref[:, cols]) * ph
            merged_sc[:, cols] = merged.astype(BF16)
        return run

    def out_piece(c):
        def run():
            cols = slice(c * MERGE_OUT_COLS, (c + 1) * MERGE_OUT_COLS)
            out_ref[:, cols] = x_ref[:, cols] + jnp.dot(merged_sc[...], wo_ref[:, cols],
                                                        preferred_element_type=F32)
        return run

    pieces = ([gate_piece(c) for c in range(D_MODEL // MERGE_GATE_COLS)]
              + [out_piece(c) for c in range(D_MODEL // MERGE_OUT_COLS)])
    n_heads = MLSTM_HEADS + HGRN_HEADS
    n_slots = MIX_SUBBLOCKS * n_heads
    slot_of = [(i * n_slots) // len(pieces) for i in range(len(pieces))]
    hooks = [(pieces[slot_of.index(i)] if i in slot_of else (lambda: None)) for i in range(n_slots)]

    for sub in range(MIX_SUBBLOCKS):
        rows = pl.ds(sub * MIX_TB, MIX_TB)
        sub_hooks = hooks[sub * n_heads:(sub + 1) * n_heads]
        _mlstm_block(mq_ref.at[rows], mk_ref.at[rows], mv_ref.at[rows], mo_ref.at[rows], gcol_ref.at[rows],
                     grow_ref.at[:, rows], cw_ref, cb_ref, bcol_ref, brow_ref, mhn_ref,
                     ym_sc.at[cur, rows], xpad_sc, c_sc, n_sc, m_sc, sub_hooks[:MLSTM_HEADS])
        _hgrn_block(hq_ref.at[rows], hf_ref.at[rows], hi_ref.at[rows], hg_ref.at[rows], lbl_ref, hhn_ref,
                    yh_sc.at[cur, rows], g_sc, k_sc, q_sc, gr_sc, st_sc, sub_hooks[MLSTM_HEADS:])


def _const_spec(shape):
    return pl.BlockSpec(shape, lambda *_: (0,) * len(shape), pipeline_mode=pl.Buffered(1))


def _mix(x2d, proj, gates_col, gates_row, conv_w, conv_b, bias_col, bias_row, m_head_norm,
         lb_logits, h_head_norm, w_m, w_h, w_o, *, seq_len):
    n, d = x2d.shape
    tb = MIX_TB * MIX_SUBBLOCKS
    nblk = n // tb
    qw, vw, hw = MLSTM_QK_W, MLSTM_V_W, HGRN_W

    def cur(col_block):
        return lambda s: (jnp.minimum(s, nblk - 1), col_block)

    def prev(col_block):
        return lambda s: (jnp.maximum(s - 1, 0), col_block)

    return pl.pallas_call(
        functools.partial(_mix_kernel, blocks_per_seq=seq_len // tb),
        grid=(nblk + 1,),
        in_specs=[
            pl.BlockSpec((tb, qw), cur(COL_MQ // qw)),
            pl.BlockSpec((tb, qw), cur(COL_MK // qw)),
            pl.BlockSpec((tb, vw), cur(COL_MV // vw)),
            pl.BlockSpec((tb, vw), cur(COL_MO // vw)),
            pl.BlockSpec((tb, LANES), cur(0)),
            pl.BlockSpec((2 * MLSTM_HEADS, tb), lambda s: (0, jnp.minimum(s, nblk - 1))),
            _const_spec((CONV_WIDTH, 2 * qw)),
            _const_spec((1, 2 * qw)),
            _const_spec((1, LANES)),
            _const_spec((2 * MLSTM_HEADS, 1)),
            _const_spec((1, vw)),
            pl.BlockSpec((tb, hw), cur(COL_HQ // hw)),
            pl.BlockSpec((tb, hw), cur(COL_HF // hw)),
            pl.BlockSpec((tb, hw), cur(COL_HI // hw)),
            pl.BlockSpec((tb, hw), cur(COL_HG // hw)),
            _const_spec((2, hw)),
            _const_spec((1, hw)),
            pl.BlockSpec((tb, d), prev(0)),
            pl.BlockSpec((tb, d), prev(COL_GM // d)),
            pl.BlockSpec((tb, d), prev(COL_GH // d)),
            _const_spec(w_m.shape),
            _const_spec(w_h.shape),
            _const_spec(w_o.shape),
        ],
        out_specs=pl.BlockSpec((tb, d), prev(0)),
        out_shape=jax.ShapeDtypeStruct((n, d), F32),
        scratch_shapes=[
            pltpu.VMEM((2, tb, vw), BF16),
            pltpu.VMEM((2, tb, hw), BF16),
            pltpu.VMEM((tb, d), BF16),
            pltpu.VMEM((MIX_TB + 8, 2 * qw), F32),
            pltpu.VMEM((MLSTM_HEADS, MLSTM_QK, MLSTM_V), F32),
            pltpu.VMEM((8, MLSTM_QK), F32),
            pltpu.VMEM((8, LANES), F32),
            pltpu.VMEM((MIX_TB, hw), F32),
            pltpu.VMEM((MIX_TB, hw), F32),
            pltpu.VMEM((MIX_TB, hw), F32),
            pltpu.VMEM((HGRN_NUM_LEVELS, MIX_TB, hw), F32),
            pltpu.VMEM((HGRN_HEADS, HGRN_DV, HGRN_DK), F32),
        ],
        compiler_params=_params("arbitrary"),
        name="mix",
    )(proj, proj, proj, proj, gates_col, gates_row, conv_w, conv_b, bias_col, bias_row, m_head_norm,
      proj, proj, proj, proj, lb_logits, h_head_norm, x2d, proj, proj, w_m, w_h, w_o)


def _xattn_prep_kernel(m_ref, g_ref, wk_ref, wv_ref, wq_ref, wo_ref, qk_ref, vo_ref):
    wk = wk_ref[...].astype(BF16)
    wv = wv_ref[...].astype(BF16)
    wq = wq_ref[...].astype(BF16)
    wo = wo_ref[...].astype(BF16)
    for i in range(m_ref.shape[0]):
        m = _rms(m_ref[i], g_ref[...]).astype(BF16)
        k = jnp.dot(m, wk, preferred_element_type=F32).astype(BF16)
        v = jnp.dot(m, wv, preferred_element_type=F32).astype(BF16)
        qk = lax.dot_general(wq, k, (((1,), (1,)), ((), ())), preferred_element_type=F32)
        qk_ref[i] = (qk * (XATTN_HEAD_DIM ** -0.5)).astype(qk_ref.dtype)
        vo_ref[i] = jnp.dot(v, wo, preferred_element_type=F32).astype(vo_ref.dtype)


def _xattn_prep(mem, g, w_kv, w_q, w_o):
    b, m, d = mem.shape
    hd = XATTN_HEAD_DIM
    nh = XATTN_HEADS
    return pl.pallas_call(
        _xattn_prep_kernel,
        grid=(nh,),
        in_specs=[
            pl.BlockSpec((b, m, d), lambda h: (0, 0, 0)),
            pl.BlockSpec((1, d), lambda h: (0, 0)),
            pl.BlockSpec((d, hd), lambda h: (0, h)),
            pl.BlockSpec((d, hd), lambda h: (0, nh + h)),
            pl.BlockSpec((d, hd), lambda h: (0, h)),
            pl.BlockSpec((hd, d), lambda h: (h, 0)),
        ],
        out_specs=[
            pl.BlockSpec((b, d, m), lambda h: (0, 0, h)),
            pl.BlockSpec((b, m, d), lambda h: (0, h, 0)),
        ],
        out_shape=[
            jax.ShapeDtypeStruct((b, d, nh * m), BF16),
            jax.ShapeDtypeStruct((b, nh * m, d), BF16),
        ],
        compiler_params=_params("arbitrary"),
        name="xattn_prep",
    )(mem, g.reshape(1, d), w_kv, w_kv, w_q, w_o)


def _xattn_kernel(x_ref, g_ref, qk_ref, vo_ref, o_ref, *, mem_len):
    x = x_ref[0]
    h = _rms(x, g_ref[...]).astype(BF16)
    s = jnp.dot(h, qk_ref[0], preferred_element_type=F32)
    probs = []
    for hd in range(XATTN_HEADS):
        s_h = s[:, hd * mem_len:(hd + 1) * mem_len]
        e = jnp.exp(s_h - jnp.max(s_h, axis=-1, keepdims=True))
        probs.append((e / jnp.sum(e, axis=-1, keepdims=True)).astype(BF16))
    p = jnp.concatenate(probs, axis=-1)
    o_ref[0] = x + jnp.dot(p, vo_ref[0], preferred_element_type=F32)


def _xattn(x, g, qk, vo, *, tm=512):
    b, t, d = x.shape
    hm = qk.shape[2]
    return pl.pallas_call(
        functools.partial(_xattn_kernel, mem_len=hm // XATTN_HEADS),
        grid=(b, t // tm),
        in_specs=[
            pl.BlockSpec((1, tm, d), lambda i, c: (i, c, 0)),
            _const_spec((1, d)),
            pl.BlockSpec((1, d, hm), lambda i, c: (i, 0, 0)),
            pl.BlockSpec((1, hm, d), lambda i, c: (i, 0, 0)),
        ],
        out_specs=pl.BlockSpec((1, tm, d), lambda i, c: (i, c, 0)),
        out_shape=jax.ShapeDtypeStruct((b, t, d), F32),
        compiler_params=_params("parallel", "parallel"),
        name="xattn",
    )(x, g.reshape(1, d), qk, vo)


def kernel(x, mem, norm_ffn1, ffn1_w1, ffn1_w3, ffn1_w2, norm_mix, w_in, mlstm_conv_w, mlstm_conv_b,
           mlstm_ig_bias, mlstm_fg_bias, mlstm_head_norm, hgrn_lb_logits, hgrn_head_norm, w_proj_m,
           w_proj_h, w_out, norm_xattn, norm_mem, xattn_wq, xattn_wkv, xattn_wo, norm_ffn2, ffn2_w1,
           ffn2_w3, ffn2_w2, norm_final):
    b, t, d = x.shape
    depth = norm_ffn1.shape[0]
    assert depth == 1 and hgrn_lb_logits.shape[0] == 2
    n = b * t
    l = 0
    bf = lambda w: w.astype(BF16)

    w_in_t = jnp.swapaxes(w_in[l], 0, 1)
    n_gate = 2 * MLSTM_HEADS
    gate_bias = jnp.concatenate([mlstm_ig_bias[l], mlstm_fg_bias[l]]).astype(F32)
    bias_col = jnp.pad(gate_bias, (0, LANES - n_gate)).reshape(1, LANES)
    bias_row = gate_bias.reshape(n_gate, 1)

    x1, hn, gates_col, gates_row = _ffn(x.reshape(n, d), norm_ffn1[l], ffn1_w1[l], ffn1_w3[l], ffn1_w2[l],
                                        norm_mix[l], w_in_t)

    proj = _in_proj(hn, w_in_t)
    x2 = _mix(x1, proj, gates_col, gates_row, mlstm_conv_w[l], mlstm_conv_b[l].reshape(1, -1),
              bias_col, bias_row, mlstm_head_norm[l].reshape(1, -1), hgrn_lb_logits,
              hgrn_head_norm[l].reshape(1, -1), bf(w_proj_m[l]), bf(w_proj_h[l]), bf(w_out[l]), seq_len=t)

    qk, vo = _xattn_prep(mem, norm_mem[l], xattn_wkv[l], xattn_wq[l], xattn_wo[l])
    x3 = _xattn(x2.reshape(b, t, d), norm_xattn[l], qk, vo)

    (out,) = _ffn(x3.reshape(n, d), norm_ffn2[l], ffn2_w1[l], ffn2_w3[l], ffn2_w2[l], norm_final)
    return out.reshape(b, t, d)
```

```python
import functools

import jax
import jax.numpy as jnp
from jax import lax
from jax.experimental import pallas as pl
from jax.experimental.pallas import tpu as pltpu

F32 = jnp.float32
BF16 = jnp.bfloat16
EPS = 1e-6
LOG2_E = 1.4426950408889634

D_MODEL = 2048
D_FF = 5632
MLSTM_HEADS = 4
MLSTM_QK = 128
MLSTM_V = 256
CONV_WIDTH = 4
HGRN_HEADS = 8
HGRN_DK = 128
HGRN_DV = 128
XATTN_HEADS = 4
XATTN_HEAD_DIM = D_MODEL // XATTN_HEADS

MLSTM_QK_W = MLSTM_HEADS * MLSTM_QK
MLSTM_V_W = MLSTM_HEADS * MLSTM_V
HGRN_W = HGRN_HEADS * HGRN_DK

VMEM_LIMIT_BYTES = 60000 * 1024
LANES = 128

COL_GM = 0
COL_GH = COL_GM + D_MODEL
COL_MQ = COL_GH + D_MODEL
COL_MK = COL_MQ + MLSTM_QK_W
COL_MV = COL_MK + MLSTM_QK_W
COL_MO = COL_MV + MLSTM_V_W
COL_HQ = COL_MO + MLSTM_V_W
COL_HF = COL_HQ + HGRN_W
COL_HI = COL_HF + HGRN_W
COL_HG = COL_HI + HGRN_W
PROJ_W = COL_HG + HGRN_W

MIX_GATE_ROW = 2 * MLSTM_QK_W + 2 * MLSTM_V_W
MIX_HGRN_ROW = MIX_GATE_ROW + 2 * MLSTM_HEADS
MIX_MERGE_ROW = MIX_HGRN_ROW + 4 * HGRN_W

PROJ_SUB_ROWS = 1024
FFN_NORM_ROWS = 64
FFN_FINISH_ROWS = 64
FFN_SUB_COLS = 256

MIX_TB = 128
HGRN_NUM_LEVELS = MIX_TB.bit_length() - 1
MIX_SUBBLOCKS = 2
MERGE_GATE_COLS = 512
MERGE_OUT_COLS = 256
HGRN_COARSE_LEVELS = tuple(1 << i for i in range(3, HGRN_NUM_LEVELS))


def _rms(x, g):
    return x * lax.rsqrt(jnp.mean(x * x, axis=-1, keepdims=True) + EPS) * g


def _sigmoid(x):
    return 1.0 / (1.0 + jnp.exp(-x))


def _log_sigmoid(x):
    return jnp.minimum(x, 0.0) - jnp.log(1.0 + jnp.exp(-jnp.abs(x)))


def _neg_abs(x):
    return pltpu.bitcast(pltpu.bitcast(x, jnp.uint32) | jnp.uint32(0x80000000), F32)


def _params(*sem, **extra):
    return pltpu.CompilerParams(dimension_semantics=sem, vmem_limit_bytes=VMEM_LIMIT_BYTES, **extra)


def _ffn_kernel(*refs, mode):
    if mode == "mix":
        (x_hbm, g_ref, w1_ref, w3_ref, w2_ref, gn_ref, wg_ref,
         o_ref, hn_ref, gcol_ref, grow_ref, h_sc, x_buf, x_sem) = refs
    else:
        x_hbm, g_ref, w1_ref, w3_ref, w2_ref, gn_ref, o_ref, h_sc, x_buf, x_sem = refs
    i = pl.program_id(0)
    j = pl.program_id(1)
    tm = x_buf.shape[0]

    def x_copy(tile):
        rows = pl.ds(pl.multiple_of(tile * tm, tm), tm)
        return pltpu.make_async_copy(x_hbm.at[rows, :], x_buf, x_sem)

    def row_chunk(r, size):
        return pl.ds(pl.multiple_of(r * size, size), size)

    @pl.when(j == 0)
    def _():
        @pl.when(i == 0)
        def _():
            x_copy(0).start()

        x_copy(i).wait()

        def norm_rows(r, carry):
            rows = row_chunk(r, FFN_NORM_ROWS)
            xr = x_buf[rows, :]
            h_sc[rows, :] = _rms(xr, g_ref[...]).astype(BF16)
            o_ref[rows, :] = 2.0 * xr
            return carry
        lax.fori_loop(0, tm // FFN_NORM_ROWS, norm_rows, 0)

    @pl.when(jnp.logical_and(j == 1, i + 1 < pl.num_programs(0)))
    def _():
        x_copy(i + 1).start()

    h = h_sc[...]
    for c in range(w1_ref.shape[1] // FFN_SUB_COLS):
        cols = slice(c * FFN_SUB_COLS, (c + 1) * FFN_SUB_COLS)
        a = jnp.dot(h, w1_ref[:, cols].astype(BF16), preferred_element_type=F32)
        b = jnp.dot(h, w3_ref[:, cols].astype(BF16), preferred_element_type=F32)
        act = (a * _sigmoid(a) * b).astype(BF16)
        o_ref[...] += jnp.dot(act, w2_ref[cols, :].astype(BF16), preferred_element_type=F32)

    @pl.when(j == pl.num_programs(1) - 1)
    def _():
        if mode == "final":
            def finish_rows(r, carry):
                rows = row_chunk(r, FFN_FINISH_ROWS)
                o_ref[rows, :] = _rms(0.5 * o_ref[rows, :], gn_ref[...])
                return carry
            lax.fori_loop(0, tm // FFN_FINISH_ROWS, finish_rows, 0)
        else:
            y = 0.5 * o_ref[...]
            o_ref[...] = y
            hn = _rms(y, gn_ref[...]).astype(BF16)
            hn_ref[...] = hn
            nt_dims = (((1,), (1,)), ((), ()))
            wg = wg_ref[...].astype(BF16)
            wg_pad = jnp.concatenate([wg, jnp.zeros((LANES - wg.shape[0], wg.shape[1]), BF16)], axis=0)
            gcol = lax.dot_general(hn, wg_pad, nt_dims, preferred_element_type=F32)
            gcol_ref[...] = gcol
            grow_ref[...] = gcol.T[0:wg.shape[0], :]


def _ffn(x2d, g, w1, w3, w2, g_next, w_gates_t=None, *, tm=1024, tf=512):
    n, d = x2d.shape
    f = w1.shape[1]
    mode = "final" if w_gates_t is None else "mix"
    in_specs = [
        pl.BlockSpec(memory_space=pl.ANY),
        pl.BlockSpec((1, d), lambda i, j: (0, 0)),
        pl.BlockSpec((d, tf), lambda i, j: (0, j)),
        pl.BlockSpec((d, tf), lambda i, j: (0, j)),
        pl.BlockSpec((tf, d), lambda i, j: (j, 0)),
        pl.BlockSpec((1, d), lambda i, j: (0, 0)),
    ]
    args = [x2d, g.reshape(1, d), w1, w3, w2, g_next.reshape(1, d)]
    single = dict(pipeline_mode=pl.Buffered(1))
    out_mode = {} if mode == "final" else single
    out_specs = [pl.BlockSpec((tm, d), lambda i, j: (i, 0), **out_mode)]
    out_shape = [jax.ShapeDtypeStruct((n, d), F32)]
    if mode == "mix":
        ng = 2 * MLSTM_HEADS
        in_specs.append(pl.BlockSpec((ng, d), lambda i, j: (MIX_GATE_ROW // ng, 0)))
        args.append(w_gates_t)
        out_specs += [
            pl.BlockSpec((tm, d), lambda i, j: (i, 0), **single),
            pl.BlockSpec((tm, LANES), lambda i, j: (i, 0)),
            pl.BlockSpec((ng, tm), lambda i, j: (0, i)),
        ]
        out_shape += [
            jax.ShapeDtypeStruct((n, d), BF16),
            jax.ShapeDtypeStruct((n, LANES), F32),
            jax.ShapeDtypeStruct((ng, n), F32),
        ]
    return pl.pallas_call(
        functools.partial(_ffn_kernel, mode=mode),
        grid=(n // tm, f // tf),
        in_specs=in_specs,
        out_specs=out_specs,
        out_shape=out_shape,
        scratch_shapes=[
            pltpu.VMEM((tm, d), BF16),
            pltpu.VMEM((tm, d), F32),
            pltpu.SemaphoreType.DMA(()),
        ],
        compiler_params=_params("arbitrary", "arbitrary"),
        name="ffn_" + mode,
    )(*args)


def _in_proj_kernel(h_ref, wt_ref, p_ref, w_sc):
    @pl.when(pl.program_id(1) == 0)
    def _():
        w_sc[...] = wt_ref[...].astype(BF16)

    for r in range(0, h_ref.shape[0], PROJ_SUB_ROWS):
        rows = slice(r, r + PROJ_SUB_ROWS)
        p_ref[rows, :] = lax.dot_general(h_ref[rows, :], w_sc[...], (((1,), (1,)), ((), ())),
                                         preferred_element_type=F32).astype(p_ref.dtype)


def _proj_row_offset(j, tn):
    n_gate_tiles = 2 * D_MODEL // tn
    n_mlstm_tiles = MIX_GATE_ROW // tn
    off8 = jnp.where(j < n_gate_tiles, MIX_MERGE_ROW // 8 + j * (tn // 8),
                     jnp.where(j < n_gate_tiles + n_mlstm_tiles, (j - n_gate_tiles) * (tn // 8),
                               MIX_HGRN_ROW // 8 + (j - n_gate_tiles - n_mlstm_tiles) * (tn // 8)))
    return off8 * 8


def _in_proj(hn, w_t, *, tm=2048, tn=1024):
    n, d = hn.shape
    return pl.pallas_call(
        _in_proj_kernel,
        grid=(PROJ_W // tn, n // tm),
        in_specs=[
            pl.BlockSpec((tm, d), lambda j, i: (i, 0)),
            pl.BlockSpec((pl.Element(tn), pl.Element(d)), lambda j, i: (_proj_row_offset(j, tn), 0)),
        ],
        out_specs=pl.BlockSpec((tm, tn), lambda j, i: (i, j)),
        out_shape=jax.ShapeDtypeStruct((n, PROJ_W), BF16),
        scratch_shapes=[pltpu.VMEM((tn, d), BF16)],
        compiler_params=_params("arbitrary", "arbitrary"),
        name="in_proj",
    )(hn, w_t)


def _mlstm_block(q_ref, k_ref, v_ref, o_ref, gcol_ref, grow_ref, cw_ref, cb_ref, bcol_ref, brow_ref,
                 hn_ref, y_ref, xpad_sc, c_sc, n_sc, m_sc, before_head):
    L = MIX_TB
    H = MLSTM_HEADS

    xpad_sc[8:8 + L, 0:MLSTM_QK_W] = q_ref[...].astype(F32)
    xpad_sc[8:8 + L, MLSTM_QK_W:2 * MLSTM_QK_W] = k_ref[...].astype(F32)
    acc = jnp.broadcast_to(cb_ref[...], (L, 2 * MLSTM_QK_W))
    for j in range(CONV_WIDTH):
        off = 8 - (CONV_WIDTH - 1) + j
        acc = acc + cw_ref[j:j + 1, :] * xpad_sc[off:off + L, :]
    xpad_sc[0:8, :] = xpad_sc[L:L + 8, :]
    qk = acc * _sigmoid(acc)

    gcol = gcol_ref[...] + bcol_ref[...]
    grow = grow_ref[...] + brow_ref[...]
    fcol = _log_sigmoid(gcol)
    frow = _log_sigmoid(grow)
    r_i = lax.broadcasted_iota(jnp.int32, (L, L), 0)
    c_i = lax.broadcasted_iota(jnp.int32, (L, L), 1)
    causal = r_i >= c_i
    tril = jnp.where(causal, 1.0, 0.0).astype(F32)
    triu = jnp.where(r_i <= c_i, 1.0, 0.0).astype(F32)
    bcol = jnp.dot(tril, fcol, preferred_element_type=F32, precision=lax.Precision.HIGHEST)
    brow = jnp.dot(frow, triu, preferred_element_type=F32, precision=lax.Precision.HIGHEST)

    for h in range(H):
        before_head[h]()
        b_c = bcol[:, H + h:H + h + 1]
        i_c = gcol[:, h:h + 1]
        b_r = brow[H + h:H + h + 1, :]
        i_r = grow[h:h + 1, :]
        m_prev = m_sc[h:h + 1, 0:1]

        d_log = jnp.where(causal, b_c - (b_r - i_r), -jnp.inf)
        inter_log = b_c + m_prev
        m_t = jnp.maximum(jnp.max(d_log, axis=1, keepdims=True), inter_log)

        q_h = qk[:, h * MLSTM_QK:(h + 1) * MLSTM_QK]
        k_h = qk[:, MLSTM_QK_W + h * MLSTM_QK:MLSTM_QK_W + (h + 1) * MLSTM_QK] * (MLSTM_QK ** -0.5)
        v_h = v_ref[:, h * MLSTM_V:(h + 1) * MLSTM_V].astype(BF16)
        q_b = q_h.astype(BF16)

        s = lax.dot_general(q_b, k_h.astype(BF16), (((1,), (1,)), ((), ())), preferred_element_type=F32)
        s = s * jnp.exp(d_log - m_t)
        w_inter = jnp.exp(inter_log - m_t)
        c_h = c_sc[h]
        n_h = n_sc[h:h + 1, :]
        num = (jnp.dot(s.astype(BF16), v_h, preferred_element_type=F32)
               + w_inter * jnp.dot(q_b, c_h.astype(BF16), preferred_element_type=F32))
        den = jnp.sum(s, axis=1, keepdims=True) + w_inter * jnp.sum(q_h * n_h, axis=1, keepdims=True)
        hh = num / jnp.maximum(jnp.abs(den), jnp.exp(-m_t))

        hn = hh * lax.rsqrt(jnp.mean(hh * hh, axis=-1, keepdims=True) + EPS)
        hn = hn * hn_ref[:, h * MLSTM_V:(h + 1) * MLSTM_V]
        y = hn * _sigmoid(o_ref[:, h * MLSTM_V:(h + 1) * MLSTM_V].astype(F32))
        y_ref[:, h * MLSTM_V:(h + 1) * MLSTM_V] = y.astype(y_ref.dtype)

        b_last = b_c[L - 1:L, :]
        a_log = b_last - b_c + i_c
        m_new = jnp.maximum(b_last + m_prev, jnp.max(a_log, axis=0, keepdims=True))
        w_a = jnp.exp(a_log - m_new)
        decay = jnp.exp(b_last + m_prev - m_new)
        kw = k_h * w_a
        c_sc[h] = decay * c_h + lax.dot_general(kw.astype(BF16), v_h, (((0,), (0,)), ((), ())),
                                                preferred_element_type=F32)
        n_sc[h:h + 1, :] = decay * n_h + jnp.sum(kw, axis=0, keepdims=True)
        m_sc[h:h + 1, :] = jnp.broadcast_to(m_new, (1, LANES))


def _hgrn_block(q_ref, f_ref, v_ref, og_ref, lbl_ref, hn_ref, y_ref, g_sc, k_sc, q_sc, gr_sc, st_sc,
                before_head):
    TB = MIX_TB
    H = HGRN_HEADS
    DK = HGRN_DK
    W = H * DK
    NT = TB // 8

    lbl = lbl_ref[...]
    lmax = jnp.max(lbl, axis=0, keepdims=True)
    le = jnp.exp(lbl - lmax)
    lb = le[1:2, :] / jnp.sum(le, axis=0, keepdims=True)

    f = lb + (1.0 - lb) * _sigmoid(f_ref[...].astype(F32))
    k_sc[...] = 1.0 - f
    qraw = q_ref[...].astype(F32)
    q_sc[...] = qraw * _sigmoid(qraw) * (DK ** -0.5)
    r_i = lax.broadcasted_iota(jnp.int32, (TB, TB), 0)
    c_i = lax.broadcasted_iota(jnp.int32, (TB, TB), 1)
    tri = jnp.where(r_i >= c_i, 1.0, 0.0).astype(F32)
    g = jnp.dot(tri, jnp.log(f) * LOG2_E, preferred_element_type=F32, precision=lax.Precision.HIGHEST)
    g_sc[...] = g

    g3 = g.reshape(NT, 8, W)
    sub = lax.broadcasted_iota(jnp.int32, (NT, 8, W), 1)
    bit0 = (sub & 1) != 0
    bit1 = (sub & 2) != 0
    bit2 = (sub & 4) != 0
    last2 = jnp.where(bit0, g3, pltpu.roll(g3, 7, 1))
    last4 = jnp.where(bit1, last2, pltpu.roll(last2, 6, 1))
    last4_r = pltpu.roll(last4, 4, 1)
    last8 = jnp.where(bit2, last4, last4_r).reshape(TB, W)
    gr_sc[0] = jnp.where(bit0, pltpu.roll(g3, 1, 1), g3).reshape(TB, W)
    gr_sc[1] = jnp.where(bit1, pltpu.roll(last2, 2, 1), last2).reshape(TB, W)
    gr_sc[2] = jnp.where(bit2, last4_r, last4).reshape(TB, W)
    for li, lvl in enumerate(HGRN_COARSE_LEVELS):
        groups = []
        for base in range(0, TB, 2 * lvl):
            src = last8[base + lvl - 8:base + lvl, :]
            groups.extend([src] * (2 * lvl // 8))
        gr_sc[3 + li] = jnp.concatenate(groups, axis=0)

    x_i = r_i ^ c_i
    nt_dims = (((1,), (1,)), ((), ()))
    for h in range(H):
        before_head[h]()
        cols = slice(h * DK, (h + 1) * DK)
        q_h = q_sc[:, cols]
        k_h = k_sc[:, cols]
        g_h = g_sc[:, cols]
        v_h = v_ref[:, cols].astype(BF16)

        q_b = q_h.astype(BF16)
        k_b = k_h.astype(BF16)
        a = lax.dot_general(q_b, k_b, nt_dims, preferred_element_type=F32)
        for li in range(HGRN_NUM_LEVELS):
            e = jnp.exp2(_neg_abs(g_h - gr_sc[li, :, cols])).astype(BF16)
            p = lax.dot_general(q_b * e, k_b * e, nt_dims, preferred_element_type=F32)
            a = jnp.where(x_i >= (1 << li), p, a)
        a = jnp.where(r_i >= c_i, a, 0.0)
        o = jnp.dot(a.astype(BF16), v_h, preferred_element_type=F32)

        st = st_sc[h]
        g_last = g_h[TB - 1:TB, :]
        qd = (q_h * jnp.exp2(g_h)).astype(BF16)
        o = o + lax.dot_general(qd, st.astype(BF16), nt_dims, preferred_element_type=F32)
        kd = (k_h * jnp.exp2(g_last - g_h)).astype(BF16)
        st_sc[h] = jnp.exp2(g_last) * st + lax.dot_general(v_h, kd, (((0,), (0,)), ((), ())),
                                                           preferred_element_type=F32)

        on = o * lax.rsqrt(jnp.mean(o * o, axis=-1, keepdims=True) + EPS) * hn_ref[:, cols]
        og = og_ref[:, cols].astype(F32)
        y_ref[:, cols] = (on * (og * _sigmoid(og))).astype(y_ref.dtype)


def _mix_kernel(mq_ref, mk_ref, mv_ref, mo_ref, gcol_ref, grow_ref, cw_ref, cb_ref, bcol_ref, brow_ref,
                mhn_ref, hq_ref, hf_ref, hi_ref, hg_ref, lbl_ref, hhn_ref,
                x_ref, gm_ref, gh_ref, wm_ref, wh_ref, wo_ref,
                out_ref,
                ym_sc, yh_sc, merged_sc, xpad_sc, c_sc, n_sc, m_sc, g_sc, k_sc, q_sc, gr_sc, st_sc,
                *, blocks_per_seq):
    s = pl.program_id(0)
    cur = s % 2
    prv = 1 - cur

    @pl.when(s == 0)
    def _():
        ym_sc[...] = jnp.zeros_like(ym_sc)
        yh_sc[...] = jnp.zeros_like(yh_sc)

    @pl.when(s % blocks_per_seq == 0)
    def _():
        xpad_sc[0:8, :] = jnp.zeros((8, 2 * MLSTM_QK_W), F32)
        c_sc[...] = jnp.zeros_like(c_sc)
        n_sc[...] = jnp.zeros_like(n_sc)
        m_sc[...] = jnp.zeros_like(m_sc)
        st_sc[...] = jnp.zeros_like(st_sc)

    def gate_piece(c):
        def run():
            cols = slice(c * MERGE_GATE_COLS, (c + 1) * MERGE_GATE_COLS)
            pm = jnp.dot(ym_sc[prv], wm_ref[:, cols], preferred_element_type=F32)
            ph = jnp.dot(yh_sc[prv], wh_ref[:, cols], preferred_element_type=F32)
            merged = (_sigmoid(gm_ref[:, cols].astype(F32)) * pm
                      + _sigmoid(gh_ref[:, cols].astype(F32)) * ph)
            merged_sc[:, cols] = merged.astype(BF16)
        return run

    def out_piece(c):
        def run():
            cols = slice(c * MERGE_OUT_COLS, (c + 1) * MERGE_OUT_COLS)
            out_ref[:, cols] = x_ref[:, cols] + jnp.dot(merged_sc[...], wo_ref[:, cols],
                                                        preferred_element_type=F32)
        return run

    pieces = ([gate_piece(c) for c in range(D_MODEL // MERGE_GATE_COLS)]
              + [out_piece(c) for c in range(D_MODEL // MERGE_OUT_COLS)])
    n_heads = MLSTM_HEADS + HGRN_HEADS
    n_slots = MIX_SUBBLOCKS * n_heads
    slot_of = [(i * n_slots) // len(pieces) for i in range(len(pieces))]
    hooks = [(pieces[slot_of.index(i)] if i in slot_of else (lambda: None)) for i in range(n_slots)]

    for sub in range(MIX_SUBBLOCKS):
        rows = pl.ds(sub * MIX_TB, MIX_TB)
        sub_hooks = hooks[sub * n_heads:(sub + 1) * n_heads]
        _mlstm_block(mq_ref.at[rows], mk_ref.at[rows], mv_ref.at[rows], mo_ref.at[rows], gcol_ref.at[rows],
                     grow_ref.at[:, rows], cw_ref, cb_ref, bcol_ref, brow_ref, mhn_ref,
                     ym_sc.at[cur, rows], xpad_sc, c_sc, n_sc, m_sc, sub_hooks[:MLSTM_HEADS])
        _hgrn_block(hq_ref.at[rows], hf_ref.at[rows], hi_ref.at[rows], hg_ref.at[rows], lbl_ref, hhn_ref,
                    yh_sc.at[cur, rows], g_sc, k_sc, q_sc, gr_sc, st_sc, sub_hooks[MLSTM_HEADS:])


def _const_spec(shape):
    return pl.BlockSpec(shape, lambda *_: (0,) * len(shape), pipeline_mode=pl.Buffered(1))


def _mix(x2d, proj, gates_col, gates_row, conv_w, conv_b, bias_col, bias_row, m_head_norm,
         lb_logits, h_head_norm, w_m, w_h, w_o, *, seq_len):
    n, d = x2d.shape
    tb = MIX_TB * MIX_SUBBLOCKS
    nblk = n // tb
    qw, vw, hw = MLSTM_QK_W, MLSTM_V_W, HGRN_W

    def cur(col_block):
        return lambda s: (jnp.minimum(s, nblk - 1), col_block)

    def prev(col_block):
        return lambda s: (jnp.maximum(s - 1, 0), col_block)

    return pl.pallas_call(
        functools.partial(_mix_kernel, blocks_per_seq=seq_len // tb),
        grid=(nblk + 1,),
        in_specs=[
            pl.BlockSpec((tb, qw), cur(COL_MQ // qw)),
            pl.BlockSpec((tb, qw), cur(COL_MK // qw)),
            pl.BlockSpec((tb, vw), cur(COL_MV // vw)),
            pl.BlockSpec((tb, vw), cur(COL_MO // vw)),
            pl.BlockSpec((tb, LANES), cur(0)),
            pl.BlockSpec((2 * MLSTM_HEADS, tb), lambda s: (0, jnp.minimum(s, nblk - 1))),
            _const_spec((CONV_WIDTH, 2 * qw)),
            _const_spec((1, 2 * qw)),
            _const_spec((1, LANES)),
            _const_spec((2 * MLSTM_HEADS, 1)),
            _const_spec((1, vw)),
            pl.BlockSpec((tb, hw), cur(COL_HQ // hw)),
            pl.BlockSpec((tb, hw), cur(COL_HF // hw)),
            pl.BlockSpec((tb, hw), cur(COL_HI // hw)),
            pl.BlockSpec((tb, hw), cur(COL_HG // hw)),
            _const_spec((2, hw)),
            _const_spec((1, hw)),
            pl.BlockSpec((tb, d), prev(0)),
            pl.BlockSpec((tb, d), prev(COL_GM // d)),
            pl.BlockSpec((tb, d), prev(COL_GH // d)),
            _const_spec(w_m.shape),
            _const_spec(w_h.shape),
            _const_spec(w_o.shape),
        ],
        out_specs=pl.BlockSpec((tb, d), prev(0)),
        out_shape=jax.ShapeDtypeStruct((n, d), F32),
        scratch_shapes=[
            pltpu.VMEM((2, tb, vw), BF16),
            pltpu.VMEM((2, tb, hw), BF16),
            pltpu.VMEM((tb, d), BF16),
            pltpu.VMEM((MIX_TB + 8, 2 * qw), F32),
            pltpu.VMEM((MLSTM_HEADS, MLSTM_QK, MLSTM_V), F32),
            pltpu.VMEM((8, MLSTM_QK), F32),
            pltpu.VMEM((8, LANES), F32),
            pltpu.VMEM((MIX_TB, hw), F32),
            pltpu.VMEM((MIX_TB, hw), F32),
            pltpu.VMEM((MIX_TB, hw), F32),
            pltpu.VMEM((HGRN_NUM_LEVELS, MIX_TB, hw), F32),
            pltpu.VMEM((HGRN_HEADS, HGRN_DV, HGRN_DK), F32),
        ],
        compiler_params=_params("arbitrary"),
        name="mix",
    )(proj, proj, proj, proj, gates_col, gates_row, conv_w, conv_b, bias_col, bias_row, m_head_norm,
      proj, proj, proj, proj, lb_logits, h_head_norm, x2d, proj, proj, w_m, w_h, w_o)


def _xattn_prep_kernel(m_ref, g_ref, wk_ref, wv_ref, wq_ref, wo_ref, qk_ref, vo_ref):
    wk = wk_ref[...].astype(BF16)
    wv = wv_ref[...].astype(BF16)
    wq = wq_ref[...].astype(BF16)
    wo = wo_ref[...].astype(BF16)
    for i in range(m_ref.shape[0]):
        m = _rms(m_ref[i], g_ref[...]).astype(BF16)
        k = jnp.dot(m, wk, preferred_element_type=F32).astype(BF16)
        v = jnp.dot(m, wv, preferred_element_type=F32).astype(BF16)
        qk = lax.dot_general(wq, k, (((1,), (1,)), ((), ())), preferred_element_type=F32)
        qk_ref[i] = (qk * (XATTN_HEAD_DIM ** -0.5)).astype(qk_ref.dtype)
        vo_ref[i] = jnp.dot(v, wo, preferred_element_type=F32).astype(vo_ref.dtype)


def _xattn_prep(mem, g, w_kv, w_q, w_o):
    b, m, d = mem.shape
    hd = XATTN_HEAD_DIM
    nh = XATTN_HEADS
    return pl.pallas_call(
        _xattn_prep_kernel,
        grid=(nh,),
        in_specs=[
            pl.BlockSpec((b, m, d), lambda h: (0, 0, 0)),
            pl.BlockSpec((1, d), lambda h: (0, 0)),
            pl.BlockSpec((d, hd), lambda h: (0, h)),
            pl.BlockSpec((d, hd), lambda h: (0, nh + h)),
            pl.BlockSpec((d, hd), lambda h: (0, h)),
            pl.BlockSpec((hd, d), lambda h: (h, 0)),
        ],
        out_specs=[
            pl.BlockSpec((b, d, m), lambda h: (0, 0, h)),
            pl.BlockSpec((b, m, d), lambda h: (0, h, 0)),
        ],
        out_shape=[
            jax.ShapeDtypeStruct((b, d, nh * m), BF16),
            jax.ShapeDtypeStruct((b, nh * m, d), BF16),
        ],
        compiler_params=_params("arbitrary"),
        name="xattn_prep",
    )(mem, g.reshape(1, d), w_kv, w_kv, w_q, w_o)


def _xattn_kernel(x_ref, g_ref, qk_ref, vo_ref, o_ref, *, mem_len):
    x = x_ref[0]
    h = _rms(x, g_ref[...]).astype(BF16)
    s = jnp.dot(h, qk_ref[0], preferred_element_type=F32)
    probs = []
    for hd in range(XATTN_HEADS):
        s_h = s[:, hd * mem_len:(hd + 1) * mem_len]
        e = jnp.exp(s_h - jnp.max(s_h, axis=-1, keepdims=True))
        probs.append((e / jnp.sum(e, axis=-1, keepdims=True)).astype(BF16))
    p = jnp.concatenate(probs, axis=-1)
    o_ref[0] = x + jnp.dot(p, vo_ref[0], preferred_element_type=F32)


def _xattn(x, g, qk, vo, *, tm=512):
    b, t, d = x.shape
    hm = qk.shape[2]
    return pl.pallas_call(
        functools.partial(_xattn_kernel, mem_len=hm // XATTN_HEADS),
        grid=(b, t // tm),
        in_specs=[
            pl.BlockSpec((1, tm, d), lambda i, c: (i, c, 0)),
            _const_spec((1, d)),
            pl.BlockSpec((1, d, hm), lambda i, c: (i, 0, 0)),
            pl.BlockSpec((1, hm, d), lambda i, c: (i, 0, 0)),
        ],
        out_specs=pl.BlockSpec((1, tm, d), lambda i, c: (i, c, 0)),
        out_shape=jax.ShapeDtypeStruct((b, t, d), F32),
        compiler_params=_params("parallel", "parallel"),
        name="xattn",
    )(x, g.reshape(1, d), qk, vo)


def kernel(x, mem, norm_ffn1, ffn1_w1, ffn1_w3, ffn1_w2, norm_mix, w_in, mlstm_conv_w, mlstm_conv_b,
           mlstm_ig_bias, mlstm_fg_bias, mlstm_head_norm, hgrn_lb_logits, hgrn_head_norm, w_proj_m,
           w_proj_h, w_out, norm_xattn, norm_mem, xattn_wq, xattn_wkv, xattn_wo, norm_ffn2, ffn2_w1,
           ffn2_w3, ffn2_w2, norm_final):
    b, t, d = x.shape
    depth = norm_ffn1.shape[0]
    assert depth == 1 and hgrn_lb_logits.shape[0] == 2
    n = b * t
    l = 0
    bf = lambda w: w.astype(BF16)

    w_in_t = jnp.swapaxes(w_in[l], 0, 1)
    n_gate = 2 * MLSTM_HEADS
    gate_bias = jnp.concatenate([mlstm_ig_bias[l], mlstm_fg_bias[l]]).astype(F32)
    bias_col = jnp.pad(gate_bias, (0, LANES - n_gate)).reshape(1, LANES)
    bias_row = gate_bias.reshape(n_gate, 1)

    x1, hn, gates_col, gates_row = _ffn(x.reshape(n, d), norm_ffn1[l], ffn1_w1[l], ffn1_w3[l], ffn1_w2[l],
                                        norm_mix[l], w_in_t)

    proj = _in_proj(hn, w_in_t)
    x2 = _mix(x1, proj, gates_col, gates_row, mlstm_conv_w[l], mlstm_conv_b[l].reshape(1, -1),
              bias_col, bias_row, mlstm_head_norm[l].reshape(1, -1), hgrn_lb_logits,
              hgrn_head_norm[l].reshape(1, -1), bf(w_proj_m[l]), bf(w_proj_h[l]), bf(w_out[l]), seq_len=t)

    qk, vo = _xattn_prep(mem, norm_mem[l], xattn_wkv[l], xattn_wq[l], xattn_wo[l])
    x3 = _xattn(x2.reshape(b, t, d), norm_xattn[l], qk, vo)

    (out,) = _ffn(x3.reshape(n, d), norm_ffn2[l], ffn2_w1[l], ffn2_w3[l], ffn2_w2[l], norm_final)
    return out.reshape(b, t, d)
```

```python
import functools

import jax
import jax.numpy as jnp
from jax import lax
from jax.experimental import pallas as pl
from jax.experimental.pallas import tpu as pltpu

F32 = jnp.float32
BF16 = jnp.bfloat16
EPS = 1e-6
LOG2_E = 1.4426950408889634

D_MODEL = 2048
D_FF = 5632
MLSTM_HEADS = 4
MLSTM_QK = 128
MLSTM_V = 256
CONV_WIDTH = 4
HGRN_HEADS = 8
HGRN_DK = 128
HGRN_DV = 128
XATTN_HEADS = 4
XATTN_HEAD_DIM = D_MODEL // XATTN_HEADS

MLSTM_QK_W = MLSTM_HEADS * MLSTM_QK
MLSTM_V_W = MLSTM_HEADS * MLSTM_V
HGRN_W = HGRN_HEADS * HGRN_DK

VMEM_LIMIT_BYTES = 60000 * 1024
LANES = 128

COL_GM = 0
COL_GH = COL_GM + D_MODEL
COL_MQ = COL_GH + D_MODEL
COL_MK = COL_MQ + MLSTM_QK_W
COL_MV = COL_MK + MLSTM_QK_W
COL_MO = COL_MV + MLSTM_V_W
COL_HQ = COL_MO + MLSTM_V_W
COL_HF = COL_HQ + HGRN_W
COL_HI = COL_HF + HGRN_W
COL_HG = COL_HI + HGRN_W
PROJ_W = COL_HG + HGRN_W

MIX_GATE_ROW = 2 * MLSTM_QK_W + 2 * MLSTM_V_W
MIX_HGRN_ROW = MIX_GATE_ROW + 2 * MLSTM_HEADS
MIX_MERGE_ROW = MIX_HGRN_ROW + 4 * HGRN_W

PROJ_SUB_ROWS = 1024
FFN_NORM_ROWS = 128
FFN_FINISH_ROWS = 128
FFN_SUB_COLS = 256

MIX_TB = 128
HGRN_NUM_LEVELS = MIX_TB.bit_length() - 1
MIX_SUBBLOCKS = 2
MERGE_GATE_COLS = 512
MERGE_OUT_COLS = 256
HGRN_COARSE_LEVELS = tuple(1 << i for i in range(3, HGRN_NUM_LEVELS))


def _rms(x, g):
    return x * lax.rsqrt(jnp.mean(x * x, axis=-1, keepdims=True) + EPS) * g


def _sigmoid(x):
    return 1.0 / (1.0 + jnp.exp(-x))


def _log_sigmoid(x):
    return jnp.minimum(x, 0.0) - jnp.log(1.0 + jnp.exp(-jnp.abs(x)))


def _neg_abs(x):
    return pltpu.bitcast(pltpu.bitcast(x, jnp.uint32) | jnp.uint32(0x80000000), F32)


def _params(*sem):
    return pltpu.CompilerParams(dimension_semantics=sem, vmem_limit_bytes=VMEM_LIMIT_BYTES)


def _ffn_kernel(*refs, mode):
    if mode == "mix":
        (x_hbm, g_ref, w1_ref, w3_ref, w2_ref, gn_ref, wg_ref,
         o_hbm, hn_ref, gcol_ref, grow_ref, h_sc, x_buf, acc_sc, x_sem, o_sem) = refs
    else:
        x_hbm, g_ref, w1_ref, w3_ref, w2_ref, gn_ref, o_hbm, h_sc, x_buf, acc_sc, x_sem, o_sem = refs
    i = pl.program_id(0)
    j = pl.program_id(1)
    n_i = pl.num_programs(0)
    tm = x_buf.shape[0]

    def tile_rows(tile):
        return pl.ds(pl.multiple_of(tile * tm, tm), tm)

    def x_copy(tile):
        return pltpu.make_async_copy(x_hbm.at[tile_rows(tile), :], x_buf, x_sem)

    def o_copy(tile):
        return pltpu.make_async_copy(acc_sc, o_hbm.at[tile_rows(tile), :], o_sem)

    def row_chunk(r, size):
        return pl.ds(pl.multiple_of(r * size, size), size)

    def swiglu_passes(first):
        h = h_sc[...]
        for c in range(w1_ref.shape[1] // FFN_SUB_COLS):
            cols = slice(c * FFN_SUB_COLS, (c + 1) * FFN_SUB_COLS)
            a = jnp.dot(h, w1_ref[:, cols].astype(BF16), preferred_element_type=F32)
            b = jnp.dot(h, w3_ref[:, cols].astype(BF16), preferred_element_type=F32)
            act = (a * _sigmoid(a) * b).astype(BF16)
            part = jnp.dot(act, w2_ref[cols, :].astype(BF16), preferred_element_type=F32)
            if first and c == 0:
                acc_sc[...] = 2.0 * x_buf[...] + part
            else:
                acc_sc[...] += part

    @pl.when(j == 0)
    def _():
        @pl.when(i == 0)
        def _():
            x_copy(0).start()

        x_copy(i).wait()

        def norm_rows(r, carry):
            rows = row_chunk(r, FFN_NORM_ROWS)
            h_sc[rows, :] = _rms(x_buf[rows, :], g_ref[...]).astype(BF16)
            return carry
        lax.fori_loop(0, tm // FFN_NORM_ROWS, norm_rows, 0)

        @pl.when(i > 0)
        def _():
            o_copy(i - 1).wait()

        swiglu_passes(first=True)

    @pl.when(j > 0)
    def _():
        swiglu_passes(first=False)

    @pl.when(jnp.logical_and(j == 1, i + 1 < n_i))
    def _():
        x_copy(i + 1).start()

    @pl.when(j == pl.num_programs(1) - 1)
    def _():
        if mode == "final":
            def finish_rows(r, carry):
                rows = row_chunk(r, FFN_FINISH_ROWS)
                acc_sc[rows, :] = _rms(0.5 * acc_sc[rows, :], gn_ref[...])
                return carry
            lax.fori_loop(0, tm // FFN_FINISH_ROWS, finish_rows, 0)
        else:
            y = 0.5 * acc_sc[...]
            acc_sc[...] = y
            hn = _rms(y, gn_ref[...]).astype(BF16)
            hn_ref[...] = hn
            nt_dims = (((1,), (1,)), ((), ()))
            wg = wg_ref[...].astype(BF16)
            wg_pad = jnp.concatenate([wg, jnp.zeros((LANES - wg.shape[0], wg.shape[1]), BF16)], axis=0)
            gcol = lax.dot_general(hn, wg_pad, nt_dims, preferred_element_type=F32)
            gcol_ref[...] = gcol
            grow_ref[...] = gcol.T[0:wg.shape[0], :]

        o_copy(i).start()

        @pl.when(i == n_i - 1)
        def _():
            o_copy(i).wait()


def _ffn(x2d, g, w1, w3, w2, g_next, w_gates_t=None, *, tm=1024, tf=512):
    n, d = x2d.shape
    f = w1.shape[1]
    mode = "final" if w_gates_t is None else "mix"
    in_specs = [
        pl.BlockSpec(memory_space=pl.ANY),
        pl.BlockSpec((1, d), lambda i, j: (0, 0)),
        pl.BlockSpec((d, tf), lambda i, j: (0, j)),
        pl.BlockSpec((d, tf), lambda i, j: (0, j)),
        pl.BlockSpec((tf, d), lambda i, j: (j, 0)),
        pl.BlockSpec((1, d), lambda i, j: (0, 0)),
    ]
    args = [x2d, g.reshape(1, d), w1, w3, w2, g_next.reshape(1, d)]
    single = dict(pipeline_mode=pl.Buffered(1))
    out_specs = [pl.BlockSpec(memory_space=pl.ANY)]
    out_shape = [jax.ShapeDtypeStruct((n, d), F32)]
    if mode == "mix":
        ng = 2 * MLSTM_HEADS
        in_specs.append(pl.BlockSpec((ng, d), lambda i, j: (MIX_GATE_ROW // ng, 0)))
        args.append(w_gates_t)
        out_specs += [
            pl.BlockSpec((tm, d), lambda i, j: (i, 0), **single),
            pl.BlockSpec((tm, LANES), lambda i, j: (i, 0)),
            pl.BlockSpec((ng, tm), lambda i, j: (0, i)),
        ]
        out_shape += [
            jax.ShapeDtypeStruct((n, d), BF16),
            jax.ShapeDtypeStruct((n, LANES), F32),
            jax.ShapeDtypeStruct((ng, n), F32),
        ]
    return pl.pallas_call(
        functools.partial(_ffn_kernel, mode=mode),
        grid=(n // tm, f // tf),
        in_specs=in_specs,
        out_specs=out_specs,
        out_shape=out_shape,
        scratch_shapes=[
            pltpu.VMEM((tm, d), BF16),
            pltpu.VMEM((tm, d), F32),
            pltpu.VMEM((tm, d), F32),
            pltpu.SemaphoreType.DMA(()),
            pltpu.SemaphoreType.DMA(()),
        ],
        compiler_params=_params("arbitrary", "arbitrary"),
        name="ffn_" + mode,
    )(*args)


def _in_proj_kernel(h_ref, wt_ref, p_ref, w_sc):
    @pl.when(pl.program_id(1) == 0)
    def _():
        w_sc[...] = wt_ref[...].astype(BF16)

    for r in range(0, h_ref.shape[0], PROJ_SUB_ROWS):
        rows = slice(r, r + PROJ_SUB_ROWS)
        p_ref[rows, :] = lax.dot_general(h_ref[rows, :], w_sc[...], (((1,), (1,)), ((), ())),
                                         preferred_element_type=F32)


def _proj_row_offset(j, tn):
    n_gate_tiles = 2 * D_MODEL // tn
    n_mlstm_tiles = MIX_GATE_ROW // tn
    off8 = jnp.where(j < n_gate_tiles, MIX_MERGE_ROW // 8 + j * (tn // 8),
                     jnp.where(j < n_gate_tiles + n_mlstm_tiles, (j - n_gate_tiles) * (tn // 8),
                               MIX_HGRN_ROW // 8 + (j - n_gate_tiles - n_mlstm_tiles) * (tn // 8)))
    return off8 * 8


def _in_proj(hn, w_t, *, tm=2048, tn=1024):
    n, d = hn.shape
    return pl.pallas_call(
        _in_proj_kernel,
        grid=(PROJ_W // tn, n // tm),
        in_specs=[
            pl.BlockSpec((tm, d), lambda j, i: (i, 0)),
            pl.BlockSpec((pl.Element(tn), pl.Element(d)), lambda j, i: (_proj_row_offset(j, tn), 0)),
        ],
        out_specs=pl.BlockSpec((tm, tn), lambda j, i: (i, j)),
        out_shape=jax.ShapeDtypeStruct((n, PROJ_W), F32),
        scratch_shapes=[pltpu.VMEM((tn, d), BF16)],
        compiler_params=_params("arbitrary", "arbitrary"),
        name="in_proj",
    )(hn, w_t)


def _mlstm_block(q_ref, k_ref, v_ref, o_ref, gcol_ref, grow_ref, cw_ref, cb_ref, bcol_ref, brow_ref,
                 hn_ref, y_ref, xpad_sc, c_sc, n_sc, m_sc, before_head):
    L = MIX_TB
    H = MLSTM_HEADS

    xpad_sc[8:8 + L, 0:MLSTM_QK_W] = q_ref[...]
    xpad_sc[8:8 + L, MLSTM_QK_W:2 * MLSTM_QK_W] = k_ref[...]
    acc = jnp.broadcast_to(cb_ref[...], (L, 2 * MLSTM_QK_W))
    for j in range(CONV_WIDTH):
        off = 8 - (CONV_WIDTH - 1) + j
        acc = acc + cw_ref[j:j + 1, :] * xpad_sc[off:off + L, :]
    xpad_sc[0:8, :] = xpad_sc[L:L + 8, :]
    qk = acc * _sigmoid(acc)

    gcol = gcol_ref[...] + bcol_ref[...]
    grow = grow_ref[...] + brow_ref[...]
    fcol = _log_sigmoid(gcol)
    frow = _log_sigmoid(grow)
    r_i = lax.broadcasted_iota(jnp.int32, (L, L), 0)
    c_i = lax.broadcasted_iota(jnp.int32, (L, L), 1)
    causal = r_i >= c_i
    tril = jnp.where(causal, 1.0, 0.0).astype(F32)
    triu = jnp.where(r_i <= c_i, 1.0, 0.0).astype(F32)
    bcol = jnp.dot(tril, fcol, preferred_element_type=F32, precision=lax.Precision.HIGHEST)
    brow = jnp.dot(frow, triu, preferred_element_type=F32, precision=lax.Precision.HIGHEST)

    for h in range(H):
        before_head[h]()
        b_c = bcol[:, H + h:H + h + 1]
        i_c = gcol[:, h:h + 1]
        b_r = brow[H + h:H + h + 1, :]
        i_r = grow[h:h + 1, :]
        m_prev = m_sc[h:h + 1, 0:1]

        d_log = jnp.where(causal, b_c - (b_r - i_r), -jnp.inf)
        inter_log = b_c + m_prev
        m_t = jnp.maximum(jnp.max(d_log, axis=1, keepdims=True), inter_log)

        q_h = qk[:, h * MLSTM_QK:(h + 1) * MLSTM_QK]
        k_h = qk[:, MLSTM_QK_W + h * MLSTM_QK:MLSTM_QK_W + (h + 1) * MLSTM_QK] * (MLSTM_QK ** -0.5)
        v_h = v_ref[:, h * MLSTM_V:(h + 1) * MLSTM_V].astype(BF16)
        q_b = q_h.astype(BF16)

        s = lax.dot_general(q_b, k_h.astype(BF16), (((1,), (1,)), ((), ())), preferred_element_type=F32)
        s = s * jnp.exp(d_log - m_t)
        w_inter = jnp.exp(inter_log - m_t)
        c_h = c_sc[h]
        n_h = n_sc[h:h + 1, :]
        num = (jnp.dot(s.astype(BF16), v_h, preferred_element_type=F32)
               + w_inter * jnp.dot(q_b, c_h.astype(BF16), preferred_element_type=F32))
        den = jnp.sum(s, axis=1, keepdims=True) + w_inter * jnp.sum(q_h * n_h, axis=1, keepdims=True)
        hh = num / jnp.maximum(jnp.abs(den), jnp.exp(-m_t))

        hn = hh * lax.rsqrt(jnp.mean(hh * hh, axis=-1, keepdims=True) + EPS)
        hn = hn * hn_ref[:, h * MLSTM_V:(h + 1) * MLSTM_V]
        y = hn * _sigmoid(o_ref[:, h * MLSTM_V:(h + 1) * MLSTM_V])
        y_ref[:, h * MLSTM_V:(h + 1) * MLSTM_V] = y.astype(y_ref.dtype)

        b_last = b_c[L - 1:L, :]
        a_log = b_last - b_c + i_c
        m_new = jnp.maximum(b_last + m_prev, jnp.max(a_log, axis=0, keepdims=True))
        w_a = jnp.exp(a_log - m_new)
        decay = jnp.exp(b_last + m_prev - m_new)
        kw = k_h * w_a
        c_sc[h] = decay * c_h + lax.dot_general(kw.astype(BF16), v_h, (((0,), (0,)), ((), ())),
                                                preferred_element_type=F32)
        n_sc[h:h + 1, :] = decay * n_h + jnp.sum(kw, axis=0, keepdims=True)
        m_sc[h:h + 1, :] = jnp.broadcast_to(m_new, (1, LANES))


def _hgrn_block(q_ref, f_ref, v_ref, og_ref, lbl_ref, hn_ref, y_ref, g_sc, k_sc, q_sc, gr_sc, st_sc,
                before_head):
    TB = MIX_TB
    H = HGRN_HEADS
    DK = HGRN_DK
    W = H * DK
    NT = TB // 8

    lbl = lbl_ref[...]
    lmax = jnp.max(lbl, axis=0, keepdims=True)
    le = jnp.exp(lbl - lmax)
    lb = le[1:2, :] / jnp.sum(le, axis=0, keepdims=True)

    f = lb + (1.0 - lb) * _sigmoid(f_ref[...])
    k_sc[...] = 1.0 - f
    qraw = q_ref[...]
    q_sc[...] = qraw * _sigmoid(qraw) * (DK ** -0.5)
    r_i = lax.broadcasted_iota(jnp.int32, (TB, TB), 0)
    c_i = lax.broadcasted_iota(jnp.int32, (TB, TB), 1)
    tri = jnp.where(r_i >= c_i, 1.0, 0.0).astype(F32)
    g = jnp.dot(tri, jnp.log(f) * LOG2_E, preferred_element_type=F32, precision=lax.Precision.HIGHEST)
    g_sc[...] = g

    g3 = g.reshape(NT, 8, W)
    sub = lax.broadcasted_iota(jnp.int32, (NT, 8, W), 1)
    bit0 = (sub & 1) != 0
    bit1 = (sub & 2) != 0
    bit2 = (sub & 4) != 0
    last2 = jnp.where(bit0, g3, pltpu.roll(g3, 7, 1))
    last4 = jnp.where(bit1, last2, pltpu.roll(last2, 6, 1))
    last4_r = pltpu.roll(last4, 4, 1)
    last8 = jnp.where(bit2, last4, last4_r).reshape(TB, W)
    gr_sc[0] = jnp.where(bit0, pltpu.roll(g3, 1, 1), g3).reshape(TB, W)
    gr_sc[1] = jnp.where(bit1, pltpu.roll(last2, 2, 1), last2).reshape(TB, W)
    gr_sc[2] = jnp.where(bit2, last4_r, last4).reshape(TB, W)
    for li, lvl in enumerate(HGRN_COARSE_LEVELS):
        groups = []
        for base in range(0, TB, 2 * lvl):
            src = last8[base + lvl - 8:base + lvl, :]
            groups.extend([src] * (2 * lvl // 8))
        gr_sc[3 + li] = jnp.concatenate(groups, axis=0)

    x_i = r_i ^ c_i
    nt_dims = (((1,), (1,)), ((), ()))
    for h in range(H):
        before_head[h]()
        cols = slice(h * DK, (h + 1) * DK)
        q_h = q_sc[:, cols]
        k_h = k_sc[:, cols]
        g_h = g_sc[:, cols]
        v_h = v_ref[:, cols].astype(BF16)

        q_b = q_h.astype(BF16)
        k_b = k_h.astype(BF16)
        a = lax.dot_general(q_b, k_b, nt_dims, preferred_element_type=F32)
        for li in range(HGRN_NUM_LEVELS):
            e = jnp.exp2(_neg_abs(g_h - gr_sc[li, :, cols])).astype(BF16)
            p = lax.dot_general(q_b * e, k_b * e, nt_dims, preferred_element_type=F32)
            a = jnp.where(x_i >= (1 << li), p, a)
        a = jnp.where(r_i >= c_i, a, 0.0)
        o = jnp.dot(a.astype(BF16), v_h, preferred_element_type=F32)

        st = st_sc[h]
        g_last = g_h[TB - 1:TB, :]
        qd = (q_h * jnp.exp2(g_h)).astype(BF16)
        o = o + lax.dot_general(qd, st.astype(BF16), nt_dims, preferred_element_type=F32)
        kd = (k_h * jnp.exp2(g_last - g_h)).astype(BF16)
        st_sc[h] = jnp.exp2(g_last) * st + lax.dot_general(v_h, kd, (((0,), (0,)), ((), ())),
                                                           preferred_element_type=F32)

        on = o * lax.rsqrt(jnp.mean(o * o, axis=-1, keepdims=True) + EPS) * hn_ref[:, cols]
        og = og_ref[:, cols]
        y_ref[:, cols] = (on * (og * _sigmoid(og))).astype(y_ref.dtype)


def _mix_kernel(mq_ref, mk_ref, mv_ref, mo_ref, gcol_ref, grow_ref, cw_ref, cb_ref, bcol_ref, brow_ref,
                mhn_ref, hq_ref, hf_ref, hi_ref, hg_ref, lbl_ref, hhn_ref,
                x_ref, gm_ref, gh_ref, wm_ref, wh_ref, wo_ref,
                out_ref,
                ym_sc, yh_sc, merged_sc, xpad_sc, c_sc, n_sc, m_sc, g_sc, k_sc, q_sc, gr_sc, st_sc,
                *, blocks_per_seq):
    s = pl.program_id(0)
    cur = s % 2
    prv = 1 - cur

    @pl.when(s == 0)
    def _():
        ym_sc[...] = jnp.zeros_like(ym_sc)
        yh_sc[...] = jnp.zeros_like(yh_sc)

    @pl.when(s % blocks_per_seq == 0)
    def _():
        xpad_sc[0:8, :] = jnp.zeros((8, 2 * MLSTM_QK_W), F32)
        c_sc[...] = jnp.zeros_like(c_sc)
        n_sc[...] = jnp.zeros_like(n_sc)
        m_sc[...] = jnp.zeros_like(m_sc)
        st_sc[...] = jnp.zeros_like(st_sc)

    def gate_piece(c):
        def run():
            cols = slice(c * MERGE_GATE_COLS, (c + 1) * MERGE_GATE_COLS)
            pm = jnp.dot(ym_sc[prv], wm_ref[:, cols], preferred_element_type=F32)
            ph = jnp.dot(yh_sc[prv], wh_ref[:, cols], preferred_element_type=F32)
            merged = _sigmoid(gm_ref[:, cols]) * pm + _sigmoid(gh_ref[:, cols]) * ph
            merged_sc[:, cols] = merged.astype(BF16)
        return run

    def out_piece(c):
        def run():
            cols = slice(c * MERGE_OUT_COLS, (c + 1) * MERGE_OUT_COLS)
            out_ref[:, cols] = x_ref[:, cols] + jnp.dot(merged_sc[...], wo_ref[:, cols],
                                                        preferred_element_type=F32)
        return run

    pieces = ([gate_piece(c) for c in range(D_MODEL // MERGE_GATE_COLS)]
              + [out_piece(c) for c in range(D_MODEL // MERGE_OUT_COLS)])
    n_heads = MLSTM_HEADS + HGRN_HEADS
    n_slots = MIX_SUBBLOCKS * n_heads
    slot_of = [(i * n_slots) // len(pieces) for i in range(len(pieces))]
    hooks = [(pieces[slot_of.index(i)] if i in slot_of else (lambda: None)) for i in range(n_slots)]

    for sub in range(MIX_SUBBLOCKS):
        rows = pl.ds(sub * MIX_TB, MIX_TB)
        sub_hooks = hooks[sub * n_heads:(sub + 1) * n_heads]
        _mlstm_block(mq_ref.at[rows], mk_ref.at[rows], mv_ref.at[rows], mo_ref.at[rows], gcol_ref.at[rows],
                     grow_ref.at[:, rows], cw_ref, cb_ref, bcol_ref, brow_ref, mhn_ref,
                     ym_sc.at[cur, rows], xpad_sc, c_sc, n_sc, m_sc, sub_hooks[:MLSTM_HEADS])
        _hgrn_block(hq_ref.at[rows], hf_ref.at[rows], hi_ref.at[rows], hg_ref.at[rows], lbl_ref, hhn_ref,
                    yh_sc.at[cur, rows], g_sc, k_sc, q_sc, gr_sc, st_sc, sub_hooks[MLSTM_HEADS:])


def _const_spec(shape):
    return pl.BlockSpec(shape, lambda *_: (0,) * len(shape), pipeline_mode=pl.Buffered(1))


def _mix(x2d, proj, gates_col, gates_row, conv_w, conv_b, bias_col, bias_row, m_head_norm,
         lb_logits, h_head_norm, w_m, w_h, w_o, *, seq_len):
    n, d = x2d.shape
    tb = MIX_TB * MIX_SUBBLOCKS
    nblk = n // tb
    qw, vw, hw = MLSTM_QK_W, MLSTM_V_W, HGRN_W

    def cur(col_block):
        return lambda s: (jnp.minimum(s, nblk - 1), col_block)

    def prev(col_block):
        return lambda s: (jnp.maximum(s - 1, 0), col_block)

    return pl.pallas_call(
        functools.partial(_mix_kernel, blocks_per_seq=seq_len // tb),
        grid=(nblk + 1,),
        in_specs=[
            pl.BlockSpec((tb, qw), cur(COL_MQ // qw)),
            pl.BlockSpec((tb, qw), cur(COL_MK // qw)),
            pl.BlockSpec((tb, vw), cur(COL_MV // vw)),
            pl.BlockSpec((tb, vw), cur(COL_MO // vw)),
            pl.BlockSpec((tb, LANES), cur(0)),
            pl.BlockSpec((2 * MLSTM_HEADS, tb), lambda s: (0, jnp.minimum(s, nblk - 1))),
            _const_spec((CONV_WIDTH, 2 * qw)),
            _const_spec((1, 2 * qw)),
            _const_spec((1, LANES)),
            _const_spec((2 * MLSTM_HEADS, 1)),
            _const_spec((1, vw)),
            pl.BlockSpec((tb, hw), cur(COL_HQ // hw)),
            pl.BlockSpec((tb, hw), cur(COL_HF // hw)),
            pl.BlockSpec((tb, hw), cur(COL_HI // hw)),
            pl.BlockSpec((tb, hw), cur(COL_HG // hw)),
            _const_spec((2, hw)),
            _const_spec((1, hw)),
            pl.BlockSpec((tb, d), prev(0)),
            pl.BlockSpec((tb, d), prev(COL_GM // d)),
            pl.BlockSpec((tb, d), prev(COL_GH // d)),
            _const_spec(w_m.shape),
            _const_spec(w_h.shape),
            _const_spec(w_o.shape),
        ],
        out_specs=pl.BlockSpec((tb, d), prev(0)),
        out_shape=jax.ShapeDtypeStruct((n, d), F32),
        scratch_shapes=[
            pltpu.VMEM((2, tb, vw), BF16),
            pltpu.VMEM((2, tb, hw), BF16),
            pltpu.VMEM((tb, d), BF16),
            pltpu.VMEM((MIX_TB + 8, 2 * qw), F32),
            pltpu.VMEM((MLSTM_HEADS, MLSTM_QK, MLSTM_V), F32),
            pltpu.VMEM((8, MLSTM_QK), F32),
            pltpu.VMEM((8, LANES), F32),
            pltpu.VMEM((MIX_TB, hw), F32),
            pltpu.VMEM((MIX_TB, hw), F32),
            pltpu.VMEM((MIX_TB, hw), F32),
            pltpu.VMEM((HGRN_NUM_LEVELS, MIX_TB, hw), F32),
            pltpu.VMEM((HGRN_HEADS, HGRN_DV, HGRN_DK), F32),
        ],
        compiler_params=_params("arbitrary"),
        name="mix",
    )(proj, proj, proj, proj, gates_col, gates_row, conv_w, conv_b, bias_col, bias_row, m_head_norm,
      proj, proj, proj, proj, lb_logits, h_head_norm, x2d, proj, proj, w_m, w_h, w_o)


def _xattn_prep_kernel(m_ref, g_ref, wk_ref, wv_ref, wq_ref, wo_ref, qk_ref, vo_ref):
    wk = wk_ref[...].astype(BF16)
    wv = wv_ref[...].astype(BF16)
    wq = wq_ref[...].astype(BF16)
    wo = wo_ref[...].astype(BF16)
    for i in range(m_ref.shape[0]):
        m = _rms(m_ref[i], g_ref[...]).astype(BF16)
        k = jnp.dot(m, wk, preferred_element_type=F32).astype(BF16)
        v = jnp.dot(m, wv, preferred_element_type=F32).astype(BF16)
        qk = lax.dot_general(wq, k, (((1,), (1,)), ((), ())), preferred_element_type=F32)
        qk_ref[i] = (qk * (XATTN_HEAD_DIM ** -0.5)).astype(qk_ref.dtype)
        vo_ref[i] = jnp.dot(v, wo, preferred_element_type=F32).astype(vo_ref.dtype)


def _xattn_prep(mem, g, w_kv, w_q, w_o):
    b, m, d = mem.shape
    hd = XATTN_HEAD_DIM
    nh = XATTN_HEADS
    return pl.pallas_call(
        _xattn_prep_kernel,
        grid=(nh,),
        in_specs=[
            pl.BlockSpec((b, m, d), lambda h: (0, 0, 0)),
            pl.BlockSpec((1, d), lambda h: (0, 0)),
            pl.BlockSpec((d, hd), lambda h: (0, h)),
            pl.BlockSpec((d, hd), lambda h: (0, nh + h)),
            pl.BlockSpec((d, hd), lambda h: (0, h)),
            pl.BlockSpec((hd, d), lambda h: (h, 0)),
        ],
        out_specs=[
            pl.BlockSpec((b, d, m), lambda h: (0, 0, h)),
            pl.BlockSpec((b, m, d), lambda h: (0, h, 0)),
        ],
        out_shape=[
            jax.ShapeDtypeStruct((b, d, nh * m), BF16),
            jax.ShapeDtypeStruct((b, nh * m, d), BF16),
        ],
        compiler_params=_params("arbitrary"),
        name="xattn_prep",
    )(mem, g.reshape(1, d), w_kv, w_kv, w_q, w_o)


def _xattn_kernel(x_ref, g_ref, qk_ref, vo_ref, o_ref, *, mem_len):
    x = x_ref[0]
    h = _rms(x, g_ref[...]).astype(BF16)
    s = jnp.dot(h, qk_ref[0], preferred_element_type=F32)
    probs = []
    for hd in range(XATTN_HEADS):
        s_h = s[:, hd * mem_len:(hd + 1) * mem_len]
        e = jnp.exp(s_h - jnp.max(s_h, axis=-1, keepdims=True))
        probs.append((e / jnp.sum(e, axis=-1, keepdims=True)).astype(BF16))
    p = jnp.concatenate(probs, axis=-1)
    o_ref[0] = x + jnp.dot(p, vo_ref[0], preferred_element_type=F32)


def _xattn(x, g, qk, vo, *, tm=512):
    b, t, d = x.shape
    hm = qk.shape[2]
    return pl.pallas_call(
        functools.partial(_xattn_kernel, mem_len=hm // XATTN_HEADS),
        grid=(b, t // tm),
        in_specs=[
            pl.BlockSpec((1, tm, d), lambda i, c: (i, c, 0)),
            _const_spec((1, d)),
            pl.BlockSpec((1, d, hm), lambda i, c: (i, 0, 0)),
            pl.BlockSpec((1, hm, d), lambda i, c: (i, 0, 0)),
        ],
        out_specs=pl.BlockSpec((1, tm, d), lambda i, c: (i, c, 0)),
        out_shape=jax.ShapeDtypeStruct((b, t, d), F32),
        compiler_params=_params("parallel", "parallel"),
        name="xattn",
    )(x, g.reshape(1, d), qk, vo)


def kernel(x, mem, norm_ffn1, ffn1_w1, ffn1_w3, ffn1_w2, norm_mix, w_in, mlstm_conv_w, mlstm_conv_b,
           mlstm_ig_bias, mlstm_fg_bias, mlstm_head_norm, hgrn_lb_logits, hgrn_head_norm, w_proj_m,
           w_proj_h, w_out, norm_xattn, norm_mem, xattn_wq, xattn_wkv, xattn_wo, norm_ffn2, ffn2_w1,
           ffn2_w3, ffn2_w2, norm_final):
    b, t, d = x.shape
    depth = norm_ffn1.shape[0]
    assert depth == 1 and hgrn_lb_logits.shape[0] == 2
    n = b * t
    l = 0
    bf = lambda w: w.astype(BF16)

    w_in_t = jnp.swapaxes(w_in[l], 0, 1)
    n_gate = 2 * MLSTM_HEADS
    gate_bias = jnp.concatenate([mlstm_ig_bias[l], mlstm_fg_bias[l]]).astype(F32)
    bias_col = jnp.pad(gate_bias, (0, LANES - n_gate)).reshape(1, LANES)
    bias_row = gate_bias.reshape(n_gate, 1)

    x1, hn, gates_col, gates_row = _ffn(x.reshape(n, d), norm_ffn1[l], ffn1_w1[l], ffn1_w3[l], ffn1_w2[l],
                                        norm_mix[l], w_in_t)

    proj = _in_proj(hn, w_in_t)
    x2 = _mix(x1, proj, gates_col, gates_row, mlstm_conv_w[l], mlstm_conv_b[l].reshape(1, -1),
              bias_col, bias_row, mlstm_head_norm[l].reshape(1, -1), hgrn_lb_logits,
              hgrn_head_norm[l].reshape(1, -1), bf(w_proj_m[l]), bf(w_proj_h[l]), bf(w_out[l]), seq_len=t)

    qk, vo = _xattn_prep(mem, norm_mem[l], xattn_wkv[l], xattn_wq[l], xattn_wo[l])
    x3 = _xattn(x2.reshape(b, t, d), norm_xattn[l], qk, vo)

    (out,) = _ffn(x3.reshape(n, d), norm_ffn2[l], ffn2_w1[l], ffn2_w3[l], ffn2_w2[l], norm_final)
    return out.reshape(b, t, d)
```

```python
import functools

import jax
import jax.numpy as jnp
from jax import lax
from jax.experimental import pallas as pl
from jax.experimental.pallas import tpu as pltpu

F32 = jnp.float32
BF16 = jnp.bfloat16
EPS = 1e-6
LOG2_E = 1.4426950408889634

D_MODEL = 2048
D_FF = 5632
MLSTM_HEADS = 4
MLSTM_QK = 128
MLSTM_V = 256
CONV_WIDTH = 4
HGRN_HEADS = 8
HGRN_DK = 128
HGRN_DV = 128
XATTN_HEADS = 4
XATTN_HEAD_DIM = D_MODEL // XATTN_HEADS

MLSTM_QK_W = MLSTM_HEADS * MLSTM_QK
MLSTM_V_W = MLSTM_HEADS * MLSTM_V
HGRN_W = HGRN_HEADS * HGRN_DK

VMEM_LIMIT_BYTES = 60000 * 1024
LANES = 128

COL_GM = 0
COL_GH = COL_GM + D_MODEL
COL_MQ = COL_GH + D_MODEL
COL_MK = COL_MQ + MLSTM_QK_W
COL_MV = COL_MK + MLSTM_QK_W
COL_MO = COL_MV + MLSTM_V_W
COL_HQ = COL_MO + MLSTM_V_W
COL_HF = COL_HQ + HGRN_W
COL_HI = COL_HF + HGRN_W
COL_HG = COL_HI + HGRN_W
PROJ_W = COL_HG + HGRN_W

MIX_GATE_ROW = 2 * MLSTM_QK_W + 2 * MLSTM_V_W
MIX_HGRN_ROW = MIX_GATE_ROW + 2 * MLSTM_HEADS
MIX_MERGE_ROW = MIX_HGRN_ROW + 4 * HGRN_W

PROJ_SUB_ROWS = 1024
FFN_NORM_ROWS = 128
FFN_FINISH_ROWS = 128
FFN_SUB_COLS = 256

MIX_TB = 128
HGRN_NUM_LEVELS = MIX_TB.bit_length() - 1
MIX_SUBBLOCKS = 2
MERGE_GATE_COLS = 512
MERGE_OUT_COLS = 256
HGRN_COARSE_LEVELS = tuple(1 << i for i in range(3, HGRN_NUM_LEVELS))


def _rms(x, g):
    return x * lax.rsqrt(jnp.mean(x * x, axis=-1, keepdims=True) + EPS) * g


def _sigmoid(x):
    return 1.0 / (1.0 + jnp.exp(-x))


def _log_sigmoid(x):
    return jnp.minimum(x, 0.0) - jnp.log(1.0 + jnp.exp(-jnp.abs(x)))


def _neg_abs(x):
    return pltpu.bitcast(pltpu.bitcast(x, jnp.uint32) | jnp.uint32(0x80000000), F32)


def _params(*sem):
    return pltpu.CompilerParams(dimension_semantics=sem, vmem_limit_bytes=VMEM_LIMIT_BYTES)


def _ffn_kernel(*refs, mode):
    if mode == "mix":
        (x_hbm, g_ref, w1_ref, w3_ref, w2_ref, gn_ref, wg_ref,
         o_hbm, hn_ref, gcol_ref, grow_ref, h_sc, x_buf, acc_sc, x_sem, o_sem) = refs
    else:
        x_hbm, g_ref, w1_ref, w3_ref, w2_ref, gn_ref, o_hbm, h_sc, x_buf, acc_sc, x_sem, o_sem = refs
    i = pl.program_id(0)
    j = pl.program_id(1)
    n_i = pl.num_programs(0)
    tm = x_buf.shape[0]
    n_acc = acc_sc.shape[0]

    def acc_slot(tile):
        return acc_sc.at[tile % n_acc if n_acc > 1 else 0]

    def tile_rows(tile):
        return pl.ds(pl.multiple_of(tile * tm, tm), tm)

    def x_copy(tile):
        return pltpu.make_async_copy(x_hbm.at[tile_rows(tile), :], x_buf, x_sem)

    def o_copy(tile):
        return pltpu.make_async_copy(acc_slot(tile), o_hbm.at[tile_rows(tile), :], o_sem)

    def row_chunk(r, size):
        return pl.ds(pl.multiple_of(r * size, size), size)

    acc_ref = acc_slot(i)

    def swiglu_passes(first):
        h = h_sc[...]
        for c in range(w1_ref.shape[1] // FFN_SUB_COLS):
            cols = slice(c * FFN_SUB_COLS, (c + 1) * FFN_SUB_COLS)
            a = jnp.dot(h, w1_ref[:, cols].astype(BF16), preferred_element_type=F32)
            b = jnp.dot(h, w3_ref[:, cols].astype(BF16), preferred_element_type=F32)
            act = (a * _sigmoid(a) * b).astype(BF16)
            part = jnp.dot(act, w2_ref[cols, :].astype(BF16), preferred_element_type=F32)
            if first and c == 0:
                acc_ref[...] = 2.0 * x_buf[...] + part
            else:
                acc_ref[...] += part

    @pl.when(j == 0)
    def _():
        @pl.when(i == 0)
        def _():
            x_copy(0).start()

        x_copy(i).wait()

        def norm_rows(r, carry):
            rows = row_chunk(r, FFN_NORM_ROWS)
            h_sc[rows, :] = _rms(x_buf[rows, :], g_ref[...]).astype(BF16)
            return carry
        lax.fori_loop(0, tm // FFN_NORM_ROWS, norm_rows, 0)

        if n_acc == 1:
            @pl.when(i > 0)
            def _():
                o_copy(i - 1).wait()

        swiglu_passes(first=True)

    @pl.when(j > 0)
    def _():
        swiglu_passes(first=False)

    @pl.when(jnp.logical_and(j == 1, i + 1 < n_i))
    def _():
        x_copy(i + 1).start()

    @pl.when(j == pl.num_programs(1) - 1)
    def _():
        if mode == "final":
            def finish_rows(r, carry):
                rows = row_chunk(r, FFN_FINISH_ROWS)
                acc_ref[rows, :] = _rms(0.5 * acc_ref[rows, :], gn_ref[...])
                return carry
            lax.fori_loop(0, tm // FFN_FINISH_ROWS, finish_rows, 0)
        else:
            y = 0.5 * acc_ref[...]
            acc_ref[...] = y
            hn = _rms(y, gn_ref[...]).astype(BF16)
            hn_ref[...] = hn
            nt_dims = (((1,), (1,)), ((), ()))
            wg = wg_ref[...].astype(BF16)
            wg_pad = jnp.concatenate([wg, jnp.zeros((LANES - wg.shape[0], wg.shape[1]), BF16)], axis=0)
            gcol = lax.dot_general(hn, wg_pad, nt_dims, preferred_element_type=F32)
            gcol_ref[...] = gcol
            grow_ref[...] = gcol.T[0:wg.shape[0], :]

        if n_acc > 1:
            @pl.when(i > 0)
            def _():
                o_copy(i - 1).wait()

        o_copy(i).start()

        @pl.when(i == n_i - 1)
        def _():
            o_copy(i).wait()


def _ffn(x2d, g, w1, w3, w2, g_next, w_gates_t=None, *, tm=1024, tf=512):
    n, d = x2d.shape
    f = w1.shape[1]
    mode = "final" if w_gates_t is None else "mix"
    in_specs = [
        pl.BlockSpec(memory_space=pl.ANY),
        pl.BlockSpec((1, d), lambda i, j: (0, 0)),
        pl.BlockSpec((d, tf), lambda i, j: (0, j)),
        pl.BlockSpec((d, tf), lambda i, j: (0, j)),
        pl.BlockSpec((tf, d), lambda i, j: (j, 0)),
        pl.BlockSpec((1, d), lambda i, j: (0, 0)),
    ]
    args = [x2d, g.reshape(1, d), w1, w3, w2, g_next.reshape(1, d)]
    single = dict(pipeline_mode=pl.Buffered(1))
    out_specs = [pl.BlockSpec(memory_space=pl.ANY)]
    out_shape = [jax.ShapeDtypeStruct((n, d), F32)]
    if mode == "mix":
        ng = 2 * MLSTM_HEADS
        in_specs.append(pl.BlockSpec((ng, d), lambda i, j: (MIX_GATE_ROW // ng, 0)))
        args.append(w_gates_t)
        out_specs += [
            pl.BlockSpec((tm, d), lambda i, j: (i, 0), **single),
            pl.BlockSpec((tm, LANES), lambda i, j: (i, 0)),
            pl.BlockSpec((ng, tm), lambda i, j: (0, i)),
        ]
        out_shape += [
            jax.ShapeDtypeStruct((n, d), BF16),
            jax.ShapeDtypeStruct((n, LANES), F32),
            jax.ShapeDtypeStruct((ng, n), F32),
        ]
    return pl.pallas_call(
        functools.partial(_ffn_kernel, mode=mode),
        grid=(n // tm, f // tf),
        in_specs=in_specs,
        out_specs=out_specs,
        out_shape=out_shape,
        scratch_shapes=[
            pltpu.VMEM((tm, d), BF16),
            pltpu.VMEM((tm, d), F32),
            pltpu.VMEM((2 if mode == "final" else 1, tm, d), F32),
            pltpu.SemaphoreType.DMA(()),
            pltpu.SemaphoreType.DMA(()),
        ],
        compiler_params=_params("arbitrary", "arbitrary"),
        name="ffn_" + mode,
    )(*args)


def _in_proj_kernel(h_ref, wt_ref, p_ref, w_sc):
    @pl.when(pl.program_id(1) == 0)
    def _():
        w_sc[...] = wt_ref[...].astype(BF16)

    for r in range(0, h_ref.shape[0], PROJ_SUB_ROWS):
        rows = slice(r, r + PROJ_SUB_ROWS)
        p_ref[rows, :] = lax.dot_general(h_ref[rows, :], w_sc[...], (((1,), (1,)), ((), ())),
                                         preferred_element_type=F32)


def _proj_row_offset(j, tn):
    n_gate_tiles = 2 * D_MODEL // tn
    n_mlstm_tiles = MIX_GATE_ROW // tn
    off8 = jnp.where(j < n_gate_tiles, MIX_MERGE_ROW // 8 + j * (tn // 8),
                     jnp.where(j < n_gate_tiles + n_mlstm_tiles, (j - n_gate_tiles) * (tn // 8),
                               MIX_HGRN_ROW // 8 + (j - n_gate_tiles - n_mlstm_tiles) * (tn // 8)))
    return off8 * 8


def _in_proj(hn, w_t, *, tm=2048, tn=1024):
    n, d = hn.shape
    return pl.pallas_call(
        _in_proj_kernel,
        grid=(PROJ_W // tn, n // tm),
        in_specs=[
            pl.BlockSpec((tm, d), lambda j, i: (i, 0)),
            pl.BlockSpec((pl.Element(tn), pl.Element(d)), lambda j, i: (_proj_row_offset(j, tn), 0)),
        ],
        out_specs=pl.BlockSpec((tm, tn), lambda j, i: (i, j)),
        out_shape=jax.ShapeDtypeStruct((n, PROJ_W), F32),
        scratch_shapes=[pltpu.VMEM((tn, d), BF16)],
        compiler_params=_params("arbitrary", "arbitrary"),
        name="in_proj",
    )(hn, w_t)


def _mlstm_block(q_ref, k_ref, v_ref, o_ref, gcol_ref, grow_ref, cw_ref, cb_ref, bcol_ref, brow_ref,
                 hn_ref, y_ref, xpad_sc, c_sc, n_sc, m_sc, before_head):
    L = MIX_TB
    H = MLSTM_HEADS

    xpad_sc[8:8 + L, 0:MLSTM_QK_W] = q_ref[...]
    xpad_sc[8:8 + L, MLSTM_QK_W:2 * MLSTM_QK_W] = k_ref[...]
    acc = jnp.broadcast_to(cb_ref[...], (L, 2 * MLSTM_QK_W))
    for j in range(CONV_WIDTH):
        off = 8 - (CONV_WIDTH - 1) + j
        acc = acc + cw_ref[j:j + 1, :] * xpad_sc[off:off + L, :]
    xpad_sc[0:8, :] = xpad_sc[L:L + 8, :]
    qk = acc * _sigmoid(acc)

    gcol = gcol_ref[...] + bcol_ref[...]
    grow = grow_ref[...] + brow_ref[...]
    fcol = _log_sigmoid(gcol)
    frow = _log_sigmoid(grow)
    r_i = lax.broadcasted_iota(jnp.int32, (L, L), 0)
    c_i = lax.broadcasted_iota(jnp.int32, (L, L), 1)
    causal = r_i >= c_i
    tril = jnp.where(causal, 1.0, 0.0).astype(F32)
    triu = jnp.where(r_i <= c_i, 1.0, 0.0).astype(F32)
    bcol = jnp.dot(tril, fcol, preferred_element_type=F32, precision=lax.Precision.HIGHEST)
    brow = jnp.dot(frow, triu, preferred_element_type=F32, precision=lax.Precision.HIGHEST)

    for h in range(H):
        before_head[h]()
        b_c = bcol[:, H + h:H + h + 1]
        i_c = gcol[:, h:h + 1]
        b_r = brow[H + h:H + h + 1, :]
        i_r = grow[h:h + 1, :]
        m_prev = m_sc[h:h + 1, 0:1]

        d_log = jnp.where(causal, b_c - (b_r - i_r), -jnp.inf)
        inter_log = b_c + m_prev
        m_t = jnp.maximum(jnp.max(d_log, axis=1, keepdims=True), inter_log)

        q_h = qk[:, h * MLSTM_QK:(h + 1) * MLSTM_QK]
        k_h = qk[:, MLSTM_QK_W + h * MLSTM_QK:MLSTM_QK_W + (h + 1) * MLSTM_QK] * (MLSTM_QK ** -0.5)
        v_h = v_ref[:, h * MLSTM_V:(h + 1) * MLSTM_V].astype(BF16)
        q_b = q_h.astype(BF16)

        s = lax.dot_general(q_b, k_h.astype(BF16), (((1,), (1,)), ((), ())), preferred_element_type=F32)
        s = s * jnp.exp(d_log - m_t)
        w_inter = jnp.exp(inter_log - m_t)
        c_h = c_sc[h]
        n_h = n_sc[h:h + 1, :]
        num = (jnp.dot(s.astype(BF16), v_h, preferred_element_type=F32)
               + w_inter * jnp.dot(q_b, c_h.astype(BF16), preferred_element_type=F32))
        den = jnp.sum(s, axis=1, keepdims=True) + w_inter * jnp.sum(q_h * n_h, axis=1, keepdims=True)
        hh = num / jnp.maximum(jnp.abs(den), jnp.exp(-m_t))

        hn = hh * lax.rsqrt(jnp.mean(hh * hh, axis=-1, keepdims=True) + EPS)
        hn = hn * hn_ref[:, h * MLSTM_V:(h + 1) * MLSTM_V]
        y = hn * _sigmoid(o_ref[:, h * MLSTM_V:(h + 1) * MLSTM_V])
        y_ref[:, h * MLSTM_V:(h + 1) * MLSTM_V] = y.astype(y_ref.dtype)

        b_last = b_c[L - 1:L, :]
        a_log = b_last - b_c + i_c
        m_new = jnp.maximum(b_last + m_prev, jnp.max(a_log, axis=0, keepdims=True))
        w_a = jnp.exp(a_log - m_new)
        decay = jnp.exp(b_last + m_prev - m_new)
        kw = k_h * w_a
        c_sc[h] = decay * c_h + lax.dot_general(kw.astype(BF16), v_h, (((0,), (0,)), ((), ())),
                                                preferred_element_type=F32)
        n_sc[h:h + 1, :] = decay * n_h + jnp.sum(kw, axis=0, keepdims=True)
        m_sc[h:h + 1, :] = jnp.broadcast_to(m_new, (1, LANES))


def _hgrn_block(q_ref, f_ref, v_ref, og_ref, lbl_ref, hn_ref, y_ref, g_sc, k_sc, q_sc, gr_sc, st_sc,
                before_head):
    TB = MIX_TB
    H = HGRN_HEADS
    DK = HGRN_DK
    W = H * DK
    NT = TB // 8

    lbl = lbl_ref[...]
    lmax = jnp.max(lbl, axis=0, keepdims=True)
    le = jnp.exp(lbl - lmax)
    lb = le[1:2, :] / jnp.sum(le, axis=0, keepdims=True)

    f = lb + (1.0 - lb) * _sigmoid(f_ref[...])
    k_sc[...] = 1.0 - f
    qraw = q_ref[...]
    q_sc[...] = qraw * _sigmoid(qraw) * (DK ** -0.5)
    r_i = lax.broadcasted_iota(jnp.int32, (TB, TB), 0)
    c_i = lax.broadcasted_iota(jnp.int32, (TB, TB), 1)
    tri = jnp.where(r_i >= c_i, 1.0, 0.0).astype(F32)
    g = jnp.dot(tri, jnp.log(f) * LOG2_E, preferred_element_type=F32, precision=lax.Precision.HIGHEST)
    g_sc[...] = g

    g3 = g.reshape(NT, 8, W)
    sub = lax.broadcasted_iota(jnp.int32, (NT, 8, W), 1)
    bit0 = (sub & 1) != 0
    bit1 = (sub & 2) != 0
    bit2 = (sub & 4) != 0
    last2 = jnp.where(bit0, g3, pltpu.roll(g3, 7, 1))
    last4 = jnp.where(bit1, last2, pltpu.roll(last2, 6, 1))
    last4_r = pltpu.roll(last4, 4, 1)
    last8 = jnp.where(bit2, last4, last4_r).reshape(TB, W)
    gr_sc[0] = jnp.where(bit0, pltpu.roll(g3, 1, 1), g3).reshape(TB, W)
    gr_sc[1] = jnp.where(bit1, pltpu.roll(last2, 2, 1), last2).reshape(TB, W)
    gr_sc[2] = jnp.where(bit2, last4_r, last4).reshape(TB, W)
    for li, lvl in enumerate(HGRN_COARSE_LEVELS):
        groups = []
        for base in range(0, TB, 2 * lvl):
            src = last8[base + lvl - 8:base + lvl, :]
            groups.extend([src] * (2 * lvl // 8))
        gr_sc[3 + li] = jnp.concatenate(groups, axis=0)

    x_i = r_i ^ c_i
    nt_dims = (((1,), (1,)), ((), ()))
    for h in range(H):
        before_head[h]()
        cols = slice(h * DK, (h + 1) * DK)
        q_h = q_sc[:, cols]
        k_h = k_sc[:, cols]
        g_h = g_sc[:, cols]
        v_h = v_ref[:, cols].astype(BF16)

        q_b = q_h.astype(BF16)
        k_b = k_h.astype(BF16)
        a = lax.dot_general(q_b, k_b, nt_dims, preferred_element_type=F32)
        for li in range(HGRN_NUM_LEVELS):
            e = jnp.exp2(_neg_abs(g_h - gr_sc[li, :, cols])).astype(BF16)
            p = lax.dot_general(q_b * e, k_b * e, nt_dims, preferred_element_type=F32)
            a = jnp.where(x_i >= (1 << li), p, a)
        a = jnp.where(r_i >= c_i, a, 0.0)
        o = jnp.dot(a.astype(BF16), v_h, preferred_element_type=F32)

        st = st_sc[h]
        g_last = g_h[TB - 1:TB, :]
        qd = (q_h * jnp.exp2(g_h)).astype(BF16)
        o = o + lax.dot_general(qd, st.astype(BF16), nt_dims, preferred_element_type=F32)
        kd = (k_h * jnp.exp2(g_last - g_h)).astype(BF16)
        st_sc[h] = jnp.exp2(g_last) * st + lax.dot_general(v_h, kd, (((0,), (0,)), ((), ())),
                                                           preferred_element_type=F32)

        on = o * lax.rsqrt(jnp.mean(o * o, axis=-1, keepdims=True) + EPS) * hn_ref[:, cols]
        og = og_ref[:, cols]
        y_ref[:, cols] = (on * (og * _sigmoid(og))).astype(y_ref.dtype)


def _mix_kernel(mq_ref, mk_ref, mv_ref, mo_ref, gcol_ref, grow_ref, cw_ref, cb_ref, bcol_ref, brow_ref,
                mhn_ref, hq_ref, hf_ref, hi_ref, hg_ref, lbl_ref, hhn_ref,
                x_ref, gm_ref, gh_ref, wm_ref, wh_ref, wo_ref,
                out_ref,
                ym_sc, yh_sc, merged_sc, xpad_sc, c_sc, n_sc, m_sc, g_sc, k_sc, q_sc, gr_sc, st_sc,
                *, blocks_per_seq):
    s = pl.program_id(0)
    cur = s % 2
    prv = 1 - cur

    @pl.when(s == 0)
    def _():
        ym_sc[...] = jnp.zeros_like(ym_sc)
        yh_sc[...] = jnp.zeros_like(yh_sc)

    @pl.when(s % blocks_per_seq == 0)
    def _():
        xpad_sc[0:8, :] = jnp.zeros((8, 2 * MLSTM_QK_W), F32)
        c_sc[...] = jnp.zeros_like(c_sc)
        n_sc[...] = jnp.zeros_like(n_sc)
        m_sc[...] = jnp.zeros_like(m_sc)
        st_sc[...] = jnp.zeros_like(st_sc)

    def gate_piece(c):
        def run():
            cols = slice(c * MERGE_GATE_COLS, (c + 1) * MERGE_GATE_COLS)
            pm = jnp.dot(ym_sc[prv], wm_ref[:, cols], preferred_element_type=F32)
            ph = jnp.dot(yh_sc[prv], wh_ref[:, cols], preferred_element_type=F32)
            merged = _sigmoid(gm_ref[:, cols]) * pm + _sigmoid(gh_ref[:, cols]) * ph
            merged_sc[:, cols] = merged.astype(BF16)
        return run

    def out_piece(c):
        def run():
            cols = slice(c * MERGE_OUT_COLS, (c + 1) * MERGE_OUT_COLS)
            out_ref[:, cols] = x_ref[:, cols] + jnp.dot(merged_sc[...], wo_ref[:, cols],
                                                        preferred_element_type=F32)
        return run

    pieces = ([gate_piece(c) for c in range(D_MODEL // MERGE_GATE_COLS)]
              + [out_piece(c) for c in range(D_MODEL // MERGE_OUT_COLS)])
    n_heads = MLSTM_HEADS + HGRN_HEADS
    n_slots = MIX_SUBBLOCKS * n_heads
    slot_of = [(i * n_slots) // len(pieces) for i in range(len(pieces))]
    hooks = [(pieces[slot_of.index(i)] if i in slot_of else (lambda: None)) for i in range(n_slots)]

    for sub in range(MIX_SUBBLOCKS):
        rows = pl.ds(sub * MIX_TB, MIX_TB)
        sub_hooks = hooks[sub * n_heads:(sub + 1) * n_heads]
        _mlstm_block(mq_ref.at[rows], mk_ref.at[rows], mv_ref.at[rows], mo_ref.at[rows], gcol_ref.at[rows],
                     grow_ref.at[:, rows], cw_ref, cb_ref, bcol_ref, brow_ref, mhn_ref,
                     ym_sc.at[cur, rows], xpad_sc, c_sc, n_sc, m_sc, sub_hooks[:MLSTM_HEADS])
        _hgrn_block(hq_ref.at[rows], hf_ref.at[rows], hi_ref.at[rows], hg_ref.at[rows], lbl_ref, hhn_ref,
                    yh_sc.at[cur, rows], g_sc, k_sc, q_sc, gr_sc, st_sc, sub_hooks[MLSTM_HEADS:])


def _const_spec(shape):
    return pl.BlockSpec(shape, lambda *_: (0,) * len(shape), pipeline_mode=pl.Buffered(1))


def _mix(x2d, proj, gates_col, gates_row, conv_w, conv_b, bias_col, bias_row, m_head_norm,
         lb_logits, h_head_norm, w_m, w_h, w_o, *, seq_len):
    n, d = x2d.shape
    tb = MIX_TB * MIX_SUBBLOCKS
    nblk = n // tb
    qw, vw, hw = MLSTM_QK_W, MLSTM_V_W, HGRN_W

    def cur(col_block):
        return lambda s: (jnp.minimum(s, nblk - 1), col_block)

    def prev(col_block):
        return lambda s: (jnp.maximum(s - 1, 0), col_block)

    return pl.pallas_call(
        functools.partial(_mix_kernel, blocks_per_seq=seq_len // tb),
        grid=(nblk + 1,),
        in_specs=[
            pl.BlockSpec((tb, qw), cur(COL_MQ // qw)),
            pl.BlockSpec((tb, qw), cur(COL_MK // qw)),
            pl.BlockSpec((tb, vw), cur(COL_MV // vw)),
            pl.BlockSpec((tb, vw), cur(COL_MO // vw)),
            pl.BlockSpec((tb, LANES), cur(0)),
            pl.BlockSpec((2 * MLSTM_HEADS, tb), lambda s: (0, jnp.minimum(s, nblk - 1))),
            _const_spec((CONV_WIDTH, 2 * qw)),
            _const_spec((1, 2 * qw)),
            _const_spec((1, LANES)),
            _const_spec((2 * MLSTM_HEADS, 1)),
            _const_spec((1, vw)),
            pl.BlockSpec((tb, hw), cur(COL_HQ // hw)),
            pl.BlockSpec((tb, hw), cur(COL_HF // hw)),
            pl.BlockSpec((tb, hw), cur(COL_HI // hw)),
            pl.BlockSpec((tb, hw), cur(COL_HG // hw)),
            _const_spec((2, hw)),
            _const_spec((1, hw)),
            pl.BlockSpec((tb, d), prev(0)),
            pl.BlockSpec((tb, d), prev(COL_GM // d)),
            pl.BlockSpec((tb, d), prev(COL_GH // d)),
            _const_spec(w_m.shape),
            _const_spec(w_h.shape),
            _const_spec(w_o.shape),
        ],
        out_specs=pl.BlockSpec((tb, d), prev(0)),
        out_shape=jax.ShapeDtypeStruct((n, d), F32),
        scratch_shapes=[
            pltpu.VMEM((2, tb, vw), BF16),
            pltpu.VMEM((2, tb, hw), BF16),
            pltpu.VMEM((tb, d), BF16),
            pltpu.VMEM((MIX_TB + 8, 2 * qw), F32),
            pltpu.VMEM((MLSTM_HEADS, MLSTM_QK, MLSTM_V), F32),
            pltpu.VMEM((8, MLSTM_QK), F32),
            pltpu.VMEM((8, LANES), F32),
            pltpu.VMEM((MIX_TB, hw), F32),
            pltpu.VMEM((MIX_TB, hw), F32),
            pltpu.VMEM((MIX_TB, hw), F32),
            pltpu.VMEM((HGRN_NUM_LEVELS, MIX_TB, hw), F32),
            pltpu.VMEM((HGRN_HEADS, HGRN_DV, HGRN_DK), F32),
        ],
        compiler_params=_params("arbitrary"),
        name="mix",
    )(proj, proj, proj, proj, gates_col, gates_row, conv_w, conv_b, bias_col, bias_row, m_head_norm,
      proj, proj, proj, proj, lb_logits, h_head_norm, x2d, proj, proj, w_m, w_h, w_o)


def _xattn_prep_kernel(m_ref, g_ref, wk_ref, wv_ref, wq_ref, wo_ref, qk_ref, vo_ref):
    wk = wk_ref[...].astype(BF16)
    wv = wv_ref[...].astype(BF16)
    wq = wq_ref[...].astype(BF16)
    wo = wo_ref[...].astype(BF16)
    for i in range(m_ref.shape[0]):
        m = _rms(m_ref[i], g_ref[...]).astype(BF16)
        k = jnp.dot(m, wk, preferred_element_type=F32).astype(BF16)
        v = jnp.dot(m, wv, preferred_element_type=F32).astype(BF16)
        qk = lax.dot_general(wq, k, (((1,), (1,)), ((), ())), preferred_element_type=F32)
        qk_ref[i] = (qk * (XATTN_HEAD_DIM ** -0.5)).astype(qk_ref.dtype)
        vo_ref[i] = jnp.dot(v, wo, preferred_element_type=F32).astype(vo_ref.dtype)


def _xattn_prep(mem, g, w_kv, w_q, w_o):
    b, m, d = mem.shape
    hd = XATTN_HEAD_DIM
    nh = XATTN_HEADS
    return pl.pallas_call(
        _xattn_prep_kernel,
        grid=(nh,),
        in_specs=[
            pl.BlockSpec((b, m, d), lambda h: (0, 0, 0)),
            pl.BlockSpec((1, d), lambda h: (0, 0)),
            pl.BlockSpec((d, hd), lambda h: (0, h)),
            pl.BlockSpec((d, hd), lambda h: (0, nh + h)),
            pl.BlockSpec((d, hd), lambda h: (0, h)),
            pl.BlockSpec((hd, d), lambda h: (h, 0)),
        ],
        out_specs=[
            pl.BlockSpec((b, d, m), lambda h: (0, 0, h)),
            pl.BlockSpec((b, m, d), lambda h: (0, h, 0)),
        ],
        out_shape=[
            jax.ShapeDtypeStruct((b, d, nh * m), BF16),
            jax.ShapeDtypeStruct((b, nh * m, d), BF16),
        ],
        compiler_params=_params("arbitrary"),
        name="xattn_prep",
    )(mem, g.reshape(1, d), w_kv, w_kv, w_q, w_o)


def _xattn_kernel(x_ref, g_ref, qk_ref, vo_ref, o_ref, *, mem_len):
    x = x_ref[0]
    h = _rms(x, g_ref[...]).astype(BF16)
    s = jnp.dot(h, qk_ref[0], preferred_element_type=F32)
    probs = []
    for hd in range(XATTN_HEADS):
        s_h = s[:, hd * mem_len:(hd + 1) * mem_len]
        e = jnp.exp(s_h - jnp.max(s_h, axis=-1, keepdims=True))
        probs.append((e / jnp.sum(e, axis=-1, keepdims=True)).astype(BF16))
    p = jnp.concatenate(probs, axis=-1)
    o_ref[0] = x + jnp.dot(p, vo_ref[0], preferred_element_type=F32)


def _xattn(x, g, qk, vo, *, tm=512):
    b, t, d = x.shape
    hm = qk.shape[2]
    return pl.pallas_call(
        functools.partial(_xattn_kernel, mem_len=hm // XATTN_HEADS),
        grid=(b, t // tm),
        in_specs=[
            pl.BlockSpec((1, tm, d), lambda i, c: (i, c, 0)),
            _const_spec((1, d)),
            pl.BlockSpec((1, d, hm), lambda i, c: (i, 0, 0)),
            pl.BlockSpec((1, hm, d), lambda i, c: (i, 0, 0)),
        ],
        out_specs=pl.BlockSpec((1, tm, d), lambda i, c: (i, c, 0)),
        out_shape=jax.ShapeDtypeStruct((b, t, d), F32),
        compiler_params=_params("parallel", "parallel"),
        name="xattn",
    )(x, g.reshape(1, d), qk, vo)


def kernel(x, mem, norm_ffn1, ffn1_w1, ffn1_w3, ffn1_w2, norm_mix, w_in, mlstm_conv_w, mlstm_conv_b,
           mlstm_ig_bias, mlstm_fg_bias, mlstm_head_norm, hgrn_lb_logits, hgrn_head_norm, w_proj_m,
           w_proj_h, w_out, norm_xattn, norm_mem, xattn_wq, xattn_wkv, xattn_wo, norm_ffn2, ffn2_w1,
           ffn2_w3, ffn2_w2, norm_final):
    b, t, d = x.shape
    depth = norm_ffn1.shape[0]
    assert depth == 1 and hgrn_lb_logits.shape[0] == 2
    n = b * t
    l = 0
    bf = lambda w: w.astype(BF16)

    w_in_t = jnp.swapaxes(w_in[l], 0, 1)
    n_gate = 2 * MLSTM_HEADS
    gate_bias = jnp.concatenate([mlstm_ig_bias[l], mlstm_fg_bias[l]]).astype(F32)
    bias_col = jnp.pad(gate_bias, (0, LANES - n_gate)).reshape(1, LANES)
    bias_row = gate_bias.reshape(n_gate, 1)

    x1, hn, gates_col, gates_row = _ffn(x.reshape(n, d), norm_ffn1[l], ffn1_w1[l], ffn1_w3[l], ffn1_w2[l],
                                        norm_mix[l], w_in_t)

    proj = _in_proj(hn, w_in_t)
    x2 = _mix(x1, proj, gates_col, gates_row, mlstm_conv_w[l], mlstm_conv_b[l].reshape(1, -1),
              bias_col, bias_row, mlstm_head_norm[l].reshape(1, -1), hgrn_lb_logits,
              hgrn_head_norm[l].reshape(1, -1), bf(w_proj_m[l]), bf(w_proj_h[l]), bf(w_out[l]), seq_len=t)

    qk, vo = _xattn_prep(mem, norm_mem[l], xattn_wkv[l], xattn_wq[l], xattn_wo[l])
    x3 = _xattn(x2.reshape(b, t, d), norm_xattn[l], qk, vo)

    (out,) = _ffn(x3.reshape(n, d), norm_ffn2[l], ffn2_w1[l], ffn2_w3[l], ffn2_w2[l], norm_final)
    return out.reshape(b, t, d)
```

```python
import functools

import jax
import jax.numpy as jnp
from jax import lax
from jax.experimental import pallas as pl
from jax.experimental.pallas import tpu as pltpu

F32 = jnp.float32
BF16 = jnp.bfloat16
EPS = 1e-6
LOG2_E = 1.4426950408889634

D_MODEL = 2048
D_FF = 5632
MLSTM_HEADS = 4
MLSTM_QK = 128
MLSTM_V = 256
CONV_WIDTH = 4
HGRN_HEADS = 8
HGRN_DK = 128
HGRN_DV = 128
XATTN_HEADS = 4
XATTN_HEAD_DIM = D_MODEL // XATTN_HEADS

MLSTM_QK_W = MLSTM_HEADS * MLSTM_QK
MLSTM_V_W = MLSTM_HEADS * MLSTM_V
HGRN_W = HGRN_HEADS * HGRN_DK

VMEM_LIMIT_BYTES = 60000 * 1024
LANES = 128

COL_GM = 0
COL_GH = COL_GM + D_MODEL
COL_MQ = COL_GH + D_MODEL
COL_MK = COL_MQ + MLSTM_QK_W
COL_MV = COL_MK + MLSTM_QK_W
COL_MO = COL_MV + MLSTM_V_W
COL_HQ = COL_MO + MLSTM_V_W
COL_HF = COL_HQ + HGRN_W
COL_HI = COL_HF + HGRN_W
COL_HG = COL_HI + HGRN_W
PROJ_W = COL_HG + HGRN_W

MIX_GATE_ROW = 2 * MLSTM_QK_W + 2 * MLSTM_V_W
MIX_HGRN_ROW = MIX_GATE_ROW + 2 * MLSTM_HEADS
MIX_MERGE_ROW = MIX_HGRN_ROW + 4 * HGRN_W

PROJ_SUB_ROWS = 1024
FFN_NORM_ROWS = 128
FFN_FINISH_ROWS = 128
FFN_SUB_COLS = 256

MIX_TB = 128
HGRN_NUM_LEVELS = MIX_TB.bit_length() - 1
MIX_SUBBLOCKS = 2
MERGE_GATE_COLS = 512
MERGE_OUT_COLS = 256
HGRN_COARSE_LEVELS = tuple(1 << i for i in range(3, HGRN_NUM_LEVELS))


def _rms(x, g):
    return x * lax.rsqrt(jnp.mean(x * x, axis=-1, keepdims=True) + EPS) * g


def _sigmoid(x):
    return 1.0 / (1.0 + jnp.exp(-x))


def _log_sigmoid(x):
    return jnp.minimum(x, 0.0) - jnp.log(1.0 + jnp.exp(-jnp.abs(x)))


def _split3(x):
    hi = x.astype(BF16)
    r1 = x - hi.astype(F32)
    mid = r1.astype(BF16)
    lo = (r1 - mid.astype(F32)).astype(BF16)
    return hi, mid, lo


def _tri_cumsum(tri, x):
    return sum(jnp.dot(tri, t, preferred_element_type=F32) for t in _split3(x))


def _neg_abs(x):
    return pltpu.bitcast(pltpu.bitcast(x, jnp.uint32) | jnp.uint32(0x80000000), F32)


def _params(*sem):
    return pltpu.CompilerParams(dimension_semantics=sem, vmem_limit_bytes=VMEM_LIMIT_BYTES)


def _ffn_kernel(*refs, mode):
    if mode == "mix":
        (x_hbm, g_ref, w1_ref, w3_ref, w2_ref, gn_ref, wg_ref,
         o_hbm, hn_ref, gcol_ref, grow_ref, h_sc, x_buf, acc_sc, x_sem, o_sem) = refs
    else:
        x_hbm, g_ref, w1_ref, w3_ref, w2_ref, gn_ref, o_hbm, h_sc, x_buf, acc_sc, x_sem, o_sem = refs
    i = pl.program_id(0)
    j = pl.program_id(1)
    n_i = pl.num_programs(0)
    tm = x_buf.shape[0]
    n_acc = acc_sc.shape[0]

    def acc_slot(tile):
        return acc_sc.at[tile % n_acc if n_acc > 1 else 0]

    def tile_rows(tile):
        return pl.ds(pl.multiple_of(tile * tm, tm), tm)

    def x_copy(tile):
        return pltpu.make_async_copy(x_hbm.at[tile_rows(tile), :], x_buf, x_sem)

    def o_copy(tile):
        return pltpu.make_async_copy(acc_slot(tile), o_hbm.at[tile_rows(tile), :], o_sem)

    def row_chunk(r, size):
        return pl.ds(pl.multiple_of(r * size, size), size)

    acc_ref = acc_slot(i)

    def swiglu_passes(first):
        h = h_sc[...]
        for c in range(w1_ref.shape[1] // FFN_SUB_COLS):
            cols = slice(c * FFN_SUB_COLS, (c + 1) * FFN_SUB_COLS)
            a = jnp.dot(h, w1_ref[:, cols].astype(BF16), preferred_element_type=F32)
            b = jnp.dot(h, w3_ref[:, cols].astype(BF16), preferred_element_type=F32)
            act = (a * _sigmoid(a) * b).astype(BF16)
            part = jnp.dot(act, w2_ref[cols, :].astype(BF16), preferred_element_type=F32)
            if first and c == 0:
                acc_ref[...] = 2.0 * x_buf[...] + part
            else:
                acc_ref[...] += part

    @pl.when(j == 0)
    def _():
        @pl.when(i == 0)
        def _():
            x_copy(0).start()

        x_copy(i).wait()

        def norm_rows(r, carry):
            rows = row_chunk(r, FFN_NORM_ROWS)
            h_sc[rows, :] = _rms(x_buf[rows, :], g_ref[...]).astype(BF16)
            return carry
        lax.fori_loop(0, tm // FFN_NORM_ROWS, norm_rows, 0)

        if n_acc == 1:
            @pl.when(i > 0)
            def _():
                o_copy(i - 1).wait()

        swiglu_passes(first=True)

    @pl.when(j > 0)
    def _():
        swiglu_passes(first=False)

    @pl.when(jnp.logical_and(j == 1, i + 1 < n_i))
    def _():
        x_copy(i + 1).start()

    @pl.when(j == pl.num_programs(1) - 1)
    def _():
        if mode == "final":
            def finish_rows(r, carry):
                rows = row_chunk(r, FFN_FINISH_ROWS)
                acc_ref[rows, :] = _rms(0.5 * acc_ref[rows, :], gn_ref[...])
                return carry
            lax.fori_loop(0, tm // FFN_FINISH_ROWS, finish_rows, 0)
        else:
            y = 0.5 * acc_ref[...]
            acc_ref[...] = y
            hn = _rms(y, gn_ref[...]).astype(BF16)
            hn_ref[...] = hn
            nt_dims = (((1,), (1,)), ((), ()))
            wg = wg_ref[...].astype(BF16)
            wg_pad = jnp.concatenate([wg, jnp.zeros((LANES - wg.shape[0], wg.shape[1]), BF16)], axis=0)
            gcol = lax.dot_general(hn, wg_pad, nt_dims, preferred_element_type=F32)
            gcol_ref[...] = gcol
            grow_ref[...] = gcol.T[0:wg.shape[0], :]

        if n_acc > 1:
            @pl.when(i > 0)
            def _():
                o_copy(i - 1).wait()

        o_copy(i).start()

        @pl.when(i == n_i - 1)
        def _():
            o_copy(i).wait()


def _ffn(x2d, g, w1, w3, w2, g_next, w_gates_t=None, *, tm=1024, tf=512):
    n, d = x2d.shape
    f = w1.shape[1]
    mode = "final" if w_gates_t is None else "mix"
    in_specs = [
        pl.BlockSpec(memory_space=pl.ANY),
        pl.BlockSpec((1, d), lambda i, j: (0, 0)),
        pl.BlockSpec((d, tf), lambda i, j: (0, j)),
        pl.BlockSpec((d, tf), lambda i, j: (0, j)),
        pl.BlockSpec((tf, d), lambda i, j: (j, 0)),
        pl.BlockSpec((1, d), lambda i, j: (0, 0)),
    ]
    args = [x2d, g.reshape(1, d), w1, w3, w2, g_next.reshape(1, d)]
    single = dict(pipeline_mode=pl.Buffered(1))
    out_specs = [pl.BlockSpec(memory_space=pl.ANY)]
    out_shape = [jax.ShapeDtypeStruct((n, d), F32)]
    if mode == "mix":
        ng = 2 * MLSTM_HEADS
        in_specs.append(pl.BlockSpec((ng, d), lambda i, j: (MIX_GATE_ROW // ng, 0)))
        args.append(w_gates_t)
        out_specs += [
            pl.BlockSpec((tm, d), lambda i, j: (i, 0), **single),
            pl.BlockSpec((tm, LANES), lambda i, j: (i, 0)),
            pl.BlockSpec((ng, tm), lambda i, j: (0, i)),
        ]
        out_shape += [
            jax.ShapeDtypeStruct((n, d), BF16),
            jax.ShapeDtypeStruct((n, LANES), F32),
            jax.ShapeDtypeStruct((ng, n), F32),
        ]
    return pl.pallas_call(
        functools.partial(_ffn_kernel, mode=mode),
        grid=(n // tm, f // tf),
        in_specs=in_specs,
        out_specs=out_specs,
        out_shape=out_shape,
        scratch_shapes=[
            pltpu.VMEM((tm, d), BF16),
            pltpu.VMEM((tm, d), F32),
            pltpu.VMEM((2 if mode == "final" else 1, tm, d), F32),
            pltpu.SemaphoreType.DMA(()),
            pltpu.SemaphoreType.DMA(()),
        ],
        compiler_params=_params("arbitrary", "arbitrary"),
        name="ffn_" + mode,
    )(*args)


def _in_proj_kernel(h_ref, wt_ref, p_ref, w_sc):
    @pl.when(pl.program_id(1) == 0)
    def _():
        w_sc[...] = wt_ref[...].astype(BF16)

    for r in range(0, h_ref.shape[0], PROJ_SUB_ROWS):
        rows = slice(r, r + PROJ_SUB_ROWS)
        p_ref[rows, :] = lax.dot_general(h_ref[rows, :], w_sc[...], (((1,), (1,)), ((), ())),
                                         preferred_element_type=F32)


def _proj_row_offset(j, tn):
    n_gate_tiles = 2 * D_MODEL // tn
    n_mlstm_tiles = MIX_GATE_ROW // tn
    off8 = jnp.where(j < n_gate_tiles, MIX_MERGE_ROW // 8 + j * (tn // 8),
                     jnp.where(j < n_gate_tiles + n_mlstm_tiles, (j - n_gate_tiles) * (tn // 8),
                               MIX_HGRN_ROW // 8 + (j - n_gate_tiles - n_mlstm_tiles) * (tn // 8)))
    return off8 * 8


def _in_proj(hn, w_t, *, tm=2048, tn=1024):
    n, d = hn.shape
    return pl.pallas_call(
        _in_proj_kernel,
        grid=(PROJ_W // tn, n // tm),
        in_specs=[
            pl.BlockSpec((tm, d), lambda j, i: (i, 0)),
            pl.BlockSpec((pl.Element(tn), pl.Element(d)), lambda j, i: (_proj_row_offset(j, tn), 0)),
        ],
        out_specs=pl.BlockSpec((tm, tn), lambda j, i: (i, j)),
        out_shape=jax.ShapeDtypeStruct((n, PROJ_W), F32),
        scratch_shapes=[pltpu.VMEM((tn, d), BF16)],
        compiler_params=_params("arbitrary", "arbitrary"),
        name="in_proj",
    )(hn, w_t)


def _mlstm_block(q_ref, k_ref, v_ref, o_ref, gcol_ref, grow_ref, cw_ref, cb_ref, bcol_ref, brow_ref,
                 hn_ref, y_ref, xpad_sc, c_sc, n_sc, m_sc, before_head):
    L = MIX_TB
    H = MLSTM_HEADS

    xpad_sc[8:8 + L, 0:MLSTM_QK_W] = q_ref[...]
    xpad_sc[8:8 + L, MLSTM_QK_W:2 * MLSTM_QK_W] = k_ref[...]
    acc = jnp.broadcast_to(cb_ref[...], (L, 2 * MLSTM_QK_W))
    for j in range(CONV_WIDTH):
        off = 8 - (CONV_WIDTH - 1) + j
        acc = acc + cw_ref[j:j + 1, :] * xpad_sc[off:off + L, :]
    xpad_sc[0:8, :] = xpad_sc[L:L + 8, :]
    qk = acc * _sigmoid(acc)

    gcol = gcol_ref[...] + bcol_ref[...]
    grow = grow_ref[...] + brow_ref[...]
    fcol = _log_sigmoid(gcol)
    frow = _log_sigmoid(grow)
    r_i = lax.broadcasted_iota(jnp.int32, (L, L), 0)
    c_i = lax.broadcasted_iota(jnp.int32, (L, L), 1)
    causal = r_i >= c_i
    tril = jnp.where(causal, 1.0, 0.0).astype(BF16)
    triu = jnp.where(r_i <= c_i, 1.0, 0.0).astype(F32)
    bcol = _tri_cumsum(tril, fcol)
    brow = jnp.dot(frow, triu, preferred_element_type=F32, precision=lax.Precision.HIGHEST)

    for h in range(H):
        before_head[h]()
        b_c = bcol[:, H + h:H + h + 1]
        i_c = gcol[:, h:h + 1]
        b_r = brow[H + h:H + h + 1, :]
        i_r = grow[h:h + 1, :]
        m_prev = m_sc[h:h + 1, 0:1]

        d_log = jnp.where(causal, b_c - (b_r - i_r), -jnp.inf)
        inter_log = b_c + m_prev
        m_t = jnp.maximum(jnp.max(d_log, axis=1, keepdims=True), inter_log)

        q_h = qk[:, h * MLSTM_QK:(h + 1) * MLSTM_QK]
        k_h = qk[:, MLSTM_QK_W + h * MLSTM_QK:MLSTM_QK_W + (h + 1) * MLSTM_QK] * (MLSTM_QK ** -0.5)
        v_h = v_ref[:, h * MLSTM_V:(h + 1) * MLSTM_V].astype(BF16)
        q_b = q_h.astype(BF16)

        s = lax.dot_general(q_b, k_h.astype(BF16), (((1,), (1,)), ((), ())), preferred_element_type=F32)
        s = s * jnp.exp(d_log - m_t)
        w_inter = jnp.exp(inter_log - m_t)
        c_h = c_sc[h]
        n_h = n_sc[h:h + 1, :]
        num = (jnp.dot(s.astype(BF16), v_h, preferred_element_type=F32)
               + w_inter * jnp.dot(q_b, c_h.astype(BF16), preferred_element_type=F32))
        den = jnp.sum(s, axis=1, keepdims=True) + w_inter * jnp.sum(q_h * n_h, axis=1, keepdims=True)
        hh = num / jnp.maximum(jnp.abs(den), jnp.exp(-m_t))

        hn = hh * lax.rsqrt(jnp.mean(hh * hh, axis=-1, keepdims=True) + EPS)
        hn = hn * hn_ref[:, h * MLSTM_V:(h + 1) * MLSTM_V]
        y = hn * _sigmoid(o_ref[:, h * MLSTM_V:(h + 1) * MLSTM_V])
        y_ref[:, h * MLSTM_V:(h + 1) * MLSTM_V] = y.astype(y_ref.dtype)

        b_last = b_c[L - 1:L, :]
        a_log = b_last - b_c + i_c
        m_new = jnp.maximum(b_last + m_prev, jnp.max(a_log, axis=0, keepdims=True))
        w_a = jnp.exp(a_log - m_new)
        decay = jnp.exp(b_last + m_prev - m_new)
        kw = k_h * w_a
        c_sc[h] = decay * c_h + lax.dot_general(kw.astype(BF16), v_h, (((0,), (0,)), ((), ())),
                                                preferred_element_type=F32)
        n_sc[h:h + 1, :] = decay * n_h + jnp.sum(kw, axis=0, keepdims=True)
        m_sc[h:h + 1, :] = jnp.broadcast_to(m_new, (1, LANES))


def _hgrn_block(q_ref, f_ref, v_ref, og_ref, lbl_ref, hn_ref, y_ref, g_sc, k_sc, q_sc, gr_sc, st_sc,
                before_head):
    TB = MIX_TB
    H = HGRN_HEADS
    DK = HGRN_DK
    W = H * DK
    NT = TB // 8

    lbl = lbl_ref[...]
    lmax = jnp.max(lbl, axis=0, keepdims=True)
    le = jnp.exp(lbl - lmax)
    lb = le[1:2, :] / jnp.sum(le, axis=0, keepdims=True)

    f = lb + (1.0 - lb) * _sigmoid(f_ref[...])
    k_sc[...] = 1.0 - f
    qraw = q_ref[...]
    q_sc[...] = qraw * _sigmoid(qraw) * (DK ** -0.5)
    r_i = lax.broadcasted_iota(jnp.int32, (TB, TB), 0)
    c_i = lax.broadcasted_iota(jnp.int32, (TB, TB), 1)
    tri = jnp.where(r_i >= c_i, 1.0, 0.0).astype(BF16)
    g = _tri_cumsum(tri, jnp.log(f) * LOG2_E)
    g_sc[...] = g

    g3 = g.reshape(NT, 8, W)
    sub = lax.broadcasted_iota(jnp.int32, (NT, 8, W), 1)
    bit0 = (sub & 1) != 0
    bit1 = (sub & 2) != 0
    bit2 = (sub & 4) != 0
    last2 = jnp.where(bit0, g3, pltpu.roll(g3, 7, 1))
    last4 = jnp.where(bit1, last2, pltpu.roll(last2, 6, 1))
    last4_r = pltpu.roll(last4, 4, 1)
    last8 = jnp.where(bit2, last4, last4_r).reshape(TB, W)
    gr_sc[0] = jnp.where(bit0, pltpu.roll(g3, 1, 1), g3).reshape(TB, W)
    gr_sc[1] = jnp.where(bit1, pltpu.roll(last2, 2, 1), last2).reshape(TB, W)
    gr_sc[2] = jnp.where(bit2, last4_r, last4).reshape(TB, W)
    for li, lvl in enumerate(HGRN_COARSE_LEVELS):
        groups = []
        for base in range(0, TB, 2 * lvl):
            src = last8[base + lvl - 8:base + lvl, :]
            groups.extend([src] * (2 * lvl // 8))
        gr_sc[3 + li] = jnp.concatenate(groups, axis=0)

    x_i = r_i ^ c_i
    nt_dims = (((1,), (1,)), ((), ()))
    for h in range(H):
        before_head[h]()
        cols = slice(h * DK, (h + 1) * DK)
        q_h = q_sc[:, cols]
        k_h = k_sc[:, cols]
        g_h = g_sc[:, cols]
        v_h = v_ref[:, cols].astype(BF16)

        q_b = q_h.astype(BF16)
        k_b = k_h.astype(BF16)
        a = lax.dot_general(q_b, k_b, nt_dims, preferred_element_type=F32)
        for li in range(HGRN_NUM_LEVELS):
            e = jnp.exp2(_neg_abs(g_h - gr_sc[li, :, cols])).astype(BF16)
            p = lax.dot_general(q_b * e, k_b * e, nt_dims, preferred_element_type=F32)
            a = jnp.where(x_i >= (1 << li), p, a)
        a = jnp.where(r_i >= c_i, a, 0.0)
        o = jnp.dot(a.astype(BF16), v_h, preferred_element_type=F32)

        st = st_sc[h]
        g_last = g_h[TB - 1:TB, :]
        qd = (q_h * jnp.exp2(g_h)).astype(BF16)
        o = o + lax.dot_general(qd, st.astype(BF16), nt_dims, preferred_element_type=F32)
        kd = (k_h * jnp.exp2(g_last - g_h)).astype(BF16)
        st_sc[h] = jnp.exp2(g_last) * st + lax.dot_general(v_h, kd, (((0,), (0,)), ((), ())),
                                                           preferred_element_type=F32)

        on = o * lax.rsqrt(jnp.mean(o * o, axis=-1, keepdims=True) + EPS) * hn_ref[:, cols]
        og = og_ref[:, cols]
        y_ref[:, cols] = (on * (og * _sigmoid(og))).astype(y_ref.dtype)


def _mix_kernel(mq_ref, mk_ref, mv_ref, mo_ref, gcol_ref, grow_ref, cw_ref, cb_ref, bcol_ref, brow_ref,
                mhn_ref, hq_ref, hf_ref, hi_ref, hg_ref, lbl_ref, hhn_ref,
                x_ref, gm_ref, gh_ref, wm_ref, wh_ref, wo_ref,
                out_ref,
                ym_sc, yh_sc, merged_sc, xpad_sc, c_sc, n_sc, m_sc, g_sc, k_sc, q_sc, gr_sc, st_sc,
                *, blocks_per_seq):
    s = pl.program_id(0)
    cur = s % 2
    prv = 1 - cur

    @pl.when(s == 0)
    def _():
        ym_sc[...] = jnp.zeros_like(ym_sc)
        yh_sc[...] = jnp.zeros_like(yh_sc)

    @pl.when(s % blocks_per_seq == 0)
    def _():
        xpad_sc[0:8, :] = jnp.zeros((8, 2 * MLSTM_QK_W), F32)
        c_sc[...] = jnp.zeros_like(c_sc)
        n_sc[...] = jnp.zeros_like(n_sc)
        m_sc[...] = jnp.zeros_like(m_sc)
        st_sc[...] = jnp.zeros_like(st_sc)

    def gate_piece(c):
        def run():
            cols = slice(c * MERGE_GATE_COLS, (c + 1) * MERGE_GATE_COLS)
            pm = jnp.dot(ym_sc[prv], wm_ref[:, cols], preferred_element_type=F32)
            ph = jnp.dot(yh_sc[prv], wh_ref[:, cols], preferred_element_type=F32)
            merged = _sigmoid(gm_ref[:, cols]) * pm + _sigmoid(gh_ref[:, cols]) * ph
            merged_sc[:, cols] = merged.astype(BF16)
        return run

    def out_piece(c):
        def run():
            cols = slice(c * MERGE_OUT_COLS, (c + 1) * MERGE_OUT_COLS)
            out_ref[:, cols] = x_ref[:, cols] + jnp.dot(merged_sc[...], wo_ref[:, cols],
                                                        preferred_element_type=F32)
        return run

    pieces = ([gate_piece(c) for c in range(D_MODEL // MERGE_GATE_COLS)]
              + [out_piece(c) for c in range(D_MODEL // MERGE_OUT_COLS)])
    n_heads = MLSTM_HEADS + HGRN_HEADS
    n_slots = MIX_SUBBLOCKS * n_heads
    slot_of = [(i * n_slots) // len(pieces) for i in range(len(pieces))]
    hooks = [(pieces[slot_of.index(i)] if i in slot_of else (lambda: None)) for i in range(n_slots)]

    for sub in range(MIX_SUBBLOCKS):
        rows = pl.ds(sub * MIX_TB, MIX_TB)
        sub_hooks = hooks[sub * n_heads:(sub + 1) * n_heads]
        _mlstm_block(mq_ref.at[rows], mk_ref.at[rows], mv_ref.at[rows], mo_ref.at[rows], gcol_ref.at[rows],
                     grow_ref.at[:, rows], cw_ref, cb_ref, bcol_ref, brow_ref, mhn_ref,
                     ym_sc.at[cur, rows], xpad_sc, c_sc, n_sc, m_sc, sub_hooks[:MLSTM_HEADS])
        _hgrn_block(hq_ref.at[rows], hf_ref.at[rows], hi_ref.at[rows], hg_ref.at[rows], lbl_ref, hhn_ref,
                    yh_sc.at[cur, rows], g_sc, k_sc, q_sc, gr_sc, st_sc, sub_hooks[MLSTM_HEADS:])


def _const_spec(shape):
    return pl.BlockSpec(shape, lambda *_: (0,) * len(shape), pipeline_mode=pl.Buffered(1))


def _mix(x2d, proj, gates_col, gates_row, conv_w, conv_b, bias_col, bias_row, m_head_norm,
         lb_logits, h_head_norm, w_m, w_h, w_o, *, seq_len):
    n, d = x2d.shape
    tb = MIX_TB * MIX_SUBBLOCKS
    nblk = n // tb
    qw, vw, hw = MLSTM_QK_W, MLSTM_V_W, HGRN_W

    def cur(col_block):
        return lambda s: (jnp.minimum(s, nblk - 1), col_block)

    def prev(col_block):
        return lambda s: (jnp.maximum(s - 1, 0), col_block)

    return pl.pallas_call(
        functools.partial(_mix_kernel, blocks_per_seq=seq_len // tb),
        grid=(nblk + 1,),
        in_specs=[
            pl.BlockSpec((tb, qw), cur(COL_MQ // qw)),
            pl.BlockSpec((tb, qw), cur(COL_MK // qw)),
            pl.BlockSpec((tb, vw), cur(COL_MV // vw)),
            pl.BlockSpec((tb, vw), cur(COL_MO // vw)),
            pl.BlockSpec((tb, LANES), cur(0)),
            pl.BlockSpec((2 * MLSTM_HEADS, tb), lambda s: (0, jnp.minimum(s, nblk - 1))),
            _const_spec((CONV_WIDTH, 2 * qw)),
            _const_spec((1, 2 * qw)),
            _const_spec((1, LANES)),
            _const_spec((2 * MLSTM_HEADS, 1)),
            _const_spec((1, vw)),
            pl.BlockSpec((tb, hw), cur(COL_HQ // hw)),
            pl.BlockSpec((tb, hw), cur(COL_HF // hw)),
            pl.BlockSpec((tb, hw), cur(COL_HI // hw)),
            pl.BlockSpec((tb, hw), cur(COL_HG // hw)),
            _const_spec((2, hw)),
            _const_spec((1, hw)),
            pl.BlockSpec((tb, d), prev(0)),
            pl.BlockSpec((tb, d), prev(COL_GM // d)),
            pl.BlockSpec((tb, d), prev(COL_GH // d)),
            _const_spec(w_m.shape),
            _const_spec(w_h.shape),
            _const_spec(w_o.shape),
        ],
        out_specs=pl.BlockSpec((tb, d), prev(0)),
        out_shape=jax.ShapeDtypeStruct((n, d), F32),
        scratch_shapes=[
            pltpu.VMEM((2, tb, vw), BF16),
            pltpu.VMEM((2, tb, hw), BF16),
            pltpu.VMEM((tb, d), BF16),
            pltpu.VMEM((MIX_TB + 8, 2 * qw), F32),
            pltpu.VMEM((MLSTM_HEADS, MLSTM_QK, MLSTM_V), F32),
            pltpu.VMEM((8, MLSTM_QK), F32),
            pltpu.VMEM((8, LANES), F32),
            pltpu.VMEM((MIX_TB, hw), F32),
            pltpu.VMEM((MIX_TB, hw), F32),
            pltpu.VMEM((MIX_TB, hw), F32),
            pltpu.VMEM((HGRN_NUM_LEVELS, MIX_TB, hw), F32),
            pltpu.VMEM((HGRN_HEADS, HGRN_DV, HGRN_DK), F32),
        ],
        compiler_params=_params("arbitrary"),
        name="mix",
    )(proj, proj, proj, proj, gates_col, gates_row, conv_w, conv_b, bias_col, bias_row, m_head_norm,
      proj, proj, proj, proj, lb_logits, h_head_norm, x2d, proj, proj, w_m, w_h, w_o)


def _xattn_prep_kernel(m_ref, g_ref, wk_ref, wv_ref, wq_ref, wo_ref, qk_ref, vo_ref):
    wk = wk_ref[...].astype(BF16)
    wv = wv_ref[...].astype(BF16)
    wq = wq_ref[...].astype(BF16)
    wo = wo_ref[...].astype(BF16)
    for i in range(m_ref.shape[0]):
        m = _rms(m_ref[i], g_ref[...]).astype(BF16)
        k = jnp.dot(m, wk, preferred_element_type=F32).astype(BF16)
        v = jnp.dot(m, wv, preferred_element_type=F32).astype(BF16)
        qk = lax.dot_general(wq, k, (((1,), (1,)), ((), ())), preferred_element_type=F32)
        qk_ref[i] = (qk * (XATTN_HEAD_DIM ** -0.5)).astype(qk_ref.dtype)
        vo_ref[i] = jnp.dot(v, wo, preferred_element_type=F32).astype(vo_ref.dtype)


def _xattn_prep(mem, g, w_kv, w_q, w_o):
    b, m, d = mem.shape
    hd = XATTN_HEAD_DIM
    nh = XATTN_HEADS
    return pl.pallas_call(
        _xattn_prep_kernel,
        grid=(nh,),
        in_specs=[
            pl.BlockSpec((b, m, d), lambda h: (0, 0, 0)),
            pl.BlockSpec((1, d), lambda h: (0, 0)),
            pl.BlockSpec((d, hd), lambda h: (0, h)),
            pl.BlockSpec((d, hd), lambda h: (0, nh + h)),
            pl.BlockSpec((d, hd), lambda h: (0, h)),
            pl.BlockSpec((hd, d), lambda h: (h, 0)),
        ],
        out_specs=[
            pl.BlockSpec((b, d, m), lambda h: (0, 0, h)),
            pl.BlockSpec((b, m, d), lambda h: (0, h, 0)),
        ],
        out_shape=[
            jax.ShapeDtypeStruct((b, d, nh * m), BF16),
            jax.ShapeDtypeStruct((b, nh * m, d), BF16),
        ],
        compiler_params=_params("arbitrary"),
        name="xattn_prep",
    )(mem, g.reshape(1, d), w_kv, w_kv, w_q, w_o)


def _xattn_kernel(x_ref, g_ref, qk_ref, vo_ref, o_ref, *, mem_len):
    x = x_ref[0]
    h = _rms(x, g_ref[...]).astype(BF16)
    s = jnp.dot(h, qk_ref[0], preferred_element_type=F32)
    probs = []
    for hd in range(XATTN_HEADS):
        s_h = s[:, hd * mem_len:(hd + 1) * mem_len]
        e = jnp.exp(s_h - jnp.max(s_h, axis=-1, keepdims=True))
        probs.append((e / jnp.sum(e, axis=-1, keepdims=True)).astype(BF16))
    p = jnp.concatenate(probs, axis=-1)
    o_ref[0] = x + jnp.dot(p, vo_ref[0], preferred_element_type=F32)


def _xattn(x, g, qk, vo, *, tm=512):
    b, t, d = x.shape
    hm = qk.shape[2]
    return pl.pallas_call(
        functools.partial(_xattn_kernel, mem_len=hm // XATTN_HEADS),
        grid=(b, t // tm),
        in_specs=[
            pl.BlockSpec((1, tm, d), lambda i, c: (i, c, 0)),
            _const_spec((1, d)),
            pl.BlockSpec((1, d, hm), lambda i, c: (i, 0, 0)),
            pl.BlockSpec((1, hm, d), lambda i, c: (i, 0, 0)),
        ],
        out_specs=pl.BlockSpec((1, tm, d), lambda i, c: (i, c, 0)),
        out_shape=jax.ShapeDtypeStruct((b, t, d), F32),
        compiler_params=_params("parallel", "parallel"),
        name="xattn",
    )(x, g.reshape(1, d), qk, vo)


def kernel(x, mem, norm_ffn1, ffn1_w1, ffn1_w3, ffn1_w2, norm_mix, w_in, mlstm_conv_w, mlstm_conv_b,
           mlstm_ig_bias, mlstm_fg_bias, mlstm_head_norm, hgrn_lb_logits, hgrn_head_norm, w_proj_m,
           w_proj_h, w_out, norm_xattn, norm_mem, xattn_wq, xattn_wkv, xattn_wo, norm_ffn2, ffn2_w1,
           ffn2_w3, ffn2_w2, norm_final):
    b, t, d = x.shape
    depth = norm_ffn1.shape[0]
    assert depth == 1 and hgrn_lb_logits.shape[0] == 2
    n = b * t
    l = 0
    bf = lambda w: w.astype(BF16)

    w_in_t = jnp.swapaxes(w_in[l], 0, 1)
    n_gate = 2 * MLSTM_HEADS
    gate_bias = jnp.concatenate([mlstm_ig_bias[l], mlstm_fg_bias[l]]).astype(F32)
    bias_col = jnp.pad(gate_bias, (0, LANES - n_gate)).reshape(1, LANES)
    bias_row = gate_bias.reshape(n_gate, 1)

    x1, hn, gates_col, gates_row = _ffn(x.reshape(n, d), norm_ffn1[l], ffn1_w1[l], ffn1_w3[l], ffn1_w2[l],
                                        norm_mix[l], w_in_t)

    proj = _in_proj(hn, w_in_t)
    x2 = _mix(x1, proj, gates_col, gates_row, mlstm_conv_w[l], mlstm_conv_b[l].reshape(1, -1),
              bias_col, bias_row, mlstm_head_norm[l].reshape(1, -1), hgrn_lb_logits,
              hgrn_head_norm[l].reshape(1, -1), bf(w_proj_m[l]), bf(w_proj_h[l]), bf(w_out[l]), seq_len=t)

    qk, vo = _xattn_prep(mem, norm_mem[l], xattn_wkv[l], xattn_wq[l], xattn_wo[l])
    x3 = _xattn(x2.reshape(b, t, d), norm_xattn[l], qk, vo)

    (out,) = _ffn(x3.reshape(n, d), norm_ffn2[l], ffn2_w1[l], ffn2_w3[l], ffn2_w2[l], norm_final)
    return out.reshape(b, t, d)
```

```python
import functools

import jax
import jax.numpy as jnp
from jax import lax
from jax.experimental import pallas as pl
from jax.experimental.pallas import tpu as pltpu

F32 = jnp.float32
BF16 = jnp.bfloat16
EPS = 1e-6
LOG2_E = 1.4426950408889634

D_MODEL = 2048
D_FF = 5632
MLSTM_HEADS = 4
MLSTM_QK = 128
MLSTM_V = 256
CONV_WIDTH = 4
HGRN_HEADS = 8
HGRN_DK = 128
HGRN_DV = 128
XATTN_HEADS = 4
XATTN_HEAD_DIM = D_MODEL // XATTN_HEADS

MLSTM_QK_W = MLSTM_HEADS * MLSTM_QK
MLSTM_V_W = MLSTM_HEADS * MLSTM_V
HGRN_W = HGRN_HEADS * HGRN_DK

VMEM_LIMIT_BYTES = 60000 * 1024
LANES = 128

COL_GM = 0
COL_GH = COL_GM + D_MODEL
COL_MQ = COL_GH + D_MODEL
COL_MK = COL_MQ + MLSTM_QK_W
COL_MV = COL_MK + MLSTM_QK_W
COL_MO = COL_MV + MLSTM_V_W
COL_HQ = COL_MO + MLSTM_V_W
COL_HF = COL_HQ + HGRN_W
COL_HI = COL_HF + HGRN_W
COL_HG = COL_HI + HGRN_W
PROJ_W = COL_HG + HGRN_W

MIX_GATE_ROW = 2 * MLSTM_QK_W + 2 * MLSTM_V_W
MIX_HGRN_ROW = MIX_GATE_ROW + 2 * MLSTM_HEADS
MIX_MERGE_ROW = MIX_HGRN_ROW + 4 * HGRN_W

PROJ_SUB_ROWS = 1024
FFN_NORM_ROWS = 128
FFN_FINISH_ROWS = 128
FFN_SUB_COLS = 256

MIX_TB = 128
HGRN_NUM_LEVELS = MIX_TB.bit_length() - 1
MIX_SUBBLOCKS = 2
MERGE_GATE_COLS = 512
MERGE_OUT_COLS = 256
HGRN_COARSE_LEVELS = tuple(1 << i for i in range(3, HGRN_NUM_LEVELS))


def _rms(x, g):
    return x * lax.rsqrt(jnp.mean(x * x, axis=-1, keepdims=True) + EPS) * g


def _sigmoid(x):
    return 1.0 / (1.0 + jnp.exp(-x))


def _log_sigmoid(x):
    return jnp.minimum(x, 0.0) - jnp.log(1.0 + jnp.exp(-jnp.abs(x)))


def _split3(x):
    hi = x.astype(BF16)
    r1 = x - hi.astype(F32)
    mid = r1.astype(BF16)
    lo = (r1 - mid.astype(F32)).astype(BF16)
    return hi, mid, lo


def _tri_cumsum(tri, x):
    return sum(jnp.dot(tri, t, preferred_element_type=F32) for t in _split3(x))


def _neg_abs(x):
    return pltpu.bitcast(pltpu.bitcast(x, jnp.uint32) | jnp.uint32(0x80000000), F32)


def _params(*sem):
    return pltpu.CompilerParams(dimension_semantics=sem, vmem_limit_bytes=VMEM_LIMIT_BYTES)


def _ffn_kernel(*refs, mode):
    if mode == "mix":
        (x_hbm, g_ref, w1_ref, w3_ref, w2_ref, gn_ref, wg_ref,
         o_hbm, hn_hbm, gcol_ref, grow_ref, h_sc, x_buf, acc_sc, x_sem, o_sem, hn_sc, hn_sem) = refs
    else:
        x_hbm, g_ref, w1_ref, w3_ref, w2_ref, gn_ref, o_hbm, h_sc, x_buf, acc_sc, x_sem, o_sem = refs
    i = pl.program_id(0)
    j = pl.program_id(1)
    n_i = pl.num_programs(0)
    tm = x_buf.shape[0]
    n_acc = acc_sc.shape[0]

    def acc_slot(tile):
        return acc_sc.at[tile % n_acc if n_acc > 1 else 0]

    def tile_rows(tile):
        return pl.ds(pl.multiple_of(tile * tm, tm), tm)

    def x_copy(tile):
        return pltpu.make_async_copy(x_hbm.at[tile_rows(tile), :], x_buf, x_sem)

    def o_copy(tile):
        return pltpu.make_async_copy(acc_slot(tile), o_hbm.at[tile_rows(tile), :], o_sem)

    def hn_copy(tile):
        return pltpu.make_async_copy(hn_sc, hn_hbm.at[tile_rows(tile), :], hn_sem)

    def row_chunk(r, size):
        return pl.ds(pl.multiple_of(r * size, size), size)

    acc_ref = acc_slot(i)

    def swiglu_passes(first):
        h = h_sc[...]
        for c in range(w1_ref.shape[1] // FFN_SUB_COLS):
            cols = slice(c * FFN_SUB_COLS, (c + 1) * FFN_SUB_COLS)
            a = jnp.dot(h, w1_ref[:, cols].astype(BF16), preferred_element_type=F32)
            b = jnp.dot(h, w3_ref[:, cols].astype(BF16), preferred_element_type=F32)
            act = (a * _sigmoid(a) * b).astype(BF16)
            part = jnp.dot(act, w2_ref[cols, :].astype(BF16), preferred_element_type=F32)
            if first and c == 0:
                acc_ref[...] = 2.0 * x_buf[...] + part
            else:
                acc_ref[...] += part

    @pl.when(j == 0)
    def _():
        @pl.when(i == 0)
        def _():
            x_copy(0).start()

        x_copy(i).wait()

        def norm_rows(r, carry):
            rows = row_chunk(r, FFN_NORM_ROWS)
            h_sc[rows, :] = _rms(x_buf[rows, :], g_ref[...]).astype(BF16)
            return carry
        lax.fori_loop(0, tm // FFN_NORM_ROWS, norm_rows, 0)

        if n_acc == 1:
            @pl.when(i > 0)
            def _():
                o_copy(i - 1).wait()

        swiglu_passes(first=True)

    @pl.when(j > 0)
    def _():
        swiglu_passes(first=False)

    @pl.when(jnp.logical_and(j == 1, i + 1 < n_i))
    def _():
        x_copy(i + 1).start()

    @pl.when(j == pl.num_programs(1) - 1)
    def _():
        if mode == "final":
            def finish_rows(r, carry):
                rows = row_chunk(r, FFN_FINISH_ROWS)
                acc_ref[rows, :] = _rms(0.5 * acc_ref[rows, :], gn_ref[...])
                return carry
            lax.fori_loop(0, tm // FFN_FINISH_ROWS, finish_rows, 0)
        else:
            @pl.when(i > 0)
            def _():
                hn_copy(i - 1).wait()

            y = 0.5 * acc_ref[...]
            acc_ref[...] = y
            hn = _rms(y, gn_ref[...]).astype(BF16)
            hn_sc[...] = hn
            nt_dims = (((1,), (1,)), ((), ()))
            wg = wg_ref[...].astype(BF16)
            wg_pad = jnp.concatenate([wg, jnp.zeros((LANES - wg.shape[0], wg.shape[1]), BF16)], axis=0)
            gcol = lax.dot_general(hn, wg_pad, nt_dims, preferred_element_type=F32)
            gcol_ref[...] = gcol
            grow_ref[...] = gcol.T[0:wg.shape[0], :]

        if n_acc > 1:
            @pl.when(i > 0)
            def _():
                o_copy(i - 1).wait()

        o_copy(i).start()
        if mode == "mix":
            hn_copy(i).start()

        @pl.when(i == n_i - 1)
        def _():
            o_copy(i).wait()
            if mode == "mix":
                hn_copy(i).wait()


def _ffn(x2d, g, w1, w3, w2, g_next, w_gates_t=None, *, tm=1024, tf=512):
    n, d = x2d.shape
    f = w1.shape[1]
    mode = "final" if w_gates_t is None else "mix"
    in_specs = [
        pl.BlockSpec(memory_space=pl.ANY),
        pl.BlockSpec((1, d), lambda i, j: (0, 0)),
        pl.BlockSpec((d, tf), lambda i, j: (0, j)),
        pl.BlockSpec((d, tf), lambda i, j: (0, j)),
        pl.BlockSpec((tf, d), lambda i, j: (j, 0)),
        pl.BlockSpec((1, d), lambda i, j: (0, 0)),
    ]
    args = [x2d, g.reshape(1, d), w1, w3, w2, g_next.reshape(1, d)]
    out_specs = [pl.BlockSpec(memory_space=pl.ANY)]
    out_shape = [jax.ShapeDtypeStruct((n, d), F32)]
    scratch_shapes = [
        pltpu.VMEM((tm, d), BF16),
        pltpu.VMEM((tm, d), F32),
        pltpu.VMEM((2 if mode == "final" else 1, tm, d), F32),
        pltpu.SemaphoreType.DMA(()),
        pltpu.SemaphoreType.DMA(()),
    ]
    if mode == "mix":
        ng = 2 * MLSTM_HEADS
        in_specs.append(pl.BlockSpec((ng, d), lambda i, j: (MIX_GATE_ROW // ng, 0)))
        args.append(w_gates_t)
        out_specs += [
            pl.BlockSpec(memory_space=pl.ANY),
            pl.BlockSpec((tm, LANES), lambda i, j: (i, 0)),
            pl.BlockSpec((ng, tm), lambda i, j: (0, i)),
        ]
        out_shape += [
            jax.ShapeDtypeStruct((n, d), BF16),
            jax.ShapeDtypeStruct((n, LANES), F32),
            jax.ShapeDtypeStruct((ng, n), F32),
        ]
        scratch_shapes += [pltpu.VMEM((tm, d), BF16), pltpu.SemaphoreType.DMA(())]
    return pl.pallas_call(
        functools.partial(_ffn_kernel, mode=mode),
        grid=(n // tm, f // tf),
        in_specs=in_specs,
        out_specs=out_specs,
        out_shape=out_shape,
        scratch_shapes=scratch_shapes,
        compiler_params=_params("arbitrary", "arbitrary"),
        name="ffn_" + mode,
    )(*args)


def _in_proj_kernel(h_ref, wt_ref, p_ref, w_sc):
    @pl.when(pl.program_id(1) == 0)
    def _():
        w_sc[...] = wt_ref[...].astype(BF16)

    for r in range(0, h_ref.shape[0], PROJ_SUB_ROWS):
        rows = slice(r, r + PROJ_SUB_ROWS)
        p_ref[rows, :] = lax.dot_general(h_ref[rows, :], w_sc[...], (((1,), (1,)), ((), ())),
                                         preferred_element_type=F32)


def _proj_row_offset(j, tn):
    n_gate_tiles = 2 * D_MODEL // tn
    n_mlstm_tiles = MIX_GATE_ROW // tn
    off8 = jnp.where(j < n_gate_tiles, MIX_MERGE_ROW // 8 + j * (tn // 8),
                     jnp.where(j < n_gate_tiles + n_mlstm_tiles, (j - n_gate_tiles) * (tn // 8),
                               MIX_HGRN_ROW // 8 + (j - n_gate_tiles - n_mlstm_tiles) * (tn // 8)))
    return off8 * 8


def _in_proj(hn, w_t, *, tm=2048, tn=1024):
    n, d = hn.shape
    return pl.pallas_call(
        _in_proj_kernel,
        grid=(PROJ_W // tn, n // tm),
        in_specs=[
            pl.BlockSpec((tm, d), lambda j, i: (i, 0)),
            pl.BlockSpec((pl.Element(tn), pl.Element(d)), lambda j, i: (_proj_row_offset(j, tn), 0)),
        ],
        out_specs=pl.BlockSpec((tm, tn), lambda j, i: (i, j)),
        out_shape=jax.ShapeDtypeStruct((n, PROJ_W), F32),
        scratch_shapes=[pltpu.VMEM((tn, d), BF16)],
        compiler_params=_params("arbitrary", "arbitrary"),
        name="in_proj",
    )(hn, w_t)


def _mlstm_block(q_ref, k_ref, v_ref, o_ref, gcol_ref, grow_ref, cw_ref, cb_ref, bcol_ref, brow_ref,
                 hn_ref, y_ref, xpad_sc, c_sc, n_sc, m_sc, before_head):
    L = MIX_TB
    H = MLSTM_HEADS

    xpad_sc[8:8 + L, 0:MLSTM_QK_W] = q_ref[...]
    xpad_sc[8:8 + L, MLSTM_QK_W:2 * MLSTM_QK_W] = k_ref[...]
    acc = jnp.broadcast_to(cb_ref[...], (L, 2 * MLSTM_QK_W))
    for j in range(CONV_WIDTH):
        off = 8 - (CONV_WIDTH - 1) + j
        acc = acc + cw_ref[j:j + 1, :] * xpad_sc[off:off + L, :]
    xpad_sc[0:8, :] = xpad_sc[L:L + 8, :]
    qk = acc * _sigmoid(acc)

    gcol = gcol_ref[...] + bcol_ref[...]
    grow = grow_ref[...] + brow_ref[...]
    fcol = _log_sigmoid(gcol)
    frow = _log_sigmoid(grow)
    r_i = lax.broadcasted_iota(jnp.int32, (L, L), 0)
    c_i = lax.broadcasted_iota(jnp.int32, (L, L), 1)
    causal = r_i >= c_i
    tril = jnp.where(causal, 1.0, 0.0).astype(BF16)
    triu = jnp.where(r_i <= c_i, 1.0, 0.0).astype(F32)
    bcol = _tri_cumsum(tril, fcol)
    brow = jnp.dot(frow, triu, preferred_element_type=F32, precision=lax.Precision.HIGHEST)

    for h in range(H):
        before_head[h]()
        b_c = bcol[:, H + h:H + h + 1]
        i_c = gcol[:, h:h + 1]
        b_r = brow[H + h:H + h + 1, :]
        i_r = grow[h:h + 1, :]
        m_prev = m_sc[h:h + 1, 0:1]

        d_log = jnp.where(causal, b_c - (b_r - i_r), -jnp.inf)
        inter_log = b_c + m_prev
        m_t = jnp.maximum(jnp.max(d_log, axis=1, keepdims=True), inter_log)

        q_h = qk[:, h * MLSTM_QK:(h + 1) * MLSTM_QK]
        k_h = qk[:, MLSTM_QK_W + h * MLSTM_QK:MLSTM_QK_W + (h + 1) * MLSTM_QK] * (MLSTM_QK ** -0.5)
        v_h = v_ref[:, h * MLSTM_V:(h + 1) * MLSTM_V].astype(BF16)
        q_b = q_h.astype(BF16)

        s = lax.dot_general(q_b, k_h.astype(BF16), (((1,), (1,)), ((), ())), preferred_element_type=F32)
        s = s * jnp.exp(d_log - m_t)
        w_inter = jnp.exp(inter_log - m_t)
        c_h = c_sc[h]
        n_h = n_sc[h:h + 1, :]
        num = (jnp.dot(s.astype(BF16), v_h, preferred_element_type=F32)
               + w_inter * jnp.dot(q_b, c_h.astype(BF16), preferred_element_type=F32))
        den = jnp.sum(s, axis=1, keepdims=True) + w_inter * jnp.sum(q_h * n_h, axis=1, keepdims=True)
        hh = num / jnp.maximum(jnp.abs(den), jnp.exp(-m_t))

        hn = hh * lax.rsqrt(jnp.mean(hh * hh, axis=-1, keepdims=True) + EPS)
        hn = hn * hn_ref[:, h * MLSTM_V:(h + 1) * MLSTM_V]
        y = hn * _sigmoid(o_ref[:, h * MLSTM_V:(h + 1) * MLSTM_V])
        y_ref[:, h * MLSTM_V:(h + 1) * MLSTM_V] = y.astype(y_ref.dtype)

        b_last = b_c[L - 1:L, :]
        a_log = b_last - b_c + i_c
        m_new = jnp.maximum(b_last + m_prev, jnp.max(a_log, axis=0, keepdims=True))
        w_a = jnp.exp(a_log - m_new)
        decay = jnp.exp(b_last + m_prev - m_new)
        kw = k_h * w_a
        c_sc[h] = decay * c_h + lax.dot_general(kw.astype(BF16), v_h, (((0,), (0,)), ((), ())),
                                                preferred_element_type=F32)
        n_sc[h:h + 1, :] = decay * n_h + jnp.sum(kw, axis=0, keepdims=True)
        m_sc[h:h + 1, :] = jnp.broadcast_to(m_new, (1, LANES))


def _hgrn_block(q_ref, f_ref, v_ref, og_ref, lbl_ref, hn_ref, y_ref, g_sc, k_sc, q_sc, gr_sc, st_sc,
                before_head):
    TB = MIX_TB
    H = HGRN_HEADS
    DK = HGRN_DK
    W = H * DK
    NT = TB // 8

    lbl = lbl_ref[...]
    lmax = jnp.max(lbl, axis=0, keepdims=True)
    le = jnp.exp(lbl - lmax)
    lb = le[1:2, :] / jnp.sum(le, axis=0, keepdims=True)

    f = lb + (1.0 - lb) * _sigmoid(f_ref[...])
    k_sc[...] = 1.0 - f
    qraw = q_ref[...]
    q_sc[...] = qraw * _sigmoid(qraw) * (DK ** -0.5)
    r_i = lax.broadcasted_iota(jnp.int32, (TB, TB), 0)
    c_i = lax.broadcasted_iota(jnp.int32, (TB, TB), 1)
    tri = jnp.where(r_i >= c_i, 1.0, 0.0).astype(BF16)
    g = _tri_cumsum(tri, jnp.log(f) * LOG2_E)
    g_sc[...] = g

    g3 = g.reshape(NT, 8, W)
    sub = lax.broadcasted_iota(jnp.int32, (NT, 8, W), 1)
    bit0 = (sub & 1) != 0
    bit1 = (sub & 2) != 0
    bit2 = (sub & 4) != 0
    last2 = jnp.where(bit0, g3, pltpu.roll(g3, 7, 1))
    last4 = jnp.where(bit1, last2, pltpu.roll(last2, 6, 1))
    last4_r = pltpu.roll(last4, 4, 1)
    last8 = jnp.where(bit2, last4, last4_r).reshape(TB, W)
    gr_sc[0] = jnp.where(bit0, pltpu.roll(g3, 1, 1), g3).reshape(TB, W)
    gr_sc[1] = jnp.where(bit1, pltpu.roll(last2, 2, 1), last2).reshape(TB, W)
    gr_sc[2] = jnp.where(bit2, last4_r, last4).reshape(TB, W)
    for li, lvl in enumerate(HGRN_COARSE_LEVELS):
        groups = []
        for base in range(0, TB, 2 * lvl):
            src = last8[base + lvl - 8:base + lvl, :]
            groups.extend([src] * (2 * lvl // 8))
        gr_sc[3 + li] = jnp.concatenate(groups, axis=0)

    x_i = r_i ^ c_i
    nt_dims = (((1,), (1,)), ((), ()))
    for h in range(H):
        before_head[h]()
        cols = slice(h * DK, (h + 1) * DK)
        q_h = q_sc[:, cols]
        k_h = k_sc[:, cols]
        g_h = g_sc[:, cols]
        v_h = v_ref[:, cols].astype(BF16)

        q_b = q_h.astype(BF16)
        k_b = k_h.astype(BF16)
        a = jnp.broadcast_to(jnp.sum(q_h * k_h, axis=1, keepdims=True), (TB, TB))
        for li in range(HGRN_NUM_LEVELS):
            e = jnp.exp2(_neg_abs(g_h - gr_sc[li, :, cols])).astype(BF16)
            p = lax.dot_general(q_b * e, k_b * e, nt_dims, preferred_element_type=F32)
            a = jnp.where(x_i >= (1 << li), p, a)
        a = jnp.where(r_i >= c_i, a, 0.0)
        o = jnp.dot(a.astype(BF16), v_h, preferred_element_type=F32)

        st = st_sc[h]
        g_last = g_h[TB - 1:TB, :]
        qd = (q_h * jnp.exp2(g_h)).astype(BF16)
        o = o + lax.dot_general(qd, st.astype(BF16), nt_dims, preferred_element_type=F32)
        kd = (k_h * jnp.exp2(g_last - g_h)).astype(BF16)
        st_sc[h] = jnp.exp2(g_last) * st + lax.dot_general(v_h, kd, (((0,), (0,)), ((), ())),
                                                           preferred_element_type=F32)

        on = o * lax.rsqrt(jnp.mean(o * o, axis=-1, keepdims=True) + EPS) * hn_ref[:, cols]
        og = og_ref[:, cols]
        y_ref[:, cols] = (on * (og * _sigmoid(og))).astype(y_ref.dtype)


def _mix_kernel(mq_ref, mk_ref, mv_ref, mo_ref, gcol_ref, grow_ref, cw_ref, cb_ref, bcol_ref, brow_ref,
                mhn_ref, hq_ref, hf_ref, hi_ref, hg_ref, lbl_ref, hhn_ref,
                x_ref, gm_ref, gh_ref, wm_ref, wh_ref, wo_ref,
                out_ref,
                ym_sc, yh_sc, merged_sc, xpad_sc, c_sc, n_sc, m_sc, g_sc, k_sc, q_sc, gr_sc, st_sc,
                *, blocks_per_seq):
    s = pl.program_id(0)
    cur = s % 2
    prv = 1 - cur

    @pl.when(s == 0)
    def _():
        ym_sc[...] = jnp.zeros_like(ym_sc)
        yh_sc[...] = jnp.zeros_like(yh_sc)

    @pl.when(s % blocks_per_seq == 0)
    def _():
        xpad_sc[0:8, :] = jnp.zeros((8, 2 * MLSTM_QK_W), F32)
        c_sc[...] = jnp.zeros_like(c_sc)
        n_sc[...] = jnp.zeros_like(n_sc)
        m_sc[...] = jnp.zeros_like(m_sc)
        st_sc[...] = jnp.zeros_like(st_sc)

    def gate_piece(c):
        def run():
            cols = slice(c * MERGE_GATE_COLS, (c + 1) * MERGE_GATE_COLS)
            pm = jnp.dot(ym_sc[prv], wm_ref[:, cols], preferred_element_type=F32)
            ph = jnp.dot(yh_sc[prv], wh_ref[:, cols], preferred_element_type=F32)
            merged = _sigmoid(gm_ref[:, cols]) * pm + _sigmoid(gh_ref[:, cols]) * ph
            merged_sc[:, cols] = merged.astype(BF16)
        return run

    def out_piece(c):
        def run():
            cols = slice(c * MERGE_OUT_COLS, (c + 1) * MERGE_OUT_COLS)
            out_ref[:, cols] = x_ref[:, cols] + jnp.dot(merged_sc[...], wo_ref[:, cols],
                                                        preferred_element_type=F32)
        return run

    pieces = ([gate_piece(c) for c in range(D_MODEL // MERGE_GATE_COLS)]
              + [out_piece(c) for c in range(D_MODEL // MERGE_OUT_COLS)])
    n_heads = MLSTM_HEADS + HGRN_HEADS
    n_slots = MIX_SUBBLOCKS * n_heads
    slot_of = [(i * n_slots) // len(pieces) for i in range(len(pieces))]
    hooks = [(pieces[slot_of.index(i)] if i in slot_of else (lambda: None)) for i in range(n_slots)]

    for sub in range(MIX_SUBBLOCKS):
        rows = pl.ds(sub * MIX_TB, MIX_TB)
        sub_hooks = hooks[sub * n_heads:(sub + 1) * n_heads]
        _mlstm_block(mq_ref.at[rows], mk_ref.at[rows], mv_ref.at[rows], mo_ref.at[rows], gcol_ref.at[rows],
                     grow_ref.at[:, rows], cw_ref, cb_ref, bcol_ref, brow_ref, mhn_ref,
                     ym_sc.at[cur, rows], xpad_sc, c_sc, n_sc, m_sc, sub_hooks[:MLSTM_HEADS])
        _hgrn_block(hq_ref.at[rows], hf_ref.at[rows], hi_ref.at[rows], hg_ref.at[rows], lbl_ref, hhn_ref,
                    yh_sc.at[cur, rows], g_sc, k_sc, q_sc, gr_sc, st_sc, sub_hooks[MLSTM_HEADS:])


def _const_spec(shape):
    return pl.BlockSpec(shape, lambda *_: (0,) * len(shape), pipeline_mode=pl.Buffered(1))


def _mix(x2d, proj, gates_col, gates_row, conv_w, conv_b, bias_col, bias_row, m_head_norm,
         lb_logits, h_head_norm, w_m, w_h, w_o, *, seq_len):
    n, d = x2d.shape
    tb = MIX_TB * MIX_SUBBLOCKS
    nblk = n // tb
    qw, vw, hw = MLSTM_QK_W, MLSTM_V_W, HGRN_W

    def cur(col_block):
        return lambda s: (jnp.minimum(s, nblk - 1), col_block)

    def prev(col_block):
        return lambda s: (jnp.maximum(s - 1, 0), col_block)

    return pl.pallas_call(
        functools.partial(_mix_kernel, blocks_per_seq=seq_len // tb),
        grid=(nblk + 1,),
        in_specs=[
            pl.BlockSpec((tb, qw), cur(COL_MQ // qw)),
            pl.BlockSpec((tb, qw), cur(COL_MK // qw)),
            pl.BlockSpec((tb, vw), cur(COL_MV // vw)),
            pl.BlockSpec((tb, vw), cur(COL_MO // vw)),
            pl.BlockSpec((tb, LANES), cur(0)),
            pl.BlockSpec((2 * MLSTM_HEADS, tb), lambda s: (0, jnp.minimum(s, nblk - 1))),
            _const_spec((CONV_WIDTH, 2 * qw)),
            _const_spec((1, 2 * qw)),
            _const_spec((1, LANES)),
            _const_spec((2 * MLSTM_HEADS, 1)),
            _const_spec((1, vw)),
            pl.BlockSpec((tb, hw), cur(COL_HQ // hw)),
            pl.BlockSpec((tb, hw), cur(COL_HF // hw)),
            pl.BlockSpec((tb, hw), cur(COL_HI // hw)),
            pl.BlockSpec((tb, hw), cur(COL_HG // hw)),
            _const_spec((2, hw)),
            _const_spec((1, hw)),
            pl.BlockSpec((tb, d), prev(0)),
            pl.BlockSpec((tb, d), prev(COL_GM // d)),
            pl.BlockSpec((tb, d), prev(COL_GH // d)),
            _const_spec(w_m.shape),
            _const_spec(w_h.shape),
            _const_spec(w_o.shape),
        ],
        out_specs=pl.BlockSpec((tb, d), prev(0)),
        out_shape=jax.ShapeDtypeStruct((n, d), F32),
        scratch_shapes=[
            pltpu.VMEM((2, tb, vw), BF16),
            pltpu.VMEM((2, tb, hw), BF16),
            pltpu.VMEM((tb, d), BF16),
            pltpu.VMEM((MIX_TB + 8, 2 * qw), F32),
            pltpu.VMEM((MLSTM_HEADS, MLSTM_QK, MLSTM_V), F32),
            pltpu.VMEM((8, MLSTM_QK), F32),
            pltpu.VMEM((8, LANES), F32),
            pltpu.VMEM((MIX_TB, hw), F32),
            pltpu.VMEM((MIX_TB, hw), F32),
            pltpu.VMEM((MIX_TB, hw), F32),
            pltpu.VMEM((HGRN_NUM_LEVELS, MIX_TB, hw), F32),
            pltpu.VMEM((HGRN_HEADS, HGRN_DV, HGRN_DK), F32),
        ],
        compiler_params=_params("arbitrary"),
        name="mix",
    )(proj, proj, proj, proj, gates_col, gates_row, conv_w, conv_b, bias_col, bias_row, m_head_norm,
      proj, proj, proj, proj, lb_logits, h_head_norm, x2d, proj, proj, w_m, w_h, w_o)


def _xattn_prep_kernel(m_ref, g_ref, wk_ref, wv_ref, wq_ref, wo_ref, qk_ref, vo_ref):
    wk = wk_ref[...].astype(BF16)
    wv = wv_ref[...].astype(BF16)
    wq = wq_ref[...].astype(BF16)
    wo = wo_ref[...].astype(BF16)
    for i in range(m_ref.shape[0]):
        m = _rms(m_ref[i], g_ref[...]).astype(BF16)
        k = jnp.dot(m, wk, preferred_element_type=F32).astype(BF16)
        v = jnp.dot(m, wv, preferred_element_type=F32).astype(BF16)
        qk = lax.dot_general(wq, k, (((1,), (1,)), ((), ())), preferred_element_type=F32)
        qk_ref[i] = (qk * (XATTN_HEAD_DIM ** -0.5)).astype(qk_ref.dtype)
        vo_ref[i] = jnp.dot(v, wo, preferred_element_type=F32).astype(vo_ref.dtype)


def _xattn_prep(mem, g, w_kv, w_q, w_o):
    b, m, d = mem.shape
    hd = XATTN_HEAD_DIM
    nh = XATTN_HEADS
    return pl.pallas_call(
        _xattn_prep_kernel,
        grid=(nh,),
        in_specs=[
            pl.BlockSpec((b, m, d), lambda h: (0, 0, 0)),
            pl.BlockSpec((1, d), lambda h: (0, 0)),
            pl.BlockSpec((d, hd), lambda h: (0, h)),
            pl.BlockSpec((d, hd), lambda h: (0, nh + h)),
            pl.BlockSpec((d, hd), lambda h: (0, h)),
            pl.BlockSpec((hd, d), lambda h: (h, 0)),
        ],
        out_specs=[
            pl.BlockSpec((b, d, m), lambda h: (0, 0, h)),
            pl.BlockSpec((b, m, d), lambda h: (0, h, 0)),
        ],
        out_shape=[
            jax.ShapeDtypeStruct((b, d, nh * m), BF16),
            jax.ShapeDtypeStruct((b, nh * m, d), BF16),
        ],
        compiler_params=_params("arbitrary"),
        name="xattn_prep",
    )(mem, g.reshape(1, d), w_kv, w_kv, w_q, w_o)


def _xattn_kernel(x_ref, g_ref, qk_ref, vo_ref, o_ref, *, mem_len):
    x = x_ref[0]
    h = _rms(x, g_ref[...]).astype(BF16)
    s = jnp.dot(h, qk_ref[0], preferred_element_type=F32)
    probs = []
    for hd in range(XATTN_HEADS):
        s_h = s[:, hd * mem_len:(hd + 1) * mem_len]
        e = jnp.exp(s_h - jnp.max(s_h, axis=-1, keepdims=True))
        probs.append((e / jnp.sum(e, axis=-1, keepdims=True)).astype(BF16))
    p = jnp.concatenate(probs, axis=-1)
    o_ref[0] = x + jnp.dot(p, vo_ref[0], preferred_element_type=F32)


def _xattn(x, g, qk, vo, *, tm=512):
    b, t, d = x.shape
    hm = qk.shape[2]
    return pl.pallas_call(
        functools.partial(_xattn_kernel, mem_len=hm // XATTN_HEADS),
        grid=(b, t // tm),
        in_specs=[
            pl.BlockSpec((1, tm, d), lambda i, c: (i, c, 0)),
            _const_spec((1, d)),
            pl.BlockSpec((1, d, hm), lambda i, c: (i, 0, 0)),
            pl.BlockSpec((1, hm, d), lambda i, c: (i, 0, 0)),
        ],
        out_specs=pl.BlockSpec((1, tm, d), lambda i, c: (i, c, 0)),
        out_shape=jax.ShapeDtypeStruct((b, t, d), F32),
        compiler_params=_params("parallel", "parallel"),
        name="xattn",
    )(x, g.reshape(1, d), qk, vo)


def kernel(x, mem, norm_ffn1, ffn1_w1, ffn1_w3, ffn1_w2, norm_mix, w_in, mlstm_conv_w, mlstm_conv_b,
           mlstm_ig_bias, mlstm_fg_bias, mlstm_head_norm, hgrn_lb_logits, hgrn_head_norm, w_proj_m,
           w_proj_h, w_out, norm_xattn, norm_mem, xattn_wq, xattn_wkv, xattn_wo, norm_ffn2, ffn2_w1,
           ffn2_w3, ffn2_w2, norm_final):
    b, t, d = x.shape
    depth = norm_ffn1.shape[0]
    assert depth == 1 and hgrn_lb_logits.shape[0] == 2
    n = b * t
    l = 0
    bf = lambda w: w.astype(BF16)

    w_in_t = jnp.swapaxes(w_in[l], 0, 1)
    n_gate = 2 * MLSTM_HEADS
    gate_bias = jnp.concatenate([mlstm_ig_bias[l], mlstm_fg_bias[l]]).astype(F32)
    bias_col = jnp.pad(gate_bias, (0, LANES - n_gate)).reshape(1, LANES)
    bias_row = gate_bias.reshape(n_gate, 1)

    x1, hn, gates_col, gates_row = _ffn(x.reshape(n, d), norm_ffn1[l], ffn1_w1[l], ffn1_w3[l], ffn1_w2[l],
                                        norm_mix[l], w_in_t)

    proj = _in_proj(hn, w_in_t)
    x2 = _mix(x1, proj, gates_col, gates_row, mlstm_conv_w[l], mlstm_conv_b[l].reshape(1, -1),
              bias_col, bias_row, mlstm_head_norm[l].reshape(1, -1), hgrn_lb_logits,
              hgrn_head_norm[l].reshape(1, -1), bf(w_proj_m[l]), bf(w_proj_h[l]), bf(w_out[l]), seq_len=t)

    qk, vo = _xattn_prep(mem, norm_mem[l], xattn_wkv[l], xattn_wq[l], xattn_wo[l])
    x3 = _xattn(x2.reshape(b, t, d), norm_xattn[l], qk, vo)

    (out,) = _ffn(x3.reshape(n, d), norm_ffn2[l], ffn2_w1[l], ffn2_w3[l], ffn2_w2[l], norm_final)
    return out.reshape(b, t, d)
```

```python
import functools

import jax
import jax.numpy as jnp
from jax import lax
from jax.experimental import pallas as pl
from jax.experimental.pallas import tpu as pltpu

F32 = jnp.float32
BF16 = jnp.bfloat16
EPS = 1e-6
LOG2_E = 1.4426950408889634

D_MODEL = 2048
D_FF = 5632
MLSTM_HEADS = 4
MLSTM_QK = 128
MLSTM_V = 256
CONV_WIDTH = 4
HGRN_HEADS = 8
HGRN_DK = 128
HGRN_DV = 128
XATTN_HEADS = 4
XATTN_HEAD_DIM = D_MODEL // XATTN_HEADS

MLSTM_QK_W = MLSTM_HEADS * MLSTM_QK
MLSTM_V_W = MLSTM_HEADS * MLSTM_V
HGRN_W = HGRN_HEADS * HGRN_DK

VMEM_LIMIT_BYTES = 60000 * 1024
LANES = 128

COL_GM = 0
COL_GH = COL_GM + D_MODEL
COL_MQ = COL_GH + D_MODEL
COL_MK = COL_MQ + MLSTM_QK_W
COL_MV = COL_MK + MLSTM_QK_W
COL_MO = COL_MV + MLSTM_V_W
COL_HQ = COL_MO + MLSTM_V_W
COL_HF = COL_HQ + HGRN_W
COL_HI = COL_HF + HGRN_W
COL_HG = COL_HI + HGRN_W
PROJ_W = COL_HG + HGRN_W

MIX_GATE_ROW = 2 * MLSTM_QK_W + 2 * MLSTM_V_W
MIX_HGRN_ROW = MIX_GATE_ROW + 2 * MLSTM_HEADS
MIX_MERGE_ROW = MIX_HGRN_ROW + 4 * HGRN_W

PROJ_SUB_ROWS = 1024
FFN_NORM_ROWS = 128
FFN_FINISH_ROWS = 128
FFN_SUB_COLS = 256

MIX_TB = 128
HGRN_NUM_LEVELS = MIX_TB.bit_length() - 1
MIX_SUBBLOCKS = 2
MERGE_GATE_COLS = 512
MERGE_OUT_COLS = 256
HGRN_COARSE_LEVELS = tuple(1 << i for i in range(3, HGRN_NUM_LEVELS))


def _rms(x, g):
    return x * lax.rsqrt(jnp.mean(x * x, axis=-1, keepdims=True) + EPS) * g


def _sigmoid(x):
    return 1.0 / (1.0 + jnp.exp(-x))


def _log_sigmoid(x):
    return jnp.minimum(x, 0.0) - jnp.log(1.0 + jnp.exp(-jnp.abs(x)))


def _split3(x):
    hi = x.astype(BF16)
    r1 = x - hi.astype(F32)
    mid = r1.astype(BF16)
    lo = (r1 - mid.astype(F32)).astype(BF16)
    return hi, mid, lo


def _tri_cumsum(tri, x):
    return sum(jnp.dot(tri, t, preferred_element_type=F32) for t in _split3(x))


def _neg_abs(x):
    return pltpu.bitcast(pltpu.bitcast(x, jnp.uint32) | jnp.uint32(0x80000000), F32)


def _params(*sem):
    return pltpu.CompilerParams(dimension_semantics=sem, vmem_limit_bytes=VMEM_LIMIT_BYTES)


def _ffn_kernel(*refs, mode):
    if mode == "mix":
        (x_hbm, g_ref, w1_ref, w3_ref, w2_ref, gn_ref, wg_ref,
         o_hbm, hn_hbm, gcol_ref, grow_ref, h_sc, x_buf, acc_sc, x_sem, o_sem, hn_sc, hn_sem) = refs
    else:
        x_hbm, g_ref, w1_ref, w3_ref, w2_ref, gn_ref, o_hbm, h_sc, x_buf, acc_sc, x_sem, o_sem = refs
    i = pl.program_id(0)
    j = pl.program_id(1)
    n_i = pl.num_programs(0)
    tm = x_buf.shape[0]
    n_acc = acc_sc.shape[0]

    def acc_slot(tile):
        return acc_sc.at[tile % n_acc if n_acc > 1 else 0]

    def tile_rows(tile):
        return pl.ds(pl.multiple_of(tile * tm, tm), tm)

    def x_copy(tile):
        return pltpu.make_async_copy(x_hbm.at[tile_rows(tile), :], x_buf, x_sem)

    def o_copy(tile):
        return pltpu.make_async_copy(acc_slot(tile), o_hbm.at[tile_rows(tile), :], o_sem)

    def hn_copy(tile):
        return pltpu.make_async_copy(hn_sc, hn_hbm.at[tile_rows(tile), :], hn_sem)

    def row_chunk(r, size):
        return pl.ds(pl.multiple_of(r * size, size), size)

    acc_ref = acc_slot(i)

    def swiglu_passes(first):
        h = h_sc[...]
        for c in range(w1_ref.shape[1] // FFN_SUB_COLS):
            cols = slice(c * FFN_SUB_COLS, (c + 1) * FFN_SUB_COLS)
            a = jnp.dot(h, w1_ref[:, cols].astype(BF16), preferred_element_type=F32)
            b = jnp.dot(h, w3_ref[:, cols].astype(BF16), preferred_element_type=F32)
            act = (a * _sigmoid(a) * b).astype(BF16)
            part = jnp.dot(act, w2_ref[cols, :].astype(BF16), preferred_element_type=F32)
            if first and c == 0:
                acc_ref[...] = 2.0 * x_buf[...] + part
            else:
                acc_ref[...] += part

    @pl.when(j == 0)
    def _():
        @pl.when(i == 0)
        def _():
            x_copy(0).start()

        x_copy(i).wait()

        def norm_rows(r, carry):
            rows = row_chunk(r, FFN_NORM_ROWS)
            h_sc[rows, :] = _rms(x_buf[rows, :], g_ref[...]).astype(BF16)
            return carry
        lax.fori_loop(0, tm // FFN_NORM_ROWS, norm_rows, 0)

        if n_acc == 1:
            @pl.when(i > 0)
            def _():
                o_copy(i - 1).wait()

        swiglu_passes(first=True)

    @pl.when(j > 0)
    def _():
        swiglu_passes(first=False)

    @pl.when(jnp.logical_and(j == 1, i + 1 < n_i))
    def _():
        x_copy(i + 1).start()

    @pl.when(j == pl.num_programs(1) - 1)
    def _():
        if mode == "final":
            def finish_rows(r, carry):
                rows = row_chunk(r, FFN_FINISH_ROWS)
                acc_ref[rows, :] = _rms(0.5 * acc_ref[rows, :], gn_ref[...])
                return carry
            lax.fori_loop(0, tm // FFN_FINISH_ROWS, finish_rows, 0)
        else:
            @pl.when(i > 0)
            def _():
                hn_copy(i - 1).wait()

            y = 0.5 * acc_ref[...]
            acc_ref[...] = y
            hn = _rms(y, gn_ref[...]).astype(BF16)
            hn_sc[...] = hn
            nt_dims = (((1,), (1,)), ((), ()))
            wg = wg_ref[...].astype(BF16)
            wg_pad = jnp.concatenate([wg, jnp.zeros((LANES - wg.shape[0], wg.shape[1]), BF16)], axis=0)
            gcol = lax.dot_general(hn, wg_pad, nt_dims, preferred_element_type=F32)
            gcol_ref[...] = gcol
            grow_ref[...] = gcol.T[0:wg.shape[0], :]

        if n_acc > 1:
            @pl.when(i > 0)
            def _():
                o_copy(i - 1).wait()

        o_copy(i).start()
        if mode == "mix":
            hn_copy(i).start()

        @pl.when(i == n_i - 1)
        def _():
            o_copy(i).wait()
            if mode == "mix":
                hn_copy(i).wait()


def _ffn(x2d, g, w1, w3, w2, g_next, w_gates_t=None, *, tm=1024, tf=512):
    n, d = x2d.shape
    f = w1.shape[1]
    mode = "final" if w_gates_t is None else "mix"
    in_specs = [
        pl.BlockSpec(memory_space=pl.ANY),
        pl.BlockSpec((1, d), lambda i, j: (0, 0)),
        pl.BlockSpec((d, tf), lambda i, j: (0, j)),
        pl.BlockSpec((d, tf), lambda i, j: (0, j)),
        pl.BlockSpec((tf, d), lambda i, j: (j, 0)),
        pl.BlockSpec((1, d), lambda i, j: (0, 0)),
    ]
    args = [x2d, g.reshape(1, d), w1, w3, w2, g_next.reshape(1, d)]
    out_specs = [pl.BlockSpec(memory_space=pl.ANY)]
    out_shape = [jax.ShapeDtypeStruct((n, d), F32)]
    scratch_shapes = [
        pltpu.VMEM((tm, d), BF16),
        pltpu.VMEM((tm, d), F32),
        pltpu.VMEM((2 if mode == "final" else 1, tm, d), F32),
        pltpu.SemaphoreType.DMA(()),
        pltpu.SemaphoreType.DMA(()),
    ]
    if mode == "mix":
        ng = 2 * MLSTM_HEADS
        in_specs.append(pl.BlockSpec((ng, d), lambda i, j: (MIX_GATE_ROW // ng, 0)))
        args.append(w_gates_t)
        out_specs += [
            pl.BlockSpec(memory_space=pl.ANY),
            pl.BlockSpec((tm, LANES), lambda i, j: (i, 0)),
            pl.BlockSpec((ng, tm), lambda i, j: (0, i)),
        ]
        out_shape += [
            jax.ShapeDtypeStruct((n, d), BF16),
            jax.ShapeDtypeStruct((n, LANES), F32),
            jax.ShapeDtypeStruct((ng, n), F32),
        ]
        scratch_shapes += [pltpu.VMEM((tm, d), BF16), pltpu.SemaphoreType.DMA(())]
    return pl.pallas_call(
        functools.partial(_ffn_kernel, mode=mode),
        grid=(n // tm, f // tf),
        in_specs=in_specs,
        out_specs=out_specs,
        out_shape=out_shape,
        scratch_shapes=scratch_shapes,
        compiler_params=_params("arbitrary", "arbitrary"),
        name="ffn_" + mode,
    )(*args)


def _in_proj_kernel(h_ref, wt_ref, p_ref, w_sc):
    @pl.when(pl.program_id(1) == 0)
    def _():
        w_sc[...] = wt_ref[...].astype(BF16)

    for r in range(0, h_ref.shape[0], PROJ_SUB_ROWS):
        rows = slice(r, r + PROJ_SUB_ROWS)
        p_ref[rows, :] = lax.dot_general(h_ref[rows, :], w_sc[...], (((1,), (1,)), ((), ())),
                                         preferred_element_type=F32)


def _proj_row_offset(j, tn):
    n_gate_tiles = 2 * D_MODEL // tn
    n_mlstm_tiles = MIX_GATE_ROW // tn
    off8 = jnp.where(j < n_gate_tiles, MIX_MERGE_ROW // 8 + j * (tn // 8),
                     jnp.where(j < n_gate_tiles + n_mlstm_tiles, (j - n_gate_tiles) * (tn // 8),
                               MIX_HGRN_ROW // 8 + (j - n_gate_tiles - n_mlstm_tiles) * (tn // 8)))
    return off8 * 8


def _in_proj(hn, w_t, *, tm=2048, tn=1024):
    n, d = hn.shape
    return pl.pallas_call(
        _in_proj_kernel,
        grid=(PROJ_W // tn, n // tm),
        in_specs=[
            pl.BlockSpec((tm, d), lambda j, i: (i, 0)),
            pl.BlockSpec((pl.Element(tn), pl.Element(d)), lambda j, i: (_proj_row_offset(j, tn), 0)),
        ],
        out_specs=pl.BlockSpec((tm, tn), lambda j, i: (i, j)),
        out_shape=jax.ShapeDtypeStruct((n, PROJ_W), F32),
        scratch_shapes=[pltpu.VMEM((tn, d), BF16)],
        compiler_params=_params("arbitrary", "arbitrary"),
        name="in_proj",
    )(hn, w_t)


def _mlstm_block(q_ref, k_ref, v_ref, o_ref, gcol_ref, grow_ref, cw_ref, cb_ref, bcol_ref, brow_ref,
                 hn_ref, y_ref, xpad_sc, c_sc, n_sc, m_sc, before_head):
    L = MIX_TB
    H = MLSTM_HEADS

    xpad_sc[8:8 + L, 0:MLSTM_QK_W] = q_ref[...]
    xpad_sc[8:8 + L, MLSTM_QK_W:2 * MLSTM_QK_W] = k_ref[...]
    acc = jnp.broadcast_to(cb_ref[...], (L, 2 * MLSTM_QK_W))
    for j in range(CONV_WIDTH):
        off = 8 - (CONV_WIDTH - 1) + j
        acc = acc + cw_ref[j:j + 1, :] * xpad_sc[off:off + L, :]
    xpad_sc[0:8, :] = xpad_sc[L:L + 8, :]
    qk = acc * _sigmoid(acc)

    gcol = gcol_ref[...] + bcol_ref[...]
    grow = grow_ref[...] + brow_ref[...]
    fcol = _log_sigmoid(gcol)
    r_i = lax.broadcasted_iota(jnp.int32, (L, L), 0)
    c_i = lax.broadcasted_iota(jnp.int32, (L, L), 1)
    causal = r_i >= c_i
    tril = jnp.where(causal, 1.0, 0.0).astype(BF16)
    bcol = _tri_cumsum(tril, fcol)
    brow = bcol.T

    for h in range(H):
        before_head[h]()
        b_c = bcol[:, H + h:H + h + 1]
        i_c = gcol[:, h:h + 1]
        b_r = brow[H + h:H + h + 1, :]
        i_r = grow[h:h + 1, :]
        m_prev = m_sc[h:h + 1, 0:1]

        d_log = jnp.where(causal, b_c - (b_r - i_r), -jnp.inf)
        inter_log = b_c + m_prev
        m_t = jnp.maximum(jnp.max(d_log, axis=1, keepdims=True), inter_log)

        q_h = qk[:, h * MLSTM_QK:(h + 1) * MLSTM_QK]
        k_h = qk[:, MLSTM_QK_W + h * MLSTM_QK:MLSTM_QK_W + (h + 1) * MLSTM_QK] * (MLSTM_QK ** -0.5)
        v_h = v_ref[:, h * MLSTM_V:(h + 1) * MLSTM_V].astype(BF16)
        q_b = q_h.astype(BF16)

        s = lax.dot_general(q_b, k_h.astype(BF16), (((1,), (1,)), ((), ())), preferred_element_type=F32)
        s = s * jnp.exp(d_log - m_t)
        w_inter = jnp.exp(inter_log - m_t)
        c_h = c_sc[h]
        n_h = n_sc[h:h + 1, :]
        lhs = jnp.concatenate([s.astype(BF16), (q_h * w_inter).astype(BF16)], axis=1)
        rhs = jnp.concatenate([v_h, c_h.astype(BF16)], axis=0)
        num = jnp.dot(lhs, rhs, preferred_element_type=F32)
        den = jnp.sum(s, axis=1, keepdims=True) + w_inter * jnp.sum(q_h * n_h, axis=1, keepdims=True)
        hh = num / jnp.maximum(jnp.abs(den), jnp.exp(-m_t))

        hn = hh * lax.rsqrt(jnp.mean(hh * hh, axis=-1, keepdims=True) + EPS)
        hn = hn * hn_ref[:, h * MLSTM_V:(h + 1) * MLSTM_V]
        y = hn * _sigmoid(o_ref[:, h * MLSTM_V:(h + 1) * MLSTM_V])
        y_ref[:, h * MLSTM_V:(h + 1) * MLSTM_V] = y.astype(y_ref.dtype)

        b_last = b_c[L - 1:L, :]
        a_log = b_last - b_c + i_c
        m_new = jnp.maximum(b_last + m_prev, jnp.max(a_log, axis=0, keepdims=True))
        w_a = jnp.exp(a_log - m_new)
        decay = jnp.exp(b_last + m_prev - m_new)
        kw = k_h * w_a
        c_sc[h] = decay * c_h + lax.dot_general(kw.astype(BF16), v_h, (((0,), (0,)), ((), ())),
                                                preferred_element_type=F32)
        n_sc[h:h + 1, :] = decay * n_h + jnp.sum(kw, axis=0, keepdims=True)
        m_sc[h:h + 1, :] = jnp.broadcast_to(m_new, (1, LANES))


def _hgrn_block(q_ref, f_ref, v_ref, og_ref, lbl_ref, hn_ref, y_ref, g_sc, k_sc, q_sc, gr_sc, st_sc,
                before_head):
    TB = MIX_TB
    H = HGRN_HEADS
    DK = HGRN_DK
    W = H * DK
    NT = TB // 8

    lbl = lbl_ref[...]
    lmax = jnp.max(lbl, axis=0, keepdims=True)
    le = jnp.exp(lbl - lmax)
    lb = le[1:2, :] / jnp.sum(le, axis=0, keepdims=True)

    f = lb + (1.0 - lb) * _sigmoid(f_ref[...])
    k_sc[...] = 1.0 - f
    qraw = q_ref[...]
    q_sc[...] = qraw * _sigmoid(qraw) * (DK ** -0.5)
    r_i = lax.broadcasted_iota(jnp.int32, (TB, TB), 0)
    c_i = lax.broadcasted_iota(jnp.int32, (TB, TB), 1)
    tri = jnp.where(r_i >= c_i, 1.0, 0.0).astype(BF16)
    g = _tri_cumsum(tri, jnp.log(f) * LOG2_E)
    g_sc[...] = g

    g3 = g.reshape(NT, 8, W)
    sub = lax.broadcasted_iota(jnp.int32, (NT, 8, W), 1)
    bit0 = (sub & 1) != 0
    bit1 = (sub & 2) != 0
    bit2 = (sub & 4) != 0
    last2 = jnp.where(bit0, g3, pltpu.roll(g3, 7, 1))
    last4 = jnp.where(bit1, last2, pltpu.roll(last2, 6, 1))
    last4_r = pltpu.roll(last4, 4, 1)
    last8 = jnp.where(bit2, last4, last4_r).reshape(TB, W)
    gr_sc[0] = jnp.where(bit0, pltpu.roll(g3, 1, 1), g3).reshape(TB, W)
    gr_sc[1] = jnp.where(bit1, pltpu.roll(last2, 2, 1), last2).reshape(TB, W)
    gr_sc[2] = jnp.where(bit2, last4_r, last4).reshape(TB, W)
    for li, lvl in enumerate(HGRN_COARSE_LEVELS):
        groups = []
        for base in range(0, TB, 2 * lvl):
            src = last8[base + lvl - 8:base + lvl, :]
            groups.extend([src] * (2 * lvl // 8))
        gr_sc[3 + li] = jnp.concatenate(groups, axis=0)

    x_i = r_i ^ c_i
    nt_dims = (((1,), (1,)), ((), ()))
    for h in range(H):
        before_head[h]()
        cols = slice(h * DK, (h + 1) * DK)
        q_h = q_sc[:, cols]
        k_h = k_sc[:, cols]
        g_h = g_sc[:, cols]
        v_h = v_ref[:, cols].astype(BF16)

        q_b = q_h.astype(BF16)
        k_b = k_h.astype(BF16)
        a = jnp.broadcast_to(jnp.sum(q_h * k_h, axis=1, keepdims=True), (TB, TB))
        for li in range(HGRN_NUM_LEVELS):
            e = jnp.exp2(_neg_abs(g_h - gr_sc[li, :, cols])).astype(BF16)
            p = lax.dot_general(q_b * e, k_b * e, nt_dims, preferred_element_type=F32)
            a = jnp.where(x_i >= (1 << li), p, a)
        a = jnp.where(r_i >= c_i, a, 0.0)
        o = jnp.dot(a.astype(BF16), v_h, preferred_element_type=F32)

        st = st_sc[h]
        g_last = g_h[TB - 1:TB, :]
        qd = (q_h * jnp.exp2(g_h)).astype(BF16)
        o = o + lax.dot_general(qd, st.astype(BF16), nt_dims, preferred_element_type=F32)
        kd = (k_h * jnp.exp2(g_last - g_h)).astype(BF16)
        st_sc[h] = jnp.exp2(g_last) * st + lax.dot_general(v_h, kd, (((0,), (0,)), ((), ())),
                                                           preferred_element_type=F32)

        on = o * lax.rsqrt(jnp.mean(o * o, axis=-1, keepdims=True) + EPS) * hn_ref[:, cols]
        og = og_ref[:, cols]
        y_ref[:, cols] = (on * (og * _sigmoid(og))).astype(y_ref.dtype)


def _mix_kernel(mq_ref, mk_ref, mv_ref, mo_ref, gcol_ref, grow_ref, cw_ref, cb_ref, bcol_ref, brow_ref,
                mhn_ref, hq_ref, hf_ref, hi_ref, hg_ref, lbl_ref, hhn_ref,
                x_ref, gm_ref, gh_ref, wm_ref, wh_ref, wo_ref,
                out_ref,
                ym_sc, yh_sc, merged_sc, xpad_sc, c_sc, n_sc, m_sc, g_sc, k_sc, q_sc, gr_sc, st_sc,
                *, blocks_per_seq):
    s = pl.program_id(0)
    cur = s % 2
    prv = 1 - cur

    @pl.when(s == 0)
    def _():
        ym_sc[...] = jnp.zeros_like(ym_sc)
        yh_sc[...] = jnp.zeros_like(yh_sc)

    @pl.when(s % blocks_per_seq == 0)
    def _():
        xpad_sc[0:8, :] = jnp.zeros((8, 2 * MLSTM_QK_W), F32)
        c_sc[...] = jnp.zeros_like(c_sc)
        n_sc[...] = jnp.zeros_like(n_sc)
        m_sc[...] = jnp.zeros_like(m_sc)
        st_sc[...] = jnp.zeros_like(st_sc)

    def gate_piece(c):
        def run():
            cols = slice(c * MERGE_GATE_COLS, (c + 1) * MERGE_GATE_COLS)
            pm = jnp.dot(ym_sc[prv], wm_ref[:, cols], preferred_element_type=F32)
            ph = jnp.dot(yh_sc[prv], wh_ref[:, cols], preferred_element_type=F32)
            merged = _sigmoid(gm_ref[:, cols]) * pm + _sigmoid(gh_ref[:, cols]) * ph
            merged_sc[:, cols] = merged.astype(BF16)
        return run

    def out_piece(c):
        def run():
            cols = slice(c * MERGE_OUT_COLS, (c + 1) * MERGE_OUT_COLS)
            out_ref[:, cols] = x_ref[:, cols] + jnp.dot(merged_sc[...], wo_ref[:, cols],
                                                        preferred_element_type=F32)
        return run

    pieces = ([gate_piece(c) for c in range(D_MODEL // MERGE_GATE_COLS)]
              + [out_piece(c) for c in range(D_MODEL // MERGE_OUT_COLS)])
    n_heads = MLSTM_HEADS + HGRN_HEADS
    n_slots = MIX_SUBBLOCKS * n_heads
    slot_of = [(i * n_slots) // len(pieces) for i in range(len(pieces))]
    hooks = [(pieces[slot_of.index(i)] if i in slot_of else (lambda: None)) for i in range(n_slots)]

    for sub in range(MIX_SUBBLOCKS):
        rows = pl.ds(sub * MIX_TB, MIX_TB)
        sub_hooks = hooks[sub * n_heads:(sub + 1) * n_heads]
        _mlstm_block(mq_ref.at[rows], mk_ref.at[rows], mv_ref.at[rows], mo_ref.at[rows], gcol_ref.at[rows],
                     grow_ref.at[:, rows], cw_ref, cb_ref, bcol_ref, brow_ref, mhn_ref,
                     ym_sc.at[cur, rows], xpad_sc, c_sc, n_sc, m_sc, sub_hooks[:MLSTM_HEADS])
        _hgrn_block(hq_ref.at[rows], hf_ref.at[rows], hi_ref.at[rows], hg_ref.at[rows], lbl_ref, hhn_ref,
                    yh_sc.at[cur, rows], g_sc, k_sc, q_sc, gr_sc, st_sc, sub_hooks[MLSTM_HEADS:])


def _const_spec(shape):
    return pl.BlockSpec(shape, lambda *_: (0,) * len(shape), pipeline_mode=pl.Buffered(1))


def _mix(x2d, proj, gates_col, gates_row, conv_w, conv_b, bias_col, bias_row, m_head_norm,
         lb_logits, h_head_norm, w_m, w_h, w_o, *, seq_len):
    n, d = x2d.shape
    tb = MIX_TB * MIX_SUBBLOCKS
    nblk = n // tb
    qw, vw, hw = MLSTM_QK_W, MLSTM_V_W, HGRN_W

    def cur(col_block):
        return lambda s: (jnp.minimum(s, nblk - 1), col_block)

    def prev(col_block):
        return lambda s: (jnp.maximum(s - 1, 0), col_block)

    return pl.pallas_call(
        functools.partial(_mix_kernel, blocks_per_seq=seq_len // tb),
        grid=(nblk + 1,),
        in_specs=[
            pl.BlockSpec((tb, qw), cur(COL_MQ // qw)),
            pl.BlockSpec((tb, qw), cur(COL_MK // qw)),
            pl.BlockSpec((tb, vw), cur(COL_MV // vw)),
            pl.BlockSpec((tb, vw), cur(COL_MO // vw)),
            pl.BlockSpec((tb, LANES), cur(0)),
            pl.BlockSpec((2 * MLSTM_HEADS, tb), lambda s: (0, jnp.minimum(s, nblk - 1))),
            _const_spec((CONV_WIDTH, 2 * qw)),
            _const_spec((1, 2 * qw)),
            _const_spec((1, LANES)),
            _const_spec((2 * MLSTM_HEADS, 1)),
            _const_spec((1, vw)),
            pl.BlockSpec((tb, hw), cur(COL_HQ // hw)),
            pl.BlockSpec((tb, hw), cur(COL_HF // hw)),
            pl.BlockSpec((tb, hw), cur(COL_HI // hw)),
            pl.BlockSpec((tb, hw), cur(COL_HG // hw)),
            _const_spec((2, hw)),
            _const_spec((1, hw)),
            pl.BlockSpec((tb, d), prev(0)),
            pl.BlockSpec((tb, d), prev(COL_GM // d)),
            pl.BlockSpec((tb, d), prev(COL_GH // d)),
            _const_spec(w_m.shape),
            _const_spec(w_h.shape),
            _const_spec(w_o.shape),
        ],
        out_specs=pl.BlockSpec((tb, d), prev(0)),
        out_shape=jax.ShapeDtypeStruct((n, d), F32),
        scratch_shapes=[
            pltpu.VMEM((2, tb, vw), BF16),
            pltpu.VMEM((2, tb, hw), BF16),
            pltpu.VMEM((tb, d), BF16),
            pltpu.VMEM((MIX_TB + 8, 2 * qw), F32),
            pltpu.VMEM((MLSTM_HEADS, MLSTM_QK, MLSTM_V), F32),
            pltpu.VMEM((8, MLSTM_QK), F32),
            pltpu.VMEM((8, LANES), F32),
            pltpu.VMEM((MIX_TB, hw), F32),
            pltpu.VMEM((MIX_TB, hw), F32),
            pltpu.VMEM((MIX_TB, hw), F32),
            pltpu.VMEM((HGRN_NUM_LEVELS, MIX_TB, hw), F32),
            pltpu.VMEM((HGRN_HEADS, HGRN_DV, HGRN_DK), F32),
        ],
        compiler_params=_params("arbitrary"),
        name="mix",
    )(proj, proj, proj, proj, gates_col, gates_row, conv_w, conv_b, bias_col, bias_row, m_head_norm,
      proj, proj, proj, proj, lb_logits, h_head_norm, x2d, proj, proj, w_m, w_h, w_o)


def _xattn_prep_kernel(m_ref, g_ref, wk_ref, wv_ref, wq_ref, wo_ref, qk_ref, vo_ref):
    wk = wk_ref[...].astype(BF16)
    wv = wv_ref[...].astype(BF16)
    wq = wq_ref[...].astype(BF16)
    wo = wo_ref[...].astype(BF16)
    for i in range(m_ref.shape[0]):
        m = _rms(m_ref[i], g_ref[...]).astype(BF16)
        k = jnp.dot(m, wk, preferred_element_type=F32).astype(BF16)
        v = jnp.dot(m, wv, preferred_element_type=F32).astype(BF16)
        qk = lax.dot_general(wq, k, (((1,), (1,)), ((), ())), preferred_element_type=F32)
        qk_ref[i] = (qk * (XATTN_HEAD_DIM ** -0.5)).astype(qk_ref.dtype)
        vo_ref[i] = jnp.dot(v, wo, preferred_element_type=F32).astype(vo_ref.dtype)


def _xattn_prep(mem, g, w_kv, w_q, w_o):
    b, m, d = mem.shape
    hd = XATTN_HEAD_DIM
    nh = XATTN_HEADS
    return pl.pallas_call(
        _xattn_prep_kernel,
        grid=(nh,),
        in_specs=[
            pl.BlockSpec((b, m, d), lambda h: (0, 0, 0)),
            pl.BlockSpec((1, d), lambda h: (0, 0)),
            pl.BlockSpec((d, hd), lambda h: (0, h)),
            pl.BlockSpec((d, hd), lambda h: (0, nh + h)),
            pl.BlockSpec((d, hd), lambda h: (0, h)),
            pl.BlockSpec((hd, d), lambda h: (h, 0)),
        ],
        out_specs=[
            pl.BlockSpec((b, d, m), lambda h: (0, 0, h)),
            pl.BlockSpec((b, m, d), lambda h: (0, h, 0)),
        ],
        out_shape=[
            jax.ShapeDtypeStruct((b, d, nh * m), BF16),
            jax.ShapeDtypeStruct((b, nh * m, d), BF16),
        ],
        compiler_params=_params("arbitrary"),
        name="xattn_prep",
    )(mem, g.reshape(1, d), w_kv, w_kv, w_q, w_o)


def _xattn_kernel(x_ref, g_ref, qk_ref, vo_ref, o_ref, *, mem_len):
    x = x_ref[0]
    h = _rms(x, g_ref[...]).astype(BF16)
    s = jnp.dot(h, qk_ref[0], preferred_element_type=F32)
    probs = []
    for hd in range(XATTN_HEADS):
        s_h = s[:, hd * mem_len:(hd + 1) * mem_len]
        e = jnp.exp(s_h - jnp.max(s_h, axis=-1, keepdims=True))
        probs.append((e / jnp.sum(e, axis=-1, keepdims=True)).astype(BF16))
    p = jnp.concatenate(probs, axis=-1)
    o_ref[0] = x + jnp.dot(p, vo_ref[0], preferred_element_type=F32)


def _xattn(x, g, qk, vo, *, tm=512):
    b, t, d = x.shape
    hm = qk.shape[2]
    return pl.pallas_call(
        functools.partial(_xattn_kernel, mem_len=hm // XATTN_HEADS),
        grid=(b, t // tm),
        in_specs=[
            pl.BlockSpec((1, tm, d), lambda i, c: (i, c, 0)),
            _const_spec((1, d)),
            pl.BlockSpec((1, d, hm), lambda i, c: (i, 0, 0)),
            pl.BlockSpec((1, hm, d), lambda i, c: (i, 0, 0)),
        ],
        out_specs=pl.BlockSpec((1, tm, d), lambda i, c: (i, c, 0)),
        out_shape=jax.ShapeDtypeStruct((b, t, d), F32),
        compiler_params=_params("parallel", "parallel"),
        name="xattn",
    )(x, g.reshape(1, d), qk, vo)


def kernel(x, mem, norm_ffn1, ffn1_w1, ffn1_w3, ffn1_w2, norm_mix, w_in, mlstm_conv_w, mlstm_conv_b,
           mlstm_ig_bias, mlstm_fg_bias, mlstm_head_norm, hgrn_lb_logits, hgrn_head_norm, w_proj_m,
           w_proj_h, w_out, norm_xattn, norm_mem, xattn_wq, xattn_wkv, xattn_wo, norm_ffn2, ffn2_w1,
           ffn2_w3, ffn2_w2, norm_final):
    b, t, d = x.shape
    depth = norm_ffn1.shape[0]
    assert depth == 1 and hgrn_lb_logits.shape[0] == 2
    n = b * t
    l = 0
    bf = lambda w: w.astype(BF16)

    w_in_t = jnp.swapaxes(w_in[l], 0, 1)
    n_gate = 2 * MLSTM_HEADS
    gate_bias = jnp.concatenate([mlstm_ig_bias[l], mlstm_fg_bias[l]]).astype(F32)
    bias_col = jnp.pad(gate_bias, (0, LANES - n_gate)).reshape(1, LANES)
    bias_row = gate_bias.reshape(n_gate, 1)

    x1, hn, gates_col, gates_row = _ffn(x.reshape(n, d), norm_ffn1[l], ffn1_w1[l], ffn1_w3[l], ffn1_w2[l],
                                        norm_mix[l], w_in_t)

    proj = _in_proj(hn, w_in_t)
    x2 = _mix(x1, proj, gates_col, gates_row, mlstm_conv_w[l], mlstm_conv_b[l].reshape(1, -1),
              bias_col, bias_row, mlstm_head_norm[l].reshape(1, -1), hgrn_lb_logits,
              hgrn_head_norm[l].reshape(1, -1), bf(w_proj_m[l]), bf(w_proj_h[l]), bf(w_out[l]), seq_len=t)

    qk, vo = _xattn_prep(mem, norm_mem[l], xattn_wkv[l], xattn_wq[l], xattn_wo[l])
    x3 = _xattn(x2.reshape(b, t, d), norm_xattn[l], qk, vo)

    (out,) = _ffn(x3.reshape(n, d), norm_ffn2[l], ffn2_w1[l], ffn2_w3[l], ffn2_w2[l], norm_final)
    return out.reshape(b, t, d)
```

```python
import functools

import jax
import jax.numpy as jnp
from jax import lax
from jax.experimental import pallas as pl
from jax.experimental.pallas import tpu as pltpu

F32 = jnp.float32
BF16 = jnp.bfloat16
EPS = 1e-6
LOG2_E = 1.4426950408889634

D_MODEL = 2048
D_FF = 5632
MLSTM_HEADS = 4
MLSTM_QK = 128
MLSTM_V = 256
CONV_WIDTH = 4
HGRN_HEADS = 8
HGRN_DK = 128
HGRN_DV = 128
XATTN_HEADS = 4
XATTN_HEAD_DIM = D_MODEL // XATTN_HEADS

MLSTM_QK_W = MLSTM_HEADS * MLSTM_QK
MLSTM_V_W = MLSTM_HEADS * MLSTM_V
HGRN_W = HGRN_HEADS * HGRN_DK

VMEM_LIMIT_BYTES = 60000 * 1024
LANES = 128

COL_GM = 0
COL_GH = COL_GM + D_MODEL
COL_MQ = COL_GH + D_MODEL
COL_MK = COL_MQ + MLSTM_QK_W
COL_MV = COL_MK + MLSTM_QK_W
COL_MO = COL_MV + MLSTM_V_W
COL_HQ = COL_MO + MLSTM_V_W
COL_HF = COL_HQ + HGRN_W
COL_HI = COL_HF + HGRN_W
COL_HG = COL_HI + HGRN_W
PROJ_W = COL_HG + HGRN_W

MIX_GATE_ROW = 2 * MLSTM_QK_W + 2 * MLSTM_V_W
MIX_HGRN_ROW = MIX_GATE_ROW + 2 * MLSTM_HEADS
MIX_MERGE_ROW = MIX_HGRN_ROW + 4 * HGRN_W

PROJ_SUB_ROWS = 1024
FFN_NORM_ROWS = 128
FFN_FINISH_ROWS = 128
FFN_SUB_COLS = 256

MIX_TB = 128
HGRN_NUM_LEVELS = MIX_TB.bit_length() - 1
MIX_SUBBLOCKS = 2
MERGE_GATE_COLS = 512
MERGE_OUT_COLS = 256
HGRN_COARSE_LEVELS = tuple(1 << i for i in range(3, HGRN_NUM_LEVELS))


def _rms(x, g):
    return x * lax.rsqrt(jnp.mean(x * x, axis=-1, keepdims=True) + EPS) * g


def _sigmoid(x):
    return 1.0 / (1.0 + jnp.exp(-x))


def _log_sigmoid(x):
    return jnp.minimum(x, 0.0) - jnp.log(1.0 + jnp.exp(-jnp.abs(x)))


def _split3(x):
    hi = x.astype(BF16)
    r1 = x - hi.astype(F32)
    mid = r1.astype(BF16)
    lo = (r1 - mid.astype(F32)).astype(BF16)
    return hi, mid, lo


def _tri_cumsum(tri, x):
    return sum(jnp.dot(tri, t, preferred_element_type=F32) for t in _split3(x))


def _neg_abs(x):
    return pltpu.bitcast(pltpu.bitcast(x, jnp.uint32) | jnp.uint32(0x80000000), F32)


def _params(*sem):
    return pltpu.CompilerParams(dimension_semantics=sem, vmem_limit_bytes=VMEM_LIMIT_BYTES)


def _ffn_kernel(*refs, mode):
    if mode == "mix":
        (x_hbm, g_ref, w1_ref, w3_ref, w2_ref, gn_ref, wg_ref,
         o_hbm, hn_hbm, gcol_ref, grow_ref, h_sc, x_buf, acc_sc, x_sem, o_sem, hn_sc, hn_sem) = refs
    else:
        x_hbm, g_ref, w1_ref, w3_ref, w2_ref, gn_ref, o_hbm, h_sc, x_buf, acc_sc, x_sem, o_sem = refs
    i = pl.program_id(0)
    j = pl.program_id(1)
    n_i = pl.num_programs(0)
    tm = x_buf.shape[0]
    n_acc = acc_sc.shape[0]

    def acc_slot(tile):
        return acc_sc.at[tile % n_acc if n_acc > 1 else 0]

    def tile_rows(tile):
        return pl.ds(pl.multiple_of(tile * tm, tm), tm)

    def x_copy(tile):
        return pltpu.make_async_copy(x_hbm.at[tile_rows(tile), :], x_buf, x_sem)

    def o_copy(tile):
        return pltpu.make_async_copy(acc_slot(tile), o_hbm.at[tile_rows(tile), :], o_sem)

    def hn_copy(tile):
        return pltpu.make_async_copy(hn_sc, hn_hbm.at[tile_rows(tile), :], hn_sem)

    def row_chunk(r, size):
        return pl.ds(pl.multiple_of(r * size, size), size)

    acc_ref = acc_slot(i)

    def swiglu_passes(first):
        h = h_sc[...]
        for c in range(w1_ref.shape[1] // FFN_SUB_COLS):
            cols = slice(c * FFN_SUB_COLS, (c + 1) * FFN_SUB_COLS)
            a = jnp.dot(h, w1_ref[:, cols].astype(BF16), preferred_element_type=F32)
            b = jnp.dot(h, w3_ref[:, cols].astype(BF16), preferred_element_type=F32)
            act = (a * _sigmoid(a) * b).astype(BF16)
            part = jnp.dot(act, w2_ref[cols, :].astype(BF16), preferred_element_type=F32)
            if first and c == 0:
                acc_ref[...] = 2.0 * x_buf[...] + part
            else:
                acc_ref[...] += part

    @pl.when(j == 0)
    def _():
        @pl.when(i == 0)
        def _():
            x_copy(0).start()

        x_copy(i).wait()

        def norm_rows(r, carry):
            rows = row_chunk(r, FFN_NORM_ROWS)
            h_sc[rows, :] = _rms(x_buf[rows, :], g_ref[...]).astype(BF16)
            return carry
        lax.fori_loop(0, tm // FFN_NORM_ROWS, norm_rows, 0)

        if n_acc == 1:
            @pl.when(i > 0)
            def _():
                o_copy(i - 1).wait()

        swiglu_passes(first=True)

    @pl.when(j > 0)
    def _():
        swiglu_passes(first=False)

    @pl.when(jnp.logical_and(j == 1, i + 1 < n_i))
    def _():
        x_copy(i + 1).start()

    @pl.when(j == pl.num_programs(1) - 1)
    def _():
        if mode == "final":
            def finish_rows(r, carry):
                rows = row_chunk(r, FFN_FINISH_ROWS)
                acc_ref[rows, :] = _rms(0.5 * acc_ref[rows, :], gn_ref[...])
                return carry
            lax.fori_loop(0, tm // FFN_FINISH_ROWS, finish_rows, 0)
        else:
            @pl.when(i > 0)
            def _():
                hn_copy(i - 1).wait()

            y = 0.5 * acc_ref[...]
            acc_ref[...] = y
            hn = _rms(y, gn_ref[...]).astype(BF16)
            hn_sc[...] = hn
            nt_dims = (((1,), (1,)), ((), ()))
            wg = wg_ref[...].astype(BF16)
            wg_pad = jnp.concatenate([wg, jnp.zeros((LANES - wg.shape[0], wg.shape[1]), BF16)], axis=0)
            gcol = lax.dot_general(hn, wg_pad, nt_dims, preferred_element_type=F32)
            gcol_ref[...] = gcol
            grow_ref[...] = gcol.T[0:wg.shape[0], :]

        if n_acc > 1:
            @pl.when(i > 0)
            def _():
                o_copy(i - 1).wait()

        o_copy(i).start()
        if mode == "mix":
            hn_copy(i).start()

        @pl.when(i == n_i - 1)
        def _():
            o_copy(i).wait()
            if mode == "mix":
                hn_copy(i).wait()


def _ffn(x2d, g, w1, w3, w2, g_next, w_gates_t=None, *, tm=1024, tf=512):
    n, d = x2d.shape
    f = w1.shape[1]
    mode = "final" if w_gates_t is None else "mix"
    in_specs = [
        pl.BlockSpec(memory_space=pl.ANY),
        pl.BlockSpec((1, d), lambda i, j: (0, 0)),
        pl.BlockSpec((d, tf), lambda i, j: (0, j)),
        pl.BlockSpec((d, tf), lambda i, j: (0, j)),
        pl.BlockSpec((tf, d), lambda i, j: (j, 0)),
        pl.BlockSpec((1, d), lambda i, j: (0, 0)),
    ]
    args = [x2d, g.reshape(1, d), w1, w3, w2, g_next.reshape(1, d)]
    out_specs = [pl.BlockSpec(memory_space=pl.ANY)]
    out_shape = [jax.ShapeDtypeStruct((n, d), F32)]
    scratch_shapes = [
        pltpu.VMEM((tm, d), BF16),
        pltpu.VMEM((tm, d), F32),
        pltpu.VMEM((2 if mode == "final" else 1, tm, d), F32),
        pltpu.SemaphoreType.DMA(()),
        pltpu.SemaphoreType.DMA(()),
    ]
    if mode == "mix":
        ng = 2 * MLSTM_HEADS
        in_specs.append(pl.BlockSpec((ng, d), lambda i, j: (MIX_GATE_ROW // ng, 0)))
        args.append(w_gates_t)
        out_specs += [
            pl.BlockSpec(memory_space=pl.ANY),
            pl.BlockSpec((tm, LANES), lambda i, j: (i, 0)),
            pl.BlockSpec((ng, tm), lambda i, j: (0, i)),
        ]
        out_shape += [
            jax.ShapeDtypeStruct((n, d), BF16),
            jax.ShapeDtypeStruct((n, LANES), F32),
            jax.ShapeDtypeStruct((ng, n), F32),
        ]
        scratch_shapes += [pltpu.VMEM((tm, d), BF16), pltpu.SemaphoreType.DMA(())]
    return pl.pallas_call(
        functools.partial(_ffn_kernel, mode=mode),
        grid=(n // tm, f // tf),
        in_specs=in_specs,
        out_specs=out_specs,
        out_shape=out_shape,
        scratch_shapes=scratch_shapes,
        compiler_params=_params("arbitrary", "arbitrary"),
        name="ffn_" + mode,
    )(*args)


def _in_proj_kernel(h_ref, wt_ref, p_ref, w_sc):
    @pl.when(pl.program_id(1) == 0)
    def _():
        w_sc[...] = wt_ref[...].astype(BF16)

    for r in range(0, h_ref.shape[0], PROJ_SUB_ROWS):
        rows = slice(r, r + PROJ_SUB_ROWS)
        p_ref[rows, :] = lax.dot_general(h_ref[rows, :], w_sc[...], (((1,), (1,)), ((), ())),
                                         preferred_element_type=F32)


def _proj_row_offset(j, tn):
    n_gate_tiles = 2 * D_MODEL // tn
    n_mlstm_tiles = MIX_GATE_ROW // tn
    off8 = jnp.where(j < n_gate_tiles, MIX_MERGE_ROW // 8 + j * (tn // 8),
                     jnp.where(j < n_gate_tiles + n_mlstm_tiles, (j - n_gate_tiles) * (tn // 8),
                               MIX_HGRN_ROW // 8 + (j - n_gate_tiles - n_mlstm_tiles) * (tn // 8)))
    return off8 * 8


def _in_proj(hn, w_t, *, tm=2048, tn=1024):
    n, d = hn.shape
    return pl.pallas_call(
        _in_proj_kernel,
        grid=(PROJ_W // tn, n // tm),
        in_specs=[
            pl.BlockSpec((tm, d), lambda j, i: (i, 0)),
            pl.BlockSpec((pl.Element(tn), pl.Element(d)), lambda j, i: (_proj_row_offset(j, tn), 0)),
        ],
        out_specs=pl.BlockSpec((tm, tn), lambda j, i: (i, j)),
        out_shape=jax.ShapeDtypeStruct((n, PROJ_W), F32),
        scratch_shapes=[pltpu.VMEM((tn, d), BF16)],
        compiler_params=_params("arbitrary", "arbitrary"),
        name="in_proj",
    )(hn, w_t)


def _mlstm_block(q_ref, k_ref, v_ref, o_ref, gcol_ref, grow_ref, cw_ref, cb_ref, bcol_ref, brow_ref,
                 hn_ref, y_ref, xpad_sc, c_sc, n_sc, m_sc, before_head):
    L = MIX_TB
    H = MLSTM_HEADS

    xpad_sc[8:8 + L, 0:MLSTM_QK_W] = q_ref[...]
    xpad_sc[8:8 + L, MLSTM_QK_W:2 * MLSTM_QK_W] = k_ref[...]
    acc = jnp.broadcast_to(cb_ref[...], (L, 2 * MLSTM_QK_W))
    for j in range(CONV_WIDTH):
        off = 8 - (CONV_WIDTH - 1) + j
        acc = acc + cw_ref[j:j + 1, :] * xpad_sc[off:off + L, :]
    xpad_sc[0:8, :] = xpad_sc[L:L + 8, :]
    qk = acc * _sigmoid(acc)

    gcol = gcol_ref[...] + bcol_ref[...]
    grow = grow_ref[...] + brow_ref[...]
    fcol = _log_sigmoid(gcol)
    r_i = lax.broadcasted_iota(jnp.int32, (L, L), 0)
    c_i = lax.broadcasted_iota(jnp.int32, (L, L), 1)
    causal = r_i >= c_i
    tril = jnp.where(causal, 1.0, 0.0).astype(BF16)
    bcol = _tri_cumsum(tril, fcol)
    brow = bcol.T

    for h in range(H):
        before_head[h]()
        b_c = bcol[:, H + h:H + h + 1]
        i_c = gcol[:, h:h + 1]
        b_r = brow[H + h:H + h + 1, :]
        i_r = grow[h:h + 1, :]
        m_prev = m_sc[h:h + 1, 0:1]

        d_log = jnp.where(causal, b_c - (b_r - i_r), -jnp.inf)
        inter_log = b_c + m_prev
        m_t = jnp.maximum(jnp.max(d_log, axis=1, keepdims=True), inter_log)

        q_h = qk[:, h * MLSTM_QK:(h + 1) * MLSTM_QK]
        k_h = qk[:, MLSTM_QK_W + h * MLSTM_QK:MLSTM_QK_W + (h + 1) * MLSTM_QK] * (MLSTM_QK ** -0.5)
        v_h = v_ref[:, h * MLSTM_V:(h + 1) * MLSTM_V].astype(BF16)
        q_b = q_h.astype(BF16)

        s = lax.dot_general(q_b, k_h.astype(BF16), (((1,), (1,)), ((), ())), preferred_element_type=F32)
        s = s * jnp.exp(d_log - m_t)
        w_inter = jnp.exp(inter_log - m_t)
        c_h = c_sc[h]
        n_h = n_sc[h:h + 1, :]
        lhs = jnp.concatenate([s.astype(BF16), (q_h * w_inter).astype(BF16)], axis=1)
        rhs = jnp.concatenate([v_h, c_h.astype(BF16)], axis=0)
        num = jnp.dot(lhs, rhs, preferred_element_type=F32)
        den = jnp.sum(s, axis=1, keepdims=True) + w_inter * jnp.sum(q_h * n_h, axis=1, keepdims=True)
        hh = num / jnp.maximum(jnp.abs(den), jnp.exp(-m_t))

        hn = hh * lax.rsqrt(jnp.mean(hh * hh, axis=-1, keepdims=True) + EPS)
        hn = hn * hn_ref[:, h * MLSTM_V:(h + 1) * MLSTM_V]
        y = hn * _sigmoid(o_ref[:, h * MLSTM_V:(h + 1) * MLSTM_V])
        y_ref[:, h * MLSTM_V:(h + 1) * MLSTM_V] = y.astype(y_ref.dtype)

        b_last = b_c[L - 1:L, :]
        a_log = b_last - b_c + i_c
        m_new = jnp.maximum(b_last + m_prev, jnp.max(a_log, axis=0, keepdims=True))
        w_a = jnp.exp(a_log - m_new)
        decay = jnp.exp(b_last + m_prev - m_new)
        kw = k_h * w_a
        c_sc[h] = decay * c_h + lax.dot_general(kw.astype(BF16), v_h, (((0,), (0,)), ((), ())),
                                                preferred_element_type=F32)
        n_sc[h:h + 1, :] = decay * n_h + jnp.sum(kw, axis=0, keepdims=True)
        m_sc[h:h + 1, :] = jnp.broadcast_to(m_new, (1, LANES))


def _hgrn_block(q_ref, f_ref, v_ref, og_ref, lbl_ref, hn_ref, y_ref, g_sc, k_sc, q_sc, gr_sc, st_sc,
                before_head):
    TB = MIX_TB
    H = HGRN_HEADS
    DK = HGRN_DK
    W = H * DK
    NT = TB // 8

    lbl = lbl_ref[...]
    lmax = jnp.max(lbl, axis=0, keepdims=True)
    le = jnp.exp(lbl - lmax)
    lb = le[1:2, :] / jnp.sum(le, axis=0, keepdims=True)

    f = lb + (1.0 - lb) * _sigmoid(f_ref[...])
    k_sc[...] = 1.0 - f
    qraw = q_ref[...]
    q_sc[...] = qraw * _sigmoid(qraw) * (DK ** -0.5)
    r_i = lax.broadcasted_iota(jnp.int32, (TB, TB), 0)
    c_i = lax.broadcasted_iota(jnp.int32, (TB, TB), 1)
    tri = jnp.where(r_i >= c_i, 1.0, 0.0).astype(BF16)
    g = _tri_cumsum(tri, jnp.log(f) * LOG2_E)
    g_sc[...] = g

    g3 = g.reshape(NT, 8, W)
    sub = lax.broadcasted_iota(jnp.int32, (NT, 8, W), 1)
    bit0 = (sub & 1) != 0
    bit1 = (sub & 2) != 0
    bit2 = (sub & 4) != 0
    last2 = jnp.where(bit0, g3, pltpu.roll(g3, 7, 1))
    last4 = jnp.where(bit1, last2, pltpu.roll(last2, 6, 1))
    last4_r = pltpu.roll(last4, 4, 1)
    last8 = jnp.where(bit2, last4, last4_r).reshape(TB, W)
    gr_sc[0] = jnp.where(bit0, pltpu.roll(g3, 1, 1), g3).reshape(TB, W)
    gr_sc[1] = jnp.where(bit1, pltpu.roll(last2, 2, 1), last2).reshape(TB, W)
    gr_sc[2] = jnp.where(bit2, last4_r, last4).reshape(TB, W)
    for li, lvl in enumerate(HGRN_COARSE_LEVELS):
        groups = []
        for base in range(0, TB, 2 * lvl):
            src = last8[base + lvl - 8:base + lvl, :]
            groups.extend([src] * (2 * lvl // 8))
        gr_sc[3 + li] = jnp.concatenate(groups, axis=0)

    x_i = r_i ^ c_i
    nt_dims = (((1,), (1,)), ((), ()))
    for h in range(H):
        before_head[h]()
        cols = slice(h * DK, (h + 1) * DK)
        q_h = q_sc[:, cols]
        k_h = k_sc[:, cols]
        g_h = g_sc[:, cols]
        v_h = v_ref[:, cols].astype(BF16)

        q_b = q_h.astype(BF16)
        k_b = k_h.astype(BF16)
        a = jnp.broadcast_to(jnp.sum(q_h * k_h, axis=1, keepdims=True), (TB, TB))
        for li in range(HGRN_NUM_LEVELS):
            e = jnp.exp2(_neg_abs(g_h - gr_sc[li, :, cols])).astype(BF16)
            p = lax.dot_general(q_b * e, k_b * e, nt_dims, preferred_element_type=F32)
            a = jnp.where(x_i >= (1 << li), p, a)
        a = jnp.where(r_i >= c_i, a, 0.0)

        st = st_sc[h]
        g_last = g_h[TB - 1:TB, :]
        qd = (q_h * jnp.exp2(g_h)).astype(BF16)
        lhs = jnp.concatenate([a.astype(BF16), qd], axis=1)
        rhs = jnp.concatenate([v_h, st.astype(BF16)], axis=0)
        o = jnp.dot(lhs, rhs, preferred_element_type=F32)
        kd = (k_h * jnp.exp2(g_last - g_h)).astype(BF16)
        decay = jnp.exp2(g_h[TB - 8:TB, :].T[:, 7:8])
        st_sc[h] = decay * st + lax.dot_general(kd, v_h, (((0,), (0,)), ((), ())),
                                                preferred_element_type=F32)

        on = o * lax.rsqrt(jnp.mean(o * o, axis=-1, keepdims=True) + EPS) * hn_ref[:, cols]
        og = og_ref[:, cols]
        y_ref[:, cols] = (on * (og * _sigmoid(og))).astype(y_ref.dtype)


def _mix_kernel(mq_ref, mk_ref, mv_ref, mo_ref, gcol_ref, grow_ref, cw_ref, cb_ref, bcol_ref, brow_ref,
                mhn_ref, hq_ref, hf_ref, hi_ref, hg_ref, lbl_ref, hhn_ref,
                x_ref, gm_ref, gh_ref, wm_ref, wh_ref, wo_ref,
                out_ref,
                ym_sc, yh_sc, merged_sc, xpad_sc, c_sc, n_sc, m_sc, g_sc, k_sc, q_sc, gr_sc, st_sc,
                *, blocks_per_seq):
    s = pl.program_id(0)
    cur = s % 2
    prv = 1 - cur

    @pl.when(s == 0)
    def _():
        ym_sc[...] = jnp.zeros_like(ym_sc)
        yh_sc[...] = jnp.zeros_like(yh_sc)

    @pl.when(s % blocks_per_seq == 0)
    def _():
        xpad_sc[0:8, :] = jnp.zeros((8, 2 * MLSTM_QK_W), F32)
        c_sc[...] = jnp.zeros_like(c_sc)
        n_sc[...] = jnp.zeros_like(n_sc)
        m_sc[...] = jnp.zeros_like(m_sc)
        st_sc[...] = jnp.zeros_like(st_sc)

    def gate_piece(c):
        def run():
            cols = slice(c * MERGE_GATE_COLS, (c + 1) * MERGE_GATE_COLS)
            pm = jnp.dot(ym_sc[prv], wm_ref[:, cols], preferred_element_type=F32)
            ph = jnp.dot(yh_sc[prv], wh_ref[:, cols], preferred_element_type=F32)
            merged = _sigmoid(gm_ref[:, cols]) * pm + _sigmoid(gh_ref[:, cols]) * ph
            merged_sc[:, cols] = merged.astype(BF16)
        return run

    def out_piece(c):
        def run():
            cols = slice(c * MERGE_OUT_COLS, (c + 1) * MERGE_OUT_COLS)
            out_ref[:, cols] = x_ref[:, cols] + jnp.dot(merged_sc[...], wo_ref[:, cols],
                                                        preferred_element_type=F32)
        return run

    pieces = ([gate_piece(c) for c in range(D_MODEL // MERGE_GATE_COLS)]
              + [out_piece(c) for c in range(D_MODEL // MERGE_OUT_COLS)])
    n_heads = MLSTM_HEADS + HGRN_HEADS
    n_slots = MIX_SUBBLOCKS * n_heads
    slot_of = [(i * n_slots) // len(pieces) for i in range(len(pieces))]
    hooks = [(pieces[slot_of.index(i)] if i in slot_of else (lambda: None)) for i in range(n_slots)]

    for sub in range(MIX_SUBBLOCKS):
        rows = pl.ds(sub * MIX_TB, MIX_TB)
        sub_hooks = hooks[sub * n_heads:(sub + 1) * n_heads]
        _mlstm_block(mq_ref.at[rows], mk_ref.at[rows], mv_ref.at[rows], mo_ref.at[rows], gcol_ref.at[rows],
                     grow_ref.at[:, rows], cw_ref, cb_ref, bcol_ref, brow_ref, mhn_ref,
                     ym_sc.at[cur, rows], xpad_sc, c_sc, n_sc, m_sc, sub_hooks[:MLSTM_HEADS])
        _hgrn_block(hq_ref.at[rows], hf_ref.at[rows], hi_ref.at[rows], hg_ref.at[rows], lbl_ref, hhn_ref,
                    yh_sc.at[cur, rows], g_sc, k_sc, q_sc, gr_sc, st_sc, sub_hooks[MLSTM_HEADS:])


def _const_spec(shape):
    return pl.BlockSpec(shape, lambda *_: (0,) * len(shape), pipeline_mode=pl.Buffered(1))


def _mix(x2d, proj, gates_col, gates_row, conv_w, conv_b, bias_col, bias_row, m_head_norm,
         lb_logits, h_head_norm, w_m, w_h, w_o, *, seq_len):
    n, d = x2d.shape
    tb = MIX_TB * MIX_SUBBLOCKS
    nblk = n // tb
    qw, vw, hw = MLSTM_QK_W, MLSTM_V_W, HGRN_W

    def cur(col_block):
        return lambda s: (jnp.minimum(s, nblk - 1), col_block)

    def prev(col_block):
        return lambda s: (jnp.maximum(s - 1, 0), col_block)

    return pl.pallas_call(
        functools.partial(_mix_kernel, blocks_per_seq=seq_len // tb),
        grid=(nblk + 1,),
        in_specs=[
            pl.BlockSpec((tb, qw), cur(COL_MQ // qw)),
            pl.BlockSpec((tb, qw), cur(COL_MK // qw)),
            pl.BlockSpec((tb, vw), cur(COL_MV // vw)),
            pl.BlockSpec((tb, vw), cur(COL_MO // vw)),
            pl.BlockSpec((tb, LANES), cur(0)),
            pl.BlockSpec((2 * MLSTM_HEADS, tb), lambda s: (0, jnp.minimum(s, nblk - 1))),
            _const_spec((CONV_WIDTH, 2 * qw)),
            _const_spec((1, 2 * qw)),
            _const_spec((1, LANES)),
            _const_spec((2 * MLSTM_HEADS, 1)),
            _const_spec((1, vw)),
            pl.BlockSpec((tb, hw), cur(COL_HQ // hw)),
            pl.BlockSpec((tb, hw), cur(COL_HF // hw)),
            pl.BlockSpec((tb, hw), cur(COL_HI // hw)),
            pl.BlockSpec((tb, hw), cur(COL_HG // hw)),
            _const_spec((2, hw)),
            _const_spec((1, hw)),
            pl.BlockSpec((tb, d), prev(0)),
            pl.BlockSpec((tb, d), prev(COL_GM // d)),
            pl.BlockSpec((tb, d), prev(COL_GH // d)),
            _const_spec(w_m.shape),
            _const_spec(w_h.shape),
            _const_spec(w_o.shape),
        ],
        out_specs=pl.BlockSpec((tb, d), prev(0)),
        out_shape=jax.ShapeDtypeStruct((n, d), F32),
        scratch_shapes=[
            pltpu.VMEM((2, tb, vw), BF16),
            pltpu.VMEM((2, tb, hw), BF16),
            pltpu.VMEM((tb, d), BF16),
            pltpu.VMEM((MIX_TB + 8, 2 * qw), F32),
            pltpu.VMEM((MLSTM_HEADS, MLSTM_QK, MLSTM_V), F32),
            pltpu.VMEM((8, MLSTM_QK), F32),
            pltpu.VMEM((8, LANES), F32),
            pltpu.VMEM((MIX_TB, hw), F32),
            pltpu.VMEM((MIX_TB, hw), F32),
            pltpu.VMEM((MIX_TB, hw), F32),
            pltpu.VMEM((HGRN_NUM_LEVELS, MIX_TB, hw), F32),
            pltpu.VMEM((HGRN_HEADS, HGRN_DK, HGRN_DV), F32),
        ],
        compiler_params=_params("arbitrary"),
        name="mix",
    )(proj, proj, proj, proj, gates_col, gates_row, conv_w, conv_b, bias_col, bias_row, m_head_norm,
      proj, proj, proj, proj, lb_logits, h_head_norm, x2d, proj, proj, w_m, w_h, w_o)


def _xattn_prep_kernel(m_ref, g_ref, wk_ref, wv_ref, wq_ref, wo_ref, qk_ref, vo_ref):
    wk = wk_ref[...].astype(BF16)
    wv = wv_ref[...].astype(BF16)
    wq = wq_ref[...].astype(BF16)
    wo = wo_ref[...].astype(BF16)
    for i in range(m_ref.shape[0]):
        m = _rms(m_ref[i], g_ref[...]).astype(BF16)
        k = jnp.dot(m, wk, preferred_element_type=F32).astype(BF16)
        v = jnp.dot(m, wv, preferred_element_type=F32).astype(BF16)
        qk = lax.dot_general(wq, k, (((1,), (1,)), ((), ())), preferred_element_type=F32)
        qk_ref[i] = (qk * (XATTN_HEAD_DIM ** -0.5)).astype(qk_ref.dtype)
        vo_ref[i] = jnp.dot(v, wo, preferred_element_type=F32).astype(vo_ref.dtype)


def _xattn_prep(mem, g, w_kv, w_q, w_o):
    b, m, d = mem.shape
    hd = XATTN_HEAD_DIM
    nh = XATTN_HEADS
    return pl.pallas_call(
        _xattn_prep_kernel,
        grid=(nh,),
        in_specs=[
            pl.BlockSpec((b, m, d), lambda h: (0, 0, 0)),
            pl.BlockSpec((1, d), lambda h: (0, 0)),
            pl.BlockSpec((d, hd), lambda h: (0, h)),
            pl.BlockSpec((d, hd), lambda h: (0, nh + h)),
            pl.BlockSpec((d, hd), lambda h: (0, h)),
            pl.BlockSpec((hd, d), lambda h: (h, 0)),
        ],
        out_specs=[
            pl.BlockSpec((b, d, m), lambda h: (0, 0, h)),
            pl.BlockSpec((b, m, d), lambda h: (0, h, 0)),
        ],
        out_shape=[
            jax.ShapeDtypeStruct((b, d, nh * m), BF16),
            jax.ShapeDtypeStruct((b, nh * m, d), BF16),
        ],
        compiler_params=_params("arbitrary"),
        name="xattn_prep",
    )(mem, g.reshape(1, d), w_kv, w_kv, w_q, w_o)


def _xattn_kernel(x_ref, g_ref, qk_ref, vo_ref, o_ref, *, mem_len):
    x = x_ref[0]
    h = _rms(x, g_ref[...]).astype(BF16)
    s = jnp.dot(h, qk_ref[0], preferred_element_type=F32)
    probs = []
    for hd in range(XATTN_HEADS):
        s_h = s[:, hd * mem_len:(hd + 1) * mem_len]
        e = jnp.exp(s_h - jnp.max(s_h, axis=-1, keepdims=True))
        probs.append((e / jnp.sum(e, axis=-1, keepdims=True)).astype(BF16))
    p = jnp.concatenate(probs, axis=-1)
    o_ref[0] = x + jnp.dot(p, vo_ref[0], preferred_element_type=F32)


def _xattn(x, g, qk, vo, *, tm=512):
    b, t, d = x.shape
    hm = qk.shape[2]
    return pl.pallas_call(
        functools.partial(_xattn_kernel, mem_len=hm // XATTN_HEADS),
        grid=(b, t // tm),
        in_specs=[
            pl.BlockSpec((1, tm, d), lambda i, c: (i, c, 0)),
            _const_spec((1, d)),
            pl.BlockSpec((1, d, hm), lambda i, c: (i, 0, 0)),
            pl.BlockSpec((1, hm, d), lambda i, c: (i, 0, 0)),
        ],
        out_specs=pl.BlockSpec((1, tm, d), lambda i, c: (i, c, 0)),
        out_shape=jax.ShapeDtypeStruct((b, t, d), F32),
        compiler_params=_params("parallel", "parallel"),
        name="xattn",
    )(x, g.reshape(1, d), qk, vo)


def kernel(x, mem, norm_ffn1, ffn1_w1, ffn1_w3, ffn1_w2, norm_mix, w_in, mlstm_conv_w, mlstm_conv_b,
           mlstm_ig_bias, mlstm_fg_bias, mlstm_head_norm, hgrn_lb_logits, hgrn_head_norm, w_proj_m,
           w_proj_h, w_out, norm_xattn, norm_mem, xattn_wq, xattn_wkv, xattn_wo, norm_ffn2, ffn2_w1,
           ffn2_w3, ffn2_w2, norm_final):
    b, t, d = x.shape
    depth = norm_ffn1.shape[0]
    assert depth == 1 and hgrn_lb_logits.shape[0] == 2
    n = b * t
    l = 0
    bf = lambda w: w.astype(BF16)

    w_in_t = jnp.swapaxes(w_in[l], 0, 1)
    n_gate = 2 * MLSTM_HEADS
    gate_bias = jnp.concatenate([mlstm_ig_bias[l], mlstm_fg_bias[l]]).astype(F32)
    bias_col = jnp.pad(gate_bias, (0, LANES - n_gate)).reshape(1, LANES)
    bias_row = gate_bias.reshape(n_gate, 1)

    x1, hn, gates_col, gates_row = _ffn(x.reshape(n, d), norm_ffn1[l], ffn1_w1[l], ffn1_w3[l], ffn1_w2[l],
                                        norm_mix[l], w_in_t)

    proj = _in_proj(hn, w_in_t)
    x2 = _mix(x1, proj, gates_col, gates_row, mlstm_conv_w[l], mlstm_conv_b[l].reshape(1, -1),
              bias_col, bias_row, mlstm_head_norm[l].reshape(1, -1), hgrn_lb_logits,
              hgrn_head_norm[l].reshape(1, -1), bf(w_proj_m[l]), bf(w_proj_h[l]), bf(w_out[l]), seq_len=t)

    qk, vo = _xattn_prep(mem, norm_mem[l], xattn_wkv[l], xattn_wq[l], xattn_wo[l])
    x3 = _xattn(x2.reshape(b, t, d), norm_xattn[l], qk, vo)

    (out,) = _ffn(x3.reshape(n, d), norm_ffn2[l], ffn2_w1[l], ffn2_w3[l], ffn2_w2[l], norm_final)
    return out.reshape(b, t, d)
```

```python
import functools

import jax
import jax.numpy as jnp
from jax import lax
from jax.experimental import pallas as pl
from jax.experimental.pallas import tpu as pltpu

F32 = jnp.float32
BF16 = jnp.bfloat16
EPS = 1e-6
LOG2_E = 1.4426950408889634

D_MODEL = 2048
D_FF = 5632
MLSTM_HEADS = 4
MLSTM_QK = 128
MLSTM_V = 256
CONV_WIDTH = 4
HGRN_HEADS = 8
HGRN_DK = 128
HGRN_DV = 128
XATTN_HEADS = 4
XATTN_HEAD_DIM = D_MODEL // XATTN_HEADS

MLSTM_QK_W = MLSTM_HEADS * MLSTM_QK
MLSTM_V_W = MLSTM_HEADS * MLSTM_V
HGRN_W = HGRN_HEADS * HGRN_DK

VMEM_LIMIT_BYTES = 60000 * 1024
LANES = 128

COL_GM = 0
COL_GH = COL_GM + D_MODEL
COL_MQ = COL_GH + D_MODEL
COL_MK = COL_MQ + MLSTM_QK_W
COL_MV = COL_MK + MLSTM_QK_W
COL_MO = COL_MV + MLSTM_V_W
COL_HQ = COL_MO + MLSTM_V_W
COL_HF = COL_HQ + HGRN_W
COL_HI = COL_HF + HGRN_W
COL_HG = COL_HI + HGRN_W
PROJ_W = COL_HG + HGRN_W

MIX_GATE_ROW = 2 * MLSTM_QK_W + 2 * MLSTM_V_W
MIX_HGRN_ROW = MIX_GATE_ROW + 2 * MLSTM_HEADS
MIX_MERGE_ROW = MIX_HGRN_ROW + 4 * HGRN_W

PROJ_SUB_ROWS = 1024
FFN_NORM_ROWS = 128
FFN_FINISH_ROWS = 128
FFN_SUB_COLS = 256

MIX_TB = 128
HGRN_NUM_LEVELS = MIX_TB.bit_length() - 1
MIX_SUBBLOCKS = 2
MERGE_GATE_COLS = 512
MERGE_OUT_COLS = 256
HGRN_HALF_LHS_MIN_LEVEL = 16
HGRN_COARSE_LEVELS = tuple(1 << i for i in range(3, HGRN_NUM_LEVELS))


def _rms(x, g):
    return x * lax.rsqrt(jnp.mean(x * x, axis=-1, keepdims=True) + EPS) * g


def _sigmoid(x):
    return 1.0 / (1.0 + jnp.exp(-x))


def _log_sigmoid(x):
    return jnp.minimum(x, 0.0) - jnp.log(1.0 + jnp.exp(-jnp.abs(x)))


def _split3(x):
    hi = x.astype(BF16)
    r1 = x - hi.astype(F32)
    mid = r1.astype(BF16)
    lo = (r1 - mid.astype(F32)).astype(BF16)
    return hi, mid, lo


def _tri_cumsum(tri, x):
    return sum(jnp.dot(tri, t, preferred_element_type=F32) for t in _split3(x))


def _neg_abs(x):
    return pltpu.bitcast(pltpu.bitcast(x, jnp.uint32) | jnp.uint32(0x80000000), F32)


def _params(*sem):
    return pltpu.CompilerParams(dimension_semantics=sem, vmem_limit_bytes=VMEM_LIMIT_BYTES)


def _ffn_kernel(*refs, mode):
    if mode == "mix":
        (x_hbm, g_ref, w1_ref, w3_ref, w2_ref, gn_ref, wg_ref,
         o_hbm, hn_hbm, gcol_ref, grow_ref, h_sc, x_buf, acc_sc, x_sem, o_sem, hn_sc, hn_sem) = refs
    else:
        x_hbm, g_ref, w1_ref, w3_ref, w2_ref, gn_ref, o_hbm, h_sc, x_buf, acc_sc, x_sem, o_sem = refs
    i = pl.program_id(0)
    j = pl.program_id(1)
    n_i = pl.num_programs(0)
    tm = x_buf.shape[0]
    n_acc = acc_sc.shape[0]

    def acc_slot(tile):
        return acc_sc.at[tile % n_acc if n_acc > 1 else 0]

    def tile_rows(tile):
        return pl.ds(pl.multiple_of(tile * tm, tm), tm)

    def x_copy(tile):
        return pltpu.make_async_copy(x_hbm.at[tile_rows(tile), :], x_buf, x_sem)

    def o_copy(tile):
        return pltpu.make_async_copy(acc_slot(tile), o_hbm.at[tile_rows(tile), :], o_sem)

    def hn_copy(tile):
        return pltpu.make_async_copy(hn_sc, hn_hbm.at[tile_rows(tile), :], hn_sem)

    def row_chunk(r, size):
        return pl.ds(pl.multiple_of(r * size, size), size)

    acc_ref = acc_slot(i)

    def swiglu_passes(first):
        h = h_sc[...]
        for c in range(w1_ref.shape[1] // FFN_SUB_COLS):
            cols = slice(c * FFN_SUB_COLS, (c + 1) * FFN_SUB_COLS)
            a = jnp.dot(h, w1_ref[:, cols].astype(BF16), preferred_element_type=F32)
            b = jnp.dot(h, w3_ref[:, cols].astype(BF16), preferred_element_type=F32)
            act = (a * _sigmoid(a) * b).astype(BF16)
            part = jnp.dot(act, w2_ref[cols, :].astype(BF16), preferred_element_type=F32)
            if first and c == 0:
                acc_ref[...] = 2.0 * x_buf[...] + part
            else:
                acc_ref[...] += part

    @pl.when(j == 0)
    def _():
        @pl.when(i == 0)
        def _():
            x_copy(0).start()

        x_copy(i).wait()

        def norm_rows(r, carry):
            rows = row_chunk(r, FFN_NORM_ROWS)
            h_sc[rows, :] = _rms(x_buf[rows, :], g_ref[...]).astype(BF16)
            return carry
        lax.fori_loop(0, tm // FFN_NORM_ROWS, norm_rows, 0)

        if n_acc == 1:
            @pl.when(i > 0)
            def _():
                o_copy(i - 1).wait()

        swiglu_passes(first=True)

    @pl.when(j > 0)
    def _():
        swiglu_passes(first=False)

    @pl.when(jnp.logical_and(j == 1, i + 1 < n_i))
    def _():
        x_copy(i + 1).start()

    @pl.when(j == pl.num_programs(1) - 1)
    def _():
        if mode == "final":
            def finish_rows(r, carry):
                rows = row_chunk(r, FFN_FINISH_ROWS)
                acc_ref[rows, :] = _rms(0.5 * acc_ref[rows, :], gn_ref[...])
                return carry
            lax.fori_loop(0, tm // FFN_FINISH_ROWS, finish_rows, 0)
        else:
            @pl.when(i > 0)
            def _():
                hn_copy(i - 1).wait()

            y = 0.5 * acc_ref[...]
            acc_ref[...] = y
            hn = _rms(y, gn_ref[...]).astype(BF16)
            hn_sc[...] = hn
            nt_dims = (((1,), (1,)), ((), ()))
            wg = wg_ref[...].astype(BF16)
            wg_pad = jnp.concatenate([wg, jnp.zeros((LANES - wg.shape[0], wg.shape[1]), BF16)], axis=0)
            gcol = lax.dot_general(hn, wg_pad, nt_dims, preferred_element_type=F32)
            gcol_ref[...] = gcol
            grow_ref[...] = gcol.T[0:wg.shape[0], :]

        if n_acc > 1:
            @pl.when(i > 0)
            def _():
                o_copy(i - 1).wait()

        o_copy(i).start()
        if mode == "mix":
            hn_copy(i).start()

        @pl.when(i == n_i - 1)
        def _():
            o_copy(i).wait()
            if mode == "mix":
                hn_copy(i).wait()


def _ffn(x2d, g, w1, w3, w2, g_next, w_gates_t=None, *, tm=1024, tf=512):
    n, d = x2d.shape
    f = w1.shape[1]
    mode = "final" if w_gates_t is None else "mix"
    in_specs = [
        pl.BlockSpec(memory_space=pl.ANY),
        pl.BlockSpec((1, d), lambda i, j: (0, 0)),
        pl.BlockSpec((d, tf), lambda i, j: (0, j)),
        pl.BlockSpec((d, tf), lambda i, j: (0, j)),
        pl.BlockSpec((tf, d), lambda i, j: (j, 0)),
        pl.BlockSpec((1, d), lambda i, j: (0, 0)),
    ]
    args = [x2d, g.reshape(1, d), w1, w3, w2, g_next.reshape(1, d)]
    out_specs = [pl.BlockSpec(memory_space=pl.ANY)]
    out_shape = [jax.ShapeDtypeStruct((n, d), F32)]
    scratch_shapes = [
        pltpu.VMEM((tm, d), BF16),
        pltpu.VMEM((tm, d), F32),
        pltpu.VMEM((2 if mode == "final" else 1, tm, d), F32),
        pltpu.SemaphoreType.DMA(()),
        pltpu.SemaphoreType.DMA(()),
    ]
    if mode == "mix":
        ng = 2 * MLSTM_HEADS
        in_specs.append(pl.BlockSpec((ng, d), lambda i, j: (MIX_GATE_ROW // ng, 0)))
        args.append(w_gates_t)
        out_specs += [
            pl.BlockSpec(memory_space=pl.ANY),
            pl.BlockSpec((tm, LANES), lambda i, j: (i, 0)),
            pl.BlockSpec((ng, tm), lambda i, j: (0, i)),
        ]
        out_shape += [
            jax.ShapeDtypeStruct((n, d), BF16),
            jax.ShapeDtypeStruct((n, LANES), F32),
            jax.ShapeDtypeStruct((ng, n), F32),
        ]
        scratch_shapes += [pltpu.VMEM((tm, d), BF16), pltpu.SemaphoreType.DMA(())]
    return pl.pallas_call(
        functools.partial(_ffn_kernel, mode=mode),
        grid=(n // tm, f // tf),
        in_specs=in_specs,
        out_specs=out_specs,
        out_shape=out_shape,
        scratch_shapes=scratch_shapes,
        compiler_params=_params("arbitrary", "arbitrary"),
        name="ffn_" + mode,
    )(*args)


def _in_proj_kernel(h_ref, wt_ref, p_ref, w_sc):
    @pl.when(pl.program_id(1) == 0)
    def _():
        w_sc[...] = wt_ref[...].astype(BF16)

    for r in range(0, h_ref.shape[0], PROJ_SUB_ROWS):
        rows = slice(r, r + PROJ_SUB_ROWS)
        p_ref[rows, :] = lax.dot_general(h_ref[rows, :], w_sc[...], (((1,), (1,)), ((), ())),
                                         preferred_element_type=F32)


def _proj_row_offset(j, tn):
    n_gate_tiles = 2 * D_MODEL // tn
    n_mlstm_tiles = MIX_GATE_ROW // tn
    off8 = jnp.where(j < n_gate_tiles, MIX_MERGE_ROW // 8 + j * (tn // 8),
                     jnp.where(j < n_gate_tiles + n_mlstm_tiles, (j - n_gate_tiles) * (tn // 8),
                               MIX_HGRN_ROW // 8 + (j - n_gate_tiles - n_mlstm_tiles) * (tn // 8)))
    return off8 * 8


def _in_proj(hn, w_t, *, tm=2048, tn=1024):
    n, d = hn.shape
    return pl.pallas_call(
        _in_proj_kernel,
        grid=(PROJ_W // tn, n // tm),
        in_specs=[
            pl.BlockSpec((tm, d), lambda j, i: (i, 0)),
            pl.BlockSpec((pl.Element(tn), pl.Element(d)), lambda j, i: (_proj_row_offset(j, tn), 0)),
        ],
        out_specs=pl.BlockSpec((tm, tn), lambda j, i: (i, j)),
        out_shape=jax.ShapeDtypeStruct((n, PROJ_W), F32),
        scratch_shapes=[pltpu.VMEM((tn, d), BF16)],
        compiler_params=_params("arbitrary", "arbitrary"),
        name="in_proj",
    )(hn, w_t)


def _mlstm_block(q_ref, k_ref, v_ref, o_ref, gcol_ref, grow_ref, cw_ref, cb_ref, bcol_ref, brow_ref,
                 hn_ref, y_ref, xpad_sc, c_sc, n_sc, m_sc, before_head):
    L = MIX_TB
    H = MLSTM_HEADS

    xpad_sc[8:8 + L, 0:MLSTM_QK_W] = q_ref[...]
    xpad_sc[8:8 + L, MLSTM_QK_W:2 * MLSTM_QK_W] = k_ref[...]
    acc = jnp.broadcast_to(cb_ref[...], (L, 2 * MLSTM_QK_W))
    for j in range(CONV_WIDTH):
        off = 8 - (CONV_WIDTH - 1) + j
        acc = acc + cw_ref[j:j + 1, :] * xpad_sc[off:off + L, :]
    xpad_sc[0:8, :] = xpad_sc[L:L + 8, :]
    qk = acc * _sigmoid(acc)

    gcol = gcol_ref[...] + bcol_ref[...]
    grow = grow_ref[...] + brow_ref[...]
    fcol = _log_sigmoid(gcol)
    r_i = lax.broadcasted_iota(jnp.int32, (L, L), 0)
    c_i = lax.broadcasted_iota(jnp.int32, (L, L), 1)
    causal = r_i >= c_i
    tril = jnp.where(causal, 1.0, 0.0).astype(BF16)
    bcol = _tri_cumsum(tril, fcol)
    brow = bcol.T

    for h in range(H):
        before_head[h]()
        b_c = bcol[:, H + h:H + h + 1]
        i_c = gcol[:, h:h + 1]
        b_r = brow[H + h:H + h + 1, :]
        i_r = grow[h:h + 1, :]
        m_prev = m_sc[h:h + 1, 0:1]

        d_log = jnp.where(causal, b_c - (b_r - i_r), -jnp.inf)
        inter_log = b_c + m_prev
        m_t = jnp.maximum(jnp.max(d_log, axis=1, keepdims=True), inter_log)

        q_h = qk[:, h * MLSTM_QK:(h + 1) * MLSTM_QK]
        k_h = qk[:, MLSTM_QK_W + h * MLSTM_QK:MLSTM_QK_W + (h + 1) * MLSTM_QK] * (MLSTM_QK ** -0.5)
        v_h = v_ref[:, h * MLSTM_V:(h + 1) * MLSTM_V].astype(BF16)
        q_b = q_h.astype(BF16)

        s = lax.dot_general(q_b, k_h.astype(BF16), (((1,), (1,)), ((), ())), preferred_element_type=F32)
        s = s * jnp.exp(d_log - m_t)
        w_inter = jnp.exp(inter_log - m_t)
        c_h = c_sc[h]
        n_h = n_sc[h:h + 1, :]
        lhs = jnp.concatenate([s.astype(BF16), (q_h * w_inter).astype(BF16)], axis=1)
        rhs = jnp.concatenate([v_h, c_h.astype(BF16)], axis=0)
        num = jnp.dot(lhs, rhs, preferred_element_type=F32)
        den = jnp.sum(s, axis=1, keepdims=True) + w_inter * jnp.sum(q_h * n_h, axis=1, keepdims=True)
        hh = num / jnp.maximum(jnp.abs(den), jnp.exp(-m_t))

        hn = hh * lax.rsqrt(jnp.mean(hh * hh, axis=-1, keepdims=True) + EPS)
        hn = hn * hn_ref[:, h * MLSTM_V:(h + 1) * MLSTM_V]
        y = hn * _sigmoid(o_ref[:, h * MLSTM_V:(h + 1) * MLSTM_V])
        y_ref[:, h * MLSTM_V:(h + 1) * MLSTM_V] = y.astype(y_ref.dtype)

        b_last = b_c[L - 1:L, :]
        a_log = b_last - b_c + i_c
        m_new = jnp.maximum(b_last + m_prev, jnp.max(a_log, axis=0, keepdims=True))
        w_a = jnp.exp(a_log - m_new)
        decay = jnp.exp(b_last + m_prev - m_new)
        kw = k_h * w_a
        c_sc[h] = decay * c_h + lax.dot_general(kw.astype(BF16), v_h, (((0,), (0,)), ((), ())),
                                                preferred_element_type=F32)
        n_sc[h:h + 1, :] = decay * n_h + jnp.sum(kw, axis=0, keepdims=True)
        m_sc[h:h + 1, :] = jnp.broadcast_to(m_new, (1, LANES))


def _hgrn_block(q_ref, f_ref, v_ref, og_ref, lbl_ref, hn_ref, y_ref, g_sc, k_sc, q_sc, gr_sc, st_sc,
                before_head):
    TB = MIX_TB
    H = HGRN_HEADS
    DK = HGRN_DK
    W = H * DK
    NT = TB // 8

    lbl = lbl_ref[...]
    lmax = jnp.max(lbl, axis=0, keepdims=True)
    le = jnp.exp(lbl - lmax)
    lb = le[1:2, :] / jnp.sum(le, axis=0, keepdims=True)

    f = lb + (1.0 - lb) * _sigmoid(f_ref[...])
    k_sc[...] = 1.0 - f
    qraw = q_ref[...]
    q_sc[...] = qraw * _sigmoid(qraw) * (DK ** -0.5)
    r_i = lax.broadcasted_iota(jnp.int32, (TB, TB), 0)
    c_i = lax.broadcasted_iota(jnp.int32, (TB, TB), 1)
    tri = jnp.where(r_i >= c_i, 1.0, 0.0).astype(BF16)
    g = _tri_cumsum(tri, jnp.log(f) * LOG2_E)
    g_sc[...] = g

    g3 = g.reshape(NT, 8, W)
    sub = lax.broadcasted_iota(jnp.int32, (NT, 8, W), 1)
    bit0 = (sub & 1) != 0
    bit1 = (sub & 2) != 0
    bit2 = (sub & 4) != 0
    last2 = jnp.where(bit0, g3, pltpu.roll(g3, 7, 1))
    last4 = jnp.where(bit1, last2, pltpu.roll(last2, 6, 1))
    last4_r = pltpu.roll(last4, 4, 1)
    last8 = jnp.where(bit2, last4, last4_r).reshape(TB, W)
    gr_sc[0] = jnp.where(bit0, pltpu.roll(g3, 1, 1), g3).reshape(TB, W)
    gr_sc[1] = jnp.where(bit1, pltpu.roll(last2, 2, 1), last2).reshape(TB, W)
    gr_sc[2] = jnp.where(bit2, last4_r, last4).reshape(TB, W)
    for li, lvl in enumerate(HGRN_COARSE_LEVELS):
        groups = []
        for base in range(0, TB, 2 * lvl):
            src = last8[base + lvl - 8:base + lvl, :]
            groups.extend([src] * (2 * lvl // 8))
        gr_sc[3 + li] = jnp.concatenate(groups, axis=0)

    x_i = r_i ^ c_i
    nt_dims = (((1,), (1,)), ((), ()))
    for h in range(H):
        before_head[h]()
        cols = slice(h * DK, (h + 1) * DK)
        q_h = q_sc[:, cols]
        k_h = k_sc[:, cols]
        g_h = g_sc[:, cols]
        v_h = v_ref[:, cols].astype(BF16)

        q_b = q_h.astype(BF16)
        k_b = k_h.astype(BF16)
        a = jnp.broadcast_to(jnp.sum(q_h * k_h, axis=1, keepdims=True), (TB, TB))
        for li in range(HGRN_NUM_LEVELS):
            lvl = 1 << li
            e = jnp.exp2(_neg_abs(g_h - gr_sc[li, :, cols])).astype(BF16)
            qe = q_b * e
            ke = k_b * e
            if lvl < HGRN_HALF_LHS_MIN_LEVEL:
                p = lax.dot_general(qe, ke, nt_dims, preferred_element_type=F32)
                a = jnp.where(x_i >= lvl, p, a)
            else:
                ups = [slice(base + lvl, base + 2 * lvl) for base in range(0, TB, 2 * lvl)]
                p_up = lax.dot_general(jnp.concatenate([qe[u] for u in ups], axis=0), ke, nt_dims,
                                       preferred_element_type=F32)
                parts = []
                for n_up, u in enumerate(ups):
                    parts.append(a[u.start - lvl:u.start])
                    parts.append(jnp.where(x_i[u] >= lvl, p_up[n_up * lvl:(n_up + 1) * lvl], a[u]))
                a = jnp.concatenate(parts, axis=0)
        a = jnp.where(r_i >= c_i, a, 0.0)

        st = st_sc[h]
        g_last = g_h[TB - 1:TB, :]
        qd = (q_h * jnp.exp2(g_h)).astype(BF16)
        lhs = jnp.concatenate([a.astype(BF16), qd], axis=1)
        rhs = jnp.concatenate([v_h, st.astype(BF16)], axis=0)
        o = jnp.dot(lhs, rhs, preferred_element_type=F32)
        kd = (k_h * jnp.exp2(g_last - g_h)).astype(BF16)
        decay = jnp.exp2(g_h[TB - 8:TB, :].T[:, 7:8])
        st_sc[h] = decay * st + lax.dot_general(kd, v_h, (((0,), (0,)), ((), ())),
                                                preferred_element_type=F32)

        on = o * lax.rsqrt(jnp.mean(o * o, axis=-1, keepdims=True) + EPS) * hn_ref[:, cols]
        og = og_ref[:, cols]
        y_ref[:, cols] = (on * (og * _sigmoid(og))).astype(y_ref.dtype)


def _mix_kernel(mq_ref, mk_ref, mv_ref, mo_ref, gcol_ref, grow_ref, cw_ref, cb_ref, bcol_ref, brow_ref,
                mhn_ref, hq_ref, hf_ref, hi_ref, hg_ref, lbl_ref, hhn_ref,
                x_ref, gm_ref, gh_ref, wm_ref, wh_ref, wo_ref,
                out_ref,
                ym_sc, yh_sc, merged_sc, xpad_sc, c_sc, n_sc, m_sc, g_sc, k_sc, q_sc, gr_sc, st_sc,
                *, blocks_per_seq):
    s = pl.program_id(0)
    cur = s % 2
    prv = 1 - cur

    @pl.when(s == 0)
    def _():
        ym_sc[...] = jnp.zeros_like(ym_sc)
        yh_sc[...] = jnp.zeros_like(yh_sc)

    @pl.when(s % blocks_per_seq == 0)
    def _():
        xpad_sc[0:8, :] = jnp.zeros((8, 2 * MLSTM_QK_W), F32)
        c_sc[...] = jnp.zeros_like(c_sc)
        n_sc[...] = jnp.zeros_like(n_sc)
        m_sc[...] = jnp.zeros_like(m_sc)
        st_sc[...] = jnp.zeros_like(st_sc)

    def gate_piece(c):
        def run():
            cols = slice(c * MERGE_GATE_COLS, (c + 1) * MERGE_GATE_COLS)
            pm = jnp.dot(ym_sc[prv], wm_ref[:, cols], preferred_element_type=F32)
            ph = jnp.dot(yh_sc[prv], wh_ref[:, cols], preferred_element_type=F32)
            merged = _sigmoid(gm_ref[:, cols]) * pm + _sigmoid(gh_ref[:, cols]) * ph
            merged_sc[:, cols] = merged.astype(BF16)
        return run

    def out_piece(c):
        def run():
            cols = slice(c * MERGE_OUT_COLS, (c + 1) * MERGE_OUT_COLS)
            out_ref[:, cols] = x_ref[:, cols] + jnp.dot(merged_sc[...], wo_ref[:, cols],
                                                        preferred_element_type=F32)
        return run

    pieces = ([gate_piece(c) for c in range(D_MODEL // MERGE_GATE_COLS)]
              + [out_piece(c) for c in range(D_MODEL // MERGE_OUT_COLS)])
    n_heads = MLSTM_HEADS + HGRN_HEADS
    n_slots = MIX_SUBBLOCKS * n_heads
    slot_of = [(i * n_slots) // len(pieces) for i in range(len(pieces))]
    hooks = [(pieces[slot_of.index(i)] if i in slot_of else (lambda: None)) for i in range(n_slots)]

    for sub in range(MIX_SUBBLOCKS):
        rows = pl.ds(sub * MIX_TB, MIX_TB)
        sub_hooks = hooks[sub * n_heads:(sub + 1) * n_heads]
        _mlstm_block(mq_ref.at[rows], mk_ref.at[rows], mv_ref.at[rows], mo_ref.at[rows], gcol_ref.at[rows],
                     grow_ref.at[:, rows], cw_ref, cb_ref, bcol_ref, brow_ref, mhn_ref,
                     ym_sc.at[cur, rows], xpad_sc, c_sc, n_sc, m_sc, sub_hooks[:MLSTM_HEADS])
        _hgrn_block(hq_ref.at[rows], hf_ref.at[rows], hi_ref.at[rows], hg_ref.at[rows], lbl_ref, hhn_ref,
                    yh_sc.at[cur, rows], g_sc, k_sc, q_sc, gr_sc, st_sc, sub_hooks[MLSTM_HEADS:])


def _const_spec(shape):
    return pl.BlockSpec(shape, lambda *_: (0,) * len(shape), pipeline_mode=pl.Buffered(1))


def _mix(x2d, proj, gates_col, gates_row, conv_w, conv_b, bias_col, bias_row, m_head_norm,
         lb_logits, h_head_norm, w_m, w_h, w_o, *, seq_len):
    n, d = x2d.shape
    tb = MIX_TB * MIX_SUBBLOCKS
    nblk = n // tb
    qw, vw, hw = MLSTM_QK_W, MLSTM_V_W, HGRN_W

    def cur(col_block):
        return lambda s: (jnp.minimum(s, nblk - 1), col_block)

    def prev(col_block):
        return lambda s: (jnp.maximum(s - 1, 0), col_block)

    return pl.pallas_call(
        functools.partial(_mix_kernel, blocks_per_seq=seq_len // tb),
        grid=(nblk + 1,),
        in_specs=[
            pl.BlockSpec((tb, qw), cur(COL_MQ // qw)),
            pl.BlockSpec((tb, qw), cur(COL_MK // qw)),
            pl.BlockSpec((tb, vw), cur(COL_MV // vw)),
            pl.BlockSpec((tb, vw), cur(COL_MO // vw)),
            pl.BlockSpec((tb, LANES), cur(0)),
            pl.BlockSpec((2 * MLSTM_HEADS, tb), lambda s: (0, jnp.minimum(s, nblk - 1))),
            _const_spec((CONV_WIDTH, 2 * qw)),
            _const_spec((1, 2 * qw)),
            _const_spec((1, LANES)),
            _const_spec((2 * MLSTM_HEADS, 1)),
            _const_spec((1, vw)),
            pl.BlockSpec((tb, hw), cur(COL_HQ // hw)),
            pl.BlockSpec((tb, hw), cur(COL_HF // hw)),
            pl.BlockSpec((tb, hw), cur(COL_HI // hw)),
            pl.BlockSpec((tb, hw), cur(COL_HG // hw)),
            _const_spec((2, hw)),
            _const_spec((1, hw)),
            pl.BlockSpec((tb, d), prev(0)),
            pl.BlockSpec((tb, d), prev(COL_GM // d)),
            pl.BlockSpec((tb, d), prev(COL_GH // d)),
            _const_spec(w_m.shape),
            _const_spec(w_h.shape),
            _const_spec(w_o.shape),
        ],
        out_specs=pl.BlockSpec((tb, d), prev(0)),
        out_shape=jax.ShapeDtypeStruct((n, d), F32),
        scratch_shapes=[
            pltpu.VMEM((2, tb, vw), BF16),
            pltpu.VMEM((2, tb, hw), BF16),
            pltpu.VMEM((tb, d), BF16),
            pltpu.VMEM((MIX_TB + 8, 2 * qw), F32),
            pltpu.VMEM((MLSTM_HEADS, MLSTM_QK, MLSTM_V), F32),
            pltpu.VMEM((8, MLSTM_QK), F32),
            pltpu.VMEM((8, LANES), F32),
            pltpu.VMEM((MIX_TB, hw), F32),
            pltpu.VMEM((MIX_TB, hw), F32),
            pltpu.VMEM((MIX_TB, hw), F32),
            pltpu.VMEM((HGRN_NUM_LEVELS, MIX_TB, hw), F32),
            pltpu.VMEM((HGRN_HEADS, HGRN_DK, HGRN_DV), F32),
        ],
        compiler_params=_params("arbitrary"),
        name="mix",
    )(proj, proj, proj, proj, gates_col, gates_row, conv_w, conv_b, bias_col, bias_row, m_head_norm,
      proj, proj, proj, proj, lb_logits, h_head_norm, x2d, proj, proj, w_m, w_h, w_o)


def _xattn_prep_kernel(m_ref, g_ref, wk_ref, wv_ref, wq_ref, wo_ref, qk_ref, vo_ref):
    wk = wk_ref[...].astype(BF16)
    wv = wv_ref[...].astype(BF16)
    wq = wq_ref[...].astype(BF16)
    wo = wo_ref[...].astype(BF16)
    for i in range(m_ref.shape[0]):
        m = _rms(m_ref[i], g_ref[...]).astype(BF16)
        k = jnp.dot(m, wk, preferred_element_type=F32).astype(BF16)
        v = jnp.dot(m, wv, preferred_element_type=F32).astype(BF16)
        qk = lax.dot_general(wq, k, (((1,), (1,)), ((), ())), preferred_element_type=F32)
        qk_ref[i] = (qk * (XATTN_HEAD_DIM ** -0.5)).astype(qk_ref.dtype)
        vo_ref[i] = jnp.dot(v, wo, preferred_element_type=F32).astype(vo_ref.dtype)


def _xattn_prep(mem, g, w_kv, w_q, w_o):
    b, m, d = mem.shape
    hd = XATTN_HEAD_DIM
    nh = XATTN_HEADS
    return pl.pallas_call(
        _xattn_prep_kernel,
        grid=(nh,),
        in_specs=[
            pl.BlockSpec((b, m, d), lambda h: (0, 0, 0)),
            pl.BlockSpec((1, d), lambda h: (0, 0)),
            pl.BlockSpec((d, hd), lambda h: (0, h)),
            pl.BlockSpec((d, hd), lambda h: (0, nh + h)),
            pl.BlockSpec((d, hd), lambda h: (0, h)),
            pl.BlockSpec((hd, d), lambda h: (h, 0)),
        ],
        out_specs=[
            pl.BlockSpec((b, d, m), lambda h: (0, 0, h)),
            pl.BlockSpec((b, m, d), lambda h: (0, h, 0)),
        ],
        out_shape=[
            jax.ShapeDtypeStruct((b, d, nh * m), BF16),
            jax.ShapeDtypeStruct((b, nh * m, d), BF16),
        ],
        compiler_params=_params("arbitrary"),
        name="xattn_prep",
    )(mem, g.reshape(1, d), w_kv, w_kv, w_q, w_o)


def _xattn_kernel(x_ref, g_ref, qk_ref, vo_ref, o_ref, *, mem_len):
    x = x_ref[0]
    h = _rms(x, g_ref[...]).astype(BF16)
    s = jnp.dot(h, qk_ref[0], preferred_element_type=F32)
    probs = []
    for hd in range(XATTN_HEADS):
        s_h = s[:, hd * mem_len:(hd + 1) * mem_len]
        e = jnp.exp(s_h - jnp.max(s_h, axis=-1, keepdims=True))
        probs.append((e / jnp.sum(e, axis=-1, keepdims=True)).astype(BF16))
    p = jnp.concatenate(probs, axis=-1)
    o_ref[0] = x + jnp.dot(p, vo_ref[0], preferred_element_type=F32)


def _xattn(x, g, qk, vo, *, tm=512):
    b, t, d = x.shape
    hm = qk.shape[2]
    return pl.pallas_call(
        functools.partial(_xattn_kernel, mem_len=hm // XATTN_HEADS),
        grid=(b, t // tm),
        in_specs=[
            pl.BlockSpec((1, tm, d), lambda i, c: (i, c, 0)),
            _const_spec((1, d)),
            pl.BlockSpec((1, d, hm), lambda i, c: (i, 0, 0)),
            pl.BlockSpec((1, hm, d), lambda i, c: (i, 0, 0)),
        ],
        out_specs=pl.BlockSpec((1, tm, d), lambda i, c: (i, c, 0)),
        out_shape=jax.ShapeDtypeStruct((b, t, d), F32),
        compiler_params=_params("parallel", "parallel"),
        name="xattn",
    )(x, g.reshape(1, d), qk, vo)


def kernel(x, mem, norm_ffn1, ffn1_w1, ffn1_w3, ffn1_w2, norm_mix, w_in, mlstm_conv_w, mlstm_conv_b,
           mlstm_ig_bias, mlstm_fg_bias, mlstm_head_norm, hgrn_lb_logits, hgrn_head_norm, w_proj_m,
           w_proj_h, w_out, norm_xattn, norm_mem, xattn_wq, xattn_wkv, xattn_wo, norm_ffn2, ffn2_w1,
           ffn2_w3, ffn2_w2, norm_final):
    b, t, d = x.shape
    depth = norm_ffn1.shape[0]
    assert depth == 1 and hgrn_lb_logits.shape[0] == 2
    n = b * t
    l = 0
    bf = lambda w: w.astype(BF16)

    w_in_t = jnp.swapaxes(w_in[l], 0, 1)
    n_gate = 2 * MLSTM_HEADS
    gate_bias = jnp.concatenate([mlstm_ig_bias[l], mlstm_fg_bias[l]]).astype(F32)
    bias_col = jnp.pad(gate_bias, (0, LANES - n_gate)).reshape(1, LANES)
    bias_row = gate_bias.reshape(n_gate, 1)

    x1, hn, gates_col, gates_row = _ffn(x.reshape(n, d), norm_ffn1[l], ffn1_w1[l], ffn1_w3[l], ffn1_w2[l],
                                        norm_mix[l], w_in_t)

    proj = _in_proj(hn, w_in_t)
    x2 = _mix(x1, proj, gates_col, gates_row, mlstm_conv_w[l], mlstm_conv_b[l].reshape(1, -1),
              bias_col, bias_row, mlstm_head_norm[l].reshape(1, -1), hgrn_lb_logits,
              hgrn_head_norm[l].reshape(1, -1), bf(w_proj_m[l]), bf(w_proj_h[l]), bf(w_out[l]), seq_len=t)

    qk, vo = _xattn_prep(mem, norm_mem[l], xattn_wkv[l], xattn_wq[l], xattn_wo[l])
    x3 = _xattn(x2.reshape(b, t, d), norm_xattn[l], qk, vo)

    (out,) = _ffn(x3.reshape(n, d), norm_ffn2[l], ffn2_w1[l], ffn2_w3[l], ffn2_w2[l], norm_final)
    return out.reshape(b, t, d)
```

```python
import functools

import jax
import jax.numpy as jnp
from jax import lax
from jax.experimental import pallas as pl
from jax.experimental.pallas import tpu as pltpu

F32 = jnp.float32
BF16 = jnp.bfloat16
EPS = 1e-6
LOG2_E = 1.4426950408889634

D_MODEL = 2048
MLSTM_HEADS = 4
MLSTM_QK = 128
MLSTM_V = 256
CONV_WIDTH = 4
HGRN_HEADS = 8
HGRN_DK = 128
HGRN_DV = 128
XATTN_HEADS = 4
XATTN_HEAD_DIM = D_MODEL // XATTN_HEADS

MLSTM_QK_W = MLSTM_HEADS * MLSTM_QK
MLSTM_V_W = MLSTM_HEADS * MLSTM_V
HGRN_W = HGRN_HEADS * HGRN_DK

VMEM_LIMIT_BYTES = 60000 * 1024
LANES = 128

COL_GM = 0
COL_GH = COL_GM + D_MODEL
COL_MQ = COL_GH + D_MODEL
COL_MK = COL_MQ + MLSTM_QK_W
COL_MV = COL_MK + MLSTM_QK_W
COL_MO = COL_MV + MLSTM_V_W
COL_HQ = COL_MO + MLSTM_V_W
COL_HF = COL_HQ + HGRN_W
COL_HI = COL_HF + HGRN_W
COL_HG = COL_HI + HGRN_W
PROJ_W = COL_HG + HGRN_W

MIX_GATE_ROW = 2 * MLSTM_QK_W + 2 * MLSTM_V_W
MIX_HGRN_ROW = MIX_GATE_ROW + 2 * MLSTM_HEADS
MIX_MERGE_ROW = MIX_HGRN_ROW + 4 * HGRN_W

PROJ_SUB_ROWS = 1024
FFN_NORM_ROWS = 128
FFN_FINISH_ROWS = 128
FFN_SUB_COLS = 256

MIX_TB = 128
HGRN_NUM_LEVELS = MIX_TB.bit_length() - 1
MIX_SUBBLOCKS = 2
MERGE_GATE_COLS = 512
MERGE_OUT_COLS = 256
HGRN_HALF_LHS_MIN_LEVEL = 16
HGRN_COARSE_LEVELS = tuple(1 << i for i in range(3, HGRN_NUM_LEVELS))


def _rms(x, g):
    return x * lax.rsqrt(jnp.mean(x * x, axis=-1, keepdims=True) + EPS) * g


def _sigmoid(x):
    return 1.0 / (1.0 + jnp.exp(-x))


def _log_sigmoid(x):
    return jnp.minimum(x, 0.0) - jnp.log(1.0 + jnp.exp(-jnp.abs(x)))


def _split3(x):
    hi = x.astype(BF16)
    r1 = x - hi.astype(F32)
    mid = r1.astype(BF16)
    lo = (r1 - mid.astype(F32)).astype(BF16)
    return hi, mid, lo


def _tri_cumsum(tri, x):
    return sum(jnp.dot(tri, t, preferred_element_type=F32) for t in _split3(x))


def _neg_abs(x):
    return pltpu.bitcast(pltpu.bitcast(x, jnp.uint32) | jnp.uint32(0x80000000), F32)


def _params(*sem):
    return pltpu.CompilerParams(dimension_semantics=sem, vmem_limit_bytes=VMEM_LIMIT_BYTES)


def _ffn_kernel(*refs, mode):
    if mode == "mix":
        (x_hbm, g_ref, w1_ref, w3_ref, w2_ref, gn_ref, wg_ref,
         o_hbm, hn_hbm, gcol_ref, grow_ref, h_sc, x_buf, acc_sc, x_sem, o_sem, hn_sc, hn_sem) = refs
    else:
        x_hbm, g_ref, w1_ref, w3_ref, w2_ref, gn_ref, o_hbm, h_sc, x_buf, acc_sc, x_sem, o_sem = refs
    i = pl.program_id(0)
    j = pl.program_id(1)
    n_i = pl.num_programs(0)
    tm = x_buf.shape[0]
    n_acc = acc_sc.shape[0]

    def acc_slot(tile):
        return acc_sc.at[tile % n_acc if n_acc > 1 else 0]

    def tile_rows(tile):
        return pl.ds(pl.multiple_of(tile * tm, tm), tm)

    def x_copy(tile):
        return pltpu.make_async_copy(x_hbm.at[tile_rows(tile), :], x_buf, x_sem)

    def o_copy(tile):
        return pltpu.make_async_copy(acc_slot(tile), o_hbm.at[tile_rows(tile), :], o_sem)

    def hn_copy(tile):
        return pltpu.make_async_copy(hn_sc, hn_hbm.at[tile_rows(tile), :], hn_sem)

    def row_chunk(r, size):
        return pl.ds(pl.multiple_of(r * size, size), size)

    acc_ref = acc_slot(i)

    def swiglu_passes(first):
        h = h_sc[...]
        for c in range(w1_ref.shape[1] // FFN_SUB_COLS):
            cols = slice(c * FFN_SUB_COLS, (c + 1) * FFN_SUB_COLS)
            a = jnp.dot(h, w1_ref[:, cols].astype(BF16), preferred_element_type=F32)
            b = jnp.dot(h, w3_ref[:, cols].astype(BF16), preferred_element_type=F32)
            act = (a * _sigmoid(a) * b).astype(BF16)
            part = jnp.dot(act, w2_ref[cols, :].astype(BF16), preferred_element_type=F32)
            if first and c == 0:
                acc_ref[...] = 2.0 * x_buf[...] + part
            else:
                acc_ref[...] += part

    @pl.when(j == 0)
    def _():
        @pl.when(i == 0)
        def _():
            x_copy(0).start()

        x_copy(i).wait()

        def norm_rows(r, carry):
            rows = row_chunk(r, FFN_NORM_ROWS)
            h_sc[rows, :] = _rms(x_buf[rows, :], g_ref[...]).astype(BF16)
            return carry
        lax.fori_loop(0, tm // FFN_NORM_ROWS, norm_rows, 0)

        if n_acc == 1:
            @pl.when(i > 0)
            def _():
                o_copy(i - 1).wait()

        swiglu_passes(first=True)

    @pl.when(j > 0)
    def _():
        swiglu_passes(first=False)

    @pl.when(jnp.logical_and(j == 1, i + 1 < n_i))
    def _():
        x_copy(i + 1).start()

    @pl.when(j == pl.num_programs(1) - 1)
    def _():
        if mode == "final":
            def finish_rows(r, carry):
                rows = row_chunk(r, FFN_FINISH_ROWS)
                acc_ref[rows, :] = _rms(0.5 * acc_ref[rows, :], gn_ref[...])
                return carry
            lax.fori_loop(0, tm // FFN_FINISH_ROWS, finish_rows, 0)
        else:
            @pl.when(i > 0)
            def _():
                hn_copy(i - 1).wait()

            y = 0.5 * acc_ref[...]
            acc_ref[...] = y
            hn = _rms(y, gn_ref[...]).astype(BF16)
            hn_sc[...] = hn
            nt_dims = (((1,), (1,)), ((), ()))
            wg = wg_ref[...].astype(BF16)
            wg_pad = jnp.concatenate([wg, jnp.zeros((LANES - wg.shape[0], wg.shape[1]), BF16)], axis=0)
            gcol = lax.dot_general(hn, wg_pad, nt_dims, preferred_element_type=F32)
            gcol_ref[...] = gcol
            grow_ref[...] = gcol.T[0:wg.shape[0], :]

        if n_acc > 1:
            @pl.when(i > 0)
            def _():
                o_copy(i - 1).wait()

        o_copy(i).start()
        if mode == "mix":
            hn_copy(i).start()

        @pl.when(i == n_i - 1)
        def _():
            o_copy(i).wait()
            if mode == "mix":
                hn_copy(i).wait()


def _ffn(x2d, g, w1, w3, w2, g_next, w_gates_t=None, *, tm=1024, tf=512):
    n, d = x2d.shape
    f = w1.shape[1]
    mode = "final" if w_gates_t is None else "mix"
    in_specs = [
        pl.BlockSpec(memory_space=pl.ANY),
        pl.BlockSpec((1, d), lambda i, j: (0, 0)),
        pl.BlockSpec((d, tf), lambda i, j: (0, j)),
        pl.BlockSpec((d, tf), lambda i, j: (0, j)),
        pl.BlockSpec((tf, d), lambda i, j: (j, 0)),
        pl.BlockSpec((1, d), lambda i, j: (0, 0)),
    ]
    args = [x2d, g.reshape(1, d), w1, w3, w2, g_next.reshape(1, d)]
    out_specs = [pl.BlockSpec(memory_space=pl.ANY)]
    out_shape = [jax.ShapeDtypeStruct((n, d), F32)]
    scratch_shapes = [
        pltpu.VMEM((tm, d), BF16),
        pltpu.VMEM((tm, d), F32),
        pltpu.VMEM((2 if mode == "final" else 1, tm, d), F32),
        pltpu.SemaphoreType.DMA(()),
        pltpu.SemaphoreType.DMA(()),
    ]
    if mode == "mix":
        ng = 2 * MLSTM_HEADS
        in_specs.append(pl.BlockSpec((ng, d), lambda i, j: (MIX_GATE_ROW // ng, 0)))
        args.append(w_gates_t)
        out_specs += [
            pl.BlockSpec(memory_space=pl.ANY),
            pl.BlockSpec((tm, LANES), lambda i, j: (i, 0)),
            pl.BlockSpec((ng, tm), lambda i, j: (0, i)),
        ]
        out_shape += [
            jax.ShapeDtypeStruct((n, d), BF16),
            jax.ShapeDtypeStruct((n, LANES), F32),
            jax.ShapeDtypeStruct((ng, n), F32),
        ]
        scratch_shapes += [pltpu.VMEM((tm, d), BF16), pltpu.SemaphoreType.DMA(())]
    return pl.pallas_call(
        functools.partial(_ffn_kernel, mode=mode),
        grid=(n // tm, f // tf),
        in_specs=in_specs,
        out_specs=out_specs,
        out_shape=out_shape,
        scratch_shapes=scratch_shapes,
        compiler_params=_params("arbitrary", "arbitrary"),
        name="ffn_" + mode,
    )(*args)


def _in_proj_kernel(h_ref, wt_ref, p_ref, w_sc):
    @pl.when(pl.program_id(1) == 0)
    def _():
        w_sc[...] = wt_ref[...].astype(BF16)

    for r in range(0, h_ref.shape[0], PROJ_SUB_ROWS):
        rows = slice(r, r + PROJ_SUB_ROWS)
        p_ref[rows, :] = lax.dot_general(h_ref[rows, :], w_sc[...], (((1,), (1,)), ((), ())),
                                         preferred_element_type=F32)


def _proj_row_offset(j, tn):
    n_gate_tiles = 2 * D_MODEL // tn
    n_mlstm_tiles = MIX_GATE_ROW // tn
    off8 = jnp.where(j < n_gate_tiles, MIX_MERGE_ROW // 8 + j * (tn // 8),
                     jnp.where(j < n_gate_tiles + n_mlstm_tiles, (j - n_gate_tiles) * (tn // 8),
                               MIX_HGRN_ROW // 8 + (j - n_gate_tiles - n_mlstm_tiles) * (tn // 8)))
    return off8 * 8


def _in_proj(hn, w_t, *, tm=2048, tn=1024):
    n, d = hn.shape
    return pl.pallas_call(
        _in_proj_kernel,
        grid=(PROJ_W // tn, n // tm),
        in_specs=[
            pl.BlockSpec((tm, d), lambda j, i: (i, 0)),
            pl.BlockSpec((pl.Element(tn), pl.Element(d)), lambda j, i: (_proj_row_offset(j, tn), 0)),
        ],
        out_specs=pl.BlockSpec((tm, tn), lambda j, i: (i, j)),
        out_shape=jax.ShapeDtypeStruct((n, PROJ_W), F32),
        scratch_shapes=[pltpu.VMEM((tn, d), BF16)],
        compiler_params=_params("arbitrary", "arbitrary"),
        name="in_proj",
    )(hn, w_t)


def _mlstm_block(q_ref, k_ref, v_ref, o_ref, gcol_ref, grow_ref, cw_ref, cb_ref, bcol_ref, brow_ref,
                 hn_ref, y_ref, xpad_sc, c_sc, n_sc, m_sc, before_head):
    L = MIX_TB
    H = MLSTM_HEADS

    xpad_sc[8:8 + L, 0:MLSTM_QK_W] = q_ref[...]
    xpad_sc[8:8 + L, MLSTM_QK_W:2 * MLSTM_QK_W] = k_ref[...]
    acc = jnp.broadcast_to(cb_ref[...], (L, 2 * MLSTM_QK_W))
    for j in range(CONV_WIDTH):
        off = 8 - (CONV_WIDTH - 1) + j
        acc = acc + cw_ref[j:j + 1, :] * xpad_sc[off:off + L, :]
    xpad_sc[0:8, :] = xpad_sc[L:L + 8, :]
    qk = acc * _sigmoid(acc)

    gcol = gcol_ref[...] + bcol_ref[...]
    grow = grow_ref[...] + brow_ref[...]
    fcol = _log_sigmoid(gcol)
    r_i = lax.broadcasted_iota(jnp.int32, (L, L), 0)
    c_i = lax.broadcasted_iota(jnp.int32, (L, L), 1)
    causal = r_i >= c_i
    tril = jnp.where(causal, 1.0, 0.0).astype(BF16)
    bcol = _tri_cumsum(tril, fcol)
    brow = bcol.T

    for h in range(H):
        before_head[h]()
        b_c = bcol[:, H + h:H + h + 1]
        i_c = gcol[:, h:h + 1]
        b_r = brow[H + h:H + h + 1, :]
        i_r = grow[h:h + 1, :]
        m_prev = m_sc[h:h + 1, 0:1]

        d_log = jnp.where(causal, b_c - (b_r - i_r), -jnp.inf)
        inter_log = b_c + m_prev
        m_t = jnp.maximum(jnp.max(d_log, axis=1, keepdims=True), inter_log)

        q_h = qk[:, h * MLSTM_QK:(h + 1) * MLSTM_QK]
        k_h = qk[:, MLSTM_QK_W + h * MLSTM_QK:MLSTM_QK_W + (h + 1) * MLSTM_QK] * (MLSTM_QK ** -0.5)
        v_h = v_ref[:, h * MLSTM_V:(h + 1) * MLSTM_V].astype(BF16)
        q_b = q_h.astype(BF16)

        s = lax.dot_general(q_b, k_h.astype(BF16), (((1,), (1,)), ((), ())), preferred_element_type=F32)
        s = s * jnp.exp(d_log - m_t)
        w_inter = jnp.exp(inter_log - m_t)
        c_h = c_sc[h]
        n_h = n_sc[h:h + 1, :]
        lhs = jnp.concatenate([s.astype(BF16), (q_h * w_inter).astype(BF16)], axis=1)
        rhs = jnp.concatenate([v_h, c_h.astype(BF16)], axis=0)
        num = jnp.dot(lhs, rhs, preferred_element_type=F32)
        den = jnp.sum(s, axis=1, keepdims=True) + w_inter * jnp.sum(q_h * n_h, axis=1, keepdims=True)
        hh = num / jnp.maximum(jnp.abs(den), jnp.exp(-m_t))

        hn = hh * lax.rsqrt(jnp.mean(hh * hh, axis=-1, keepdims=True) + EPS)
        hn = hn * hn_ref[:, h * MLSTM_V:(h + 1) * MLSTM_V]
        y = hn * _sigmoid(o_ref[:, h * MLSTM_V:(h + 1) * MLSTM_V])
        y_ref[:, h * MLSTM_V:(h + 1) * MLSTM_V] = y.astype(y_ref.dtype)

        b_last = b_c[L - 1:L, :]
        a_log = b_last - b_c + i_c
        m_new = jnp.maximum(b_last + m_prev, jnp.max(a_log, axis=0, keepdims=True))
        w_a = jnp.exp(a_log - m_new)
        decay = jnp.exp(b_last + m_prev - m_new)
        kw = k_h * w_a
        c_sc[h] = decay * c_h + lax.dot_general(kw.astype(BF16), v_h, (((0,), (0,)), ((), ())),
                                                preferred_element_type=F32)
        n_sc[h:h + 1, :] = decay * n_h + jnp.sum(kw, axis=0, keepdims=True)
        m_sc[h:h + 1, :] = jnp.broadcast_to(m_new, (1, LANES))


def _hgrn_block(q_ref, f_ref, v_ref, og_ref, lbl_ref, hn_ref, y_ref, g_sc, k_sc, q_sc, gr_sc, st_sc,
                before_head):
    TB = MIX_TB
    H = HGRN_HEADS
    DK = HGRN_DK
    W = H * DK
    NT = TB // 8

    lbl = lbl_ref[...]
    lmax = jnp.max(lbl, axis=0, keepdims=True)
    le = jnp.exp(lbl - lmax)
    lb = le[1:2, :] / jnp.sum(le, axis=0, keepdims=True)

    f = lb + (1.0 - lb) * _sigmoid(f_ref[...])
    k_sc[...] = 1.0 - f
    qraw = q_ref[...]
    q_sc[...] = qraw * _sigmoid(qraw) * (DK ** -0.5)
    r_i = lax.broadcasted_iota(jnp.int32, (TB, TB), 0)
    c_i = lax.broadcasted_iota(jnp.int32, (TB, TB), 1)
    tri = jnp.where(r_i >= c_i, 1.0, 0.0).astype(BF16)
    g = _tri_cumsum(tri, jnp.log(f) * LOG2_E)
    g_sc[...] = g

    g3 = g.reshape(NT, 8, W)
    sub = lax.broadcasted_iota(jnp.int32, (NT, 8, W), 1)
    bit0 = (sub & 1) != 0
    bit1 = (sub & 2) != 0
    bit2 = (sub & 4) != 0
    last2 = jnp.where(bit0, g3, pltpu.roll(g3, 7, 1))
    last4 = jnp.where(bit1, last2, pltpu.roll(last2, 6, 1))
    last4_r = pltpu.roll(last4, 4, 1)
    last8 = jnp.where(bit2, last4, last4_r).reshape(TB, W)
    gr_sc[0] = jnp.where(bit0, pltpu.roll(g3, 1, 1), g3).reshape(TB, W)
    gr_sc[1] = jnp.where(bit1, pltpu.roll(last2, 2, 1), last2).reshape(TB, W)
    gr_sc[2] = jnp.where(bit2, last4_r, last4).reshape(TB, W)
    for li, lvl in enumerate(HGRN_COARSE_LEVELS):
        groups = []
        for base in range(0, TB, 2 * lvl):
            src = last8[base + lvl - 8:base + lvl, :]
            groups.extend([src] * (2 * lvl // 8))
        gr_sc[3 + li] = jnp.concatenate(groups, axis=0)

    x_i = r_i ^ c_i
    nt_dims = (((1,), (1,)), ((), ()))
    for h in range(H):
        before_head[h]()
        cols = slice(h * DK, (h + 1) * DK)
        q_h = q_sc[:, cols]
        k_h = k_sc[:, cols]
        g_h = g_sc[:, cols]
        v_h = v_ref[:, cols].astype(BF16)

        q_b = q_h.astype(BF16)
        k_b = k_h.astype(BF16)
        a = jnp.broadcast_to(jnp.sum(q_h * k_h, axis=1, keepdims=True), (TB, TB))
        for li in range(HGRN_NUM_LEVELS):
            lvl = 1 << li
            e = jnp.exp2(_neg_abs(g_h - gr_sc[li, :, cols])).astype(BF16)
            qe = q_b * e
            ke = k_b * e
            if lvl < HGRN_HALF_LHS_MIN_LEVEL:
                p = lax.dot_general(qe, ke, nt_dims, preferred_element_type=F32)
                a = jnp.where(x_i >= lvl, p, a)
            else:
                ups = [slice(base + lvl, base + 2 * lvl) for base in range(0, TB, 2 * lvl)]
                p_up = lax.dot_general(jnp.concatenate([qe[u] for u in ups], axis=0), ke, nt_dims,
                                       preferred_element_type=F32)
                parts = []
                for n_up, u in enumerate(ups):
                    parts.append(a[u.start - lvl:u.start])
                    parts.append(jnp.where(x_i[u] >= lvl, p_up[n_up * lvl:(n_up + 1) * lvl], a[u]))
                a = jnp.concatenate(parts, axis=0)
        a = jnp.where(r_i >= c_i, a, 0.0)

        st = st_sc[h]
        g_last = g_h[TB - 1:TB, :]
        qd = (q_h * jnp.exp2(g_h)).astype(BF16)
        lhs = jnp.concatenate([a.astype(BF16), qd], axis=1)
        rhs = jnp.concatenate([v_h, st.astype(BF16)], axis=0)
        o = jnp.dot(lhs, rhs, preferred_element_type=F32)
        kd = (k_h * jnp.exp2(g_last - g_h)).astype(BF16)
        decay = jnp.exp2(g_h[TB - 8:TB, :].T[:, 7:8])
        st_sc[h] = decay * st + lax.dot_general(kd, v_h, (((0,), (0,)), ((), ())),
                                                preferred_element_type=F32)

        on = o * lax.rsqrt(jnp.mean(o * o, axis=-1, keepdims=True) + EPS) * hn_ref[:, cols]
        og = og_ref[:, cols]
        y_ref[:, cols] = (on * (og * _sigmoid(og))).astype(y_ref.dtype)


def _mix_kernel(mq_ref, mk_ref, mv_ref, mo_ref, gcol_ref, grow_ref, cw_ref, cb_ref, bcol_ref, brow_ref,
                mhn_ref, hq_ref, hf_ref, hi_ref, hg_ref, lbl_ref, hhn_ref,
                x_ref, gm_ref, gh_ref, wm_ref, wh_ref, wo_ref,
                out_ref,
                ym_sc, yh_sc, merged_sc, xpad_sc, c_sc, n_sc, m_sc, g_sc, k_sc, q_sc, gr_sc, st_sc,
                *, blocks_per_seq):
    s = pl.program_id(0)
    cur = s % 2
    prv = 1 - cur

    @pl.when(s == 0)
    def _():
        ym_sc[...] = jnp.zeros_like(ym_sc)
        yh_sc[...] = jnp.zeros_like(yh_sc)

    @pl.when(s % blocks_per_seq == 0)
    def _():
        xpad_sc[0:8, :] = jnp.zeros((8, 2 * MLSTM_QK_W), F32)
        c_sc[...] = jnp.zeros_like(c_sc)
        n_sc[...] = jnp.zeros_like(n_sc)
        m_sc[...] = jnp.zeros_like(m_sc)
        st_sc[...] = jnp.zeros_like(st_sc)

    def gate_piece(c):
        def run():
            cols = slice(c * MERGE_GATE_COLS, (c + 1) * MERGE_GATE_COLS)
            pm = jnp.dot(ym_sc[prv], wm_ref[:, cols], preferred_element_type=F32)
            ph = jnp.dot(yh_sc[prv], wh_ref[:, cols], preferred_element_type=F32)
            merged = _sigmoid(gm_ref[:, cols]) * pm + _sigmoid(gh_ref[:, cols]) * ph
            merged_sc[:, cols] = merged.astype(BF16)
        return run

    def out_piece(c):
        def run():
            cols = slice(c * MERGE_OUT_COLS, (c + 1) * MERGE_OUT_COLS)
            out_ref[:, cols] = x_ref[:, cols] + jnp.dot(merged_sc[...], wo_ref[:, cols],
                                                        preferred_element_type=F32)
        return run

    pieces = ([gate_piece(c) for c in range(D_MODEL // MERGE_GATE_COLS)]
              + [out_piece(c) for c in range(D_MODEL // MERGE_OUT_COLS)])
    n_heads = MLSTM_HEADS + HGRN_HEADS
    n_slots = MIX_SUBBLOCKS * n_heads
    slot_of = [(i * n_slots) // len(pieces) for i in range(len(pieces))]
    hooks = [(pieces[slot_of.index(i)] if i in slot_of else (lambda: None)) for i in range(n_slots)]

    for sub in range(MIX_SUBBLOCKS):
        rows = pl.ds(sub * MIX_TB, MIX_TB)
        sub_hooks = hooks[sub * n_heads:(sub + 1) * n_heads]
        _mlstm_block(mq_ref.at[rows], mk_ref.at[rows], mv_ref.at[rows], mo_ref.at[rows], gcol_ref.at[rows],
                     grow_ref.at[:, rows], cw_ref, cb_ref, bcol_ref, brow_ref, mhn_ref,
                     ym_sc.at[cur, rows], xpad_sc, c_sc, n_sc, m_sc, sub_hooks[:MLSTM_HEADS])
        _hgrn_block(hq_ref.at[rows], hf_ref.at[rows], hi_ref.at[rows], hg_ref.at[rows], lbl_ref, hhn_ref,
                    yh_sc.at[cur, rows], g_sc, k_sc, q_sc, gr_sc, st_sc, sub_hooks[MLSTM_HEADS:])


def _const_spec(shape):
    return pl.BlockSpec(shape, lambda *_: (0,) * len(shape), pipeline_mode=pl.Buffered(1))


def _mix(x2d, proj, gates_col, gates_row, conv_w, conv_b, bias_col, bias_row, m_head_norm,
         lb_logits, h_head_norm, w_m, w_h, w_o, *, seq_len):
    n, d = x2d.shape
    tb = MIX_TB * MIX_SUBBLOCKS
    nblk = n // tb
    qw, vw, hw = MLSTM_QK_W, MLSTM_V_W, HGRN_W

    def cur(col_block):
        return lambda s: (jnp.minimum(s, nblk - 1), col_block)

    def prev(col_block):
        return lambda s: (jnp.maximum(s - 1, 0), col_block)

    return pl.pallas_call(
        functools.partial(_mix_kernel, blocks_per_seq=seq_len // tb),
        grid=(nblk + 1,),
        in_specs=[
            pl.BlockSpec((tb, qw), cur(COL_MQ // qw)),
            pl.BlockSpec((tb, qw), cur(COL_MK // qw)),
            pl.BlockSpec((tb, vw), cur(COL_MV // vw)),
            pl.BlockSpec((tb, vw), cur(COL_MO // vw)),
            pl.BlockSpec((tb, LANES), cur(0)),
            pl.BlockSpec((2 * MLSTM_HEADS, tb), lambda s: (0, jnp.minimum(s, nblk - 1))),
            _const_spec((CONV_WIDTH, 2 * qw)),
            _const_spec((1, 2 * qw)),
            _const_spec((1, LANES)),
            _const_spec((2 * MLSTM_HEADS, 1)),
            _const_spec((1, vw)),
            pl.BlockSpec((tb, hw), cur(COL_HQ // hw)),
            pl.BlockSpec((tb, hw), cur(COL_HF // hw)),
            pl.BlockSpec((tb, hw), cur(COL_HI // hw)),
            pl.BlockSpec((tb, hw), cur(COL_HG // hw)),
            _const_spec((2, hw)),
            _const_spec((1, hw)),
            pl.BlockSpec((tb, d), prev(0)),
            pl.BlockSpec((tb, d), prev(COL_GM // d)),
            pl.BlockSpec((tb, d), prev(COL_GH // d)),
            _const_spec(w_m.shape),
            _const_spec(w_h.shape),
            _const_spec(w_o.shape),
        ],
        out_specs=pl.BlockSpec((tb, d), prev(0)),
        out_shape=jax.ShapeDtypeStruct((n, d), F32),
        scratch_shapes=[
            pltpu.VMEM((2, tb, vw), BF16),
            pltpu.VMEM((2, tb, hw), BF16),
            pltpu.VMEM((tb, d), BF16),
            pltpu.VMEM((MIX_TB + 8, 2 * qw), F32),
            pltpu.VMEM((MLSTM_HEADS, MLSTM_QK, MLSTM_V), F32),
            pltpu.VMEM((8, MLSTM_QK), F32),
            pltpu.VMEM((8, LANES), F32),
            pltpu.VMEM((MIX_TB, hw), F32),
            pltpu.VMEM((MIX_TB, hw), F32),
            pltpu.VMEM((MIX_TB, hw), F32),
            pltpu.VMEM((HGRN_NUM_LEVELS, MIX_TB, hw), F32),
            pltpu.VMEM((HGRN_HEADS, HGRN_DK, HGRN_DV), F32),
        ],
        compiler_params=_params("arbitrary"),
        name="mix",
    )(proj, proj, proj, proj, gates_col, gates_row, conv_w, conv_b, bias_col, bias_row, m_head_norm,
      proj, proj, proj, proj, lb_logits, h_head_norm, x2d, proj, proj, w_m, w_h, w_o)


def _xattn_prep_kernel(m_ref, g_ref, wk_ref, wv_ref, wq_ref, wo_ref, qk_ref, vo_ref):
    wk = wk_ref[...].astype(BF16)
    wv = wv_ref[...].astype(BF16)
    wq = wq_ref[...].astype(BF16)
    wo = wo_ref[...].astype(BF16)
    for i in range(m_ref.shape[0]):
        m = _rms(m_ref[i], g_ref[...]).astype(BF16)
        k = jnp.dot(m, wk, preferred_element_type=F32).astype(BF16)
        v = jnp.dot(m, wv, preferred_element_type=F32).astype(BF16)
        qk = lax.dot_general(wq, k, (((1,), (1,)), ((), ())), preferred_element_type=F32)
        qk_ref[i] = (qk * (XATTN_HEAD_DIM ** -0.5)).astype(qk_ref.dtype)
        vo_ref[i] = jnp.dot(v, wo, preferred_element_type=F32).astype(vo_ref.dtype)


def _xattn_prep(mem, g, w_kv, w_q, w_o):
    b, m, d = mem.shape
    hd = XATTN_HEAD_DIM
    nh = XATTN_HEADS
    return pl.pallas_call(
        _xattn_prep_kernel,
        grid=(nh,),
        in_specs=[
            pl.BlockSpec((b, m, d), lambda h: (0, 0, 0)),
            pl.BlockSpec((1, d), lambda h: (0, 0)),
            pl.BlockSpec((d, hd), lambda h: (0, h)),
            pl.BlockSpec((d, hd), lambda h: (0, nh + h)),
            pl.BlockSpec((d, hd), lambda h: (0, h)),
            pl.BlockSpec((hd, d), lambda h: (h, 0)),
        ],
        out_specs=[
            pl.BlockSpec((b, d, m), lambda h: (0, 0, h)),
            pl.BlockSpec((b, m, d), lambda h: (0, h, 0)),
        ],
        out_shape=[
            jax.ShapeDtypeStruct((b, d, nh * m), BF16),
            jax.ShapeDtypeStruct((b, nh * m, d), BF16),
        ],
        compiler_params=_params("arbitrary"),
        name="xattn_prep",
    )(mem, g.reshape(1, d), w_kv, w_kv, w_q, w_o)


def _xattn_kernel(x_ref, g_ref, qk_ref, vo_ref, o_ref, *, mem_len):
    x = x_ref[0]
    h = _rms(x, g_ref[...]).astype(BF16)
    s = jnp.dot(h, qk_ref[0], preferred_element_type=F32)
    probs = []
    for hd in range(XATTN_HEADS):
        s_h = s[:, hd * mem_len:(hd + 1) * mem_len]
        e = jnp.exp(s_h - jnp.max(s_h, axis=-1, keepdims=True))
        probs.append((e / jnp.sum(e, axis=-1, keepdims=True)).astype(BF16))
    p = jnp.concatenate(probs, axis=-1)
    o_ref[0] = x + jnp.dot(p, vo_ref[0], preferred_element_type=F32)


def _xattn(x, g, qk, vo, *, tm=512):
    b, t, d = x.shape
    hm = qk.shape[2]
    return pl.pallas_call(
        functools.partial(_xattn_kernel, mem_len=hm // XATTN_HEADS),
        grid=(b, t // tm),
        in_specs=[
            pl.BlockSpec((1, tm, d), lambda i, c: (i, c, 0)),
            _const_spec((1, d)),
            pl.BlockSpec((1, d, hm), lambda i, c: (i, 0, 0)),
            pl.BlockSpec((1, hm, d), lambda i, c: (i, 0, 0)),
        ],
        out_specs=pl.BlockSpec((1, tm, d), lambda i, c: (i, c, 0)),
        out_shape=jax.ShapeDtypeStruct((b, t, d), F32),
        compiler_params=_params("parallel", "parallel"),
        name="xattn",
    )(x, g.reshape(1, d), qk, vo)


def kernel(x, mem, norm_ffn1, ffn1_w1, ffn1_w3, ffn1_w2, norm_mix, w_in, mlstm_conv_w, mlstm_conv_b,
           mlstm_ig_bias, mlstm_fg_bias, mlstm_head_norm, hgrn_lb_logits, hgrn_head_norm, w_proj_m,
           w_proj_h, w_out, norm_xattn, norm_mem, xattn_wq, xattn_wkv, xattn_wo, norm_ffn2, ffn2_w1,
           ffn2_w3, ffn2_w2, norm_final):
    b, t, d = x.shape
    depth = norm_ffn1.shape[0]
    assert depth == 1 and hgrn_lb_logits.shape[0] == 2
    n = b * t
    l = 0
    bf = lambda w: w.astype(BF16)

    w_in_t = jnp.swapaxes(w_in[l], 0, 1)
    n_gate = 2 * MLSTM_HEADS
    gate_bias = jnp.concatenate([mlstm_ig_bias[l], mlstm_fg_bias[l]]).astype(F32)
    bias_col = jnp.pad(gate_bias, (0, LANES - n_gate)).reshape(1, LANES)
    bias_row = gate_bias.reshape(n_gate, 1)

    x1, hn, gates_col, gates_row = _ffn(x.reshape(n, d), norm_ffn1[l], ffn1_w1[l], ffn1_w3[l], ffn1_w2[l],
                                        norm_mix[l], w_in_t)

    proj = _in_proj(hn, w_in_t)
    x2 = _mix(x1, proj, gates_col, gates_row, mlstm_conv_w[l], mlstm_conv_b[l].reshape(1, -1),
              bias_col, bias_row, mlstm_head_norm[l].reshape(1, -1), hgrn_lb_logits,
              hgrn_head_norm[l].reshape(1, -1), bf(w_proj_m[l]), bf(w_proj_h[l]), bf(w_out[l]), seq_len=t)

    qk, vo = _xattn_prep(mem, norm_mem[l], xattn_wkv[l], xattn_wq[l], xattn_wo[l])
    x3 = _xattn(x2.reshape(b, t, d), norm_xattn[l], qk, vo)

    (out,) = _ffn(x3.reshape(n, d), norm_ffn2[l], ffn2_w1[l], ffn2_w3[l], ffn2_w2[l], norm_final)
    return out.reshape(b, t, d)
```

```python
import functools

import jax
import jax.numpy as jnp
from jax import lax
from jax.experimental import pallas as pl
from jax.experimental.pallas import tpu as pltpu

F32 = jnp.float32
BF16 = jnp.bfloat16
EPS = 1e-6
LOG2_E = 1.4426950408889634

D_MODEL = 2048
MLSTM_HEADS = 4
MLSTM_QK = 128
MLSTM_V = 256
CONV_WIDTH = 4
HGRN_HEADS = 8
HGRN_DK = 128
HGRN_DV = 128
XATTN_HEADS = 4
XATTN_HEAD_DIM = D_MODEL // XATTN_HEADS

MLSTM_QK_W = MLSTM_HEADS * MLSTM_QK
MLSTM_V_W = MLSTM_HEADS * MLSTM_V
HGRN_W = HGRN_HEADS * HGRN_DK

VMEM_LIMIT_BYTES = 60000 * 1024
LANES = 128

COL_GM = 0
COL_GH = COL_GM + D_MODEL
COL_MQ = COL_GH + D_MODEL
COL_MK = COL_MQ + MLSTM_QK_W
COL_MV = COL_MK + MLSTM_QK_W
COL_MO = COL_MV + MLSTM_V_W
COL_HQ = COL_MO + MLSTM_V_W
COL_HF = COL_HQ + HGRN_W
COL_HI = COL_HF + HGRN_W
COL_HG = COL_HI + HGRN_W
PROJ_W = COL_HG + HGRN_W

MIX_GATE_ROW = 2 * MLSTM_QK_W + 2 * MLSTM_V_W
MIX_HGRN_ROW = MIX_GATE_ROW + 2 * MLSTM_HEADS
MIX_MERGE_ROW = MIX_HGRN_ROW + 4 * HGRN_W

PROJ_SUB_ROWS = 1024
FFN_NORM_ROWS = 128
FFN_FINISH_ROWS = 128
FFN_SUB_COLS = 256

MIX_TB = 128
HGRN_NUM_LEVELS = MIX_TB.bit_length() - 1
MIX_SUBBLOCKS = 2
MERGE_GATE_COLS = 512
MERGE_OUT_COLS = 256
HGRN_HALF_LHS_MIN_LEVEL = 16
HGRN_COARSE_LEVELS = tuple(1 << i for i in range(3, HGRN_NUM_LEVELS))


def _rms(x, g):
    return x * lax.rsqrt(jnp.mean(x * x, axis=-1, keepdims=True) + EPS) * g


def _sigmoid(x):
    return 1.0 / (1.0 + jnp.exp(-x))


def _log_sigmoid(x):
    return jnp.minimum(x, 0.0) - jnp.log(1.0 + jnp.exp(-jnp.abs(x)))


def _split3(x):
    hi = x.astype(BF16)
    r1 = x - hi.astype(F32)
    mid = r1.astype(BF16)
    lo = (r1 - mid.astype(F32)).astype(BF16)
    return hi, mid, lo


def _tri_cumsum(tri, x):
    return sum(jnp.dot(tri, t, preferred_element_type=F32) for t in _split3(x))


def _neg_abs(x):
    return pltpu.bitcast(pltpu.bitcast(x, jnp.uint32) | jnp.uint32(0x80000000), F32)


def _params(*sem):
    return pltpu.CompilerParams(dimension_semantics=sem, vmem_limit_bytes=VMEM_LIMIT_BYTES)


def _ffn_kernel(*refs, mode):
    if mode == "mix":
        (x_hbm, g_ref, w1_ref, w3_ref, w2_ref, gn_ref, wg_ref,
         o_hbm, hn_hbm, gcol_ref, grow_ref, h_sc, x_buf, acc_sc, x_sem, o_sem, hn_sc, hn_sem) = refs
    else:
        x_hbm, g_ref, w1_ref, w3_ref, w2_ref, gn_ref, o_hbm, h_sc, x_buf, acc_sc, x_sem, o_sem = refs
    i = pl.program_id(0)
    j = pl.program_id(1)
    n_i = pl.num_programs(0)
    tm = x_buf.shape[0]
    n_acc = acc_sc.shape[0]

    def acc_slot(tile):
        return acc_sc.at[tile % n_acc if n_acc > 1 else 0]

    def tile_rows(tile):
        return pl.ds(pl.multiple_of(tile * tm, tm), tm)

    def x_copy(tile):
        return pltpu.make_async_copy(x_hbm.at[tile_rows(tile), :], x_buf, x_sem)

    def o_copy(tile):
        return pltpu.make_async_copy(acc_slot(tile), o_hbm.at[tile_rows(tile), :], o_sem)

    def hn_copy(tile):
        return pltpu.make_async_copy(hn_sc, hn_hbm.at[tile_rows(tile), :], hn_sem)

    def row_chunk(r, size):
        return pl.ds(pl.multiple_of(r * size, size), size)

    acc_ref = acc_slot(i)

    def swiglu_passes(first):
        h = h_sc[...]
        for c in range(w1_ref.shape[1] // FFN_SUB_COLS):
            cols = slice(c * FFN_SUB_COLS, (c + 1) * FFN_SUB_COLS)
            a = jnp.dot(h, w1_ref[:, cols].astype(BF16), preferred_element_type=F32)
            b = jnp.dot(h, w3_ref[:, cols].astype(BF16), preferred_element_type=F32)
            act = (a * _sigmoid(a) * b).astype(BF16)
            part = jnp.dot(act, w2_ref[cols, :].astype(BF16), preferred_element_type=F32)
            if first and c == 0:
                acc_ref[...] = 2.0 * x_buf[...] + part
            else:
                acc_ref[...] += part

    @pl.when(j == 0)
    def _():
        @pl.when(i == 0)
        def _():
            x_copy(0).start()

        x_copy(i).wait()

        def norm_rows(r, carry):
            rows = row_chunk(r, FFN_NORM_ROWS)
            h_sc[rows, :] = _rms(x_buf[rows, :], g_ref[...]).astype(BF16)
            return carry
        lax.fori_loop(0, tm // FFN_NORM_ROWS, norm_rows, 0)

        if n_acc == 1:
            @pl.when(i > 0)
            def _():
                o_copy(i - 1).wait()

        swiglu_passes(first=True)

    @pl.when(j > 0)
    def _():
        swiglu_passes(first=False)

    @pl.when(jnp.logical_and(j == 1, i + 1 < n_i))
    def _():
        x_copy(i + 1).start()

    @pl.when(j == pl.num_programs(1) - 1)
    def _():
        if mode == "final":
            def finish_rows(r, carry):
                rows = row_chunk(r, FFN_FINISH_ROWS)
                acc_ref[rows, :] = _rms(0.5 * acc_ref[rows, :], gn_ref[...])
                return carry
            lax.fori_loop(0, tm // FFN_FINISH_ROWS, finish_rows, 0)
        else:
            @pl.when(i > 0)
            def _():
                hn_copy(i - 1).wait()

            y = 0.5 * acc_ref[...]
            acc_ref[...] = y
            hn = _rms(y, gn_ref[...]).astype(BF16)
            hn_sc[...] = hn
            nt_dims = (((1,), (1,)), ((), ()))
            wg = wg_ref[...].astype(BF16)
            wg_pad = jnp.concatenate([wg, jnp.zeros((LANES - wg.shape[0], wg.shape[1]), BF16)], axis=0)
            gcol = lax.dot_general(hn, wg_pad, nt_dims, preferred_element_type=F32)
            gcol_ref[...] = gcol
            grow_ref[...] = gcol.T[0:wg.shape[0], :]

        if n_acc > 1:
            @pl.when(i > 0)
            def _():
                o_copy(i - 1).wait()

        o_copy(i).start()
        if mode == "mix":
            hn_copy(i).start()

        @pl.when(i == n_i - 1)
        def _():
            o_copy(i).wait()
            if mode == "mix":
                hn_copy(i).wait()


def _ffn(x2d, g, w1, w3, w2, g_next, w_gates_t=None, *, tm=1024, tf=512):
    n, d = x2d.shape
    f = w1.shape[1]
    mode = "final" if w_gates_t is None else "mix"
    in_specs = [
        pl.BlockSpec(memory_space=pl.ANY),
        pl.BlockSpec((1, d), lambda i, j: (0, 0)),
        pl.BlockSpec((d, tf), lambda i, j: (0, j)),
        pl.BlockSpec((d, tf), lambda i, j: (0, j)),
        pl.BlockSpec((tf, d), lambda i, j: (j, 0)),
        pl.BlockSpec((1, d), lambda i, j: (0, 0)),
    ]
    args = [x2d, g.reshape(1, d), w1, w3, w2, g_next.reshape(1, d)]
    out_specs = [pl.BlockSpec(memory_space=pl.ANY)]
    out_shape = [jax.ShapeDtypeStruct((n, d), F32)]
    scratch_shapes = [
        pltpu.VMEM((tm, d), BF16),
        pltpu.VMEM((tm, d), F32),
        pltpu.VMEM((2 if mode == "final" else 1, tm, d), F32),
        pltpu.SemaphoreType.DMA(()),
        pltpu.SemaphoreType.DMA(()),
    ]
    if mode == "mix":
        ng = 2 * MLSTM_HEADS
        in_specs.append(pl.BlockSpec((ng, d), lambda i, j: (MIX_GATE_ROW // ng, 0)))
        args.append(w_gates_t)
        out_specs += [
            pl.BlockSpec(memory_space=pl.ANY),
            pl.BlockSpec((tm, LANES), lambda i, j: (i, 0)),
            pl.BlockSpec((ng, tm), lambda i, j: (0, i)),
        ]
        out_shape += [
            jax.ShapeDtypeStruct((n, d), BF16),
            jax.ShapeDtypeStruct((n, LANES), F32),
            jax.ShapeDtypeStruct((ng, n), F32),
        ]
        scratch_shapes += [pltpu.VMEM((tm, d), BF16), pltpu.SemaphoreType.DMA(())]
    return pl.pallas_call(
        functools.partial(_ffn_kernel, mode=mode),
        grid=(n // tm, f // tf),
        in_specs=in_specs,
        out_specs=out_specs,
        out_shape=out_shape,
        scratch_shapes=scratch_shapes,
        compiler_params=_params("arbitrary", "arbitrary"),
        name="ffn_" + mode,
    )(*args)


def _in_proj_kernel(h_ref, wt_ref, p_ref, w_sc):
    @pl.when(pl.program_id(1) == 0)
    def _():
        w_sc[...] = wt_ref[...].astype(BF16)

    for r in range(0, h_ref.shape[0], PROJ_SUB_ROWS):
        rows = slice(r, r + PROJ_SUB_ROWS)
        p_ref[rows, :] = lax.dot_general(h_ref[rows, :], w_sc[...], (((1,), (1,)), ((), ())),
                                         preferred_element_type=F32)


def _proj_row_offset(j, tn):
    n_gate_tiles = 2 * D_MODEL // tn
    n_mlstm_tiles = MIX_GATE_ROW // tn
    off8 = jnp.where(j < n_gate_tiles, MIX_MERGE_ROW // 8 + j * (tn // 8),
                     jnp.where(j < n_gate_tiles + n_mlstm_tiles, (j - n_gate_tiles) * (tn // 8),
                               MIX_HGRN_ROW // 8 + (j - n_gate_tiles - n_mlstm_tiles) * (tn // 8)))
    return off8 * 8


def _in_proj(hn, w_t, *, tm=2048, tn=1024):
    n, d = hn.shape
    return pl.pallas_call(
        _in_proj_kernel,
        grid=(PROJ_W // tn, n // tm),
        in_specs=[
            pl.BlockSpec((tm, d), lambda j, i: (i, 0)),
            pl.BlockSpec((pl.Element(tn), pl.Element(d)), lambda j, i: (_proj_row_offset(j, tn), 0)),
        ],
        out_specs=pl.BlockSpec((tm, tn), lambda j, i: (i, j)),
        out_shape=jax.ShapeDtypeStruct((n, PROJ_W), F32),
        scratch_shapes=[pltpu.VMEM((tn, d), BF16)],
        compiler_params=_params("arbitrary", "arbitrary"),
        name="in_proj",
    )(hn, w_t)


def _mlstm_block(q_ref, k_ref, v_ref, o_ref, gcol_ref, grow_ref, cw_ref, cb_ref, bcol_ref, brow_ref,
                 hn_ref, y_ref, xpad_sc, c_sc, n_sc, m_sc, before_head):
    L = MIX_TB
    H = MLSTM_HEADS

    xpad_sc[8:8 + L, 0:MLSTM_QK_W] = q_ref[...]
    xpad_sc[8:8 + L, MLSTM_QK_W:2 * MLSTM_QK_W] = k_ref[...]
    acc = jnp.broadcast_to(cb_ref[...], (L, 2 * MLSTM_QK_W))
    for j in range(CONV_WIDTH):
        off = 8 - (CONV_WIDTH - 1) + j
        acc = acc + cw_ref[j:j + 1, :] * xpad_sc[off:off + L, :]
    xpad_sc[0:8, :] = xpad_sc[L:L + 8, :]
    qk = acc * _sigmoid(acc)

    gcol = gcol_ref[...] + bcol_ref[...]
    grow = grow_ref[...] + brow_ref[...]
    fcol = _log_sigmoid(gcol)
    r_i = lax.broadcasted_iota(jnp.int32, (L, L), 0)
    c_i = lax.broadcasted_iota(jnp.int32, (L, L), 1)
    causal = r_i >= c_i
    tril = jnp.where(causal, 1.0, 0.0).astype(BF16)
    bcol = _tri_cumsum(tril, fcol)
    brow = bcol.T

    for h in range(H):
        before_head[h]()
        b_c = bcol[:, H + h:H + h + 1]
        i_c = gcol[:, h:h + 1]
        b_r = brow[H + h:H + h + 1, :]
        i_r = grow[h:h + 1, :]
        m_prev = m_sc[h:h + 1, 0:1]

        d_log = jnp.where(causal, b_c - (b_r - i_r), -jnp.inf)
        inter_log = b_c + m_prev
        m_t = jnp.maximum(jnp.max(d_log, axis=1, keepdims=True), inter_log)

        q_h = qk[:, h * MLSTM_QK:(h + 1) * MLSTM_QK]
        k_h = qk[:, MLSTM_QK_W + h * MLSTM_QK:MLSTM_QK_W + (h + 1) * MLSTM_QK] * (MLSTM_QK ** -0.5)
        v_h = v_ref[:, h * MLSTM_V:(h + 1) * MLSTM_V].astype(BF16)
        q_b = q_h.astype(BF16)

        s = lax.dot_general(q_b, k_h.astype(BF16), (((1,), (1,)), ((), ())), preferred_element_type=F32)
        s = s * jnp.exp(d_log - m_t)
        w_inter = jnp.exp(inter_log - m_t)
        c_h = c_sc[h]
        n_h = n_sc[h:h + 1, :]
        lhs = jnp.concatenate([s.astype(BF16), (q_h * w_inter).astype(BF16)], axis=1)
        rhs = jnp.concatenate([v_h, c_h.astype(BF16)], axis=0)
        num = jnp.dot(lhs, rhs, preferred_element_type=F32)
        den = jnp.sum(s, axis=1, keepdims=True) + w_inter * jnp.sum(q_h * n_h, axis=1, keepdims=True)
        hh = num / jnp.maximum(jnp.abs(den), jnp.exp(-m_t))

        hn = hh * lax.rsqrt(jnp.mean(hh * hh, axis=-1, keepdims=True) + EPS)
        hn = hn * hn_ref[:, h * MLSTM_V:(h + 1) * MLSTM_V]
        y = hn * _sigmoid(o_ref[:, h * MLSTM_V:(h + 1) * MLSTM_V])
        y_ref[:, h * MLSTM_V:(h + 1) * MLSTM_V] = y.astype(y_ref.dtype)

        b_last = b_c[L - 1:L, :]
        a_log = b_last - b_c + i_c
        m_new = jnp.maximum(b_last + m_prev, jnp.max(a_log, axis=0, keepdims=True))
        w_a = jnp.exp(a_log - m_new)
        decay = jnp.exp(b_last + m_prev - m_new)
        kw = k_h * w_a
        c_sc[h] = decay * c_h + lax.dot_general(kw.astype(BF16), v_h, (((0,), (0,)), ((), ())),
                                                preferred_element_type=F32)
        n_sc[h:h + 1, :] = decay * n_h + jnp.sum(kw, axis=0, keepdims=True)
        m_sc[h:h + 1, :] = jnp.broadcast_to(m_new, (1, LANES))


def _hgrn_block(q_ref, f_ref, v_ref, og_ref, lbl_ref, hn_ref, y_ref, g_sc, k_sc, q_sc, gr_sc, st_sc,
                before_head):
    TB = MIX_TB
    H = HGRN_HEADS
    DK = HGRN_DK
    W = H * DK
    NT = TB // 8

    lbl = lbl_ref[...]
    lmax = jnp.max(lbl, axis=0, keepdims=True)
    le = jnp.exp(lbl - lmax)
    lb = le[1:2, :] / jnp.sum(le, axis=0, keepdims=True)

    f = lb + (1.0 - lb) * _sigmoid(f_ref[...])
    k_sc[...] = 1.0 - f
    qraw = q_ref[...]
    q_sc[...] = qraw * _sigmoid(qraw) * (DK ** -0.5)
    r_i = lax.broadcasted_iota(jnp.int32, (TB, TB), 0)
    c_i = lax.broadcasted_iota(jnp.int32, (TB, TB), 1)
    tri = jnp.where(r_i >= c_i, 1.0, 0.0).astype(BF16)
    g = _tri_cumsum(tri, jnp.log(f) * LOG2_E)
    g_sc[...] = g

    g3 = g.reshape(NT, 8, W)
    sub = lax.broadcasted_iota(jnp.int32, (NT, 8, W), 1)
    bit0 = (sub & 1) != 0
    bit1 = (sub & 2) != 0
    bit2 = (sub & 4) != 0
    last2 = jnp.where(bit0, g3, pltpu.roll(g3, 7, 1))
    last4 = jnp.where(bit1, last2, pltpu.roll(last2, 6, 1))
    last4_r = pltpu.roll(last4, 4, 1)
    last8 = jnp.where(bit2, last4, last4_r).reshape(TB, W)
    gr_sc[0] = jnp.where(bit1, pltpu.roll(last2, 2, 1), last2).reshape(TB, W)
    gr_sc[1] = jnp.where(bit2, last4_r, last4).reshape(TB, W)
    for li, lvl in enumerate(HGRN_COARSE_LEVELS):
        groups = []
        for base in range(0, TB, 2 * lvl):
            src = last8[base + lvl - 8:base + lvl, :]
            groups.extend([src] * (2 * lvl // 8))
        gr_sc[2 + li] = jnp.concatenate(groups, axis=0)

    x_i = r_i ^ c_i
    nt_dims = (((1,), (1,)), ((), ()))
    for h in range(H):
        before_head[h]()
        cols = slice(h * DK, (h + 1) * DK)
        q_h = q_sc[:, cols]
        k_h = k_sc[:, cols]
        g_h = g_sc[:, cols]
        v_h = v_ref[:, cols].astype(BF16)

        q_b = q_h.astype(BF16)
        k_b = k_h.astype(BF16)
        a = jnp.broadcast_to(jnp.sum(q_h * k_h, axis=1, keepdims=True), (TB, TB))
        k_prev = pltpu.roll(k_h.reshape(NT, 8, DK), 1, 1).reshape(TB, DK)
        a1 = jnp.sum(q_h * (1.0 - k_h) * k_prev, axis=1, keepdims=True)
        a = jnp.where(x_i >= 1, jnp.broadcast_to(a1, (TB, TB)), a)
        for li in range(1, HGRN_NUM_LEVELS):
            lvl = 1 << li
            e = jnp.exp2(_neg_abs(g_h - gr_sc[li - 1, :, cols])).astype(BF16)
            qe = q_b * e
            ke = k_b * e
            if lvl < HGRN_HALF_LHS_MIN_LEVEL:
                p = lax.dot_general(qe, ke, nt_dims, preferred_element_type=F32)
                a = jnp.where(x_i >= lvl, p, a)
            else:
                ups = [slice(base + lvl, base + 2 * lvl) for base in range(0, TB, 2 * lvl)]
                p_up = lax.dot_general(jnp.concatenate([qe[u] for u in ups], axis=0), ke, nt_dims,
                                       preferred_element_type=F32)
                parts = []
                for n_up, u in enumerate(ups):
                    parts.append(a[u.start - lvl:u.start])
                    parts.append(jnp.where(x_i[u] >= lvl, p_up[n_up * lvl:(n_up + 1) * lvl], a[u]))
                a = jnp.concatenate(parts, axis=0)
        a = jnp.where(r_i >= c_i, a, 0.0)

        st = st_sc[h]
        g_last = g_h[TB - 1:TB, :]
        qd = (q_h * jnp.exp2(g_h)).astype(BF16)
        lhs = jnp.concatenate([a.astype(BF16), qd], axis=1)
        rhs = jnp.concatenate([v_h, st.astype(BF16)], axis=0)
        o = jnp.dot(lhs, rhs, preferred_element_type=F32)
        kd = (k_h * jnp.exp2(g_last - g_h)).astype(BF16)
        decay = jnp.exp2(g_h[TB - 8:TB, :].T[:, 7:8])
        st_sc[h] = decay * st + lax.dot_general(kd, v_h, (((0,), (0,)), ((), ())),
                                                preferred_element_type=F32)

        on = o * lax.rsqrt(jnp.mean(o * o, axis=-1, keepdims=True) + EPS) * hn_ref[:, cols]
        og = og_ref[:, cols]
        y_ref[:, cols] = (on * (og * _sigmoid(og))).astype(y_ref.dtype)


def _mix_kernel(mq_ref, mk_ref, mv_ref, mo_ref, gcol_ref, grow_ref, cw_ref, cb_ref, bcol_ref, brow_ref,
                mhn_ref, hq_ref, hf_ref, hi_ref, hg_ref, lbl_ref, hhn_ref,
                x_ref, gm_ref, gh_ref, wm_ref, wh_ref, wo_ref,
                out_ref,
                ym_sc, yh_sc, merged_sc, xpad_sc, c_sc, n_sc, m_sc, g_sc, k_sc, q_sc, gr_sc, st_sc,
                *, blocks_per_seq):
    s = pl.program_id(0)
    cur = s % 2
    prv = 1 - cur

    @pl.when(s == 0)
    def _():
        ym_sc[...] = jnp.zeros_like(ym_sc)
        yh_sc[...] = jnp.zeros_like(yh_sc)

    @pl.when(s % blocks_per_seq == 0)
    def _():
        xpad_sc[0:8, :] = jnp.zeros((8, 2 * MLSTM_QK_W), F32)
        c_sc[...] = jnp.zeros_like(c_sc)
        n_sc[...] = jnp.zeros_like(n_sc)
        m_sc[...] = jnp.zeros_like(m_sc)
        st_sc[...] = jnp.zeros_like(st_sc)

    def gate_piece(c):
        def run():
            cols = slice(c * MERGE_GATE_COLS, (c + 1) * MERGE_GATE_COLS)
            pm = jnp.dot(ym_sc[prv], wm_ref[:, cols], preferred_element_type=F32)
            ph = jnp.dot(yh_sc[prv], wh_ref[:, cols], preferred_element_type=F32)
            merged = _sigmoid(gm_ref[:, cols]) * pm + _sigmoid(gh_ref[:, cols]) * ph
            merged_sc[:, cols] = merged.astype(BF16)
        return run

    def out_piece(c):
        def run():
            cols = slice(c * MERGE_OUT_COLS, (c + 1) * MERGE_OUT_COLS)
            out_ref[:, cols] = x_ref[:, cols] + jnp.dot(merged_sc[...], wo_ref[:, cols],
                                                        preferred_element_type=F32)
        return run

    pieces = ([gate_piece(c) for c in range(D_MODEL // MERGE_GATE_COLS)]
              + [out_piece(c) for c in range(D_MODEL // MERGE_OUT_COLS)])
    n_heads = MLSTM_HEADS + HGRN_HEADS
    n_slots = MIX_SUBBLOCKS * n_heads
    slot_of = [(i * n_slots) // len(pieces) for i in range(len(pieces))]
    hooks = [(pieces[slot_of.index(i)] if i in slot_of else (lambda: None)) for i in range(n_slots)]

    for sub in range(MIX_SUBBLOCKS):
        rows = pl.ds(sub * MIX_TB, MIX_TB)
        sub_hooks = hooks[sub * n_heads:(sub + 1) * n_heads]
        _mlstm_block(mq_ref.at[rows], mk_ref.at[rows], mv_ref.at[rows], mo_ref.at[rows], gcol_ref.at[rows],
                     grow_ref.at[:, rows], cw_ref, cb_ref, bcol_ref, brow_ref, mhn_ref,
                     ym_sc.at[cur, rows], xpad_sc, c_sc, n_sc, m_sc, sub_hooks[:MLSTM_HEADS])
        _hgrn_block(hq_ref.at[rows], hf_ref.at[rows], hi_ref.at[rows], hg_ref.at[rows], lbl_ref, hhn_ref,
                    yh_sc.at[cur, rows], g_sc, k_sc, q_sc, gr_sc, st_sc, sub_hooks[MLSTM_HEADS:])


def _const_spec(shape):
    return pl.BlockSpec(shape, lambda *_: (0,) * len(shape), pipeline_mode=pl.Buffered(1))


def _mix(x2d, proj, gates_col, gates_row, conv_w, conv_b, bias_col, bias_row, m_head_norm,
         lb_logits, h_head_norm, w_m, w_h, w_o, *, seq_len):
    n, d = x2d.shape
    tb = MIX_TB * MIX_SUBBLOCKS
    nblk = n // tb
    qw, vw, hw = MLSTM_QK_W, MLSTM_V_W, HGRN_W

    def cur(col_block):
        return lambda s: (jnp.minimum(s, nblk - 1), col_block)

    def prev(col_block):
        return lambda s: (jnp.maximum(s - 1, 0), col_block)

    return pl.pallas_call(
        functools.partial(_mix_kernel, blocks_per_seq=seq_len // tb),
        grid=(nblk + 1,),
        in_specs=[
            pl.BlockSpec((tb, qw), cur(COL_MQ // qw)),
            pl.BlockSpec((tb, qw), cur(COL_MK // qw)),
            pl.BlockSpec((tb, vw), cur(COL_MV // vw)),
            pl.BlockSpec((tb, vw), cur(COL_MO // vw)),
            pl.BlockSpec((tb, LANES), cur(0)),
            pl.BlockSpec((2 * MLSTM_HEADS, tb), lambda s: (0, jnp.minimum(s, nblk - 1))),
            _const_spec((CONV_WIDTH, 2 * qw)),
            _const_spec((1, 2 * qw)),
            _const_spec((1, LANES)),
            _const_spec((2 * MLSTM_HEADS, 1)),
            _const_spec((1, vw)),
            pl.BlockSpec((tb, hw), cur(COL_HQ // hw)),
            pl.BlockSpec((tb, hw), cur(COL_HF // hw)),
            pl.BlockSpec((tb, hw), cur(COL_HI // hw)),
            pl.BlockSpec((tb, hw), cur(COL_HG // hw)),
            _const_spec((2, hw)),
            _const_spec((1, hw)),
            pl.BlockSpec((tb, d), prev(0)),
            pl.BlockSpec((tb, d), prev(COL_GM // d)),
            pl.BlockSpec((tb, d), prev(COL_GH // d)),
            _const_spec(w_m.shape),
            _const_spec(w_h.shape),
            _const_spec(w_o.shape),
        ],
        out_specs=pl.BlockSpec((tb, d), prev(0)),
        out_shape=jax.ShapeDtypeStruct((n, d), F32),
        scratch_shapes=[
            pltpu.VMEM((2, tb, vw), BF16),
            pltpu.VMEM((2, tb, hw), BF16),
            pltpu.VMEM((tb, d), BF16),
            pltpu.VMEM((MIX_TB + 8, 2 * qw), F32),
            pltpu.VMEM((MLSTM_HEADS, MLSTM_QK, MLSTM_V), F32),
            pltpu.VMEM((8, MLSTM_QK), F32),
            pltpu.VMEM((8, LANES), F32),
            pltpu.VMEM((MIX_TB, hw), F32),
            pltpu.VMEM((MIX_TB, hw), F32),
            pltpu.VMEM((MIX_TB, hw), F32),
            pltpu.VMEM((HGRN_NUM_LEVELS - 1, MIX_TB, hw), F32),
            pltpu.VMEM((HGRN_HEADS, HGRN_DK, HGRN_DV), F32),
        ],
        compiler_params=_params("arbitrary"),
        name="mix",
    )(proj, proj, proj, proj, gates_col, gates_row, conv_w, conv_b, bias_col, bias_row, m_head_norm,
      proj, proj, proj, proj, lb_logits, h_head_norm, x2d, proj, proj, w_m, w_h, w_o)


def _xattn_prep_kernel(m_ref, g_ref, wk_ref, wv_ref, wq_ref, wo_ref, qk_ref, vo_ref):
    wk = wk_ref[...].astype(BF16)
    wv = wv_ref[...].astype(BF16)
    wq = wq_ref[...].astype(BF16)
    wo = wo_ref[...].astype(BF16)
    for i in range(m_ref.shape[0]):
        m = _rms(m_ref[i], g_ref[...]).astype(BF16)
        k = jnp.dot(m, wk, preferred_element_type=F32).astype(BF16)
        v = jnp.dot(m, wv, preferred_element_type=F32).astype(BF16)
        qk = lax.dot_general(wq, k, (((1,), (1,)), ((), ())), preferred_element_type=F32)
        qk_ref[i] = (qk * (XATTN_HEAD_DIM ** -0.5)).astype(qk_ref.dtype)
        vo_ref[i] = jnp.dot(v, wo, preferred_element_type=F32).astype(vo_ref.dtype)


def _xattn_prep(mem, g, w_kv, w_q, w_o):
    b, m, d = mem.shape
    hd = XATTN_HEAD_DIM
    nh = XATTN_HEADS
    return pl.pallas_call(
        _xattn_prep_kernel,
        grid=(nh,),
        in_specs=[
            pl.BlockSpec((b, m, d), lambda h: (0, 0, 0)),
            pl.BlockSpec((1, d), lambda h: (0, 0)),
            pl.BlockSpec((d, hd), lambda h: (0, h)),
            pl.BlockSpec((d, hd), lambda h: (0, nh + h)),
            pl.BlockSpec((d, hd), lambda h: (0, h)),
            pl.BlockSpec((hd, d), lambda h: (h, 0)),
        ],
        out_specs=[
            pl.BlockSpec((b, d, m), lambda h: (0, 0, h)),
            pl.BlockSpec((b, m, d), lambda h: (0, h, 0)),
        ],
        out_shape=[
            jax.ShapeDtypeStruct((b, d, nh * m), BF16),
            jax.ShapeDtypeStruct((b, nh * m, d), BF16),
        ],
        compiler_params=_params("arbitrary"),
        name="xattn_prep",
    )(mem, g.reshape(1, d), w_kv, w_kv, w_q, w_o)


def _xattn_kernel(x_ref, g_ref, qk_ref, vo_ref, o_ref, *, mem_len):
    x = x_ref[0]
    h = _rms(x, g_ref[...]).astype(BF16)
    s = jnp.dot(h, qk_ref[0], preferred_element_type=F32)
    probs = []
    for hd in range(XATTN_HEADS):
        s_h = s[:, hd * mem_len:(hd + 1) * mem_len]
        e = jnp.exp(s_h - jnp.max(s_h, axis=-1, keepdims=True))
        probs.append((e / jnp.sum(e, axis=-1, keepdims=True)).astype(BF16))
    p = jnp.concatenate(probs, axis=-1)
    o_ref[0] = x + jnp.dot(p, vo_ref[0], preferred_element_type=F32)


def _xattn(x, g, qk, vo, *, tm=512):
    b, t, d = x.shape
    hm = qk.shape[2]
    return pl.pallas_call(
        functools.partial(_xattn_kernel, mem_len=hm // XATTN_HEADS),
        grid=(b, t // tm),
        in_specs=[
            pl.BlockSpec((1, tm, d), lambda i, c: (i, c, 0)),
            _const_spec((1, d)),
            pl.BlockSpec((1, d, hm), lambda i, c: (i, 0, 0)),
            pl.BlockSpec((1, hm, d), lambda i, c: (i, 0, 0)),
        ],
        out_specs=pl.BlockSpec((1, tm, d), lambda i, c: (i, c, 0)),
        out_shape=jax.ShapeDtypeStruct((b, t, d), F32),
        compiler_params=_params("parallel", "parallel"),
        name="xattn",
    )(x, g.reshape(1, d), qk, vo)


def kernel(x, mem, norm_ffn1, ffn1_w1, ffn1_w3, ffn1_w2, norm_mix, w_in, mlstm_conv_w, mlstm_conv_b,
           mlstm_ig_bias, mlstm_fg_bias, mlstm_head_norm, hgrn_lb_logits, hgrn_head_norm, w_proj_m,
           w_proj_h, w_out, norm_xattn, norm_mem, xattn_wq, xattn_wkv, xattn_wo, norm_ffn2, ffn2_w1,
           ffn2_w3, ffn2_w2, norm_final):
    b, t, d = x.shape
    depth = norm_ffn1.shape[0]
    assert depth == 1 and hgrn_lb_logits.shape[0] == 2
    n = b * t
    l = 0
    bf = lambda w: w.astype(BF16)

    w_in_t = jnp.swapaxes(w_in[l], 0, 1)
    n_gate = 2 * MLSTM_HEADS
    gate_bias = jnp.concatenate([mlstm_ig_bias[l], mlstm_fg_bias[l]]).astype(F32)
    bias_col = jnp.pad(gate_bias, (0, LANES - n_gate)).reshape(1, LANES)
    bias_row = gate_bias.reshape(n_gate, 1)

    x1, hn, gates_col, gates_row = _ffn(x.reshape(n, d), norm_ffn1[l], ffn1_w1[l], ffn1_w3[l], ffn1_w2[l],
                                        norm_mix[l], w_in_t)

    proj = _in_proj(hn, w_in_t)
    x2 = _mix(x1, proj, gates_col, gates_row, mlstm_conv_w[l], mlstm_conv_b[l].reshape(1, -1),
              bias_col, bias_row, mlstm_head_norm[l].reshape(1, -1), hgrn_lb_logits,
              hgrn_head_norm[l].reshape(1, -1), bf(w_proj_m[l]), bf(w_proj_h[l]), bf(w_out[l]), seq_len=t)

    qk, vo = _xattn_prep(mem, norm_mem[l], xattn_wkv[l], xattn_wq[l], xattn_wo[l])
    x3 = _xattn(x2.reshape(b, t, d), norm_xattn[l], qk, vo)

    (out,) = _ffn(x3.reshape(n, d), norm_ffn2[l], ffn2_w1[l], ffn2_w3[l], ffn2_w2[l], norm_final)
    return out.reshape(b, t, d)
```

```python
import functools

import jax
import jax.numpy as jnp
from jax import lax
from jax.experimental import pallas as pl
from jax.experimental.pallas import tpu as pltpu

F32 = jnp.float32
BF16 = jnp.bfloat16
EPS = 1e-6
LOG2_E = 1.4426950408889634

D_MODEL = 2048
MLSTM_HEADS = 4
MLSTM_QK = 128
MLSTM_V = 256
CONV_WIDTH = 4
HGRN_HEADS = 8
HGRN_DK = 128
HGRN_DV = 128
XATTN_HEADS = 4
XATTN_HEAD_DIM = D_MODEL // XATTN_HEADS

MLSTM_QK_W = MLSTM_HEADS * MLSTM_QK
MLSTM_V_W = MLSTM_HEADS * MLSTM_V
HGRN_W = HGRN_HEADS * HGRN_DK

VMEM_LIMIT_BYTES = 60000 * 1024
LANES = 128

COL_GM = 0
COL_GH = COL_GM + D_MODEL
COL_MQ = COL_GH + D_MODEL
COL_MK = COL_MQ + MLSTM_QK_W
COL_MV = COL_MK + MLSTM_QK_W
COL_MO = COL_MV + MLSTM_V_W
COL_HQ = COL_MO + MLSTM_V_W
COL_HF = COL_HQ + HGRN_W
COL_HI = COL_HF + HGRN_W
COL_HG = COL_HI + HGRN_W
PROJ_W = COL_HG + HGRN_W

MIX_GATE_ROW = 2 * MLSTM_QK_W + 2 * MLSTM_V_W
MIX_HGRN_ROW = MIX_GATE_ROW + 2 * MLSTM_HEADS
MIX_MERGE_ROW = MIX_HGRN_ROW + 4 * HGRN_W

PROJ_SUB_ROWS = 1024
FFN_NORM_ROWS = 128
FFN_FINISH_ROWS = 128
FFN_SUB_COLS = 256

MIX_TB = 128
HGRN_NUM_LEVELS = MIX_TB.bit_length() - 1
MIX_SUBBLOCKS = 2
MERGE_GATE_COLS = 512
MERGE_OUT_COLS = 256
HGRN_HALF_LHS_MIN_LEVEL = 16
HGRN_COARSE_LEVELS = tuple(1 << i for i in range(3, HGRN_NUM_LEVELS))


def _rms(x, g):
    return x * lax.rsqrt(jnp.mean(x * x, axis=-1, keepdims=True) + EPS) * g


def _sigmoid(x):
    return 1.0 / (1.0 + jnp.exp(-x))


def _log_sigmoid(x):
    return jnp.minimum(x, 0.0) - jnp.log(1.0 + jnp.exp(-jnp.abs(x)))


def _split3(x):
    hi = x.astype(BF16)
    r1 = x - hi.astype(F32)
    mid = r1.astype(BF16)
    lo = (r1 - mid.astype(F32)).astype(BF16)
    return hi, mid, lo


def _tri_cumsum(tri, x):
    return sum(jnp.dot(tri, t, preferred_element_type=F32) for t in _split3(x))


def _neg_abs(x):
    return pltpu.bitcast(pltpu.bitcast(x, jnp.uint32) | jnp.uint32(0x80000000), F32)


def _params(*sem):
    return pltpu.CompilerParams(dimension_semantics=sem, vmem_limit_bytes=VMEM_LIMIT_BYTES)


def _ffn_kernel(*refs, mode):
    if mode == "mix":
        (x_hbm, g_ref, w1_ref, w3_ref, w2_ref, gn_ref, wg_ref,
         o_hbm, hn_hbm, gcol_ref, grow_ref, h_sc, x_buf, acc_sc, x_sem, o_sem, hn_sc, hn_sem) = refs
    else:
        x_hbm, g_ref, w1_ref, w3_ref, w2_ref, gn_ref, o_hbm, h_sc, x_buf, acc_sc, x_sem, o_sem = refs
    i = pl.program_id(0)
    j = pl.program_id(1)
    n_i = pl.num_programs(0)
    tm = x_buf.shape[0]
    n_acc = acc_sc.shape[0]

    def acc_slot(tile):
        return acc_sc.at[tile % n_acc if n_acc > 1 else 0]

    def tile_rows(tile):
        return pl.ds(pl.multiple_of(tile * tm, tm), tm)

    def x_copy(tile):
        return pltpu.make_async_copy(x_hbm.at[tile_rows(tile), :], x_buf, x_sem)

    def o_copy(tile):
        return pltpu.make_async_copy(acc_slot(tile), o_hbm.at[tile_rows(tile), :], o_sem)

    def hn_copy(tile):
        return pltpu.make_async_copy(hn_sc, hn_hbm.at[tile_rows(tile), :], hn_sem)

    def row_chunk(r, size):
        return pl.ds(pl.multiple_of(r * size, size), size)

    acc_ref = acc_slot(i)

    def swiglu_passes(first):
        h = h_sc[...]
        for c in range(w1_ref.shape[1] // FFN_SUB_COLS):
            cols = slice(c * FFN_SUB_COLS, (c + 1) * FFN_SUB_COLS)
            a = jnp.dot(h, w1_ref[:, cols].astype(BF16), preferred_element_type=F32)
            b = jnp.dot(h, w3_ref[:, cols].astype(BF16), preferred_element_type=F32)
            act = (a * _sigmoid(a) * b).astype(BF16)
            part = jnp.dot(act, w2_ref[cols, :].astype(BF16), preferred_element_type=F32)
            if first and c == 0:
                acc_ref[...] = 2.0 * x_buf[...] + part
            else:
                acc_ref[...] += part

    @pl.when(j == 0)
    def _():
        @pl.when(i == 0)
        def _():
            x_copy(0).start()

        x_copy(i).wait()

        def norm_rows(r, carry):
            rows = row_chunk(r, FFN_NORM_ROWS)
            h_sc[rows, :] = _rms(x_buf[rows, :], g_ref[...]).astype(BF16)
            return carry
        lax.fori_loop(0, tm // FFN_NORM_ROWS, norm_rows, 0)

        if n_acc == 1:
            @pl.when(i > 0)
            def _():
                o_copy(i - 1).wait()

        swiglu_passes(first=True)

    @pl.when(j > 0)
    def _():
        swiglu_passes(first=False)

    @pl.when(jnp.logical_and(j == 1, i + 1 < n_i))
    def _():
        x_copy(i + 1).start()

    @pl.when(j == pl.num_programs(1) - 1)
    def _():
        if mode == "final":
            def finish_rows(r, carry):
                rows = row_chunk(r, FFN_FINISH_ROWS)
                acc_ref[rows, :] = _rms(0.5 * acc_ref[rows, :], gn_ref[...])
                return carry
            lax.fori_loop(0, tm // FFN_FINISH_ROWS, finish_rows, 0)
        else:
            @pl.when(i > 0)
            def _():
                hn_copy(i - 1).wait()

            y = 0.5 * acc_ref[...]
            acc_ref[...] = y
            hn = _rms(y, gn_ref[...]).astype(BF16)
            hn_sc[...] = hn
            nt_dims = (((1,), (1,)), ((), ()))
            wg = wg_ref[...].astype(BF16)
            wg_pad = jnp.concatenate([wg, jnp.zeros((LANES - wg.shape[0], wg.shape[1]), BF16)], axis=0)
            gcol = lax.dot_general(hn, wg_pad, nt_dims, preferred_element_type=F32)
            gcol_ref[...] = gcol
            grow_ref[...] = gcol.T[0:wg.shape[0], :]

        if n_acc > 1:
            @pl.when(i > 0)
            def _():
                o_copy(i - 1).wait()

        o_copy(i).start()
        if mode == "mix":
            hn_copy(i).start()

        @pl.when(i == n_i - 1)
        def _():
            o_copy(i).wait()
            if mode == "mix":
                hn_copy(i).wait()


def _ffn(x2d, g, w1, w3, w2, g_next, w_gates_t=None, *, tm=1024, tf=512):
    n, d = x2d.shape
    f = w1.shape[1]
    mode = "final" if w_gates_t is None else "mix"
    in_specs = [
        pl.BlockSpec(memory_space=pl.ANY),
        pl.BlockSpec((1, d), lambda i, j: (0, 0)),
        pl.BlockSpec((d, tf), lambda i, j: (0, j)),
        pl.BlockSpec((d, tf), lambda i, j: (0, j)),
        pl.BlockSpec((tf, d), lambda i, j: (j, 0)),
        pl.BlockSpec((1, d), lambda i, j: (0, 0)),
    ]
    args = [x2d, g.reshape(1, d), w1, w3, w2, g_next.reshape(1, d)]
    out_specs = [pl.BlockSpec(memory_space=pl.ANY)]
    out_shape = [jax.ShapeDtypeStruct((n, d), F32)]
    scratch_shapes = [
        pltpu.VMEM((tm, d), BF16),
        pltpu.VMEM((tm, d), F32),
        pltpu.VMEM((2 if mode == "final" else 1, tm, d), F32),
        pltpu.SemaphoreType.DMA(()),
        pltpu.SemaphoreType.DMA(()),
    ]
    if mode == "mix":
        ng = 2 * MLSTM_HEADS
        in_specs.append(pl.BlockSpec((ng, d), lambda i, j: (MIX_GATE_ROW // ng, 0)))
        args.append(w_gates_t)
        out_specs += [
            pl.BlockSpec(memory_space=pl.ANY),
            pl.BlockSpec((tm, LANES), lambda i, j: (i, 0)),
            pl.BlockSpec((ng, tm), lambda i, j: (0, i)),
        ]
        out_shape += [
            jax.ShapeDtypeStruct((n, d), BF16),
            jax.ShapeDtypeStruct((n, LANES), F32),
            jax.ShapeDtypeStruct((ng, n), F32),
        ]
        scratch_shapes += [pltpu.VMEM((tm, d), BF16), pltpu.SemaphoreType.DMA(())]
    return pl.pallas_call(
        functools.partial(_ffn_kernel, mode=mode),
        grid=(n // tm, f // tf),
        in_specs=in_specs,
        out_specs=out_specs,
        out_shape=out_shape,
        scratch_shapes=scratch_shapes,
        compiler_params=_params("arbitrary", "arbitrary"),
        name="ffn_" + mode,
    )(*args)


def _in_proj_kernel(h_ref, wt_ref, p_ref, w_sc):
    @pl.when(pl.program_id(1) == 0)
    def _():
        w_sc[...] = wt_ref[...].astype(BF16)

    for r in range(0, h_ref.shape[0], PROJ_SUB_ROWS):
        rows = slice(r, r + PROJ_SUB_ROWS)
        p_ref[rows, :] = lax.dot_general(h_ref[rows, :], w_sc[...], (((1,), (1,)), ((), ())),
                                         preferred_element_type=F32)


def _proj_row_offset(j, tn):
    n_gate_tiles = 2 * D_MODEL // tn
    n_mlstm_tiles = MIX_GATE_ROW // tn
    off8 = jnp.where(j < n_gate_tiles, MIX_MERGE_ROW // 8 + j * (tn // 8),
                     jnp.where(j < n_gate_tiles + n_mlstm_tiles, (j - n_gate_tiles) * (tn // 8),
                               MIX_HGRN_ROW // 8 + (j - n_gate_tiles - n_mlstm_tiles) * (tn // 8)))
    return off8 * 8


def _in_proj(hn, w_t, *, tm=2048, tn=1024):
    n, d = hn.shape
    return pl.pallas_call(
        _in_proj_kernel,
        grid=(PROJ_W // tn, n // tm),
        in_specs=[
            pl.BlockSpec((tm, d), lambda j, i: (i, 0)),
            pl.BlockSpec((pl.Element(tn), pl.Element(d)), lambda j, i: (_proj_row_offset(j, tn), 0)),
        ],
        out_specs=pl.BlockSpec((tm, tn), lambda j, i: (i, j)),
        out_shape=jax.ShapeDtypeStruct((n, PROJ_W), F32),
        scratch_shapes=[pltpu.VMEM((tn, d), BF16)],
        compiler_params=_params("arbitrary", "arbitrary"),
        name="in_proj",
    )(hn, w_t)


def _mlstm_block(q_ref, k_ref, v_ref, o_ref, gcol_ref, grow_ref, cw_ref, cb_ref, bcol_ref, brow_ref,
                 hn_ref, y_ref, xpad_sc, c_sc, n_sc, m_sc, before_head):
    L = MIX_TB
    H = MLSTM_HEADS

    xpad_sc[8:8 + L, 0:MLSTM_QK_W] = q_ref[...]
    xpad_sc[8:8 + L, MLSTM_QK_W:2 * MLSTM_QK_W] = k_ref[...]
    acc = jnp.broadcast_to(cb_ref[...], (L, 2 * MLSTM_QK_W))
    for j in range(CONV_WIDTH):
        off = 8 - (CONV_WIDTH - 1) + j
        acc = acc + cw_ref[j:j + 1, :] * xpad_sc[off:off + L, :]
    xpad_sc[0:8, :] = xpad_sc[L:L + 8, :]
    qk = acc * _sigmoid(acc)

    gcol = gcol_ref[...] + bcol_ref[...]
    grow = grow_ref[...] + brow_ref[...]
    fcol = _log_sigmoid(gcol)
    r_i = lax.broadcasted_iota(jnp.int32, (L, L), 0)
    c_i = lax.broadcasted_iota(jnp.int32, (L, L), 1)
    causal = r_i >= c_i
    tril = jnp.where(causal, 1.0, 0.0).astype(BF16)
    bcol = _tri_cumsum(tril, fcol)
    brow = bcol.T

    for h in range(H):
        before_head[h]()
        b_c = bcol[:, H + h:H + h + 1]
        i_c = gcol[:, h:h + 1]
        b_r = brow[H + h:H + h + 1, :]
        i_r = grow[h:h + 1, :]
        m_prev = m_sc[h:h + 1, 0:1]

        d_log = jnp.where(causal, b_c - (b_r - i_r), -jnp.inf)
        inter_log = b_c + m_prev
        m_t = jnp.maximum(jnp.max(d_log, axis=1, keepdims=True), inter_log)

        q_h = qk[:, h * MLSTM_QK:(h + 1) * MLSTM_QK]
        k_h = qk[:, MLSTM_QK_W + h * MLSTM_QK:MLSTM_QK_W + (h + 1) * MLSTM_QK] * (MLSTM_QK ** -0.5)
        v_h = v_ref[:, h * MLSTM_V:(h + 1) * MLSTM_V].astype(BF16)
        q_b = q_h.astype(BF16)

        s = lax.dot_general(q_b, k_h.astype(BF16), (((1,), (1,)), ((), ())), preferred_element_type=F32)
        s = s * jnp.exp(d_log - m_t)
        w_inter = jnp.exp(inter_log - m_t)
        c_h = c_sc[h]
        n_h = n_sc[h:h + 1, :]
        lhs = jnp.concatenate([s.astype(BF16), (q_h * w_inter).astype(BF16)], axis=1)
        rhs = jnp.concatenate([v_h, c_h.astype(BF16)], axis=0)
        num = jnp.dot(lhs, rhs, preferred_element_type=F32)
        den = jnp.sum(s, axis=1, keepdims=True) + w_inter * jnp.sum(q_h * n_h, axis=1, keepdims=True)
        hh = num / jnp.maximum(jnp.abs(den), jnp.exp(-m_t))

        hn = hh * lax.rsqrt(jnp.mean(hh * hh, axis=-1, keepdims=True) + EPS)
        hn = hn * hn_ref[:, h * MLSTM_V:(h + 1) * MLSTM_V]
        y = hn * _sigmoid(o_ref[:, h * MLSTM_V:(h + 1) * MLSTM_V])
        y_ref[:, h * MLSTM_V:(h + 1) * MLSTM_V] = y.astype(y_ref.dtype)

        b_last = b_c[L - 1:L, :]
        a_log = b_last - b_c + i_c
        m_new = jnp.maximum(b_last + m_prev, jnp.max(a_log, axis=0, keepdims=True))
        w_a = jnp.exp(a_log - m_new)
        decay = jnp.exp(b_last + m_prev - m_new)
        kw = k_h * w_a
        c_sc[h] = decay * c_h + lax.dot_general(kw.astype(BF16), v_h, (((0,), (0,)), ((), ())),
                                                preferred_element_type=F32)
        n_sc[h:h + 1, :] = decay * n_h + jnp.sum(kw, axis=0, keepdims=True)
        m_sc[h:h + 1, :] = jnp.broadcast_to(m_new, (1, LANES))


def _hgrn_block(q_ref, f_ref, v_ref, og_ref, lbl_ref, hn_ref, y_ref, g_sc, k_sc, q_sc, gr_sc, st_sc,
                before_head):
    TB = MIX_TB
    H = HGRN_HEADS
    DK = HGRN_DK
    W = H * DK
    NT = TB // 8

    lbl = lbl_ref[...]
    lmax = jnp.max(lbl, axis=0, keepdims=True)
    le = jnp.exp(lbl - lmax)
    lb = le[1:2, :] / jnp.sum(le, axis=0, keepdims=True)

    f = lb + (1.0 - lb) * _sigmoid(f_ref[...])
    k_sc[...] = 1.0 - f
    qraw = q_ref[...]
    q_sc[...] = qraw * _sigmoid(qraw) * (DK ** -0.5)
    r_i = lax.broadcasted_iota(jnp.int32, (TB, TB), 0)
    c_i = lax.broadcasted_iota(jnp.int32, (TB, TB), 1)
    tri = jnp.where(r_i >= c_i, 1.0, 0.0).astype(BF16)
    g = _tri_cumsum(tri, jnp.log(f) * LOG2_E)
    g_sc[...] = g

    g3 = g.reshape(NT, 8, W)
    sub = lax.broadcasted_iota(jnp.int32, (NT, 8, W), 1)
    bit0 = (sub & 1) != 0
    bit1 = (sub & 2) != 0
    bit2 = (sub & 4) != 0
    last2 = jnp.where(bit0, g3, pltpu.roll(g3, 7, 1))
    last4 = jnp.where(bit1, last2, pltpu.roll(last2, 6, 1))
    last4_r = pltpu.roll(last4, 4, 1)
    last8 = jnp.where(bit2, last4, last4_r).reshape(TB, W)
    gr_sc[0] = jnp.where(bit0, pltpu.roll(g3, 1, 1), g3).reshape(TB, W)
    gr_sc[1] = jnp.where(bit1, pltpu.roll(last2, 2, 1), last2).reshape(TB, W)
    gr_sc[2] = jnp.where(bit2, last4_r, last4).reshape(TB, W)
    for li, lvl in enumerate(HGRN_COARSE_LEVELS):
        groups = []
        for base in range(0, TB, 2 * lvl):
            src = last8[base + lvl - 8:base + lvl, :]
            groups.extend([src] * (2 * lvl // 8))
        gr_sc[3 + li] = jnp.concatenate(groups, axis=0)

    x_i = r_i ^ c_i
    nt_dims = (((1,), (1,)), ((), ()))
    for h in range(H):
        before_head[h]()
        cols = slice(h * DK, (h + 1) * DK)
        q_h = q_sc[:, cols]
        k_h = k_sc[:, cols]
        g_h = g_sc[:, cols]
        v_h = v_ref[:, cols].astype(BF16)

        q_b = q_h.astype(BF16)
        k_b = k_h.astype(BF16)
        a = jnp.broadcast_to(jnp.sum(q_h * k_h, axis=1, keepdims=True), (TB, TB))
        for li in range(HGRN_NUM_LEVELS):
            lvl = 1 << li
            e = jnp.exp2(_neg_abs(g_h - gr_sc[li, :, cols])).astype(BF16)
            qe = q_b * e
            ke = k_b * e
            if lvl < HGRN_HALF_LHS_MIN_LEVEL:
                p = lax.dot_general(qe, ke, nt_dims, preferred_element_type=F32)
                a = jnp.where(x_i >= lvl, p, a)
            else:
                ups = [slice(base + lvl, base + 2 * lvl) for base in range(0, TB, 2 * lvl)]
                p_up = lax.dot_general(jnp.concatenate([qe[u] for u in ups], axis=0), ke, nt_dims,
                                       preferred_element_type=F32)
                parts = []
                for n_up, u in enumerate(ups):
                    parts.append(a[u.start - lvl:u.start])
                    parts.append(jnp.where(x_i[u] >= lvl, p_up[n_up * lvl:(n_up + 1) * lvl], a[u]))
                a = jnp.concatenate(parts, axis=0)
        a = jnp.where(r_i >= c_i, a, 0.0)

        st = st_sc[h]
        g_last = g_h[TB - 1:TB, :]
        qd = (q_h * jnp.exp2(g_h)).astype(BF16)
        lhs = jnp.concatenate([a.astype(BF16), qd], axis=1)
        rhs = jnp.concatenate([v_h, st.astype(BF16)], axis=0)
        o = jnp.dot(lhs, rhs, preferred_element_type=F32)
        kd = (k_h * jnp.exp2(g_last - g_h)).astype(BF16)
        decay = jnp.exp2(g_h[TB - 8:TB, :].T[:, 7:8])
        st_sc[h] = decay * st + lax.dot_general(kd, v_h, (((0,), (0,)), ((), ())),
                                                preferred_element_type=F32)

        on = o * lax.rsqrt(jnp.mean(o * o, axis=-1, keepdims=True) + EPS) * hn_ref[:, cols]
        og = og_ref[:, cols]
        y_ref[:, cols] = (on * (og * _sigmoid(og))).astype(y_ref.dtype)


def _mix_kernel(mq_ref, mk_ref, mv_ref, mo_ref, gcol_ref, grow_ref, cw_ref, cb_ref, bcol_ref, brow_ref,
                mhn_ref, hq_ref, hf_ref, hi_ref, hg_ref, lbl_ref, hhn_ref,
                x_ref, gm_ref, gh_ref, wm_ref, wh_ref, wo_ref,
                out_ref,
                ym_sc, yh_sc, merged_sc, xpad_sc, c_sc, n_sc, m_sc, g_sc, k_sc, q_sc, gr_sc, st_sc,
                *, blocks_per_seq):
    s = pl.program_id(0)
    cur = s % 2
    prv = 1 - cur

    @pl.when(s == 0)
    def _():
        ym_sc[...] = jnp.zeros_like(ym_sc)
        yh_sc[...] = jnp.zeros_like(yh_sc)

    @pl.when(s % blocks_per_seq == 0)
    def _():
        xpad_sc[0:8, :] = jnp.zeros((8, 2 * MLSTM_QK_W), F32)
        c_sc[...] = jnp.zeros_like(c_sc)
        n_sc[...] = jnp.zeros_like(n_sc)
        m_sc[...] = jnp.zeros_like(m_sc)
        st_sc[...] = jnp.zeros_like(st_sc)

    def gate_piece(c):
        def run():
            cols = slice(c * MERGE_GATE_COLS, (c + 1) * MERGE_GATE_COLS)
            pm = jnp.dot(ym_sc[prv], wm_ref[:, cols], preferred_element_type=F32)
            ph = jnp.dot(yh_sc[prv], wh_ref[:, cols], preferred_element_type=F32)
            merged = _sigmoid(gm_ref[:, cols]) * pm + _sigmoid(gh_ref[:, cols]) * ph
            merged_sc[:, cols] = merged.astype(BF16)
        return run

    def out_piece(c):
        def run():
            cols = slice(c * MERGE_OUT_COLS, (c + 1) * MERGE_OUT_COLS)
            out_ref[:, cols] = x_ref[:, cols] + jnp.dot(merged_sc[...], wo_ref[:, cols],
                                                        preferred_element_type=F32)
        return run

    pieces = ([gate_piece(c) for c in range(D_MODEL // MERGE_GATE_COLS)]
              + [out_piece(c) for c in range(D_MODEL // MERGE_OUT_COLS)])
    n_heads = MLSTM_HEADS + HGRN_HEADS
    n_slots = MIX_SUBBLOCKS * n_heads
    slot_of = [(i * n_slots) // len(pieces) for i in range(len(pieces))]
    hooks = [(pieces[slot_of.index(i)] if i in slot_of else (lambda: None)) for i in range(n_slots)]

    for sub in range(MIX_SUBBLOCKS):
        rows = pl.ds(sub * MIX_TB, MIX_TB)
        sub_hooks = hooks[sub * n_heads:(sub + 1) * n_heads]
        _mlstm_block(mq_ref.at[rows], mk_ref.at[rows], mv_ref.at[rows], mo_ref.at[rows], gcol_ref.at[rows],
                     grow_ref.at[:, rows], cw_ref, cb_ref, bcol_ref, brow_ref, mhn_ref,
                     ym_sc.at[cur, rows], xpad_sc, c_sc, n_sc, m_sc, sub_hooks[:MLSTM_HEADS])
        _hgrn_block(hq_ref.at[rows], hf_ref.at[rows], hi_ref.at[rows], hg_ref.at[rows], lbl_ref, hhn_ref,
                    yh_sc.at[cur, rows], g_sc, k_sc, q_sc, gr_sc, st_sc, sub_hooks[MLSTM_HEADS:])


def _const_spec(shape):
    return pl.BlockSpec(shape, lambda *_: (0,) * len(shape), pipeline_mode=pl.Buffered(1))


def _mix(x2d, proj, gates_col, gates_row, conv_w, conv_b, bias_col, bias_row, m_head_norm,
         lb_logits, h_head_norm, w_m, w_h, w_o, *, seq_len):
    n, d = x2d.shape
    tb = MIX_TB * MIX_SUBBLOCKS
    nblk = n // tb
    qw, vw, hw = MLSTM_QK_W, MLSTM_V_W, HGRN_W

    def cur(col_block):
        return lambda s: (jnp.minimum(s, nblk - 1), col_block)

    def prev(col_block):
        return lambda s: (jnp.maximum(s - 1, 0), col_block)

    return pl.pallas_call(
        functools.partial(_mix_kernel, blocks_per_seq=seq_len // tb),
        grid=(nblk + 1,),
        in_specs=[
            pl.BlockSpec((tb, qw), cur(COL_MQ // qw)),
            pl.BlockSpec((tb, qw), cur(COL_MK // qw)),
            pl.BlockSpec((tb, vw), cur(COL_MV // vw)),
            pl.BlockSpec((tb, vw), cur(COL_MO // vw)),
            pl.BlockSpec((tb, LANES), cur(0)),
            pl.BlockSpec((2 * MLSTM_HEADS, tb), lambda s: (0, jnp.minimum(s, nblk - 1))),
            _const_spec((CONV_WIDTH, 2 * qw)),
            _const_spec((1, 2 * qw)),
            _const_spec((1, LANES)),
            _const_spec((2 * MLSTM_HEADS, 1)),
            _const_spec((1, vw)),
            pl.BlockSpec((tb, hw), cur(COL_HQ // hw)),
            pl.BlockSpec((tb, hw), cur(COL_HF // hw)),
            pl.BlockSpec((tb, hw), cur(COL_HI // hw)),
            pl.BlockSpec((tb, hw), cur(COL_HG // hw)),
            _const_spec((2, hw)),
            _const_spec((1, hw)),
            pl.BlockSpec((tb, d), prev(0)),
            pl.BlockSpec((tb, d), prev(COL_GM // d)),
            pl.BlockSpec((tb, d), prev(COL_GH // d)),
            _const_spec(w_m.shape),
            _const_spec(w_h.shape),
            _const_spec(w_o.shape),
        ],
        out_specs=pl.BlockSpec((tb, d), prev(0)),
        out_shape=jax.ShapeDtypeStruct((n, d), F32),
        scratch_shapes=[
            pltpu.VMEM((2, tb, vw), BF16),
            pltpu.VMEM((2, tb, hw), BF16),
            pltpu.VMEM((tb, d), BF16),
            pltpu.VMEM((MIX_TB + 8, 2 * qw), F32),
            pltpu.VMEM((MLSTM_HEADS, MLSTM_QK, MLSTM_V), F32),
            pltpu.VMEM((8, MLSTM_QK), F32),
            pltpu.VMEM((8, LANES), F32),
            pltpu.VMEM((MIX_TB, hw), F32),
            pltpu.VMEM((MIX_TB, hw), F32),
            pltpu.VMEM((MIX_TB, hw), F32),
            pltpu.VMEM((HGRN_NUM_LEVELS, MIX_TB, hw), F32),
            pltpu.VMEM((HGRN_HEADS, HGRN_DK, HGRN_DV), F32),
        ],
        compiler_params=_params("arbitrary"),
        name="mix",
    )(proj, proj, proj, proj, gates_col, gates_row, conv_w, conv_b, bias_col, bias_row, m_head_norm,
      proj, proj, proj, proj, lb_logits, h_head_norm, x2d, proj, proj, w_m, w_h, w_o)


def _xattn_prep_kernel(m_ref, g_ref, wk_ref, wv_ref, wq_ref, wo_ref, qk_ref, vo_ref):
    wk = wk_ref[...].astype(BF16)
    wv = wv_ref[...].astype(BF16)
    wq = wq_ref[...].astype(BF16)
    wo = wo_ref[...].astype(BF16)
    for i in range(m_ref.shape[0]):
        m = _rms(m_ref[i], g_ref[...]).astype(BF16)
        k = jnp.dot(m, wk, preferred_element_type=F32).astype(BF16)
        v = jnp.dot(m, wv, preferred_element_type=F32).astype(BF16)
        qk = lax.dot_general(wq, k, (((1,), (1,)), ((), ())), preferred_element_type=F32)
        qk_ref[i] = (qk * (XATTN_HEAD_DIM ** -0.5)).astype(qk_ref.dtype)
        vo_ref[i] = jnp.dot(v, wo, preferred_element_type=F32).astype(vo_ref.dtype)


def _xattn_prep(mem, g, w_kv, w_q, w_o):
    b, m, d = mem.shape
    hd = XATTN_HEAD_DIM
    nh = XATTN_HEADS
    return pl.pallas_call(
        _xattn_prep_kernel,
        grid=(nh,),
        in_specs=[
            pl.BlockSpec((b, m, d), lambda h: (0, 0, 0)),
            pl.BlockSpec((1, d), lambda h: (0, 0)),
            pl.BlockSpec((d, hd), lambda h: (0, h)),
            pl.BlockSpec((d, hd), lambda h: (0, nh + h)),
            pl.BlockSpec((d, hd), lambda h: (0, h)),
            pl.BlockSpec((hd, d), lambda h: (h, 0)),
        ],
        out_specs=[
            pl.BlockSpec((b, d, m), lambda h: (0, 0, h)),
            pl.BlockSpec((b, m, d), lambda h: (0, h, 0)),
        ],
        out_shape=[
            jax.ShapeDtypeStruct((b, d, nh * m), BF16),
            jax.ShapeDtypeStruct((b, nh * m, d), BF16),
        ],
        compiler_params=_params("arbitrary"),
        name="xattn_prep",
    )(mem, g.reshape(1, d), w_kv, w_kv, w_q, w_o)


def _xattn_kernel(x_ref, g_ref, qk_ref, vo_ref, o_ref, *, mem_len):
    x = x_ref[0]
    h = _rms(x, g_ref[...]).astype(BF16)
    s = jnp.dot(h, qk_ref[0], preferred_element_type=F32)
    probs = []
    for hd in range(XATTN_HEADS):
        s_h = s[:, hd * mem_len:(hd + 1) * mem_len]
        e = jnp.exp(s_h - jnp.max(s_h, axis=-1, keepdims=True))
        probs.append((e / jnp.sum(e, axis=-1, keepdims=True)).astype(BF16))
    p = jnp.concatenate(probs, axis=-1)
    o_ref[0] = x + jnp.dot(p, vo_ref[0], preferred_element_type=F32)


def _xattn(x, g, qk, vo, *, tm=1024):
    b, t, d = x.shape
    hm = qk.shape[2]
    return pl.pallas_call(
        functools.partial(_xattn_kernel, mem_len=hm // XATTN_HEADS),
        grid=(b, t // tm),
        in_specs=[
            pl.BlockSpec((1, tm, d), lambda i, c: (i, c, 0)),
            _const_spec((1, d)),
            pl.BlockSpec((1, d, hm), lambda i, c: (i, 0, 0)),
            pl.BlockSpec((1, hm, d), lambda i, c: (i, 0, 0)),
        ],
        out_specs=pl.BlockSpec((1, tm, d), lambda i, c: (i, c, 0)),
        out_shape=jax.ShapeDtypeStruct((b, t, d), F32),
        compiler_params=_params("parallel", "parallel"),
        name="xattn",
    )(x, g.reshape(1, d), qk, vo)


def kernel(x, mem, norm_ffn1, ffn1_w1, ffn1_w3, ffn1_w2, norm_mix, w_in, mlstm_conv_w, mlstm_conv_b,
           mlstm_ig_bias, mlstm_fg_bias, mlstm_head_norm, hgrn_lb_logits, hgrn_head_norm, w_proj_m,
           w_proj_h, w_out, norm_xattn, norm_mem, xattn_wq, xattn_wkv, xattn_wo, norm_ffn2, ffn2_w1,
           ffn2_w3, ffn2_w2, norm_final):
    b, t, d = x.shape
    depth = norm_ffn1.shape[0]
    assert depth == 1 and hgrn_lb_logits.shape[0] == 2
    n = b * t
    l = 0
    bf = lambda w: w.astype(BF16)

    w_in_t = jnp.swapaxes(w_in[l], 0, 1)
    n_gate = 2 * MLSTM_HEADS
    gate_bias = jnp.concatenate([mlstm_ig_bias[l], mlstm_fg_bias[l]]).astype(F32)
    bias_col = jnp.pad(gate_bias, (0, LANES - n_gate)).reshape(1, LANES)
    bias_row = gate_bias.reshape(n_gate, 1)

    x1, hn, gates_col, gates_row = _ffn(x.reshape(n, d), norm_ffn1[l], ffn1_w1[l], ffn1_w3[l], ffn1_w2[l],
                                        norm_mix[l], w_in_t)

    proj = _in_proj(hn, w_in_t)
    x2 = _mix(x1, proj, gates_col, gates_row, mlstm_conv_w[l], mlstm_conv_b[l].reshape(1, -1),
              bias_col, bias_row, mlstm_head_norm[l].reshape(1, -1), hgrn_lb_logits,
              hgrn_head_norm[l].reshape(1, -1), bf(w_proj_m[l]), bf(w_proj_h[l]), bf(w_out[l]), seq_len=t)

    qk, vo = _xattn_prep(mem, norm_mem[l], xattn_wkv[l], xattn_wq[l], xattn_wo[l])
    x3 = _xattn(x2.reshape(b, t, d), norm_xattn[l], qk, vo)

    (out,) = _ffn(x3.reshape(n, d), norm_ffn2[l], ffn2_w1[l], ffn2_w3[l], ffn2_w2[l], norm_final)
    return out.reshape(b, t, d)
```

```python
import functools

import jax
import jax.numpy as jnp
from jax import lax
from jax.experimental import pallas as pl
from jax.experimental.pallas import tpu as pltpu

F32 = jnp.float32
BF16 = jnp.bfloat16
EPS = 1e-6
LOG2_E = 1.4426950408889634

D_MODEL = 2048
MLSTM_HEADS = 4
MLSTM_QK = 128
MLSTM_V = 256
CONV_WIDTH = 4
HGRN_HEADS = 8
HGRN_DK = 128
HGRN_DV = 128
XATTN_HEADS = 4
XATTN_HEAD_DIM = D_MODEL // XATTN_HEADS

MLSTM_QK_W = MLSTM_HEADS * MLSTM_QK
MLSTM_V_W = MLSTM_HEADS * MLSTM_V
HGRN_W = HGRN_HEADS * HGRN_DK

VMEM_LIMIT_BYTES = 60000 * 1024
LANES = 128

COL_GM = 0
COL_GH = COL_GM + D_MODEL
COL_MQ = COL_GH + D_MODEL
COL_MK = COL_MQ + MLSTM_QK_W
COL_MV = COL_MK + MLSTM_QK_W
COL_MO = COL_MV + MLSTM_V_W
COL_HQ = COL_MO + MLSTM_V_W
COL_HF = COL_HQ + HGRN_W
COL_HI = COL_HF + HGRN_W
COL_HG = COL_HI + HGRN_W
PROJ_W = COL_HG + HGRN_W

MIX_GATE_ROW = 2 * MLSTM_QK_W + 2 * MLSTM_V_W
MIX_HGRN_ROW = MIX_GATE_ROW + 2 * MLSTM_HEADS
MIX_MERGE_ROW = MIX_HGRN_ROW + 4 * HGRN_W

PROJ_SUB_ROWS = 1024
FFN_NORM_ROWS = 128
FFN_FINISH_ROWS = 128
TILE_COPY_PRIORITY = 1
FFN_SUB_COLS = 256

MIX_TB = 128
HGRN_NUM_LEVELS = MIX_TB.bit_length() - 1
MIX_SUBBLOCKS = 2
MERGE_GATE_COLS = 512
MERGE_OUT_COLS = 256
HGRN_HALF_LHS_MIN_LEVEL = 16
HGRN_COARSE_LEVELS = tuple(1 << i for i in range(3, HGRN_NUM_LEVELS))


def _rms(x, g):
    return x * lax.rsqrt(jnp.mean(x * x, axis=-1, keepdims=True) + EPS) * g


def _sigmoid(x):
    return 1.0 / (1.0 + jnp.exp(-x))


def _log_sigmoid(x):
    return jnp.minimum(x, 0.0) - jnp.log(1.0 + jnp.exp(-jnp.abs(x)))


def _split3(x):
    hi = x.astype(BF16)
    r1 = x - hi.astype(F32)
    mid = r1.astype(BF16)
    lo = (r1 - mid.astype(F32)).astype(BF16)
    return hi, mid, lo


def _tri_cumsum(tri, x):
    return sum(jnp.dot(tri, t, preferred_element_type=F32) for t in _split3(x))


def _neg_abs(x):
    return pltpu.bitcast(pltpu.bitcast(x, jnp.uint32) | jnp.uint32(0x80000000), F32)


def _params(*sem):
    return pltpu.CompilerParams(dimension_semantics=sem, vmem_limit_bytes=VMEM_LIMIT_BYTES)


def _ffn_kernel(*refs, mode):
    if mode == "mix":
        (x_hbm, g_ref, w1_ref, w3_ref, w2_ref, gn_ref, wg_ref,
         o_hbm, hn_hbm, gcol_ref, grow_ref, h_sc, x_buf, acc_sc, x_sem, o_sem, hn_sc, hn_sem) = refs
    else:
        x_hbm, g_ref, w1_ref, w3_ref, w2_ref, gn_ref, o_hbm, h_sc, x_buf, acc_sc, x_sem, o_sem = refs
    i = pl.program_id(0)
    j = pl.program_id(1)
    n_i = pl.num_programs(0)
    tm = x_buf.shape[0]
    n_acc = acc_sc.shape[0]

    def acc_slot(tile):
        return acc_sc.at[tile % n_acc if n_acc > 1 else 0]

    def tile_rows(tile):
        return pl.ds(pl.multiple_of(tile * tm, tm), tm)

    def x_copy(tile):
        return pltpu.make_async_copy(x_hbm.at[tile_rows(tile), :], x_buf, x_sem)

    def o_copy(tile):
        return pltpu.make_async_copy(acc_slot(tile), o_hbm.at[tile_rows(tile), :], o_sem)

    def hn_copy(tile):
        return pltpu.make_async_copy(hn_sc, hn_hbm.at[tile_rows(tile), :], hn_sem)

    def row_chunk(r, size):
        return pl.ds(pl.multiple_of(r * size, size), size)

    acc_ref = acc_slot(i)

    def swiglu_passes(first):
        h = h_sc[...]
        for c in range(w1_ref.shape[1] // FFN_SUB_COLS):
            cols = slice(c * FFN_SUB_COLS, (c + 1) * FFN_SUB_COLS)
            a = jnp.dot(h, w1_ref[:, cols].astype(BF16), preferred_element_type=F32)
            b = jnp.dot(h, w3_ref[:, cols].astype(BF16), preferred_element_type=F32)
            act = (a * _sigmoid(a) * b).astype(BF16)
            part = jnp.dot(act, w2_ref[cols, :].astype(BF16), preferred_element_type=F32)
            if first and c == 0:
                acc_ref[...] = 2.0 * x_buf[...] + part
            else:
                acc_ref[...] += part

    @pl.when(j == 0)
    def _():
        @pl.when(i == 0)
        def _():
            x_copy(0).start()

        x_copy(i).wait()

        def norm_rows(r, carry):
            rows = row_chunk(r, FFN_NORM_ROWS)
            h_sc[rows, :] = _rms(x_buf[rows, :], g_ref[...]).astype(BF16)
            return carry
        lax.fori_loop(0, tm // FFN_NORM_ROWS, norm_rows, 0)

        if n_acc == 1:
            @pl.when(i > 0)
            def _():
                o_copy(i - 1).wait()

        swiglu_passes(first=True)

    @pl.when(j > 0)
    def _():
        swiglu_passes(first=False)

    @pl.when(jnp.logical_and(j == 1, i + 1 < n_i))
    def _():
        x_copy(i + 1).start(priority=TILE_COPY_PRIORITY)

    @pl.when(j == pl.num_programs(1) - 1)
    def _():
        if mode == "final":
            def finish_rows(r, carry):
                rows = row_chunk(r, FFN_FINISH_ROWS)
                acc_ref[rows, :] = _rms(0.5 * acc_ref[rows, :], gn_ref[...])
                return carry
            lax.fori_loop(0, tm // FFN_FINISH_ROWS, finish_rows, 0)
        else:
            @pl.when(i > 0)
            def _():
                hn_copy(i - 1).wait()

            y = 0.5 * acc_ref[...]
            acc_ref[...] = y
            hn = _rms(y, gn_ref[...]).astype(BF16)
            hn_sc[...] = hn
            nt_dims = (((1,), (1,)), ((), ()))
            wg = wg_ref[...].astype(BF16)
            wg_pad = jnp.concatenate([wg, jnp.zeros((LANES - wg.shape[0], wg.shape[1]), BF16)], axis=0)
            gcol = lax.dot_general(hn, wg_pad, nt_dims, preferred_element_type=F32)
            gcol_ref[...] = gcol
            grow_ref[...] = gcol.T[0:wg.shape[0], :]

        if n_acc > 1:
            @pl.when(i > 0)
            def _():
                o_copy(i - 1).wait()

        o_copy(i).start(priority=TILE_COPY_PRIORITY)
        if mode == "mix":
            hn_copy(i).start(priority=TILE_COPY_PRIORITY)

        @pl.when(i == n_i - 1)
        def _():
            o_copy(i).wait()
            if mode == "mix":
                hn_copy(i).wait()


def _ffn(x2d, g, w1, w3, w2, g_next, w_gates_t=None, *, tm=1024, tf=512):
    n, d = x2d.shape
    f = w1.shape[1]
    mode = "final" if w_gates_t is None else "mix"
    in_specs = [
        pl.BlockSpec(memory_space=pl.ANY),
        pl.BlockSpec((1, d), lambda i, j: (0, 0)),
        pl.BlockSpec((d, tf), lambda i, j: (0, j)),
        pl.BlockSpec((d, tf), lambda i, j: (0, j)),
        pl.BlockSpec((tf, d), lambda i, j: (j, 0)),
        pl.BlockSpec((1, d), lambda i, j: (0, 0)),
    ]
    args = [x2d, g.reshape(1, d), w1, w3, w2, g_next.reshape(1, d)]
    out_specs = [pl.BlockSpec(memory_space=pl.ANY)]
    out_shape = [jax.ShapeDtypeStruct((n, d), F32)]
    scratch_shapes = [
        pltpu.VMEM((tm, d), BF16),
        pltpu.VMEM((tm, d), F32),
        pltpu.VMEM((2 if mode == "final" else 1, tm, d), F32),
        pltpu.SemaphoreType.DMA(()),
        pltpu.SemaphoreType.DMA(()),
    ]
    if mode == "mix":
        ng = 2 * MLSTM_HEADS
        in_specs.append(pl.BlockSpec((ng, d), lambda i, j: (MIX_GATE_ROW // ng, 0)))
        args.append(w_gates_t)
        out_specs += [
            pl.BlockSpec(memory_space=pl.ANY),
            pl.BlockSpec((tm, LANES), lambda i, j: (i, 0)),
            pl.BlockSpec((ng, tm), lambda i, j: (0, i)),
        ]
        out_shape += [
            jax.ShapeDtypeStruct((n, d), BF16),
            jax.ShapeDtypeStruct((n, LANES), F32),
            jax.ShapeDtypeStruct((ng, n), F32),
        ]
        scratch_shapes += [pltpu.VMEM((tm, d), BF16), pltpu.SemaphoreType.DMA(())]
    return pl.pallas_call(
        functools.partial(_ffn_kernel, mode=mode),
        grid=(n // tm, f // tf),
        in_specs=in_specs,
        out_specs=out_specs,
        out_shape=out_shape,
        scratch_shapes=scratch_shapes,
        compiler_params=_params("arbitrary", "arbitrary"),
        name="ffn_" + mode,
    )(*args)


def _in_proj_kernel(h_ref, wt_ref, p_ref, w_sc):
    @pl.when(pl.program_id(1) == 0)
    def _():
        w_sc[...] = wt_ref[...].astype(BF16)

    for r in range(0, h_ref.shape[0], PROJ_SUB_ROWS):
        rows = slice(r, r + PROJ_SUB_ROWS)
        p_ref[rows, :] = lax.dot_general(h_ref[rows, :], w_sc[...], (((1,), (1,)), ((), ())),
                                         preferred_element_type=F32)


def _proj_row_offset(j, tn):
    n_gate_tiles = 2 * D_MODEL // tn
    n_mlstm_tiles = MIX_GATE_ROW // tn
    off8 = jnp.where(j < n_gate_tiles, MIX_MERGE_ROW // 8 + j * (tn // 8),
                     jnp.where(j < n_gate_tiles + n_mlstm_tiles, (j - n_gate_tiles) * (tn // 8),
                               MIX_HGRN_ROW // 8 + (j - n_gate_tiles - n_mlstm_tiles) * (tn // 8)))
    return off8 * 8


def _in_proj(hn, w_t, *, tm=2048, tn=1024):
    n, d = hn.shape
    return pl.pallas_call(
        _in_proj_kernel,
        grid=(PROJ_W // tn, n // tm),
        in_specs=[
            pl.BlockSpec((tm, d), lambda j, i: (i, 0)),
            pl.BlockSpec((pl.Element(tn), pl.Element(d)), lambda j, i: (_proj_row_offset(j, tn), 0)),
        ],
        out_specs=pl.BlockSpec((tm, tn), lambda j, i: (i, j)),
        out_shape=jax.ShapeDtypeStruct((n, PROJ_W), F32),
        scratch_shapes=[pltpu.VMEM((tn, d), BF16)],
        compiler_params=_params("arbitrary", "arbitrary"),
        name="in_proj",
    )(hn, w_t)


def _mlstm_block(q_ref, k_ref, v_ref, o_ref, gcol_ref, grow_ref, cw_ref, cb_ref, bcol_ref, brow_ref,
                 hn_ref, y_ref, xpad_sc, c_sc, n_sc, m_sc, before_head):
    L = MIX_TB
    H = MLSTM_HEADS

    xpad_sc[8:8 + L, 0:MLSTM_QK_W] = q_ref[...]
    xpad_sc[8:8 + L, MLSTM_QK_W:2 * MLSTM_QK_W] = k_ref[...]
    acc = jnp.broadcast_to(cb_ref[...], (L, 2 * MLSTM_QK_W))
    for j in range(CONV_WIDTH):
        off = 8 - (CONV_WIDTH - 1) + j
        acc = acc + cw_ref[j:j + 1, :] * xpad_sc[off:off + L, :]
    xpad_sc[0:8, :] = xpad_sc[L:L + 8, :]
    qk = acc * _sigmoid(acc)

    gcol = gcol_ref[...] + bcol_ref[...]
    grow = grow_ref[...] + brow_ref[...]
    fcol = _log_sigmoid(gcol)
    r_i = lax.broadcasted_iota(jnp.int32, (L, L), 0)
    c_i = lax.broadcasted_iota(jnp.int32, (L, L), 1)
    causal = r_i >= c_i
    tril = jnp.where(causal, 1.0, 0.0).astype(BF16)
    bcol = _tri_cumsum(tril, fcol)
    brow = bcol.T

    for h in range(H):
        before_head[h]()
        b_c = bcol[:, H + h:H + h + 1]
        i_c = gcol[:, h:h + 1]
        b_r = brow[H + h:H + h + 1, :]
        i_r = grow[h:h + 1, :]
        m_prev = m_sc[h:h + 1, 0:1]

        d_log = jnp.where(causal, b_c - (b_r - i_r), -jnp.inf)
        inter_log = b_c + m_prev
        m_t = jnp.maximum(jnp.max(d_log, axis=1, keepdims=True), inter_log)

        q_h = qk[:, h * MLSTM_QK:(h + 1) * MLSTM_QK]
        k_h = qk[:, MLSTM_QK_W + h * MLSTM_QK:MLSTM_QK_W + (h + 1) * MLSTM_QK] * (MLSTM_QK ** -0.5)
        v_h = v_ref[:, h * MLSTM_V:(h + 1) * MLSTM_V].astype(BF16)
        q_b = q_h.astype(BF16)

        s = lax.dot_general(q_b, k_h.astype(BF16), (((1,), (1,)), ((), ())), preferred_element_type=F32)
        s = s * jnp.exp(d_log - m_t)
        w_inter = jnp.exp(inter_log - m_t)
        c_h = c_sc[h]
        n_h = n_sc[h:h + 1, :]
        lhs = jnp.concatenate([s.astype(BF16), (q_h * w_inter).astype(BF16)], axis=1)
        rhs = jnp.concatenate([v_h, c_h.astype(BF16)], axis=0)
        num = jnp.dot(lhs, rhs, preferred_element_type=F32)
        den = jnp.sum(s, axis=1, keepdims=True) + w_inter * jnp.sum(q_h * n_h, axis=1, keepdims=True)
        hh = num / jnp.maximum(jnp.abs(den), jnp.exp(-m_t))

        hn = hh * lax.rsqrt(jnp.mean(hh * hh, axis=-1, keepdims=True) + EPS)
        hn = hn * hn_ref[:, h * MLSTM_V:(h + 1) * MLSTM_V]
        y = hn * _sigmoid(o_ref[:, h * MLSTM_V:(h + 1) * MLSTM_V])
        y_ref[:, h * MLSTM_V:(h + 1) * MLSTM_V] = y.astype(y_ref.dtype)

        b_last = b_c[L - 1:L, :]
        a_log = b_last - b_c + i_c
        m_new = jnp.maximum(b_last + m_prev, jnp.max(a_log, axis=0, keepdims=True))
        w_a = jnp.exp(a_log - m_new)
        decay = jnp.exp(b_last + m_prev - m_new)
        kw = k_h * w_a
        c_sc[h] = decay * c_h + lax.dot_general(kw.astype(BF16), v_h, (((0,), (0,)), ((), ())),
                                                preferred_element_type=F32)
        n_sc[h:h + 1, :] = decay * n_h + jnp.sum(kw, axis=0, keepdims=True)
        m_sc[h:h + 1, :] = jnp.broadcast_to(m_new, (1, LANES))


def _hgrn_block(q_ref, f_ref, v_ref, og_ref, lbl_ref, hn_ref, y_ref, g_sc, k_sc, q_sc, gr_sc, st_sc,
                before_head):
    TB = MIX_TB
    H = HGRN_HEADS
    DK = HGRN_DK
    W = H * DK
    NT = TB // 8

    lbl = lbl_ref[...]
    lmax = jnp.max(lbl, axis=0, keepdims=True)
    le = jnp.exp(lbl - lmax)
    lb = le[1:2, :] / jnp.sum(le, axis=0, keepdims=True)

    f = lb + (1.0 - lb) * _sigmoid(f_ref[...])
    k_sc[...] = 1.0 - f
    qraw = q_ref[...]
    q_sc[...] = qraw * _sigmoid(qraw) * (DK ** -0.5)
    r_i = lax.broadcasted_iota(jnp.int32, (TB, TB), 0)
    c_i = lax.broadcasted_iota(jnp.int32, (TB, TB), 1)
    tri = jnp.where(r_i >= c_i, 1.0, 0.0).astype(BF16)
    g = _tri_cumsum(tri, jnp.log(f) * LOG2_E)
    g_sc[...] = g

    g3 = g.reshape(NT, 8, W)
    sub = lax.broadcasted_iota(jnp.int32, (NT, 8, W), 1)
    bit0 = (sub & 1) != 0
    bit1 = (sub & 2) != 0
    bit2 = (sub & 4) != 0
    last2 = jnp.where(bit0, g3, pltpu.roll(g3, 7, 1))
    last4 = jnp.where(bit1, last2, pltpu.roll(last2, 6, 1))
    last4_r = pltpu.roll(last4, 4, 1)
    last8 = jnp.where(bit2, last4, last4_r).reshape(TB, W)
    gr_sc[0] = jnp.where(bit0, pltpu.roll(g3, 1, 1), g3).reshape(TB, W)
    gr_sc[1] = jnp.where(bit1, pltpu.roll(last2, 2, 1), last2).reshape(TB, W)
    gr_sc[2] = jnp.where(bit2, last4_r, last4).reshape(TB, W)
    for li, lvl in enumerate(HGRN_COARSE_LEVELS):
        groups = []
        for base in range(0, TB, 2 * lvl):
            src = last8[base + lvl - 8:base + lvl, :]
            groups.extend([src] * (2 * lvl // 8))
        gr_sc[3 + li] = jnp.concatenate(groups, axis=0)

    x_i = r_i ^ c_i
    nt_dims = (((1,), (1,)), ((), ()))
    for h in range(H):
        before_head[h]()
        cols = slice(h * DK, (h + 1) * DK)
        q_h = q_sc[:, cols]
        k_h = k_sc[:, cols]
        g_h = g_sc[:, cols]
        v_h = v_ref[:, cols].astype(BF16)

        q_b = q_h.astype(BF16)
        k_b = k_h.astype(BF16)
        a = jnp.broadcast_to(jnp.sum(q_h * k_h, axis=1, keepdims=True), (TB, TB))
        for li in range(HGRN_NUM_LEVELS):
            lvl = 1 << li
            e = jnp.exp2(_neg_abs(g_h - gr_sc[li, :, cols])).astype(BF16)
            qe = q_b * e
            ke = k_b * e
            if lvl < HGRN_HALF_LHS_MIN_LEVEL:
                p = lax.dot_general(qe, ke, nt_dims, preferred_element_type=F32)
                a = jnp.where(x_i >= lvl, p, a)
            else:
                ups = [slice(base + lvl, base + 2 * lvl) for base in range(0, TB, 2 * lvl)]
                p_up = lax.dot_general(jnp.concatenate([qe[u] for u in ups], axis=0), ke, nt_dims,
                                       preferred_element_type=F32)
                parts = []
                for n_up, u in enumerate(ups):
                    parts.append(a[u.start - lvl:u.start])
                    parts.append(jnp.where(x_i[u] >= lvl, p_up[n_up * lvl:(n_up + 1) * lvl], a[u]))
                a = jnp.concatenate(parts, axis=0)
        a = jnp.where(r_i >= c_i, a, 0.0)

        st = st_sc[h]
        g_last = g_h[TB - 1:TB, :]
        qd = (q_h * jnp.exp2(g_h)).astype(BF16)
        lhs = jnp.concatenate([a.astype(BF16), qd], axis=1)
        rhs = jnp.concatenate([v_h, st.astype(BF16)], axis=0)
        o = jnp.dot(lhs, rhs, preferred_element_type=F32)
        kd = (k_h * jnp.exp2(g_last - g_h)).astype(BF16)
        decay = jnp.exp2(g_h[TB - 8:TB, :].T[:, 7:8])
        st_sc[h] = decay * st + lax.dot_general(kd, v_h, (((0,), (0,)), ((), ())),
                                                preferred_element_type=F32)

        on = o * lax.rsqrt(jnp.mean(o * o, axis=-1, keepdims=True) + EPS) * hn_ref[:, cols]
        og = og_ref[:, cols]
        y_ref[:, cols] = (on * (og * _sigmoid(og))).astype(y_ref.dtype)


def _mix_kernel(mq_ref, mk_ref, mv_ref, mo_ref, gcol_ref, grow_ref, cw_ref, cb_ref, bcol_ref, brow_ref,
                mhn_ref, hq_ref, hf_ref, hi_ref, hg_ref, lbl_ref, hhn_ref,
                x_ref, gm_ref, gh_ref, wm_ref, wh_ref, wo_ref,
                out_ref,
                ym_sc, yh_sc, merged_sc, xpad_sc, c_sc, n_sc, m_sc, g_sc, k_sc, q_sc, gr_sc, st_sc,
                *, blocks_per_seq):
    s = pl.program_id(0)
    cur = s % 2
    prv = 1 - cur

    @pl.when(s == 0)
    def _():
        ym_sc[...] = jnp.zeros_like(ym_sc)
        yh_sc[...] = jnp.zeros_like(yh_sc)

    @pl.when(s % blocks_per_seq == 0)
    def _():
        xpad_sc[0:8, :] = jnp.zeros((8, 2 * MLSTM_QK_W), F32)
        c_sc[...] = jnp.zeros_like(c_sc)
        n_sc[...] = jnp.zeros_like(n_sc)
        m_sc[...] = jnp.zeros_like(m_sc)
        st_sc[...] = jnp.zeros_like(st_sc)

    def gate_piece(c):
        def run():
            cols = slice(c * MERGE_GATE_COLS, (c + 1) * MERGE_GATE_COLS)
            pm = jnp.dot(ym_sc[prv], wm_ref[:, cols], preferred_element_type=F32)
            ph = jnp.dot(yh_sc[prv], wh_ref[:, cols], preferred_element_type=F32)
            merged = _sigmoid(gm_ref[:, cols]) * pm + _sigmoid(gh_ref[:, cols]) * ph
            merged_sc[:, cols] = merged.astype(BF16)
        return run

    def out_piece(c):
        def run():
            cols = slice(c * MERGE_OUT_COLS, (c + 1) * MERGE_OUT_COLS)
            out_ref[:, cols] = x_ref[:, cols] + jnp.dot(merged_sc[...], wo_ref[:, cols],
                                                        preferred_element_type=F32)
        return run

    pieces = ([gate_piece(c) for c in range(D_MODEL // MERGE_GATE_COLS)]
              + [out_piece(c) for c in range(D_MODEL // MERGE_OUT_COLS)])
    n_heads = MLSTM_HEADS + HGRN_HEADS
    n_slots = MIX_SUBBLOCKS * n_heads
    slot_of = [(i * n_slots) // len(pieces) for i in range(len(pieces))]
    hooks = [(pieces[slot_of.index(i)] if i in slot_of else (lambda: None)) for i in range(n_slots)]

    for sub in range(MIX_SUBBLOCKS):
        rows = pl.ds(sub * MIX_TB, MIX_TB)
        sub_hooks = hooks[sub * n_heads:(sub + 1) * n_heads]
        _mlstm_block(mq_ref.at[rows], mk_ref.at[rows], mv_ref.at[rows], mo_ref.at[rows], gcol_ref.at[rows],
                     grow_ref.at[:, rows], cw_ref, cb_ref, bcol_ref, brow_ref, mhn_ref,
                     ym_sc.at[cur, rows], xpad_sc, c_sc, n_sc, m_sc, sub_hooks[:MLSTM_HEADS])
        _hgrn_block(hq_ref.at[rows], hf_ref.at[rows], hi_ref.at[rows], hg_ref.at[rows], lbl_ref, hhn_ref,
                    yh_sc.at[cur, rows], g_sc, k_sc, q_sc, gr_sc, st_sc, sub_hooks[MLSTM_HEADS:])


def _const_spec(shape):
    return pl.BlockSpec(shape, lambda *_: (0,) * len(shape), pipeline_mode=pl.Buffered(1))


def _mix(x2d, proj, gates_col, gates_row, conv_w, conv_b, bias_col, bias_row, m_head_norm,
         lb_logits, h_head_norm, w_m, w_h, w_o, *, seq_len):
    n, d = x2d.shape
    tb = MIX_TB * MIX_SUBBLOCKS
    nblk = n // tb
    qw, vw, hw = MLSTM_QK_W, MLSTM_V_W, HGRN_W

    def cur(col_block):
        return lambda s: (jnp.minimum(s, nblk - 1), col_block)

    def prev(col_block):
        return lambda s: (jnp.maximum(s - 1, 0), col_block)

    return pl.pallas_call(
        functools.partial(_mix_kernel, blocks_per_seq=seq_len // tb),
        grid=(nblk + 1,),
        in_specs=[
            pl.BlockSpec((tb, qw), cur(COL_MQ // qw)),
            pl.BlockSpec((tb, qw), cur(COL_MK // qw)),
            pl.BlockSpec((tb, vw), cur(COL_MV // vw)),
            pl.BlockSpec((tb, vw), cur(COL_MO // vw)),
            pl.BlockSpec((tb, LANES), cur(0)),
            pl.BlockSpec((2 * MLSTM_HEADS, tb), lambda s: (0, jnp.minimum(s, nblk - 1))),
            _const_spec((CONV_WIDTH, 2 * qw)),
            _const_spec((1, 2 * qw)),
            _const_spec((1, LANES)),
            _const_spec((2 * MLSTM_HEADS, 1)),
            _const_spec((1, vw)),
            pl.BlockSpec((tb, hw), cur(COL_HQ // hw)),
            pl.BlockSpec((tb, hw), cur(COL_HF // hw)),
            pl.BlockSpec((tb, hw), cur(COL_HI // hw)),
            pl.BlockSpec((tb, hw), cur(COL_HG // hw)),
            _const_spec((2, hw)),
            _const_spec((1, hw)),
            pl.BlockSpec((tb, d), prev(0)),
            pl.BlockSpec((tb, d), prev(COL_GM // d)),
            pl.BlockSpec((tb, d), prev(COL_GH // d)),
            _const_spec(w_m.shape),
            _const_spec(w_h.shape),
            _const_spec(w_o.shape),
        ],
        out_specs=pl.BlockSpec((tb, d), prev(0)),
        out_shape=jax.ShapeDtypeStruct((n, d), F32),
        scratch_shapes=[
            pltpu.VMEM((2, tb, vw), BF16),
            pltpu.VMEM((2, tb, hw), BF16),
            pltpu.VMEM((tb, d), BF16),
            pltpu.VMEM((MIX_TB + 8, 2 * qw), F32),
            pltpu.VMEM((MLSTM_HEADS, MLSTM_QK, MLSTM_V), F32),
            pltpu.VMEM((8, MLSTM_QK), F32),
            pltpu.VMEM((8, LANES), F32),
            pltpu.VMEM((MIX_TB, hw), F32),
            pltpu.VMEM((MIX_TB, hw), F32),
            pltpu.VMEM((MIX_TB, hw), F32),
            pltpu.VMEM((HGRN_NUM_LEVELS, MIX_TB, hw), F32),
            pltpu.VMEM((HGRN_HEADS, HGRN_DK, HGRN_DV), F32),
        ],
        compiler_params=_params("arbitrary"),
        name="mix",
    )(proj, proj, proj, proj, gates_col, gates_row, conv_w, conv_b, bias_col, bias_row, m_head_norm,
      proj, proj, proj, proj, lb_logits, h_head_norm, x2d, proj, proj, w_m, w_h, w_o)


def _xattn_prep_kernel(m_ref, g_ref, wk_ref, wv_ref, wq_ref, wo_ref, qk_ref, vo_ref):
    wk = wk_ref[...].astype(BF16)
    wv = wv_ref[...].astype(BF16)
    wq = wq_ref[...].astype(BF16)
    wo = wo_ref[...].astype(BF16)
    for i in range(m_ref.shape[0]):
        m = _rms(m_ref[i], g_ref[...]).astype(BF16)
        k = jnp.dot(m, wk, preferred_element_type=F32).astype(BF16)
        v = jnp.dot(m, wv, preferred_element_type=F32).astype(BF16)
        qk = lax.dot_general(wq, k, (((1,), (1,)), ((), ())), preferred_element_type=F32)
        qk_ref[i] = (qk * (XATTN_HEAD_DIM ** -0.5)).astype(qk_ref.dtype)
        vo_ref[i] = jnp.dot(v, wo, preferred_element_type=F32).astype(vo_ref.dtype)


def _xattn_prep(mem, g, w_kv, w_q, w_o):
    b, m, d = mem.shape
    hd = XATTN_HEAD_DIM
    nh = XATTN_HEADS
    return pl.pallas_call(
        _xattn_prep_kernel,
        grid=(nh,),
        in_specs=[
            pl.BlockSpec((b, m, d), lambda h: (0, 0, 0)),
            pl.BlockSpec((1, d), lambda h: (0, 0)),
            pl.BlockSpec((d, hd), lambda h: (0, h)),
            pl.BlockSpec((d, hd), lambda h: (0, nh + h)),
            pl.BlockSpec((d, hd), lambda h: (0, h)),
            pl.BlockSpec((hd, d), lambda h: (h, 0)),
        ],
        out_specs=[
            pl.BlockSpec((b, d, m), lambda h: (0, 0, h)),
            pl.BlockSpec((b, m, d), lambda h: (0, h, 0)),
        ],
        out_shape=[
            jax.ShapeDtypeStruct((b, d, nh * m), BF16),
            jax.ShapeDtypeStruct((b, nh * m, d), BF16),
        ],
        compiler_params=_params("arbitrary"),
        name="xattn_prep",
    )(mem, g.reshape(1, d), w_kv, w_kv, w_q, w_o)


def _xattn_kernel(x_ref, g_ref, qk_ref, vo_ref, o_ref, *, mem_len):
    x = x_ref[0]
    h = _rms(x, g_ref[...]).astype(BF16)
    s = jnp.dot(h, qk_ref[0], preferred_element_type=F32)
    probs = []
    for hd in range(XATTN_HEADS):
        s_h = s[:, hd * mem_len:(hd + 1) * mem_len]
        e = jnp.exp(s_h - jnp.max(s_h, axis=-1, keepdims=True))
        probs.append((e / jnp.sum(e, axis=-1, keepdims=True)).astype(BF16))
    p = jnp.concatenate(probs, axis=-1)
    o_ref[0] = x + jnp.dot(p, vo_ref[0], preferred_element_type=F32)


def _xattn(x, g, qk, vo, *, tm=512):
    b, t, d = x.shape
    hm = qk.shape[2]
    return pl.pallas_call(
        functools.partial(_xattn_kernel, mem_len=hm // XATTN_HEADS),
        grid=(b, t // tm),
        in_specs=[
            pl.BlockSpec((1, tm, d), lambda i, c: (i, c, 0)),
            _const_spec((1, d)),
            pl.BlockSpec((1, d, hm), lambda i, c: (i, 0, 0)),
            pl.BlockSpec((1, hm, d), lambda i, c: (i, 0, 0)),
        ],
        out_specs=pl.BlockSpec((1, tm, d), lambda i, c: (i, c, 0)),
        out_shape=jax.ShapeDtypeStruct((b, t, d), F32),
        compiler_params=_params("parallel", "parallel"),
        name="xattn",
    )(x, g.reshape(1, d), qk, vo)


def kernel(x, mem, norm_ffn1, ffn1_w1, ffn1_w3, ffn1_w2, norm_mix, w_in, mlstm_conv_w, mlstm_conv_b,
           mlstm_ig_bias, mlstm_fg_bias, mlstm_head_norm, hgrn_lb_logits, hgrn_head_norm, w_proj_m,
           w_proj_h, w_out, norm_xattn, norm_mem, xattn_wq, xattn_wkv, xattn_wo, norm_ffn2, ffn2_w1,
           ffn2_w3, ffn2_w2, norm_final):
    b, t, d = x.shape
    depth = norm_ffn1.shape[0]
    assert depth == 1 and hgrn_lb_logits.shape[0] == 2
    n = b * t
    l = 0
    bf = lambda w: w.astype(BF16)

    w_in_t = jnp.swapaxes(w_in[l], 0, 1)
    n_gate = 2 * MLSTM_HEADS
    gate_bias = jnp.concatenate([mlstm_ig_bias[l], mlstm_fg_bias[l]]).astype(F32)
    bias_col = jnp.pad(gate_bias, (0, LANES - n_gate)).reshape(1, LANES)
    bias_row = gate_bias.reshape(n_gate, 1)

    x1, hn, gates_col, gates_row = _ffn(x.reshape(n, d), norm_ffn1[l], ffn1_w1[l], ffn1_w3[l], ffn1_w2[l],
                                        norm_mix[l], w_in_t)

    proj = _in_proj(hn, w_in_t)
    x2 = _mix(x1, proj, gates_col, gates_row, mlstm_conv_w[l], mlstm_conv_b[l].reshape(1, -1),
              bias_col, bias_row, mlstm_head_norm[l].reshape(1, -1), hgrn_lb_logits,
              hgrn_head_norm[l].reshape(1, -1), bf(w_proj_m[l]), bf(w_proj_h[l]), bf(w_out[l]), seq_len=t)

    qk, vo = _xattn_prep(mem, norm_mem[l], xattn_wkv[l], xattn_wq[l], xattn_wo[l])
    x3 = _xattn(x2.reshape(b, t, d), norm_xattn[l], qk, vo)

    (out,) = _ffn(x3.reshape(n, d), norm_ffn2[l], ffn2_w1[l], ffn2_w3[l], ffn2_w2[l], norm_final)
    return out.reshape(b, t, d)
```
